```python
import jax, jax.numpy as jnp
from jax import lax
import numpy as np

D_MODEL = 1024
BATCH = 32
SEQ = 256
DEPTH = 2
DEC_BATCH = 2
DEC_SEQ = 1024
PAST_LEN = 256

GRID_W = 64
N_MIXERS = 2
N_RWKV_LAYERS = (DEPTH + 1) // 2
N_NA_LAYERS = DEPTH // 2
HEAD_DIM = 64
N_HEADS = D_MODEL // HEAD_DIM
DECAY_LORA = 64
ICLR_LORA = 64
GATE_LORA = 128
WIN_ROWS = 8
WIN_COLS = 16
N_GROUPS = 4
EXPERTS_PER_GROUP = 4
N_EXPERTS = N_GROUPS * EXPERTS_PER_GROUP
TOP_K_IN_GROUP = 2
D_EXPERT = 256
N_MOD = 6
NORM_EPS = 1e-6
GN_EPS = 64e-5
ATTN_BLOCK = 128

kernel_name = 'bidir_rwkv7_natten_hmoe_step'


def rmsnorm(x, g):
    xf = x.astype(jnp.float32)
    y = xf * lax.rsqrt(jnp.mean(xf * xf, axis=-1, keepdims=True) + NORM_EPS)
    return (y * g.astype(jnp.float32)).astype(x.dtype)


def modulation(cvec, ada_w, ada_b):
    m = jax.nn.silu(cvec) @ ada_w + ada_b
    return jnp.split(jnp.expand_dims(m, -2), N_MOD, axis=-1)


def modulate(x, g, shift, scale):
    return rmsnorm(x, g) * (1 + scale) + shift


def centred_shift(x):
    xp = jnp.pad(x, ((0, 0), (1, 1), (0, 0)))
    return 0.5 * (xp[:, :-2] + xp[:, 2:])


def wkv_scan(r, w, k, v, a, b, s0, reverse):
    def step(S, inp):
        r_t, w_t, k_t, v_t, a_t, b_t = inp
        sa = jnp.einsum('bhij,bhj->bhi', S, a_t)
        S = (S * w_t[..., None, :] + v_t[..., :, None] * k_t[..., None, :]
             + sa[..., :, None] * b_t[..., None, :])
        return S, jnp.einsum('bhij,bhj->bhi', S, r_t)
    xs = tuple(jnp.swapaxes(t.astype(jnp.float32), 0, 1) for t in (r, w, k, v, a, b))
    S, ys = lax.scan(step, s0.astype(jnp.float32), xs, reverse=reverse)
    return jnp.swapaxes(ys, 0, 1), S


def rwkv_time_mix(h, s0, mu, w_r, w_k, w_v, w_o, w0, w1, w2, a0, a1, a2,
                  g1, g2, k_k, k_a, r_k, lnx_g, lnx_b):
    B, T, D = h.shape

    def heads(t):
        return t.reshape(B, T, N_HEADS, HEAD_DIM)

    dx = centred_shift(h) - h
    xr, xw, xk, xv, xa, xg = [h + dx * mu[n] for n in range(6)]
    r = heads(xr @ w_r)
    k_raw = xk @ w_k
    k = heads(k_raw)
    v = heads(xv @ w_v)
    g = jax.nn.sigmoid(xg @ g1) @ g2
    kk = heads(k_raw * k_k).astype(jnp.float32)
    kk = (kk / jnp.maximum(jnp.sqrt(jnp.sum(kk * kk, -1, keepdims=True)), 1e-12)).astype(h.dtype)
    k_a_h = k_a.reshape(N_HEADS, HEAD_DIM)
    y_sum = jnp.zeros((B, T, N_HEADS, HEAD_DIM), jnp.float32)
    bonus = jnp.zeros((B, T, N_HEADS, HEAD_DIM), jnp.float32)
    finals = []
    for d in range(2):
        logw = -jax.nn.softplus(-(w0[d] + jnp.tanh(xw @ w1[d]) @ w2[d]).astype(jnp.float32)) - 0.5
        decay = heads(jnp.exp(-jnp.exp(logw)))
        a = heads(jax.nn.sigmoid(a0[d] + (xa @ a1[d]) @ a2[d]))
        k_d = k * (1 + (a - 1) * k_a_h)
        y_d, s_d = wkv_scan(r, decay, k_d, v, -kk, kk * a, s0[:, d], reverse=(d == 1))
        y_sum = y_sum + y_d
        bonus = bonus + jnp.sum((r * k_d * r_k).astype(jnp.float32), -1, keepdims=True) * v.astype(jnp.float32)
        finals.append(s_d)
    mean = jnp.mean(y_sum, -1, keepdims=True)
    var = jnp.mean(jnp.square(y_sum - mean), -1, keepdims=True)
    yn = ((y_sum - mean) * lax.rsqrt(var + GN_EPS)).reshape(B, T, D) * lnx_g + lnx_b
    y = (yn + bonus.reshape(B, T, D)).astype(h.dtype)
    return (y * g) @ w_o, jnp.stack(finals, axis=1)


def na_qkv(h, w_qkv):
    B, T, _ = h.shape
    q, k, v = jnp.split(h @ w_qkv, 3, axis=-1)
    shp = (B, T, N_HEADS, HEAD_DIM)
    return q.reshape(shp), k.reshape(shp), v.reshape(shp)


def ctx_attention(q, k, v):
    B, L = q.shape[:2]
    nb = L // ATTN_BLOCK
    qb = jnp.moveaxis(q.reshape(B, nb, ATTN_BLOCK, N_HEADS, HEAD_DIM), 1, 0)
    scale = HEAD_DIM ** -0.5

    def block(qi):
        s = jnp.einsum('bqhd,blhd->bhql', qi, k).astype(jnp.float32) * scale
        p = jax.nn.softmax(s, axis=-1).astype(v.dtype)
        return jnp.einsum('bhql,blhd->bqhd', p, v)

    o = lax.map(block, qb)
    return jnp.moveaxis(o, 0, 1).reshape(B, L, N_HEADS * HEAD_DIM)


def latent_neighbourhood_attention(q, k, v, k_ctx, v_ctx, rpb):
    B, T = q.shape[:2]
    rows = T // GRID_W
    kr = min(WIN_ROWS, rows)
    kc = WIN_COLS
    qb = WIN_COLS
    cb = 2 * WIN_COLS
    nqb = GRID_W // qb
    scale = HEAD_DIM ** -0.5
    q5 = q.reshape(B, rows, nqb, qb, N_HEADS, HEAD_DIM)
    k4 = k.reshape(B, rows, GRID_W, N_HEADS, HEAD_DIM)
    v4 = v.reshape(B, rows, GRID_W, N_HEADS, HEAD_DIM)
    r_ar = jnp.arange(rows)
    row_idx = jnp.clip(r_ar - kr // 2, 0, rows - kr)[:, None] + jnp.arange(kr)
    col_band = jnp.clip(jnp.arange(nqb) * qb - kc // 2, 0, GRID_W - cb)[:, None] + jnp.arange(cb)
    qcol = jnp.arange(GRID_W).reshape(nqb, qb)
    q_cstart = jnp.clip(qcol - kc // 2, 0, GRID_W - kc)
    col_mask = ((col_band[:, None, :] >= q_cstart[..., None])
                & (col_band[:, None, :] < q_cstart[..., None] + kc))
    ridx = row_idx[:, None, :, None]
    cidx = col_band[None, :, None, :]
    k_blk = k4[:, ridx, cidx]
    v_blk = v4[:, ridx, cidx]
    dr = row_idx - r_ar[:, None]
    dc = jnp.clip(col_band[:, None, :] - qcol[..., None], -(kc - 1), kc - 1)
    bias = rpb[:, dr[:, None, None, :, None] + WIN_ROWS - 1,
               dc[None, :, :, None, :] + WIN_COLS - 1].astype(jnp.float32)
    s_loc = jnp.einsum('brjqhd,brjkwhd->bhrjqkw', q5, k_blk).astype(jnp.float32) * scale + bias
    s_loc = jnp.where(col_mask[:, :, None, :], s_loc, -jnp.inf)
    s_ctx = jnp.einsum('brjqhd,blhd->bhrjql', q5, k_ctx).astype(jnp.float32) * scale
    n_loc = kr * cb
    s = jnp.concatenate([s_loc.reshape(B, N_HEADS, rows, nqb, qb, n_loc), s_ctx], axis=-1)
    p = jax.nn.softmax(s, axis=-1).astype(v.dtype)
    p_loc = p[..., :n_loc].reshape(B, N_HEADS, rows, nqb, qb, kr, cb)
    p_ctx = p[..., n_loc:]
    o = (jnp.einsum('bhrjqkw,brjkwhd->brjqhd', p_loc, v_blk)
         + jnp.einsum('bhrjql,blhd->brjqhd', p_ctx, v_ctx))
    return o.reshape(B, T, N_HEADS * HEAD_DIM)


def hier_moe(x, w_grp, b_grp, w_exp, b_exp, w1, w3, w2):
    shp = x.shape
    xt = x.reshape(-1, shp[-1])
    pg = jax.nn.softmax((xt @ w_grp + b_grp).astype(jnp.float32), axis=-1)
    p_sel, g_sel = lax.top_k(pg, 1)
    le = (xt @ w_exp + b_exp).astype(jnp.float32).reshape(-1, N_GROUPS, EXPERTS_PER_GROUP)
    le_g = jnp.take_along_axis(le, g_sel[:, :, None], axis=1)[:, 0]
    vals, idx = lax.top_k(jax.nn.softmax(le_g, axis=-1), TOP_K_IN_GROUP)
    wts = p_sel * vals / jnp.sum(vals, -1, keepdims=True)
    eidx = g_sel * EXPERTS_PER_GROUP + idx
    gate = jnp.sum(jax.nn.one_hot(eidx, N_EXPERTS, dtype=jnp.float32) * wts[..., None], axis=1).astype(x.dtype)
    hid = jax.nn.silu(jnp.einsum('nd,edf->nef', xt, w1)) * jnp.einsum('nd,edf->nef', xt, w3)
    y = jnp.einsum('nef,efd->nd', hid * gate[..., None], w2)
    return y.reshape(shp)


def setup_inputs(seed: int = 0) -> dict:
    key = jax.random.key(seed)
    keys = iter(jax.random.split(key, 64))
    f32 = jnp.float32
    D, H, N = D_MODEL, N_HEADS, HEAD_DIM
    NA, NB = N_RWKV_LAYERS, N_NA_LAYERS

    def nrm(shape, scale):
        return scale * jax.random.normal(next(keys), shape, f32)

    def unif(shape, lo, hi):
        return jax.random.uniform(next(keys), shape, f32, lo, hi)

    return {
        'x_prompt': nrm((BATCH, SEQ, D), 1.0),
        'x_sample': nrm((DEC_BATCH, DEC_SEQ, D), 1.0),
        'c': nrm((DEC_BATCH, D), 1.0),
        'state_rwkv': nrm((DEC_BATCH, NA, 2, H, N, N), 0.5),
        'cache_na_k': nrm((DEC_BATCH, NB, PAST_LEN, H, N), 1.0),
        'cache_na_v': nrm((DEC_BATCH, NB, PAST_LEN, H, N), 1.0),
        'c_ctx': nrm((D,), 1.0),
        'norm_mix': 1.0 + nrm((DEPTH, D), 0.1),
        'norm_ffn': 1.0 + nrm((DEPTH, D), 0.1),
        'ada_w': nrm((DEPTH, D, N_MOD * D), 0.5 * D ** -0.5),
        'ada_b': nrm((DEPTH, N_MOD * D), 0.02),
        'rwkv_mu': unif((NA, 6, D), 0.0, 1.0),
        'rwkv_w_r': nrm((NA, D, D), D ** -0.5),
        'rwkv_w_k': nrm((NA, D, D), D ** -0.5),
        'rwkv_w_v': nrm((NA, D, D), D ** -0.5),
        'rwkv_w_o': nrm((NA, D, D), D ** -0.5),
        'rwkv_w0': unif((NA, 2, D), -5.0, 0.0),
        'rwkv_w1': nrm((NA, 2, D, DECAY_LORA), D ** -0.5),
        'rwkv_w2': nrm((NA, 2, DECAY_LORA, D), 0.5 * DECAY_LORA ** -0.5),
        'rwkv_a0': nrm((NA, 2, D), 0.5),
        'rwkv_a1': nrm((NA, 2, D, ICLR_LORA), D ** -0.5),
        'rwkv_a2': nrm((NA, 2, ICLR_LORA, D), ICLR_LORA ** -0.5),
        'rwkv_g1': nrm((NA, D, GATE_LORA), D ** -0.5),
        'rwkv_g2': nrm((NA, GATE_LORA, D), GATE_LORA ** -0.5),
        'rwkv_k_k': 0.85 + nrm((NA, D), 0.05),
        'rwkv_k_a': 1.0 + nrm((NA, D), 0.05),
        'rwkv_r_k': nrm((NA, H, N), 0.1),
        'rwkv_lnx_g': 1.0 + nrm((NA, D), 0.1),
        'rwkv_lnx_b': nrm((NA, D), 0.01),
        'na_w_qkv': nrm((NB, D, 3 * D), D ** -0.5),
        'na_w_o': nrm((NB, D, D), D ** -0.5),
        'na_rpb': nrm((NB, H, 2 * WIN_ROWS - 1, 2 * WIN_COLS - 1), 0.5),
        'moe_w_grp': nrm((DEPTH, D, N_GROUPS), D ** -0.5),
        'moe_b_grp': nrm((DEPTH, N_GROUPS), 0.01),
        'moe_w_exp': nrm((DEPTH, D, N_EXPERTS), D ** -0.5),
        'moe_b_exp': nrm((DEPTH, N_EXPERTS), 0.01),
        'moe_w1': nrm((DEPTH, N_EXPERTS, D, D_EXPERT), D ** -0.5),
        'moe_w3': nrm((DEPTH, N_EXPERTS, D, D_EXPERT), D ** -0.5),
        'moe_w2': nrm((DEPTH, N_EXPERTS, D_EXPERT, D), D_EXPERT ** -0.5),
        'final_norm': 1.0 + nrm((D,), 0.1),
    }


def reference(x_prompt, x_sample, c, state_rwkv, cache_na_k, cache_na_v, c_ctx,
              norm_mix, norm_ffn, ada_w, ada_b,
              rwkv_mu, rwkv_w_r, rwkv_w_k, rwkv_w_v, rwkv_w_o, rwkv_w0, rwkv_w1, rwkv_w2,
              rwkv_a0, rwkv_a1, rwkv_a2, rwkv_g1, rwkv_g2, rwkv_k_k, rwkv_k_a, rwkv_r_k,
              rwkv_lnx_g, rwkv_lnx_b,
              na_w_qkv, na_w_o, na_rpb,
              moe_w_grp, moe_b_grp, moe_w_exp, moe_b_exp, moe_w1, moe_w3, moe_w2,
              final_norm):
    y_p, y_s = x_prompt, x_sample
    new_states, new_k, new_v = [], [], []
    for i in range(DEPTH):
        j = i // N_MIXERS
        sh_p, sc_p, gt_p, sh2_p, sc2_p, gt2_p = modulation(c_ctx, ada_w[i], ada_b[i])
        sh_s, sc_s, gt_s, sh2_s, sc2_s, gt2_s = modulation(c, ada_w[i], ada_b[i])
        h_p = modulate(y_p, norm_mix[i], sh_p, sc_p)
        h_s = modulate(y_s, norm_mix[i], sh_s, sc_s)
        if i % N_MIXERS == 0:
            rw = (rwkv_mu[j], rwkv_w_r[j], rwkv_w_k[j], rwkv_w_v[j], rwkv_w_o[j],
                  rwkv_w0[j], rwkv_w1[j], rwkv_w2[j], rwkv_a0[j], rwkv_a1[j], rwkv_a2[j],
                  rwkv_g1[j], rwkv_g2[j], rwkv_k_k[j], rwkv_k_a[j], rwkv_r_k[j],
                  rwkv_lnx_g[j], rwkv_lnx_b[j])
            s_zero = jnp.zeros((y_p.shape[0], 2, N_HEADS, HEAD_DIM, HEAD_DIM), jnp.float32)
            o_p, st = rwkv_time_mix(h_p, s_zero, *rw)
            o_s, _ = rwkv_time_mix(h_s, state_rwkv[:, j], *rw)
            new_states.append(st)
        else:
            q_p, k_p, v_p = na_qkv(h_p, na_w_qkv[j])
            o_p = ctx_attention(q_p, k_p, v_p) @ na_w_o[j]
            q_s, k_s, v_s = na_qkv(h_s, na_w_qkv[j])
            o_s = latent_neighbourhood_attention(q_s, k_s, v_s, cache_na_k[:, j], cache_na_v[:, j],
                                                 na_rpb[j]) @ na_w_o[j]
            new_k.append(k_p)
            new_v.append(v_p)
        y_p = y_p + gt_p * o_p
        y_s = y_s + gt_s * o_s
        moe = (moe_w_grp[i], moe_b_grp[i], moe_w_exp[i], moe_b_exp[i], moe_w1[i], moe_w3[i], moe_w2[i])
        y_p = y_p + gt2_p * hier_moe(modulate(y_p, norm_ffn[i], sh2_p, sc2_p), *moe)
        y_s = y_s + gt2_s * hier_moe(modulate(y_s, norm_ffn[i], sh2_s, sc2_s), *moe)
    y_prompt = rmsnorm(y_p, final_norm)
    y_sample = rmsnorm(y_s, final_norm)
    new_state_rwkv = jnp.stack(new_states, axis=1)
    new_cache_na_k = jnp.stack(new_k, axis=1)
    new_cache_na_v = jnp.stack(new_v, axis=1)
    return (y_prompt, y_sample, new_state_rwkv, new_cache_na_k, new_cache_na_v)
```

```python
import functools
import math

import jax
import jax.numpy as jnp
from jax import lax
from jax.experimental import pallas as pl
from jax.experimental.pallas import tpu as pltpu

F32 = jnp.float32
BF = jnp.bfloat16

D = 1024
NH = 16
HD = 64
NB_P, T_P = 32, 256
NB_S, T_S = 2, 1024
NPR = NB_P * T_P
NSR = NB_S * T_S
NTOK = NPR + NSR
TM = 256
NT = NTOK // TM
NT_P = NPR // TM
TPS = T_S // TM
TMX = 1024
NMOD = 6
NE = 16
EPG = 4
DE = 256
CH = 64
LANES = 128
GRID_W = 64
WIN_R, WIN_C = 8, 16
NORM_EPS = 1e-6
GN_EPS = 64e-5
NEG = -1e30
VMEM_LIMIT = 56 * 1024 * 1024


def _cp(sem):
    return pltpu.CompilerParams(dimension_semantics=sem, vmem_limit_bytes=VMEM_LIMIT)


def _bdot(a, b):
    return jnp.dot(a.astype(BF), b.astype(BF), preferred_element_type=F32)


def _bdot_nt(a, b):
    return lax.dot_general(a.astype(BF), b.astype(BF), (((1,), (1,)), ((), ())),
                           preferred_element_type=F32)


def _split2(x):
    hi = x.astype(BF)
    lo = (x - hi.astype(F32)).astype(BF)
    return hi, lo


def _split3(x):
    hi = x.astype(BF)
    r1 = x - hi.astype(F32)
    mid = r1.astype(BF)
    lo = (r1 - mid.astype(F32)).astype(BF)
    return hi, mid, lo


def _seg_sum(x, bo):
    outs = []
    for c in range(x.shape[1] // 256):
        hi, lo = _split2(x[:, c * 256:(c + 1) * 256])
        outs.append(jnp.dot(hi, bo, preferred_element_type=F32)
                    + jnp.dot(lo, bo, preferred_element_type=F32))
    return outs[0] if len(outs) == 1 else jnp.concatenate(outs, axis=1)


def _tile_mod_row(i):
    return jnp.where(i < NT_P, 0, 1 + (i - NT_P) // TPS)


def _modulate(x, g, sh, sc):
    ms = jnp.mean(x * x, axis=-1, keepdims=True)
    return x * lax.rsqrt(ms + NORM_EPS) * g * (1.0 + sc) + sh


def _mod_vec(mod_ref, row, k):
    return mod_ref[pl.ds(row, 1), pl.ds(k * D, D)]


def _adaln_kernel(c_ref, w_ref, b_ref, o_ref):
    c = c_ref[...]
    s = c * jax.nn.sigmoid(c)
    o_ref[...] = _bdot(s, w_ref[...]) + b_ref[...]


def _adaln(c8, ada_w, ada_b):
    nl = ada_w.shape[0]
    tn = 1536
    return pl.pallas_call(
        _adaln_kernel,
        grid=(nl, NMOD * D // tn),
        in_specs=[pl.BlockSpec((8, D), lambda l, j: (0, 0)),
                  pl.BlockSpec((None, D, tn), lambda l, j: (l, 0, j)),
                  pl.BlockSpec((None, 1, tn), lambda l, j: (l, 0, j))],
        out_specs=pl.BlockSpec((None, 8, tn), lambda l, j: (l, 0, j)),
        out_shape=jax.ShapeDtypeStruct((nl, 8, NMOD * D), F32),
        compiler_params=_cp(("parallel", "parallel")),
        name="adaln",
    )(c8, ada_w, ada_b.reshape(nl, 1, NMOD * D))


def _front_kernel(y_ref, yp_ref, yn_ref, mod_ref, nrm_ref, mu_ref, wr_ref, wk_ref, wv_ref,
                  w1_ref, a1_ref, g1_ref, w2_ref, a2_ref, g2_ref, w0_ref, a0_ref, kkw_ref, bo_ref,
                  r_o, k_o, v_o, g_o, kk_o, lw0_o, lw1_o, ag0_o, ag1_o):
    i = pl.program_id(0)
    row = _tile_mod_row(i)
    q = (i - NT_P) % TPS
    first = jnp.logical_or(i < NT_P, q == 0)
    last = jnp.logical_or(i < NT_P, q == TPS - 1)
    sh = _mod_vec(mod_ref, row, 0)
    sc = _mod_vec(mod_ref, row, 1)
    g = nrm_ref[...]
    h = _modulate(y_ref[...], g, sh, sc)
    hp = _modulate(yp_ref[...], g, sh, sc)[7:8]
    hn = _modulate(yn_ref[...], g, sh, sc)[0:1]
    hp = jnp.where(first, 0.0, hp)
    hn = jnp.where(last, 0.0, hn)
    rid = lax.broadcasted_iota(jnp.int32, (TM, D), 0)
    prev = jnp.where(rid == 0, hp, pltpu.roll(h, 1, 0))
    nxt = jnp.where(rid == TM - 1, hn, pltpu.roll(h, TM - 1, 0))
    dx = 0.5 * (prev + nxt) - h

    def mix(n):
        return (h + dx * mu_ref[n:n + 1, :]).astype(BF)

    r_o[...] = jnp.dot(mix(0), wr_ref[...], preferred_element_type=F32)
    xw = mix(1)
    lora_w = _bdot(jnp.tanh(jnp.dot(xw, w1_ref[...], preferred_element_type=F32)), w2_ref[...])
    c05 = math.exp(-0.5)
    lw0_o[...] = -c05 * jax.nn.sigmoid(w0_ref[0:1, :] + lora_w[:, :D])
    lw1_o[...] = -c05 * jax.nn.sigmoid(w0_ref[1:2, :] + lora_w[:, D:])
    k_raw = jnp.dot(mix(2), wk_ref[...], preferred_element_type=F32)
    k_o[...] = k_raw
    kq = k_raw * kkw_ref[...]
    ss = _seg_sum(kq * kq, bo_ref[...])
    kk_o[...] = kq / jnp.maximum(jnp.sqrt(ss), 1e-12)
    v_o[...] = jnp.dot(mix(3), wv_ref[...], preferred_element_type=F32)
    xa = mix(4)
    lora_a = _bdot(jnp.dot(xa, a1_ref[...], preferred_element_type=F32), a2_ref[...])
    ag0_o[...] = jax.nn.sigmoid(a0_ref[0:1, :] + lora_a[:, :D])
    ag1_o[...] = jax.nn.sigmoid(a0_ref[1:2, :] + lora_a[:, D:])
    xg = mix(5)
    g_o[...] = _bdot(jax.nn.sigmoid(jnp.dot(xg, g1_ref[...], preferred_element_type=F32)), g2_ref[...])


def _rwkv_front(y, mod, nrm, mu, wr, wk, wv, w1c, a1c, g1, w2bd, a2bd, g2, w0, a0, kkw, bo):
    tile = pl.BlockSpec((TM, D), lambda i: (i, 0))
    nblk8 = NTOK // 8
    full = lambda a: pl.BlockSpec(a.shape, lambda i: (0,) * a.ndim)
    ins = [y, y, y, mod, nrm, mu, wr, wk, wv, w1c, a1c, g1, w2bd, a2bd, g2, w0, a0, kkw, bo]
    in_specs = [tile,
                pl.BlockSpec((8, D), lambda i: (jnp.maximum(i * (TM // 8) - 1, 0), 0)),
                pl.BlockSpec((8, D), lambda i: (jnp.minimum((i + 1) * (TM // 8), nblk8 - 1), 0))]
    in_specs += [full(a) for a in ins[3:]]
    out = jax.ShapeDtypeStruct((NTOK, D), F32)
    return pl.pallas_call(
        _front_kernel,
        grid=(NT,),
        in_specs=in_specs,
        out_specs=[tile] * 9,
        out_shape=[out] * 9,
        compiler_params=_cp(("parallel",)),
        name="rwkv_front",
    )(*ins)


def _wkv_chunk(d, S, r, k, v, kk, lw, ag, ka, rk, cst):
    tri4, strict, incl, lo_half, hi_half, h0, eye, bd, bo2 = cst
    b = kk * ag
    kd = k * (1.0 + (ag - 1.0) * ka)
    hi, mid, lo = _split3(lw)
    cs = jnp.dot(tri4[d], jnp.concatenate([hi, mid, lo, jnp.zeros_like(lo)], axis=0),
                 preferred_element_type=F32)
    tot = cs[CH - 1:CH, :] if d == 0 else cs[0:1, :]
    p_in = jnp.exp(cs)
    p_inv = jnp.exp(-cs)
    p_end = jnp.exp(tot - cs)
    at = -kk * jnp.exp(cs - lw)
    rt = r * p_in
    bt = b * p_inv
    kt = kd * p_inv
    zero = jnp.zeros_like(at)
    lhs4 = jnp.concatenate([jnp.where(h0, at, zero), jnp.where(h0, zero, at),
                            jnp.where(h0, rt, zero), jnp.where(h0, zero, rt)], axis=0)
    gram = _bdot_nt(lhs4, jnp.concatenate([bt, kt], axis=0))
    w0y0 = _bdot_nt(jnp.concatenate([at, rt], axis=0), S)
    w0, y0 = w0y0[:CH], w0y0[CH:]
    vb = v.astype(BF)
    vv = jnp.concatenate([vb, vb], axis=0)
    a_rows = [jnp.where(strict[d], gram[h * CH:(h + 1) * CH], 0.0) for h in range(2)]
    ws = [w0 + jnp.dot(jnp.where(hi_half, a_rows[h], 0.0).astype(BF), vv,
                       preferred_element_type=F32) for h in range(2)]
    wm = jnp.where(h0, ws[0], ws[1]).astype(BF)
    wm2 = jnp.concatenate([wm, wm], axis=0)
    us = []
    for h in range(2):
        lp = jnp.where(lo_half, a_rows[h], 0.0)
        x = eye + lp
        zpad = jnp.zeros((CH, LANES), BF)
        for _ in range(int(math.log2(CH)) - 1):
            lpb = lp.astype(BF)
            lp = jnp.dot(lpb, jnp.concatenate([lpb, zpad], axis=0), preferred_element_type=F32)
            x = x + jnp.dot(x.astype(BF), jnp.concatenate([lp.astype(BF), zpad], axis=0),
                            preferred_element_type=F32)
        us.append(jnp.dot(x.astype(BF), wm2, preferred_element_type=F32))
    u = jnp.where(h0, us[0], us[1])
    uv = jnp.concatenate([u.astype(BF), vb], axis=0)
    ys = [jnp.dot(jnp.where(incl[d], gram[(2 + h) * CH:(3 + h) * CH], 0.0).astype(BF), uv,
                  preferred_element_type=F32) for h in range(2)]
    y = y0 + jnp.where(h0, ys[0], ys[1])
    uvt = jnp.concatenate([u, v], axis=0).T
    ds = _bdot(uvt, jnp.concatenate([b * p_end, kd * p_end], axis=0))
    s_new = S * jnp.exp(tot) + jnp.where(bd, ds, 0.0)
    qh, ql = _split2(r * kd * rk)
    qs = jnp.dot(qh, bo2, preferred_element_type=F32) + jnp.dot(ql, bo2, preferred_element_type=F32)
    return s_new, y, qs * v


def _wkv_consts():
    t = lax.broadcasted_iota(jnp.int32, (CH, 4 * CH), 0)
    s = lax.broadcasted_iota(jnp.int32, (CH, 4 * CH), 1) & (CH - 1)
    tri4 = [jnp.where(s <= t, 1.0, 0.0).astype(BF), jnp.where(s >= t, 1.0, 0.0).astype(BF)]
    t2 = lax.broadcasted_iota(jnp.int32, (CH, LANES), 0)
    l2 = lax.broadcasted_iota(jnp.int32, (CH, LANES), 1)
    s2 = l2 & (CH - 1)
    strict = [s2 < t2, s2 > t2]
    incl = [s2 <= t2, s2 >= t2]
    lo_half = l2 < CH
    hi_half = l2 >= CH
    h0 = l2 < HD
    eye = jnp.where(l2 == t2, 1.0, 0.0)
    ri = lax.broadcasted_iota(jnp.int32, (LANES, LANES), 0)
    ci = lax.broadcasted_iota(jnp.int32, (LANES, LANES), 1)
    bd = (ri // HD) == (ci // HD)
    bo2 = jnp.where(bd, 1.0, 0.0).astype(BF)
    return tri4, strict, incl, lo_half, hi_half, h0, eye, bd, bo2


def _wkv_kernel(T, has_s0, *refs):
    (r_ref, k_ref, v_ref, kk_ref, lw0_ref, lw1_ref, ag0_ref, ag1_ref, ka_ref, rk_ref), rest = refs[:10], refs[10:]
    if has_s0:
        s0_ref, y_ref, bon_ref, s_scr = rest
    else:
        y_ref, bon_ref, st_ref, s_scr = rest
    nch = T // CH
    npair = y_ref.shape[1] // LANES
    cst = _wkv_consts()
    if has_s0:
        s_scr[...] = s0_ref[...]
    else:
        s_scr[...] = jnp.zeros_like(s_scr)
    y_ref[...] = jnp.zeros_like(y_ref)
    bon_ref[...] = jnp.zeros_like(bon_ref)
    lw_refs = (lw0_ref, lw1_ref)
    ag_refs = (ag0_ref, ag1_ref)

    def body(c, carry):
        for p in range(npair):
            ls = pl.ds(p * LANES, LANES)
            ka = ka_ref[:, ls]
            rk = rk_ref[:, ls]
            for d in range(2):
                r0 = pl.multiple_of((c if d == 0 else nch - 1 - c) * CH, CH)
                rs = pl.ds(r0, CH)
                s_new, y, bon = _wkv_chunk(
                    d, s_scr[d, p], r_ref[rs, ls], k_ref[rs, ls], v_ref[rs, ls], kk_ref[rs, ls],
                    lw_refs[d][rs, ls], ag_refs[d][rs, ls], ka, rk, cst)
                s_scr[d, p] = s_new
                y_ref[rs, ls] += y
                bon_ref[rs, ls] += bon
        return carry

    lax.fori_loop(0, nch, body, 0)
    if not has_s0:
        st_ref[...] = s_scr[...]


def _wkv(arrs, ka, rk, T, nb, row_blk0, s0_bd=None, lanes=256):
    ng = D // lanes
    npair = lanes // LANES
    seq = pl.BlockSpec((T, lanes), lambda b, g: (row_blk0 + b, g))
    vec = pl.BlockSpec((1, lanes), lambda b, g: (0, g))
    st = pl.BlockSpec((None, 2, npair, LANES, LANES), lambda b, g: (b, 0, g, 0, 0))
    out_seq = pl.BlockSpec((T, lanes), lambda b, g: (b, g))
    in_specs = [seq] * 8 + [vec, vec]
    ins = list(arrs) + [ka, rk]
    out_specs = [out_seq, out_seq]
    out_shape = [jax.ShapeDtypeStruct((nb * T, D), F32)] * 2
    if s0_bd is not None:
        in_specs.append(st)
        ins.append(s0_bd)
    else:
        out_specs.append(st)
        out_shape.append(jax.ShapeDtypeStruct((nb, 2, D // LANES, LANES, LANES), F32))
    return pl.pallas_call(
        functools.partial(_wkv_kernel, T, s0_bd is not None),
        grid=(nb, ng),
        in_specs=in_specs,
        out_specs=out_specs,
        out_shape=out_shape,
        scratch_shapes=[pltpu.VMEM((2, npair, LANES, LANES), F32)],
        compiler_params=_cp(("parallel", "parallel")),
        name="wkv_T%d" % T,
    )(*ins)


def _proj_kernel(gn, y_ref, ap_ref, as_ref, *rest):
    if gn:
        bp_ref, bs_ref, g_ref, lng_ref, lnb_ref, bo_ref, mod_ref, w_ref, o_ref = rest
    else:
        mod_ref, w_ref, o_ref = rest
    i = pl.program_id(0)
    is_p = i < NT_P
    a = jnp.where(is_p, ap_ref[...], as_ref[...])
    if gn:
        bo = bo_ref[...]
        mean = _seg_sum(a, bo) * (1.0 / HD)
        cen = a - mean
        var = _seg_sum(cen * cen, bo) * (1.0 / HD)
        yn = cen * lax.rsqrt(var + GN_EPS) * lng_ref[...] + lnb_ref[...]
        a = (yn + jnp.where(is_p, bp_ref[...], bs_ref[...])) * g_ref[...]
    gt = _mod_vec(mod_ref, _tile_mod_row(i), 2)
    o_ref[...] = y_ref[...] + gt * _bdot(a, w_ref[...])


def _proj_residual(y, a_p, a_s, mod, w, gn_args=None, name="proj"):
    tile = pl.BlockSpec((TM, D), lambda i: (i, 0))
    tile_p = pl.BlockSpec((TM, D), lambda i: (jnp.minimum(i, NT_P - 1), 0))
    tile_s = pl.BlockSpec((TM, D), lambda i: (jnp.maximum(i - NT_P, 0), 0))
    full = lambda a: pl.BlockSpec(a.shape, lambda i: (0,) * a.ndim)
    ins = [y, a_p, a_s]
    in_specs = [tile, tile_p, tile_s]
    if gn_args is not None:
        b_p, b_s, g, lng, lnb, bo = gn_args
        ins += [b_p, b_s, g, lng, lnb, bo]
        in_specs += [tile_p, tile_s, tile, full(lng), full(lnb), full(bo)]
    ins += [mod, w]
    in_specs += [full(mod), full(w)]
    return pl.pallas_call(
        functools.partial(_proj_kernel, gn_args is not None),
        grid=(NT,),
        in_specs=in_specs,
        out_specs=tile,
        out_shape=jax.ShapeDtypeStruct((NTOK, D), F32),
        compiler_params=_cp(("parallel",)),
        name=name,
    )(*ins)


def _router_kernel(y_ref, mod_ref, nrm_ref, wr_ref, br_ref, xt_o, gate_o):
    i = pl.program_id(0)
    row = _tile_mod_row(i)
    xt = _modulate(y_ref[...], nrm_ref[...], _mod_vec(mod_ref, row, 3), _mod_vec(mod_ref, row, 4))
    xt_o[...] = xt.astype(BF)
    x1, x2, x3 = _split3(xt)
    w1, w2, w3 = _split3(wr_ref[...])
    dot = lambda a, b: jnp.dot(a, b, preferred_element_type=F32)
    logits = (dot(x3, w1) + dot(x2, w2) + dot(x1, w3)) + (dot(x2, w1) + dot(x1, w2)) + dot(x1, w1)
    logits = logits + br_ref[...]
    lane = lax.broadcasted_iota(jnp.int32, logits.shape, 1).astype(F32)
    isg = jnp.logical_and(lane >= NE, lane < NE + NE // EPG)
    mg = jnp.max(jnp.where(isg, logits, NEG), axis=-1, keepdims=True)
    eg = jnp.where(isg, jnp.exp(jnp.minimum(logits - mg, 0.0)), 0.0)
    p_sel = 1.0 / jnp.sum(eg, axis=-1, keepdims=True)
    gidx = jnp.min(jnp.where(jnp.logical_and(isg, logits == mg), lane, 1e3), axis=-1, keepdims=True) - NE
    lo_l = gidx * EPG
    ing = jnp.logical_and(lane >= lo_l, lane < lo_l + EPG)
    me = jnp.max(jnp.where(ing, logits, NEG), axis=-1, keepdims=True)
    ee = jnp.where(ing, jnp.exp(jnp.minimum(logits - me, 0.0)), 0.0)
    se = jnp.sum(ee, axis=-1, keepdims=True)
    pe = ee / se
    i1 = jnp.min(jnp.where(jnp.logical_and(ing, logits == me), lane, 1e3), axis=-1, keepdims=True)
    v1 = 1.0 / se
    rest = jnp.logical_and(ing, lane != i1)
    v2 = jnp.max(jnp.where(rest, pe, -1.0), axis=-1, keepdims=True)
    i2 = jnp.min(jnp.where(jnp.logical_and(rest, pe == v2), lane, 1e3), axis=-1, keepdims=True)
    den = v1 + v2
    gate_o[...] = (jnp.where(lane == i1, p_sel * v1 / den, 0.0)
                   + jnp.where(lane == i2, p_sel * v2 / den, 0.0))


def _moe_router(y, mod, nrm, wr, br):
    tile = pl.BlockSpec((TM, D), lambda i: (i, 0))
    full = lambda a: pl.BlockSpec(a.shape, lambda i: (0,) * a.ndim)
    return pl.pallas_call(
        _router_kernel,
        grid=(NT,),
        in_specs=[tile, full(mod), full(nrm), full(wr), full(br)],
        out_specs=[tile, pl.BlockSpec((TM, LANES), lambda i: (i, 0))],
        out_shape=[jax.ShapeDtypeStruct((NTOK, D), BF), jax.ShapeDtypeStruct((NTOK, LANES), F32)],
        compiler_params=_cp(("parallel",)),
        name="moe_router",
    )(y, mod, nrm, wr, br)


def _experts_kernel(final, xt_ref, gate_ref, w13_ref, w2_ref, y_ref, mod_ref, fin_ref, o_ref):
    i = pl.program_id(0)
    e = pl.program_id(1)

    @pl.when(e == 0)
    def _():
        o_ref[...] = jnp.zeros_like(o_ref)

    h13 = jnp.dot(xt_ref[...], w13_ref[...], preferred_element_type=F32)
    h1, h3 = h13[:, :DE], h13[:, DE:]
    lane = lax.broadcasted_iota(jnp.int32, gate_ref.shape, 1)
    ge = jnp.sum(jnp.where(lane == e, gate_ref[...], 0.0), axis=-1, keepdims=True)
    hid = (h1 * jax.nn.sigmoid(h1)) * h3 * ge
    o_ref[...] += jnp.dot(hid.astype(BF), w2_ref[...], preferred_element_type=F32)

    @pl.when(e == NE - 1)
    def _():
        row = jnp.where(i < NPR // TMX, 0, i - NPR // TMX + 1)
        out = y_ref[...] + _mod_vec(mod_ref, row, 5) * o_ref[...]
        if final:
            ms = jnp.mean(out * out, axis=-1, keepdims=True)
            out = out * lax.rsqrt(ms + NORM_EPS) * fin_ref[...]
        o_ref[...] = out


def _moe_experts(xt, gate, w13, w2, y, mod, fin, final):
    tile = pl.BlockSpec((TMX, D), lambda i, e: (i, 0))
    full = lambda a: pl.BlockSpec(a.shape, lambda i, e: (0,) * a.ndim)
    return pl.pallas_call(
        functools.partial(_experts_kernel, final),
        grid=(NTOK // TMX, NE),
        in_specs=[tile, pl.BlockSpec((TMX, LANES), lambda i, e: (i, 0)),
                  pl.BlockSpec((None, D, 2 * DE), lambda i, e: (e, 0, 0)),
                  pl.BlockSpec((None, DE, D), lambda i, e: (e, 0, 0)),
                  tile, full(mod), full(fin)],
        out_specs=tile,
        out_shape=jax.ShapeDtypeStruct((NTOK, D), F32),
        compiler_params=_cp(("parallel", "arbitrary")),
        name="moe_experts",
    )(xt, gate, w13, w2, y, mod, fin)


def _qkv_kernel(y_ref, mod_ref, nrm_ref, w_ref, q_o, k_o, v_o):
    i = pl.program_id(0)
    row = _tile_mod_row(i)
    h = _modulate(y_ref[...], nrm_ref[...], _mod_vec(mod_ref, row, 0), _mod_vec(mod_ref, row, 1))
    qkv = _bdot(h, w_ref[...])
    q_o[...] = qkv[:, :D]
    k_o[...] = qkv[:, D:2 * D]
    v_o[...] = qkv[:, 2 * D:]


def _na_qkv(y, mod, nrm, w):
    tile = pl.BlockSpec((TM, D), lambda i: (i, 0))
    full = lambda a: pl.BlockSpec(a.shape, lambda i: (0,) * a.ndim)
    out = jax.ShapeDtypeStruct((NTOK, D), F32)
    return pl.pallas_call(
        _qkv_kernel,
        grid=(NT,),
        in_specs=[tile, full(mod), full(nrm), full(w)],
        out_specs=[tile] * 3,
        out_shape=[out] * 3,
        compiler_params=_cp(("parallel",)),
        name="na_qkv",
    )(y, mod, nrm, w)


def _softmax_rows(s):
    m = jnp.max(s, axis=-1, keepdims=True)
    e = jnp.exp(s - m)
    return e / jnp.sum(e, axis=-1, keepdims=True)


def _ctx_attn_kernel(q_ref, k_ref, v_ref, o_ref):
    scale = HD ** -0.5
    h0 = lax.broadcasted_iota(jnp.int32, (T_P, LANES), 1) < HD
    for p in range(D // LANES):
        ls = pl.ds(p * LANES, LANES)
        q = q_ref[:, ls]
        kb = k_ref[:, ls].astype(BF)
        vb = v_ref[:, ls].astype(BF)
        outs = []
        for h in range(2):
            qm = jnp.where(h0 if h == 0 else jnp.logical_not(h0), q, 0.0)
            s = _bdot_nt(qm, kb) * scale
            outs.append(jnp.dot(_softmax_rows(s).astype(BF), vb, preferred_element_type=F32))
        o_ref[:, ls] = jnp.where(h0, outs[0], outs[1])


def _ctx_attn(q, k, v):
    blk = pl.BlockSpec((T_P, D), lambda b: (b, 0))
    return pl.pallas_call(
        _ctx_attn_kernel,
        grid=(NB_P,),
        in_specs=[blk] * 3,
        out_specs=blk,
        out_shape=jax.ShapeDtypeStruct((NPR, D), F32),
        compiler_params=_cp(("parallel",)),
        name="ctx_attn",
    )(q, k, v)


def _nbr_attn_kernel(q_ref, k_ref, v_ref, kc_ref, vc_ref, tz_ref, o_ref):
    scale = HD ** -0.5
    rows = T_S // GRID_W
    nloc = WIN_R * GRID_W
    h0 = lax.broadcasted_iota(jnp.int32, (GRID_W, LANES), 1) < HD
    kcb = kc_ref[...].astype(BF)
    vcb = vc_ref[...].astype(BF)

    def body(r, carry):
        start = jnp.clip(r - WIN_R // 2, 0, rows - WIN_R)
        qs = pl.ds(pl.multiple_of(r * GRID_W, GRID_W), GRID_W)
        ks = pl.ds(pl.multiple_of(start * GRID_W, GRID_W), nloc)
        q = q_ref[qs, :]
        klb = k_ref[ks, :].astype(BF)
        vlb = v_ref[ks, :].astype(BF)
        j0 = start - r + WIN_R - 1
        outs = []
        for h in range(2):
            qm = jnp.where(h0 if h == 0 else jnp.logical_not(h0), q, 0.0)
            bias = jnp.concatenate([tz_ref[h, j0 + 2 * m] for m in range(WIN_R // 2)], axis=1)
            sl = _bdot_nt(qm, klb) * scale + bias
            sc = _bdot_nt(qm, kcb) * scale
            m = jnp.maximum(jnp.max(sl, axis=-1, keepdims=True), jnp.max(sc, axis=-1, keepdims=True))
            el = jnp.exp(sl - m)
            ec = jnp.exp(sc - m)
            den = jnp.sum(el, axis=-1, keepdims=True) + jnp.sum(ec, axis=-1, keepdims=True)
            outs.append(jnp.dot((el / den).astype(BF), vlb, preferred_element_type=F32)
                        + jnp.dot((ec / den).astype(BF), vcb, preferred_element_type=F32))
        o_ref[qs, :] = jnp.where(h0, outs[0], outs[1])
        return carry

    lax.fori_loop(0, rows, body, 0)


def _nbr_attn(q, k, v, kc, vc, tz2):
    npair = D // LANES
    seq = pl.BlockSpec((T_S, LANES), lambda b, p: (NPR // T_S + b, p))
    ctx = pl.BlockSpec((kc.shape[0] // NB_S, LANES), lambda b, p: (b, p))
    return pl.pallas_call(
        _nbr_attn_kernel,
        grid=(NB_S, npair),
        in_specs=[seq, seq, seq, ctx, ctx,
                  pl.BlockSpec((2,) + tz2.shape[1:], lambda b, p: (p, 0, 0, 0))],
        out_specs=pl.BlockSpec((T_S, LANES), lambda b, p: (b, p)),
        out_shape=jax.ShapeDtypeStruct((NSR, D), F32),
        compiler_params=_cp(("parallel", "parallel")),
        name="nbr_attn",
    )(q, k, v, kc, vc, tz2)


def _block_diag2(a, b):
    z = jnp.zeros_like(a)
    return jnp.concatenate([jnp.concatenate([a, z], axis=1), jnp.concatenate([z, b], axis=1)], axis=0)


def _nbr_bias_table(rpb):
    qc = jnp.arange(GRID_W)[:, None]
    kc = jnp.arange(GRID_W)[None, :]
    cstart = jnp.clip(qc - WIN_C // 2, 0, GRID_W - WIN_C)
    valid = jnp.logical_and(kc >= cstart, kc < cstart + WIN_C)
    dc = jnp.clip(kc - qc, -(WIN_C - 1), WIN_C - 1) + WIN_C - 1
    tz = jnp.where(valid, rpb[:, :, dc], NEG)
    return jnp.concatenate([tz[:, :-1], tz[:, 1:]], axis=-1)


def kernel(x_prompt, x_sample, c, state_rwkv, cache_na_k, cache_na_v, c_ctx, norm_mix, norm_ffn, ada_w, ada_b, rwkv_mu, rwkv_w_r, rwkv_w_k, rwkv_w_v, rwkv_w_o, rwkv_w0, rwkv_w1, rwkv_w2, rwkv_a0, rwkv_a1, rwkv_a2, rwkv_g1, rwkv_g2, rwkv_k_k, rwkv_k_a, rwkv_r_k, rwkv_lnx_g, rwkv_lnx_b, na_w_qkv, na_w_o, na_rpb, moe_w_grp, moe_b_grp, moe_w_exp, moe_b_exp, moe_w1, moe_w3, moe_w2, final_norm):
    y = jnp.concatenate([x_prompt.reshape(NPR, D), x_sample.reshape(NSR, D)], axis=0)
    c8 = jnp.concatenate([c_ctx[None, :], c, jnp.zeros((8 - 1 - NB_S, D), F32)], axis=0)
    mod = _adaln(c8, ada_w, ada_b)
    ri = jnp.arange(2 * LANES)[:, None] // HD
    bo = (ri == ri.T).astype(BF)
    row = lambda a: a.reshape(1, D)

    w1c = jnp.concatenate([rwkv_w1[0, 0], rwkv_w1[0, 1]], axis=1).astype(BF)
    a1c = jnp.concatenate([rwkv_a1[0, 0], rwkv_a1[0, 1]], axis=1).astype(BF)
    w2bd = _block_diag2(rwkv_w2[0, 0], rwkv_w2[0, 1]).astype(BF)
    a2bd = _block_diag2(rwkv_a2[0, 0], rwkv_a2[0, 1]).astype(BF)
    r, k, v, g, kk, lw0, lw1, ag0, ag1 = _rwkv_front(
        y, mod[0], row(norm_mix[0]), rwkv_mu[0], rwkv_w_r[0].astype(BF), rwkv_w_k[0].astype(BF),
        rwkv_w_v[0].astype(BF), w1c, a1c, rwkv_g1[0].astype(BF), w2bd, a2bd, rwkv_g2[0].astype(BF),
        rwkv_w0[0], rwkv_a0[0], row(rwkv_k_k[0]), bo)
    arrs = (r, k, v, kk, lw0, lw1, ag0, ag1)
    ka, rk = row(rwkv_k_a[0]), row(rwkv_r_k[0])
    ys_p, bon_p, st_bd = _wkv(arrs, ka, rk, T_P, NB_P, 0)
    s0 = state_rwkv[:, 0].reshape(NB_S, 2, NH // 2, 2, HD, HD)
    z = jnp.zeros_like(s0[:, :, :, 0])
    s0_bd = jnp.concatenate([jnp.concatenate([s0[:, :, :, 0], z], axis=-1),
                             jnp.concatenate([z, s0[:, :, :, 1]], axis=-1)], axis=-2)
    ys_s, bon_s = _wkv(arrs, ka, rk, T_S, NB_S, NPR // T_S, s0_bd=s0_bd)
    new_state = jnp.stack([st_bd[..., :HD, :HD], st_bd[..., HD:, HD:]], axis=3)
    new_state_rwkv = new_state.reshape(NB_P, 1, 2, NH, HD, HD)
    y = _proj_residual(y, ys_p, ys_s, mod[0], rwkv_w_o[0].astype(BF),
                       gn_args=(bon_p, bon_s, g, row(rwkv_lnx_g[0]), row(rwkv_lnx_b[0]), bo),
                       name="rwkv_out")

    def moe(y, i, final):
        wr = jnp.concatenate([moe_w_exp[i], moe_w_grp[i],
                              jnp.zeros((D, LANES - NE - NE // EPG), F32)], axis=1)
        br = jnp.concatenate([moe_b_exp[i], moe_b_grp[i],
                              jnp.zeros((LANES - NE - NE // EPG,), F32)]).reshape(1, LANES)
        xt, gate = _moe_router(y, mod[i], row(norm_ffn[i]), wr, br)
        w13 = jnp.concatenate([moe_w1[i], moe_w3[i]], axis=-1).astype(BF)
        return _moe_experts(xt, gate, w13, moe_w2[i].astype(BF), y, mod[i], row(final_norm), final)

    y = moe(y, 0, False)

    q, k, v = _na_qkv(y, mod[1], row(norm_mix[1]), na_w_qkv[0].astype(BF))
    o_p = _ctx_attn(q, k, v)
    kc = cache_na_k[:, 0].reshape(NB_S * cache_na_k.shape[2], D)
    vc = cache_na_v[:, 0].reshape(NB_S * cache_na_v.shape[2], D)
    o_s = _nbr_attn(q, k, v, kc, vc, _nbr_bias_table(na_rpb[0]))
    y = _proj_residual(y, o_p, o_s, mod[1], na_w_o[0].astype(BF), name="attn_out")
    y = moe(y, 1, True)

    y_prompt = y[:NPR].reshape(NB_P, T_P, D)
    y_sample = y[NPR:].reshape(NB_S, T_S, D)
    new_k = k[:NPR].reshape(NB_P, 1, T_P, NH, HD)
    new_v = v[:NPR].reshape(NB_P, 1, T_P, NH, HD)
    return (y_prompt, y_sample, new_state_rwkv, new_k, new_v)
```

```python
import functools
import math

import jax
import jax.numpy as jnp
from jax import lax
from jax.experimental import pallas as pl
from jax.experimental.pallas import tpu as pltpu

F32 = jnp.float32
BF = jnp.bfloat16

D = 1024
NH = 16
HD = 64
NB_P, T_P = 32, 256
NB_S, T_S = 2, 1024
NPR = NB_P * T_P
NSR = NB_S * T_S
NTOK = NPR + NSR
TM = 256
NT = NTOK // TM
NT_P = NPR // TM
TPS = T_S // TM
TMX = 1024
NMOD = 6
NE = 16
EPG = 4
DE = 256
CH = 64
LANES = 128
GRID_W = 64
WIN_R, WIN_C = 8, 16
NORM_EPS = 1e-6
GN_EPS = 64e-5
NEG = -1e30
VMEM_LIMIT = 56 * 1024 * 1024


def _cp(sem):
    return pltpu.CompilerParams(dimension_semantics=sem, vmem_limit_bytes=VMEM_LIMIT)


def _bdot(a, b):
    return jnp.dot(a.astype(BF), b.astype(BF), preferred_element_type=F32)


def _bdot_nt(a, b):
    return lax.dot_general(a.astype(BF), b.astype(BF), (((1,), (1,)), ((), ())),
                           preferred_element_type=F32)


def _split2(x):
    hi = x.astype(BF)
    lo = (x - hi.astype(F32)).astype(BF)
    return hi, lo


def _split3(x):
    hi = x.astype(BF)
    r1 = x - hi.astype(F32)
    mid = r1.astype(BF)
    lo = (r1 - mid.astype(F32)).astype(BF)
    return hi, mid, lo


def _seg_sum(x, bo):
    outs = []
    for c in range(x.shape[1] // 256):
        hi, lo = _split2(x[:, c * 256:(c + 1) * 256])
        outs.append(jnp.dot(hi, bo, preferred_element_type=F32)
                    + jnp.dot(lo, bo, preferred_element_type=F32))
    return outs[0] if len(outs) == 1 else jnp.concatenate(outs, axis=1)


def _tile_mod_row(i):
    return jnp.where(i < NT_P, 0, 1 + (i - NT_P) // TPS)


def _modulate(x, g, sh, sc):
    ms = jnp.mean(x * x, axis=-1, keepdims=True)
    return x * lax.rsqrt(ms + NORM_EPS) * g * (1.0 + sc) + sh


def _mod_vec(mod_ref, row, k):
    return mod_ref[pl.ds(row, 1), pl.ds(k * D, D)]


def _adaln_kernel(c_ref, w_ref, b_ref, o_ref):
    c = c_ref[...]
    s = c * jax.nn.sigmoid(c)
    o_ref[...] = _bdot(s, w_ref[...]) + b_ref[...]


def _adaln(c8, ada_w, ada_b):
    nl = ada_w.shape[0]
    tn = 1536
    return pl.pallas_call(
        _adaln_kernel,
        grid=(nl, NMOD * D // tn),
        in_specs=[pl.BlockSpec((8, D), lambda l, j: (0, 0)),
                  pl.BlockSpec((None, D, tn), lambda l, j: (l, 0, j)),
                  pl.BlockSpec((None, 1, tn), lambda l, j: (l, 0, j))],
        out_specs=pl.BlockSpec((None, 8, tn), lambda l, j: (l, 0, j)),
        out_shape=jax.ShapeDtypeStruct((nl, 8, NMOD * D), F32),
        compiler_params=_cp(("parallel", "parallel")),
        name="adaln",
    )(c8, ada_w, ada_b.reshape(nl, 1, NMOD * D))


def _front_kernel(y_ref, yp_ref, yn_ref, mod_ref, nrm_ref, mu_ref, wr_ref, wk_ref, wv_ref,
                  w1_ref, a1_ref, g1_ref, w2_ref, a2_ref, g2_ref, w0_ref, a0_ref, kkw_ref, bo_ref,
                  r_o, k_o, v_o, g_o, kk_o, lw0_o, lw1_o, ag0_o, ag1_o):
    i = pl.program_id(0)
    row = _tile_mod_row(i)
    q = (i - NT_P) % TPS
    first = jnp.logical_or(i < NT_P, q == 0)
    last = jnp.logical_or(i < NT_P, q == TPS - 1)
    sh = _mod_vec(mod_ref, row, 0)
    sc = _mod_vec(mod_ref, row, 1)
    g = nrm_ref[...]
    h = _modulate(y_ref[...], g, sh, sc)
    hp = _modulate(yp_ref[...], g, sh, sc)[7:8]
    hn = _modulate(yn_ref[...], g, sh, sc)[0:1]
    hp = jnp.where(first, 0.0, hp)
    hn = jnp.where(last, 0.0, hn)
    rid = lax.broadcasted_iota(jnp.int32, (TM, D), 0)
    prev = jnp.where(rid == 0, hp, pltpu.roll(h, 1, 0))
    nxt = jnp.where(rid == TM - 1, hn, pltpu.roll(h, TM - 1, 0))
    dx = 0.5 * (prev + nxt) - h

    def mix(n):
        return (h + dx * mu_ref[n:n + 1, :]).astype(BF)

    r_o[...] = jnp.dot(mix(0), wr_ref[...], preferred_element_type=F32)
    xw = mix(1)
    lora_w = _bdot(jnp.tanh(jnp.dot(xw, w1_ref[...], preferred_element_type=F32)), w2_ref[...])
    c05 = math.exp(-0.5)
    lw0_o[...] = -c05 * jax.nn.sigmoid(w0_ref[0:1, :] + lora_w[:, :D])
    lw1_o[...] = -c05 * jax.nn.sigmoid(w0_ref[1:2, :] + lora_w[:, D:])
    k_raw = jnp.dot(mix(2), wk_ref[...], preferred_element_type=F32)
    k_o[...] = k_raw
    kq = k_raw * kkw_ref[...]
    ss = _seg_sum(kq * kq, bo_ref[...])
    kk_o[...] = kq / jnp.maximum(jnp.sqrt(ss), 1e-12)
    v_o[...] = jnp.dot(mix(3), wv_ref[...], preferred_element_type=F32)
    xa = mix(4)
    lora_a = _bdot(jnp.dot(xa, a1_ref[...], preferred_element_type=F32), a2_ref[...])
    ag0_o[...] = jax.nn.sigmoid(a0_ref[0:1, :] + lora_a[:, :D])
    ag1_o[...] = jax.nn.sigmoid(a0_ref[1:2, :] + lora_a[:, D:])
    xg = mix(5)
    g_o[...] = _bdot(jax.nn.sigmoid(jnp.dot(xg, g1_ref[...], preferred_element_type=F32)), g2_ref[...])


def _rwkv_front(y, mod, nrm, mu, wr, wk, wv, w1c, a1c, g1, w2bd, a2bd, g2, w0, a0, kkw, bo):
    tile = pl.BlockSpec((TM, D), lambda i: (i, 0))
    nblk8 = NTOK // 8
    full = lambda a: pl.BlockSpec(a.shape, lambda i: (0,) * a.ndim)
    ins = [y, y, y, mod, nrm, mu, wr, wk, wv, w1c, a1c, g1, w2bd, a2bd, g2, w0, a0, kkw, bo]
    in_specs = [tile,
                pl.BlockSpec((8, D), lambda i: (jnp.maximum(i * (TM // 8) - 1, 0), 0)),
                pl.BlockSpec((8, D), lambda i: (jnp.minimum((i + 1) * (TM // 8), nblk8 - 1), 0))]
    in_specs += [full(a) for a in ins[3:]]
    out = jax.ShapeDtypeStruct((NTOK, D), F32)
    return pl.pallas_call(
        _front_kernel,
        grid=(NT,),
        in_specs=in_specs,
        out_specs=[tile] * 9,
        out_shape=[out] * 9,
        compiler_params=_cp(("parallel",)),
        name="rwkv_front",
    )(*ins)


def _wkv_chunk(d, S, r, k, v, kk, lw, ag, ka, rk, cst):
    tri4, strict, incl, lo_half, hi_half, h0, eye, bd, bo2 = cst
    dot = lambda a, b: jnp.dot(a, b, preferred_element_type=F32)
    b = kk * ag
    kd = k * (1.0 + (ag - 1.0) * ka)
    hi, mid, lo = _split3(lw)
    cs = dot(tri4[d], jnp.concatenate([hi, mid, lo, jnp.zeros_like(lo)], axis=0))
    qh, ql = _split2(r * kd * rk)
    bonus = (dot(qh, bo2) + dot(ql, bo2)) * v
    yield
    tot = cs[CH - 1:CH, :] if d == 0 else cs[0:1, :]
    p_in = jnp.exp(cs)
    p_inv = jnp.exp(-cs)
    p_end = jnp.exp(tot - cs)
    at = -kk * jnp.exp(cs - lw)
    rt = r * p_in
    zero = jnp.zeros_like(at)
    lhs4 = jnp.concatenate([jnp.where(h0, at, zero), jnp.where(h0, zero, at),
                            jnp.where(h0, rt, zero), jnp.where(h0, zero, rt)], axis=0)
    gram = _bdot_nt(lhs4, jnp.concatenate([b * p_inv, kd * p_inv], axis=0))
    w0y0 = _bdot_nt(jnp.concatenate([at, rt], axis=0), S)
    yield
    w0, y0 = w0y0[:CH], w0y0[CH:]
    vb = v.astype(BF)
    vv = jnp.concatenate([vb, vb], axis=0)
    zpad = jnp.zeros((CH, LANES), BF)
    pad = lambda m: jnp.concatenate([m, zpad], axis=0)
    a_rows = [jnp.where(strict[d], gram[h * CH:(h + 1) * CH], 0.0) for h in range(2)]
    ws = [w0 + dot(jnp.where(hi_half, a_rows[h], 0.0).astype(BF), vv) for h in range(2)]
    lps = [jnp.where(lo_half, a_rows[h], 0.0) for h in range(2)]
    xs = [eye + lps[h] for h in range(2)]
    lpb = [lps[h].astype(BF) for h in range(2)]
    lps = [dot(lpb[h], pad(lpb[h])) for h in range(2)]
    yield
    nsq = int(math.log2(CH)) - 1
    for it in range(nsq):
        lpb = [lps[h].astype(BF) for h in range(2)]
        xs = [xs[h] + dot(xs[h].astype(BF), pad(lpb[h])) for h in range(2)]
        if it + 1 < nsq:
            lps = [dot(lpb[h], pad(lpb[h])) for h in range(2)]
        yield
    wm = jnp.where(h0, ws[0], ws[1]).astype(BF)
    wm2 = jnp.concatenate([wm, wm], axis=0)
    us = [dot(xs[h].astype(BF), wm2) for h in range(2)]
    yield
    u = jnp.where(h0, us[0], us[1])
    uv = jnp.concatenate([u.astype(BF), vb], axis=0)
    ys = [dot(jnp.where(incl[d], gram[(2 + h) * CH:(3 + h) * CH], 0.0).astype(BF), uv) for h in range(2)]
    uvt = jnp.concatenate([u, v], axis=0).T
    ds = _bdot(uvt, jnp.concatenate([b * p_end, kd * p_end], axis=0))
    yield
    y = y0 + jnp.where(h0, ys[0], ys[1])
    s_new = S * jnp.exp(tot) + jnp.where(bd, ds, 0.0)
    return s_new, y, bonus


def _run_lockstep(gens):
    results = [None] * len(gens)
    pending = list(range(len(gens)))
    while pending:
        for i in list(pending):
            try:
                next(gens[i])
            except StopIteration as stop:
                results[i] = stop.value
                pending.remove(i)
    return results


def _wkv_consts():
    t = lax.broadcasted_iota(jnp.int32, (CH, 4 * CH), 0)
    s = lax.broadcasted_iota(jnp.int32, (CH, 4 * CH), 1) & (CH - 1)
    tri4 = [jnp.where(s <= t, 1.0, 0.0).astype(BF), jnp.where(s >= t, 1.0, 0.0).astype(BF)]
    t2 = lax.broadcasted_iota(jnp.int32, (CH, LANES), 0)
    l2 = lax.broadcasted_iota(jnp.int32, (CH, LANES), 1)
    s2 = l2 & (CH - 1)
    strict = [s2 < t2, s2 > t2]
    incl = [s2 <= t2, s2 >= t2]
    lo_half = l2 < CH
    hi_half = l2 >= CH
    h0 = l2 < HD
    eye = jnp.where(l2 == t2, 1.0, 0.0)
    ri = lax.broadcasted_iota(jnp.int32, (LANES, LANES), 0)
    ci = lax.broadcasted_iota(jnp.int32, (LANES, LANES), 1)
    bd = (ri // HD) == (ci // HD)
    bo2 = jnp.where(bd, 1.0, 0.0).astype(BF)
    return tri4, strict, incl, lo_half, hi_half, h0, eye, bd, bo2


def _wkv_kernel(T, has_s0, *refs):
    (r_ref, k_ref, v_ref, kk_ref, lw0_ref, lw1_ref, ag0_ref, ag1_ref, ka_ref, rk_ref), rest = refs[:10], refs[10:]
    if has_s0:
        s0_ref, y_ref, bon_ref, s_scr = rest
    else:
        y_ref, bon_ref, st_ref, s_scr = rest
    nch = T // CH
    npair = y_ref.shape[1] // LANES
    cst = _wkv_consts()
    if has_s0:
        s_scr[...] = s0_ref[...]
    else:
        s_scr[...] = jnp.zeros_like(s_scr)
    y_ref[...] = jnp.zeros_like(y_ref)
    bon_ref[...] = jnp.zeros_like(bon_ref)
    lw_refs = (lw0_ref, lw1_ref)
    ag_refs = (ag0_ref, ag1_ref)

    def body(c, carry):
        chains = [(p, d) for p in range(npair) for d in range(2)]
        sl = {}
        for p, d in chains:
            r0 = pl.multiple_of((c if d == 0 else nch - 1 - c) * CH, CH)
            sl[p, d] = (pl.ds(r0, CH), pl.ds(p * LANES, LANES))
        args = {}
        for p, d in chains:
            rs, ls = sl[p, d]
            args[p, d] = (s_scr[d, p], r_ref[rs, ls], k_ref[rs, ls], v_ref[rs, ls], kk_ref[rs, ls],
                          lw_refs[d][rs, ls], ag_refs[d][rs, ls], ka_ref[:, ls], rk_ref[:, ls],
                          y_ref[rs, ls], bon_ref[rs, ls])
        outs = _run_lockstep([_wkv_chunk(d, *args[p, d][:9], cst) for p, d in chains])
        for (p, d), (s_new, y, bon) in zip(chains, outs):
            rs, ls = sl[p, d]
            s_scr[d, p] = s_new
            y_ref[rs, ls] = args[p, d][9] + y
            bon_ref[rs, ls] = args[p, d][10] + bon
        return carry

    lax.fori_loop(0, nch, body, 0)
    if not has_s0:
        st_ref[...] = s_scr[...]


def _wkv(arrs, ka, rk, T, nb, row_blk0, s0_bd=None, lanes=256):
    ng = D // lanes
    npair = lanes // LANES
    seq = pl.BlockSpec((T, lanes), lambda b, g: (row_blk0 + b, g))
    vec = pl.BlockSpec((1, lanes), lambda b, g: (0, g))
    st = pl.BlockSpec((None, 2, npair, LANES, LANES), lambda b, g: (b, 0, g, 0, 0))
    out_seq = pl.BlockSpec((T, lanes), lambda b, g: (b, g))
    in_specs = [seq] * 8 + [vec, vec]
    ins = list(arrs) + [ka, rk]
    out_specs = [out_seq, out_seq]
    out_shape = [jax.ShapeDtypeStruct((nb * T, D), F32)] * 2
    if s0_bd is not None:
        in_specs.append(st)
        ins.append(s0_bd)
    else:
        out_specs.append(st)
        out_shape.append(jax.ShapeDtypeStruct((nb, 2, D // LANES, LANES, LANES), F32))
    return pl.pallas_call(
        functools.partial(_wkv_kernel, T, s0_bd is not None),
        grid=(nb, ng),
        in_specs=in_specs,
        out_specs=out_specs,
        out_shape=out_shape,
        scratch_shapes=[pltpu.VMEM((2, npair, LANES, LANES), F32)],
        compiler_params=_cp(("parallel", "parallel")),
        name="wkv_T%d" % T,
    )(*ins)


def _proj_kernel(gn, y_ref, ap_ref, as_ref, *rest):
    if gn:
        bp_ref, bs_ref, g_ref, lng_ref, lnb_ref, bo_ref, mod_ref, w_ref, o_ref = rest
    else:
        mod_ref, w_ref, o_ref = rest
    i = pl.program_id(0)
    is_p = i < NT_P
    a = jnp.where(is_p, ap_ref[...], as_ref[...])
    if gn:
        bo = bo_ref[...]
        mean = _seg_sum(a, bo) * (1.0 / HD)
        cen = a - mean
        var = _seg_sum(cen * cen, bo) * (1.0 / HD)
        yn = cen * lax.rsqrt(var + GN_EPS) * lng_ref[...] + lnb_ref[...]
        a = (yn + jnp.where(is_p, bp_ref[...], bs_ref[...])) * g_ref[...]
    gt = _mod_vec(mod_ref, _tile_mod_row(i), 2)
    o_ref[...] = y_ref[...] + gt * _bdot(a, w_ref[...])


def _proj_residual(y, a_p, a_s, mod, w, gn_args=None, name="proj"):
    tile = pl.BlockSpec((TM, D), lambda i: (i, 0))
    tile_p = pl.BlockSpec((TM, D), lambda i: (jnp.minimum(i, NT_P - 1), 0))
    tile_s = pl.BlockSpec((TM, D), lambda i: (jnp.maximum(i - NT_P, 0), 0))
    full = lambda a: pl.BlockSpec(a.shape, lambda i: (0,) * a.ndim)
    ins = [y, a_p, a_s]
    in_specs = [tile, tile_p, tile_s]
    if gn_args is not None:
        b_p, b_s, g, lng, lnb, bo = gn_args
        ins += [b_p, b_s, g, lng, lnb, bo]
        in_specs += [tile_p, tile_s, tile, full(lng), full(lnb), full(bo)]
    ins += [mod, w]
    in_specs += [full(mod), full(w)]
    return pl.pallas_call(
        functools.partial(_proj_kernel, gn_args is not None),
        grid=(NT,),
        in_specs=in_specs,
        out_specs=tile,
        out_shape=jax.ShapeDtypeStruct((NTOK, D), F32),
        compiler_params=_cp(("parallel",)),
        name=name,
    )(*ins)


def _router_kernel(y_ref, mod_ref, nrm_ref, wr_ref, br_ref, xt_o, gate_o):
    i = pl.program_id(0)
    row = _tile_mod_row(i)
    xt = _modulate(y_ref[...], nrm_ref[...], _mod_vec(mod_ref, row, 3), _mod_vec(mod_ref, row, 4))
    xt_o[...] = xt.astype(BF)
    x1, x2, x3 = _split3(xt)
    w1, w2, w3 = _split3(wr_ref[...])
    dot = lambda a, b: jnp.dot(a, b, preferred_element_type=F32)
    logits = (dot(x3, w1) + dot(x2, w2) + dot(x1, w3)) + (dot(x2, w1) + dot(x1, w2)) + dot(x1, w1)
    logits = logits + br_ref[...]
    lane = lax.broadcasted_iota(jnp.int32, logits.shape, 1).astype(F32)
    isg = jnp.logical_and(lane >= NE, lane < NE + NE // EPG)
    mg = jnp.max(jnp.where(isg, logits, NEG), axis=-1, keepdims=True)
    eg = jnp.where(isg, jnp.exp(jnp.minimum(logits - mg, 0.0)), 0.0)
    p_sel = 1.0 / jnp.sum(eg, axis=-1, keepdims=True)
    gidx = jnp.min(jnp.where(jnp.logical_and(isg, logits == mg), lane, 1e3), axis=-1, keepdims=True) - NE
    lo_l = gidx * EPG
    ing = jnp.logical_and(lane >= lo_l, lane < lo_l + EPG)
    me = jnp.max(jnp.where(ing, logits, NEG), axis=-1, keepdims=True)
    ee = jnp.where(ing, jnp.exp(jnp.minimum(logits - me, 0.0)), 0.0)
    se = jnp.sum(ee, axis=-1, keepdims=True)
    pe = ee / se
    i1 = jnp.min(jnp.where(jnp.logical_and(ing, logits == me), lane, 1e3), axis=-1, keepdims=True)
    v1 = 1.0 / se
    rest = jnp.logical_and(ing, lane != i1)
    v2 = jnp.max(jnp.where(rest, pe, -1.0), axis=-1, keepdims=True)
    i2 = jnp.min(jnp.where(jnp.logical_and(rest, pe == v2), lane, 1e3), axis=-1, keepdims=True)
    den = v1 + v2
    gate_o[...] = (jnp.where(lane == i1, p_sel * v1 / den, 0.0)
                   + jnp.where(lane == i2, p_sel * v2 / den, 0.0))


def _moe_router(y, mod, nrm, wr, br):
    tile = pl.BlockSpec((TM, D), lambda i: (i, 0))
    full = lambda a: pl.BlockSpec(a.shape, lambda i: (0,) * a.ndim)
    return pl.pallas_call(
        _router_kernel,
        grid=(NT,),
        in_specs=[tile, full(mod), full(nrm), full(wr), full(br)],
        out_specs=[tile, pl.BlockSpec((TM, LANES), lambda i: (i, 0))],
        out_shape=[jax.ShapeDtypeStruct((NTOK, D), BF), jax.ShapeDtypeStruct((NTOK, LANES), F32)],
        compiler_params=_cp(("parallel",)),
        name="moe_router",
    )(y, mod, nrm, wr, br)


def _experts_kernel(final, xt_ref, gate_ref, w13_ref, w2_ref, y_ref, mod_ref, fin_ref, o_ref):
    i = pl.program_id(0)
    e = pl.program_id(1)

    @pl.when(e == 0)
    def _():
        o_ref[...] = jnp.zeros_like(o_ref)

    h13 = jnp.dot(xt_ref[...], w13_ref[...], preferred_element_type=F32)
    h1, h3 = h13[:, :DE], h13[:, DE:]
    lane = lax.broadcasted_iota(jnp.int32, gate_ref.shape, 1)
    ge = jnp.sum(jnp.where(lane == e, gate_ref[...], 0.0), axis=-1, keepdims=True)
    hid = (h1 * jax.nn.sigmoid(h1)) * h3 * ge
    o_ref[...] += jnp.dot(hid.astype(BF), w2_ref[...], preferred_element_type=F32)

    @pl.when(e == NE - 1)
    def _():
        row = jnp.where(i < NPR // TMX, 0, i - NPR // TMX + 1)
        out = y_ref[...] + _mod_vec(mod_ref, row, 5) * o_ref[...]
        if final:
            ms = jnp.mean(out * out, axis=-1, keepdims=True)
            out = out * lax.rsqrt(ms + NORM_EPS) * fin_ref[...]
        o_ref[...] = out


def _moe_experts(xt, gate, w13, w2, y, mod, fin, final):
    tile = pl.BlockSpec((TMX, D), lambda i, e: (i, 0))
    full = lambda a: pl.BlockSpec(a.shape, lambda i, e: (0,) * a.ndim)
    return pl.pallas_call(
        functools.partial(_experts_kernel, final),
        grid=(NTOK // TMX, NE),
        in_specs=[tile, pl.BlockSpec((TMX, LANES), lambda i, e: (i, 0)),
                  pl.BlockSpec((None, D, 2 * DE), lambda i, e: (e, 0, 0)),
                  pl.BlockSpec((None, DE, D), lambda i, e: (e, 0, 0)),
                  tile, full(mod), full(fin)],
        out_specs=tile,
        out_shape=jax.ShapeDtypeStruct((NTOK, D), F32),
        compiler_params=_cp(("parallel", "arbitrary")),
        name="moe_experts",
    )(xt, gate, w13, w2, y, mod, fin)


def _qkv_kernel(y_ref, mod_ref, nrm_ref, w_ref, q_o, k_o, v_o):
    i = pl.program_id(0)
    row = _tile_mod_row(i)
    h = _modulate(y_ref[...], nrm_ref[...], _mod_vec(mod_ref, row, 0), _mod_vec(mod_ref, row, 1))
    qkv = _bdot(h, w_ref[...])
    q_o[...] = qkv[:, :D]
    k_o[...] = qkv[:, D:2 * D]
    v_o[...] = qkv[:, 2 * D:]


def _na_qkv(y, mod, nrm, w):
    tile = pl.BlockSpec((TM, D), lambda i: (i, 0))
    full = lambda a: pl.BlockSpec(a.shape, lambda i: (0,) * a.ndim)
    out = jax.ShapeDtypeStruct((NTOK, D), F32)
    return pl.pallas_call(
        _qkv_kernel,
        grid=(NT,),
        in_specs=[tile, full(mod), full(nrm), full(w)],
        out_specs=[tile] * 3,
        out_shape=[out] * 3,
        compiler_params=_cp(("parallel",)),
        name="na_qkv",
    )(y, mod, nrm, w)


def _softmax_rows(s):
    m = jnp.max(s, axis=-1, keepdims=True)
    e = jnp.exp(s - m)
    return e / jnp.sum(e, axis=-1, keepdims=True)


def _ctx_attn_kernel(q_ref, k_ref, v_ref, o_ref):
    scale = HD ** -0.5
    h0 = lax.broadcasted_iota(jnp.int32, (T_P, LANES), 1) < HD
    for p in range(D // LANES):
        ls = pl.ds(p * LANES, LANES)
        q = q_ref[:, ls]
        kb = k_ref[:, ls].astype(BF)
        vb = v_ref[:, ls].astype(BF)
        outs = []
        for h in range(2):
            qm = jnp.where(h0 if h == 0 else jnp.logical_not(h0), q, 0.0)
            s = _bdot_nt(qm, kb) * scale
            outs.append(jnp.dot(_softmax_rows(s).astype(BF), vb, preferred_element_type=F32))
        o_ref[:, ls] = jnp.where(h0, outs[0], outs[1])


def _ctx_attn(q, k, v):
    blk = pl.BlockSpec((T_P, D), lambda b: (b, 0))
    return pl.pallas_call(
        _ctx_attn_kernel,
        grid=(NB_P,),
        in_specs=[blk] * 3,
        out_specs=blk,
        out_shape=jax.ShapeDtypeStruct((NPR, D), F32),
        compiler_params=_cp(("parallel",)),
        name="ctx_attn",
    )(q, k, v)


def _nbr_attn_kernel(q_ref, k_ref, v_ref, kc_ref, vc_ref, tz_ref, o_ref):
    scale = HD ** -0.5
    rows = T_S // GRID_W
    nloc = WIN_R * GRID_W
    h0 = lax.broadcasted_iota(jnp.int32, (GRID_W, LANES), 1) < HD
    kcb = kc_ref[...].astype(BF)
    vcb = vc_ref[...].astype(BF)

    def body(r, carry):
        start = jnp.clip(r - WIN_R // 2, 0, rows - WIN_R)
        qs = pl.ds(pl.multiple_of(r * GRID_W, GRID_W), GRID_W)
        ks = pl.ds(pl.multiple_of(start * GRID_W, GRID_W), nloc)
        q = q_ref[qs, :]
        klb = k_ref[ks, :].astype(BF)
        vlb = v_ref[ks, :].astype(BF)
        j0 = start - r + WIN_R - 1
        outs = []
        for h in range(2):
            qm = jnp.where(h0 if h == 0 else jnp.logical_not(h0), q, 0.0)
            bias = jnp.concatenate([tz_ref[h, j0 + 2 * m] for m in range(WIN_R // 2)], axis=1)
            sl = _bdot_nt(qm, klb) * scale + bias
            sc = _bdot_nt(qm, kcb) * scale
            m = jnp.maximum(jnp.max(sl, axis=-1, keepdims=True), jnp.max(sc, axis=-1, keepdims=True))
            el = jnp.exp(sl - m)
            ec = jnp.exp(sc - m)
            den = jnp.sum(el, axis=-1, keepdims=True) + jnp.sum(ec, axis=-1, keepdims=True)
            outs.append(jnp.dot((el / den).astype(BF), vlb, preferred_element_type=F32)
                        + jnp.dot((ec / den).astype(BF), vcb, preferred_element_type=F32))
        o_ref[qs, :] = jnp.where(h0, outs[0], outs[1])
        return carry

    lax.fori_loop(0, rows, body, 0)


def _nbr_attn(q, k, v, kc, vc, tz2):
    npair = D // LANES
    seq = pl.BlockSpec((T_S, LANES), lambda b, p: (NPR // T_S + b, p))
    ctx = pl.BlockSpec((kc.shape[0] // NB_S, LANES), lambda b, p: (b, p))
    return pl.pallas_call(
        _nbr_attn_kernel,
        grid=(NB_S, npair),
        in_specs=[seq, seq, seq, ctx, ctx,
                  pl.BlockSpec((2,) + tz2.shape[1:], lambda b, p: (p, 0, 0, 0))],
        out_specs=pl.BlockSpec((T_S, LANES), lambda b, p: (b, p)),
        out_shape=jax.ShapeDtypeStruct((NSR, D), F32),
        compiler_params=_cp(("parallel", "parallel")),
        name="nbr_attn",
    )(q, k, v, kc, vc, tz2)


def _block_diag2(a, b):
    z = jnp.zeros_like(a)
    return jnp.concatenate([jnp.concatenate([a, z], axis=1), jnp.concatenate([z, b], axis=1)], axis=0)


def _nbr_bias_table(rpb):
    qc = jnp.arange(GRID_W)[:, None]
    kc = jnp.arange(GRID_W)[None, :]
    cstart = jnp.clip(qc - WIN_C // 2, 0, GRID_W - WIN_C)
    valid = jnp.logical_and(kc >= cstart, kc < cstart + WIN_C)
    dc = jnp.clip(kc - qc, -(WIN_C - 1), WIN_C - 1) + WIN_C - 1
    tz = jnp.where(valid, rpb[:, :, dc], NEG)
    return jnp.concatenate([tz[:, :-1], tz[:, 1:]], axis=-1)


def kernel(x_prompt, x_sample, c, state_rwkv, cache_na_k, cache_na_v, c_ctx, norm_mix, norm_ffn, ada_w, ada_b, rwkv_mu, rwkv_w_r, rwkv_w_k, rwkv_w_v, rwkv_w_o, rwkv_w0, rwkv_w1, rwkv_w2, rwkv_a0, rwkv_a1, rwkv_a2, rwkv_g1, rwkv_g2, rwkv_k_k, rwkv_k_a, rwkv_r_k, rwkv_lnx_g, rwkv_lnx_b, na_w_qkv, na_w_o, na_rpb, moe_w_grp, moe_b_grp, moe_w_exp, moe_b_exp, moe_w1, moe_w3, moe_w2, final_norm):
    y = jnp.concatenate([x_prompt.reshape(NPR, D), x_sample.reshape(NSR, D)], axis=0)
    c8 = jnp.concatenate([c_ctx[None, :], c, jnp.zeros((8 - 1 - NB_S, D), F32)], axis=0)
    mod = _adaln(c8, ada_w, ada_b)
    ri = jnp.arange(2 * LANES)[:, None] // HD
    bo = (ri == ri.T).astype(BF)
    row = lambda a: a.reshape(1, D)

    w1c = jnp.concatenate([rwkv_w1[0, 0], rwkv_w1[0, 1]], axis=1).astype(BF)
    a1c = jnp.concatenate([rwkv_a1[0, 0], rwkv_a1[0, 1]], axis=1).astype(BF)
    w2bd = _block_diag2(rwkv_w2[0, 0], rwkv_w2[0, 1]).astype(BF)
    a2bd = _block_diag2(rwkv_a2[0, 0], rwkv_a2[0, 1]).astype(BF)
    r, k, v, g, kk, lw0, lw1, ag0, ag1 = _rwkv_front(
        y, mod[0], row(norm_mix[0]), rwkv_mu[0], rwkv_w_r[0].astype(BF), rwkv_w_k[0].astype(BF),
        rwkv_w_v[0].astype(BF), w1c, a1c, rwkv_g1[0].astype(BF), w2bd, a2bd, rwkv_g2[0].astype(BF),
        rwkv_w0[0], rwkv_a0[0], row(rwkv_k_k[0]), bo)
    arrs = (r, k, v, kk, lw0, lw1, ag0, ag1)
    ka, rk = row(rwkv_k_a[0]), row(rwkv_r_k[0])
    ys_p, bon_p, st_bd = _wkv(arrs, ka, rk, T_P, NB_P, 0)
    s0 = state_rwkv[:, 0].reshape(NB_S, 2, NH // 2, 2, HD, HD)
    z = jnp.zeros_like(s0[:, :, :, 0])
    s0_bd = jnp.concatenate([jnp.concatenate([s0[:, :, :, 0], z], axis=-1),
                             jnp.concatenate([z, s0[:, :, :, 1]], axis=-1)], axis=-2)
    ys_s, bon_s = _wkv(arrs, ka, rk, T_S, NB_S, NPR // T_S, s0_bd=s0_bd)
    new_state = jnp.stack([st_bd[..., :HD, :HD], st_bd[..., HD:, HD:]], axis=3)
    new_state_rwkv = new_state.reshape(NB_P, 1, 2, NH, HD, HD)
    y = _proj_residual(y, ys_p, ys_s, mod[0], rwkv_w_o[0].astype(BF),
                       gn_args=(bon_p, bon_s, g, row(rwkv_lnx_g[0]), row(rwkv_lnx_b[0]), bo),
                       name="rwkv_out")

    def moe(y, i, final):
        wr = jnp.concatenate([moe_w_exp[i], moe_w_grp[i],
                              jnp.zeros((D, LANES - NE - NE // EPG), F32)], axis=1)
        br = jnp.concatenate([moe_b_exp[i], moe_b_grp[i],
                              jnp.zeros((LANES - NE - NE // EPG,), F32)]).reshape(1, LANES)
        xt, gate = _moe_router(y, mod[i], row(norm_ffn[i]), wr, br)
        w13 = jnp.concatenate([moe_w1[i], moe_w3[i]], axis=-1).astype(BF)
        return _moe_experts(xt, gate, w13, moe_w2[i].astype(BF), y, mod[i], row(final_norm), final)

    y = moe(y, 0, False)

    q, k, v = _na_qkv(y, mod[1], row(norm_mix[1]), na_w_qkv[0].astype(BF))
    o_p = _ctx_attn(q, k, v)
    kc = cache_na_k[:, 0].reshape(NB_S * cache_na_k.shape[2], D)
    vc = cache_na_v[:, 0].reshape(NB_S * cache_na_v.shape[2], D)
    o_s = _nbr_attn(q, k, v, kc, vc, _nbr_bias_table(na_rpb[0]))
    y = _proj_residual(y, o_p, o_s, mod[1], na_w_o[0].astype(BF), name="attn_out")
    y = moe(y, 1, True)

    y_prompt = y[:NPR].reshape(NB_P, T_P, D)
    y_sample = y[NPR:].reshape(NB_S, T_S, D)
    new_k = k[:NPR].reshape(NB_P, 1, T_P, NH, HD)
    new_v = v[:NPR].reshape(NB_P, 1, T_P, NH, HD)
    return (y_prompt, y_sample, new_state_rwkv, new_k, new_v)
```

```python
import functools
import math

import jax
import jax.numpy as jnp
from jax import lax
from jax.experimental import pallas as pl
from jax.experimental.pallas import tpu as pltpu

F32 = jnp.float32
BF = jnp.bfloat16

D = 1024
NH = 16
HD = 64
NB_P, T_P = 32, 256
NB_S, T_S = 2, 1024
NPR = NB_P * T_P
NSR = NB_S * T_S
NTOK = NPR + NSR
TM = 256
NT = NTOK // TM
NT_P = NPR // TM
TPS = T_S // TM
TMX = 1024
NMOD = 6
NE = 16
EPG = 4
DE = 256
CH = 64
SB = 16
LANES = 128
GRID_W = 64
WIN_R, WIN_C = 8, 16
NORM_EPS = 1e-6
GN_EPS = 64e-5
NEG = -1e30
VMEM_LIMIT = 56 * 1024 * 1024


def _cp(sem):
    return pltpu.CompilerParams(dimension_semantics=sem, vmem_limit_bytes=VMEM_LIMIT)


def _bdot(a, b):
    return jnp.dot(a.astype(BF), b.astype(BF), preferred_element_type=F32)


def _bdot_nt(a, b):
    return lax.dot_general(a.astype(BF), b.astype(BF), (((1,), (1,)), ((), ())),
                           preferred_element_type=F32)


def _split2(x):
    hi = x.astype(BF)
    lo = (x - hi.astype(F32)).astype(BF)
    return hi, lo


def _split3(x):
    hi = x.astype(BF)
    r1 = x - hi.astype(F32)
    mid = r1.astype(BF)
    lo = (r1 - mid.astype(F32)).astype(BF)
    return hi, mid, lo


def _seg_sum(x, bo):
    outs = []
    for c in range(x.shape[1] // 256):
        hi, lo = _split2(x[:, c * 256:(c + 1) * 256])
        outs.append(jnp.dot(hi, bo, preferred_element_type=F32)
                    + jnp.dot(lo, bo, preferred_element_type=F32))
    return outs[0] if len(outs) == 1 else jnp.concatenate(outs, axis=1)


def _tile_mod_row(i):
    return jnp.where(i < NT_P, 0, 1 + (i - NT_P) // TPS)


def _modulate(x, g, sh, sc):
    ms = jnp.mean(x * x, axis=-1, keepdims=True)
    return x * lax.rsqrt(ms + NORM_EPS) * g * (1.0 + sc) + sh


def _mod_vec(mod_ref, row, k):
    return mod_ref[pl.ds(row, 1), pl.ds(k * D, D)]


def _adaln_kernel(c_ref, w_ref, b_ref, o_ref):
    c = c_ref[...]
    s = c * jax.nn.sigmoid(c)
    o_ref[...] = _bdot(s, w_ref[...]) + b_ref[...]


def _adaln(c8, ada_w, ada_b):
    nl = ada_w.shape[0]
    tn = 1536
    return pl.pallas_call(
        _adaln_kernel,
        grid=(nl, NMOD * D // tn),
        in_specs=[pl.BlockSpec((8, D), lambda l, j: (0, 0)),
                  pl.BlockSpec((None, D, tn), lambda l, j: (l, 0, j)),
                  pl.BlockSpec((None, 1, tn), lambda l, j: (l, 0, j))],
        out_specs=pl.BlockSpec((None, 8, tn), lambda l, j: (l, 0, j)),
        out_shape=jax.ShapeDtypeStruct((nl, 8, NMOD * D), F32),
        compiler_params=_cp(("parallel", "parallel")),
        name="adaln",
    )(c8, ada_w, ada_b.reshape(nl, 1, NMOD * D))


def _front_kernel(y_ref, yp_ref, yn_ref, mod_ref, nrm_ref, mu_ref, wr_ref, wk_ref, wv_ref,
                  w1_ref, a1_ref, g1_ref, w2_ref, a2_ref, g2_ref, w0_ref, a0_ref, kkw_ref, bo_ref,
                  r_o, k_o, v_o, g_o, kk_o, lw0_o, lw1_o, ag0_o, ag1_o):
    i = pl.program_id(0)
    row = _tile_mod_row(i)
    q = (i - NT_P) % TPS
    first = jnp.logical_or(i < NT_P, q == 0)
    last = jnp.logical_or(i < NT_P, q == TPS - 1)
    sh = _mod_vec(mod_ref, row, 0)
    sc = _mod_vec(mod_ref, row, 1)
    g = nrm_ref[...]
    h = _modulate(y_ref[...], g, sh, sc)
    hp = _modulate(yp_ref[...], g, sh, sc)[7:8]
    hn = _modulate(yn_ref[...], g, sh, sc)[0:1]
    hp = jnp.where(first, 0.0, hp)
    hn = jnp.where(last, 0.0, hn)
    rid = lax.broadcasted_iota(jnp.int32, (TM, D), 0)
    prev = jnp.where(rid == 0, hp, pltpu.roll(h, 1, 0))
    nxt = jnp.where(rid == TM - 1, hn, pltpu.roll(h, TM - 1, 0))
    dx = 0.5 * (prev + nxt) - h

    def mix(n):
        return (h + dx * mu_ref[n:n + 1, :]).astype(BF)

    r_o[...] = jnp.dot(mix(0), wr_ref[...], preferred_element_type=F32)
    xw = mix(1)
    lora_w = _bdot(jnp.tanh(jnp.dot(xw, w1_ref[...], preferred_element_type=F32)), w2_ref[...])
    c05 = math.exp(-0.5)
    lw0_o[...] = -c05 * jax.nn.sigmoid(w0_ref[0:1, :] + lora_w[:, :D])
    lw1_o[...] = -c05 * jax.nn.sigmoid(w0_ref[1:2, :] + lora_w[:, D:])
    k_raw = jnp.dot(mix(2), wk_ref[...], preferred_element_type=F32)
    k_o[...] = k_raw
    kq = k_raw * kkw_ref[...]
    ss = _seg_sum(kq * kq, bo_ref[...])
    kk_o[...] = kq / jnp.maximum(jnp.sqrt(ss), 1e-12)
    v_o[...] = jnp.dot(mix(3), wv_ref[...], preferred_element_type=F32)
    xa = mix(4)
    lora_a = _bdot(jnp.dot(xa, a1_ref[...], preferred_element_type=F32), a2_ref[...])
    ag0_o[...] = jax.nn.sigmoid(a0_ref[0:1, :] + lora_a[:, :D])
    ag1_o[...] = jax.nn.sigmoid(a0_ref[1:2, :] + lora_a[:, D:])
    xg = mix(5)
    g_o[...] = _bdot(jax.nn.sigmoid(jnp.dot(xg, g1_ref[...], preferred_element_type=F32)), g2_ref[...])


def _rwkv_front(y, mod, nrm, mu, wr, wk, wv, w1c, a1c, g1, w2bd, a2bd, g2, w0, a0, kkw, bo):
    tile = pl.BlockSpec((TM, D), lambda i: (i, 0))
    nblk8 = NTOK // 8
    full = lambda a: pl.BlockSpec(a.shape, lambda i: (0,) * a.ndim)
    ins = [y, y, y, mod, nrm, mu, wr, wk, wv, w1c, a1c, g1, w2bd, a2bd, g2, w0, a0, kkw, bo]
    in_specs = [tile,
                pl.BlockSpec((8, D), lambda i: (jnp.maximum(i * (TM // 8) - 1, 0), 0)),
                pl.BlockSpec((8, D), lambda i: (jnp.minimum((i + 1) * (TM // 8), nblk8 - 1), 0))]
    in_specs += [full(a) for a in ins[3:]]
    out = jax.ShapeDtypeStruct((NTOK, D), F32)
    return pl.pallas_call(
        _front_kernel,
        grid=(NT,),
        in_specs=in_specs,
        out_specs=[tile] * 9,
        out_shape=[out] * 9,
        compiler_params=_cp(("parallel",)),
        name="rwkv_front",
    )(*ins)


def _wkv_chunk(d, S, r, k, v, kk, lw, ag, ka, rk, cst):
    tri4, strict, incl, h0, bd, bo2, same_blk, bd_sb, eye_c, lane_blk = cst
    dot = lambda a, b: jnp.dot(a, b, preferred_element_type=F32)
    nblk = CH // SB
    h0s = lax.broadcasted_iota(jnp.int32, (SB, LANES), 1) < HD

    def stack_heads(x, swap=False):
        zero = jnp.zeros_like(x)
        parts = [jnp.where(h0, x, zero), jnp.where(h0, zero, x)]
        return jnp.concatenate(parts[::-1] if swap else parts, axis=0)

    def dot_split(a, bm):
        a_hi, a_lo = _split2(a)
        b_hi, b_lo = _split2(bm)
        return dot(a_hi, b_hi) + (dot(a_hi, b_lo) + dot(a_lo, b_hi))

    def expand(mc):
        return jnp.where(bd_sb, jnp.concatenate([mc] * (LANES // SB), axis=0), 0.0)

    b = kk * ag
    kd = k * (1.0 + (ag - 1.0) * ka)
    hi, mid, lo = _split3(lw)
    cs = dot(tri4[d], jnp.concatenate([hi, mid, lo, jnp.zeros_like(lo)], axis=0))
    qh, ql = _split2(r * kd * rk)
    bonus = (dot(qh, bo2) + dot(ql, bo2)) * v
    yield
    tot = cs[CH - 1:CH, :] if d == 0 else cs[0:1, :]
    p_inv = jnp.exp(-cs)
    p_end = jnp.exp(tot - cs)
    at = -kk * jnp.exp(cs - lw)
    rt = r * jnp.exp(cs)
    bt = (b * p_inv).astype(BF)
    kt = (kd * p_inv).astype(BF)
    ar = jnp.concatenate([at, rt], axis=0)
    h02 = jnp.concatenate([h0, h0], axis=0)
    g0 = _bdot_nt(jnp.where(h02, ar, 0.0), jnp.concatenate([bt, kt], axis=0))
    g1 = _bdot_nt(jnp.where(h02, 0.0, ar), jnp.concatenate([kt, bt], axis=0))
    w0y0 = _bdot_nt(ar, S)
    yield
    w0, y0 = w0y0[:CH], w0y0[CH:]
    lab = jnp.where(strict[d], jnp.where(h0, g0[:CH], g1[:CH]), 0.0)
    lak = jnp.where(strict[d], jnp.where(h0, g1[:CH], g0[:CH]), 0.0)
    mrb = jnp.where(incl[d], jnp.where(h0, g0[CH:], g1[CH:]), 0.0)
    mrk = jnp.where(incl[d], jnp.where(h0, g1[CH:], g0[CH:]), 0.0)
    vstk_sw = stack_heads(v, swap=True).astype(BF)
    w = w0 + dot(lak.astype(BF), vstk_sw)
    ldiag = jnp.where(same_blk, lab, 0.0)
    loff = jnp.where(same_blk, 0.0, lab)
    mc = ldiag[0:SB]
    for i in range(1, nblk):
        mc = mc + ldiag[i * SB:(i + 1) * SB]
    xc = eye_c + mc
    mc = dot_split(mc, expand(mc))
    yield
    nsq = int(math.log2(SB)) - 1
    for it in range(nsq):
        last = it + 1 == nsq
        res = dot_split(xc if last else jnp.concatenate([xc, mc], axis=0), expand(mc))
        xc = xc + res[:SB]
        if not last:
            mc = res[SB:]
        yield
    ublk = [None] * nblk
    order = list(range(nblk)) if d == 0 else list(range(nblk - 1, -1, -1))
    for n, i in enumerate(order):
        wi = w[i * SB:(i + 1) * SB]
        if n > 0:
            ucur = jnp.concatenate([jnp.zeros((SB, LANES), F32) if ub is None else ub
                                    for ub in ublk], axis=0)
            wi = wi + dot(loff[i * SB:(i + 1) * SB].astype(BF), stack_heads(ucur).astype(BF))
            yield
        m0 = jnp.where(h0s, wi, 0.0).astype(BF)
        m1 = jnp.where(h0s, 0.0, wi).astype(BF)
        rhs = jnp.concatenate([m0] * nblk + [m1] * nblk, axis=0)
        ublk[i] = dot(jnp.where(lane_blk[i], xc, 0.0).astype(BF), rhs)
        yield
    u = jnp.concatenate(ublk, axis=0)
    y = dot(jnp.concatenate([mrb, mrk], axis=1).astype(BF),
            jnp.concatenate([stack_heads(u).astype(BF), vstk_sw], axis=0))
    uvt = jnp.concatenate([u, v], axis=0).T
    ds = _bdot(uvt, jnp.concatenate([b * p_end, kd * p_end], axis=0))
    yield
    s_new = S * jnp.exp(tot) + jnp.where(bd, ds, 0.0)
    return s_new, y0 + y, bonus


def _run_lockstep(gens):
    results = [None] * len(gens)
    pending = list(range(len(gens)))
    while pending:
        for i in list(pending):
            try:
                next(gens[i])
            except StopIteration as stop:
                results[i] = stop.value
                pending.remove(i)
    return results


def _wkv_consts():
    t = lax.broadcasted_iota(jnp.int32, (CH, 4 * CH), 0)
    s = lax.broadcasted_iota(jnp.int32, (CH, 4 * CH), 1) & (CH - 1)
    tri4 = [jnp.where(s <= t, 1.0, 0.0).astype(BF), jnp.where(s >= t, 1.0, 0.0).astype(BF)]
    t2 = lax.broadcasted_iota(jnp.int32, (CH, LANES), 0)
    l2 = lax.broadcasted_iota(jnp.int32, (CH, LANES), 1)
    s2 = l2 & (CH - 1)
    strict = [s2 < t2, s2 > t2]
    incl = [s2 <= t2, s2 >= t2]
    h0 = l2 < HD
    same_blk = (t2 // SB) == (s2 // SB)
    ri = lax.broadcasted_iota(jnp.int32, (LANES, LANES), 0)
    ci = lax.broadcasted_iota(jnp.int32, (LANES, LANES), 1)
    bd = (ri // HD) == (ci // HD)
    bo2 = jnp.where(bd, 1.0, 0.0).astype(BF)
    bd_sb = (ri // SB) == (ci // SB)
    t3 = lax.broadcasted_iota(jnp.int32, (SB, LANES), 0)
    l3 = lax.broadcasted_iota(jnp.int32, (SB, LANES), 1)
    eye_c = jnp.where((l3 & (SB - 1)) == t3, 1.0, 0.0)
    lane_blk = [((l3 & (CH - 1)) // SB) == i for i in range(CH // SB)]
    return tri4, strict, incl, h0, bd, bo2, same_blk, bd_sb, eye_c, lane_blk


def _wkv_kernel(T, has_s0, *refs):
    (r_ref, k_ref, v_ref, kk_ref, lw0_ref, lw1_ref, ag0_ref, ag1_ref, ka_ref, rk_ref), rest = refs[:10], refs[10:]
    if has_s0:
        s0_ref, y_ref, bon_ref, s_scr = rest
    else:
        y_ref, bon_ref, st_ref, s_scr = rest
    nch = T // CH
    npair = y_ref.shape[1] // LANES
    cst = _wkv_consts()
    if has_s0:
        s_scr[...] = s0_ref[...]
    else:
        s_scr[...] = jnp.zeros_like(s_scr)
    y_ref[...] = jnp.zeros_like(y_ref)
    bon_ref[...] = jnp.zeros_like(bon_ref)
    lw_refs = (lw0_ref, lw1_ref)
    ag_refs = (ag0_ref, ag1_ref)

    def body(c, carry):
        chains = [(p, d) for p in range(npair) for d in range(2)]
        sl = {}
        for p, d in chains:
            r0 = pl.multiple_of((c if d == 0 else nch - 1 - c) * CH, CH)
            sl[p, d] = (pl.ds(r0, CH), pl.ds(p * LANES, LANES))
        args = {}
        for p, d in chains:
            rs, ls = sl[p, d]
            args[p, d] = (s_scr[d, p], r_ref[rs, ls], k_ref[rs, ls], v_ref[rs, ls], kk_ref[rs, ls],
                          lw_refs[d][rs, ls], ag_refs[d][rs, ls], ka_ref[:, ls], rk_ref[:, ls],
                          y_ref[rs, ls], bon_ref[rs, ls])
        outs = _run_lockstep([_wkv_chunk(d, *args[p, d][:9], cst) for p, d in chains])
        for (p, d), (s_new, y, bon) in zip(chains, outs):
            rs, ls = sl[p, d]
            s_scr[d, p] = s_new
            y_ref[rs, ls] = args[p, d][9] + y
            bon_ref[rs, ls] = args[p, d][10] + bon
        return carry

    lax.fori_loop(0, nch, body, 0)
    if not has_s0:
        st_ref[...] = s_scr[...]


def _wkv(arrs, ka, rk, T, nb, row_blk0, s0_bd=None, lanes=512):
    ng = D // lanes
    npair = lanes // LANES
    seq = pl.BlockSpec((T, lanes), lambda b, g: (row_blk0 + b, g))
    vec = pl.BlockSpec((1, lanes), lambda b, g: (0, g))
    st = pl.BlockSpec((None, 2, npair, LANES, LANES), lambda b, g: (b, 0, g, 0, 0))
    out_seq = pl.BlockSpec((T, lanes), lambda b, g: (b, g))
    in_specs = [seq] * 8 + [vec, vec]
    ins = list(arrs) + [ka, rk]
    out_specs = [out_seq, out_seq]
    out_shape = [jax.ShapeDtypeStruct((nb * T, D), F32)] * 2
    if s0_bd is not None:
        in_specs.append(st)
        ins.append(s0_bd)
    else:
        out_specs.append(st)
        out_shape.append(jax.ShapeDtypeStruct((nb, 2, D // LANES, LANES, LANES), F32))
    return pl.pallas_call(
        functools.partial(_wkv_kernel, T, s0_bd is not None),
        grid=(nb, ng),
        in_specs=in_specs,
        out_specs=out_specs,
        out_shape=out_shape,
        scratch_shapes=[pltpu.VMEM((2, npair, LANES, LANES), F32)],
        compiler_params=_cp(("parallel", "parallel")),
        name="wkv_T%d" % T,
    )(*ins)


def _proj_kernel(gn, y_ref, ap_ref, as_ref, *rest):
    if gn:
        bp_ref, bs_ref, g_ref, lng_ref, lnb_ref, bo_ref, mod_ref, w_ref, o_ref = rest
    else:
        mod_ref, w_ref, o_ref = rest
    i = pl.program_id(0)
    is_p = i < NT_P
    a = jnp.where(is_p, ap_ref[...], as_ref[...])
    if gn:
        bo = bo_ref[...]
        mean = _seg_sum(a, bo) * (1.0 / HD)
        cen = a - mean
        var = _seg_sum(cen * cen, bo) * (1.0 / HD)
        yn = cen * lax.rsqrt(var + GN_EPS) * lng_ref[...] + lnb_ref[...]
        a = (yn + jnp.where(is_p, bp_ref[...], bs_ref[...])) * g_ref[...]
    gt = _mod_vec(mod_ref, _tile_mod_row(i), 2)
    o_ref[...] = y_ref[...] + gt * _bdot(a, w_ref[...])


def _proj_residual(y, a_p, a_s, mod, w, gn_args=None, name="proj"):
    tile = pl.BlockSpec((TM, D), lambda i: (i, 0))
    tile_p = pl.BlockSpec((TM, D), lambda i: (jnp.minimum(i, NT_P - 1), 0))
    tile_s = pl.BlockSpec((TM, D), lambda i: (jnp.maximum(i - NT_P, 0), 0))
    full = lambda a: pl.BlockSpec(a.shape, lambda i: (0,) * a.ndim)
    ins = [y, a_p, a_s]
    in_specs = [tile, tile_p, tile_s]
    if gn_args is not None:
        b_p, b_s, g, lng, lnb, bo = gn_args
        ins += [b_p, b_s, g, lng, lnb, bo]
        in_specs += [tile_p, tile_s, tile, full(lng), full(lnb), full(bo)]
    ins += [mod, w]
    in_specs += [full(mod), full(w)]
    return pl.pallas_call(
        functools.partial(_proj_kernel, gn_args is not None),
        grid=(NT,),
        in_specs=in_specs,
        out_specs=tile,
        out_shape=jax.ShapeDtypeStruct((NTOK, D), F32),
        compiler_params=_cp(("parallel",)),
        name=name,
    )(*ins)


def _router_kernel(y_ref, mod_ref, nrm_ref, wr_ref, br_ref, xt_o, gate_o):
    i = pl.program_id(0)
    row = _tile_mod_row(i)
    xt = _modulate(y_ref[...], nrm_ref[...], _mod_vec(mod_ref, row, 3), _mod_vec(mod_ref, row, 4))
    xt_o[...] = xt.astype(BF)
    x1, x2, x3 = _split3(xt)
    w1, w2, w3 = _split3(wr_ref[...])
    dot = lambda a, b: jnp.dot(a, b, preferred_element_type=F32)
    logits = (dot(x3, w1) + dot(x2, w2) + dot(x1, w3)) + (dot(x2, w1) + dot(x1, w2)) + dot(x1, w1)
    logits = logits + br_ref[...]
    lane = lax.broadcasted_iota(jnp.int32, logits.shape, 1).astype(F32)
    isg = jnp.logical_and(lane >= NE, lane < NE + NE // EPG)
    mg = jnp.max(jnp.where(isg, logits, NEG), axis=-1, keepdims=True)
    eg = jnp.where(isg, jnp.exp(jnp.minimum(logits - mg, 0.0)), 0.0)
    p_sel = 1.0 / jnp.sum(eg, axis=-1, keepdims=True)
    gidx = jnp.min(jnp.where(jnp.logical_and(isg, logits == mg), lane, 1e3), axis=-1, keepdims=True) - NE
    lo_l = gidx * EPG
    ing = jnp.logical_and(lane >= lo_l, lane < lo_l + EPG)
    me = jnp.max(jnp.where(ing, logits, NEG), axis=-1, keepdims=True)
    ee = jnp.where(ing, jnp.exp(jnp.minimum(logits - me, 0.0)), 0.0)
    se = jnp.sum(ee, axis=-1, keepdims=True)
    pe = ee / se
    i1 = jnp.min(jnp.where(jnp.logical_and(ing, logits == me), lane, 1e3), axis=-1, keepdims=True)
    v1 = 1.0 / se
    rest = jnp.logical_and(ing, lane != i1)
    v2 = jnp.max(jnp.where(rest, pe, -1.0), axis=-1, keepdims=True)
    i2 = jnp.min(jnp.where(jnp.logical_and(rest, pe == v2), lane, 1e3), axis=-1, keepdims=True)
    den = v1 + v2
    gate_o[...] = (jnp.where(lane == i1, p_sel * v1 / den, 0.0)
                   + jnp.where(lane == i2, p_sel * v2 / den, 0.0))


def _moe_router(y, mod, nrm, wr, br):
    tile = pl.BlockSpec((TM, D), lambda i: (i, 0))
    full = lambda a: pl.BlockSpec(a.shape, lambda i: (0,) * a.ndim)
    return pl.pallas_call(
        _router_kernel,
        grid=(NT,),
        in_specs=[tile, full(mod), full(nrm), full(wr), full(br)],
        out_specs=[tile, pl.BlockSpec((TM, LANES), lambda i: (i, 0))],
        out_shape=[jax.ShapeDtypeStruct((NTOK, D), BF), jax.ShapeDtypeStruct((NTOK, LANES), F32)],
        compiler_params=_cp(("parallel",)),
        name="moe_router",
    )(y, mod, nrm, wr, br)


def _experts_kernel(final, xt_ref, gate_ref, w13_ref, w2_ref, y_ref, mod_ref, fin_ref, o_ref):
    i = pl.program_id(0)
    e = pl.program_id(1)

    @pl.when(e == 0)
    def _():
        o_ref[...] = jnp.zeros_like(o_ref)

    h13 = jnp.dot(xt_ref[...], w13_ref[...], preferred_element_type=F32)
    h1, h3 = h13[:, :DE], h13[:, DE:]
    lane = lax.broadcasted_iota(jnp.int32, gate_ref.shape, 1)
    ge = jnp.sum(jnp.where(lane == e, gate_ref[...], 0.0), axis=-1, keepdims=True)
    hid = (h1 * jax.nn.sigmoid(h1)) * h3 * ge
    o_ref[...] += jnp.dot(hid.astype(BF), w2_ref[...], preferred_element_type=F32)

    @pl.when(e == NE - 1)
    def _():
        row = jnp.where(i < NPR // TMX, 0, i - NPR // TMX + 1)
        out = y_ref[...] + _mod_vec(mod_ref, row, 5) * o_ref[...]
        if final:
            ms = jnp.mean(out * out, axis=-1, keepdims=True)
            out = out * lax.rsqrt(ms + NORM_EPS) * fin_ref[...]
        o_ref[...] = out


def _moe_experts(xt, gate, w13, w2, y, mod, fin, final):
    tile = pl.BlockSpec((TMX, D), lambda i, e: (i, 0))
    full = lambda a: pl.BlockSpec(a.shape, lambda i, e: (0,) * a.ndim)
    return pl.pallas_call(
        functools.partial(_experts_kernel, final),
        grid=(NTOK // TMX, NE),
        in_specs=[tile, pl.BlockSpec((TMX, LANES), lambda i, e: (i, 0)),
                  pl.BlockSpec((None, D, 2 * DE), lambda i, e: (e, 0, 0)),
                  pl.BlockSpec((None, DE, D), lambda i, e: (e, 0, 0)),
                  tile, full(mod), full(fin)],
        out_specs=tile,
        out_shape=jax.ShapeDtypeStruct((NTOK, D), F32),
        compiler_params=_cp(("parallel", "arbitrary")),
        name="moe_experts",
    )(xt, gate, w13, w2, y, mod, fin)


def _qkv_kernel(y_ref, mod_ref, nrm_ref, w_ref, q_o, k_o, v_o):
    i = pl.program_id(0)
    row = _tile_mod_row(i)
    h = _modulate(y_ref[...], nrm_ref[...], _mod_vec(mod_ref, row, 0), _mod_vec(mod_ref, row, 1))
    qkv = _bdot(h, w_ref[...])
    q_o[...] = qkv[:, :D]
    k_o[...] = qkv[:, D:2 * D]
    v_o[...] = qkv[:, 2 * D:]


def _na_qkv(y, mod, nrm, w):
    tile = pl.BlockSpec((TM, D), lambda i: (i, 0))
    full = lambda a: pl.BlockSpec(a.shape, lambda i: (0,) * a.ndim)
    out = jax.ShapeDtypeStruct((NTOK, D), F32)
    return pl.pallas_call(
        _qkv_kernel,
        grid=(NT,),
        in_specs=[tile, full(mod), full(nrm), full(w)],
        out_specs=[tile] * 3,
        out_shape=[out] * 3,
        compiler_params=_cp(("parallel",)),
        name="na_qkv",
    )(y, mod, nrm, w)


def _softmax_rows(s):
    m = jnp.max(s, axis=-1, keepdims=True)
    e = jnp.exp(s - m)
    return e / jnp.sum(e, axis=-1, keepdims=True)


def _ctx_attn_kernel(q_ref, k_ref, v_ref, o_ref):
    scale = HD ** -0.5
    h0 = lax.broadcasted_iota(jnp.int32, (T_P, LANES), 1) < HD
    for p in range(D // LANES):
        ls = pl.ds(p * LANES, LANES)
        q = q_ref[:, ls]
        kb = k_ref[:, ls].astype(BF)
        vb = v_ref[:, ls].astype(BF)
        outs = []
        for h in range(2):
            qm = jnp.where(h0 if h == 0 else jnp.logical_not(h0), q, 0.0)
            s = _bdot_nt(qm, kb) * scale
            outs.append(jnp.dot(_softmax_rows(s).astype(BF), vb, preferred_element_type=F32))
        o_ref[:, ls] = jnp.where(h0, outs[0], outs[1])


def _ctx_attn(q, k, v):
    blk = pl.BlockSpec((T_P, D), lambda b: (b, 0))
    return pl.pallas_call(
        _ctx_attn_kernel,
        grid=(NB_P,),
        in_specs=[blk] * 3,
        out_specs=blk,
        out_shape=jax.ShapeDtypeStruct((NPR, D), F32),
        compiler_params=_cp(("parallel",)),
        name="ctx_attn",
    )(q, k, v)


def _nbr_attn_kernel(q_ref, k_ref, v_ref, kc_ref, vc_ref, tz_ref, o_ref):
    scale = HD ** -0.5
    rows = T_S // GRID_W
    nloc = WIN_R * GRID_W
    h0 = lax.broadcasted_iota(jnp.int32, (GRID_W, LANES), 1) < HD
    kcb = kc_ref[...].astype(BF)
    vcb = vc_ref[...].astype(BF)

    def body(r, carry):
        start = jnp.clip(r - WIN_R // 2, 0, rows - WIN_R)
        qs = pl.ds(pl.multiple_of(r * GRID_W, GRID_W), GRID_W)
        ks = pl.ds(pl.multiple_of(start * GRID_W, GRID_W), nloc)
        q = q_ref[qs, :]
        klb = k_ref[ks, :].astype(BF)
        vlb = v_ref[ks, :].astype(BF)
        j0 = start - r + WIN_R - 1
        outs = []
        for h in range(2):
            qm = jnp.where(h0 if h == 0 else jnp.logical_not(h0), q, 0.0)
            bias = jnp.concatenate([tz_ref[h, j0 + 2 * m] for m in range(WIN_R // 2)], axis=1)
            sl = _bdot_nt(qm, klb) * scale + bias
            sc = _bdot_nt(qm, kcb) * scale
            m = jnp.maximum(jnp.max(sl, axis=-1, keepdims=True), jnp.max(sc, axis=-1, keepdims=True))
            el = jnp.exp(sl - m)
            ec = jnp.exp(sc - m)
            den = jnp.sum(el, axis=-1, keepdims=True) + jnp.sum(ec, axis=-1, keepdims=True)
            outs.append(jnp.dot((el / den).astype(BF), vlb, preferred_element_type=F32)
                        + jnp.dot((ec / den).astype(BF), vcb, preferred_element_type=F32))
        o_ref[qs, :] = jnp.where(h0, outs[0], outs[1])
        return carry

    lax.fori_loop(0, rows, body, 0)


def _nbr_attn(q, k, v, kc, vc, tz2):
    npair = D // LANES
    seq = pl.BlockSpec((T_S, LANES), lambda b, p: (NPR // T_S + b, p))
    ctx = pl.BlockSpec((kc.shape[0] // NB_S, LANES), lambda b, p: (b, p))
    return pl.pallas_call(
        _nbr_attn_kernel,
        grid=(NB_S, npair),
        in_specs=[seq, seq, seq, ctx, ctx,
                  pl.BlockSpec((2,) + tz2.shape[1:], lambda b, p: (p, 0, 0, 0))],
        out_specs=pl.BlockSpec((T_S, LANES), lambda b, p: (b, p)),
        out_shape=jax.ShapeDtypeStruct((NSR, D), F32),
        compiler_params=_cp(("parallel", "parallel")),
        name="nbr_attn",
    )(q, k, v, kc, vc, tz2)


def _block_diag2(a, b):
    z = jnp.zeros_like(a)
    return jnp.concatenate([jnp.concatenate([a, z], axis=1), jnp.concatenate([z, b], axis=1)], axis=0)


def _nbr_bias_table(rpb):
    qc = jnp.arange(GRID_W)[:, None]
    kc = jnp.arange(GRID_W)[None, :]
    cstart = jnp.clip(qc - WIN_C // 2, 0, GRID_W - WIN_C)
    valid = jnp.logical_and(kc >= cstart, kc < cstart + WIN_C)
    dc = jnp.clip(kc - qc, -(WIN_C - 1), WIN_C - 1) + WIN_C - 1
    tz = jnp.where(valid, rpb[:, :, dc], NEG)
    return jnp.concatenate([tz[:, :-1], tz[:, 1:]], axis=-1)


def kernel(x_prompt, x_sample, c, state_rwkv, cache_na_k, cache_na_v, c_ctx, norm_mix, norm_ffn, ada_w, ada_b, rwkv_mu, rwkv_w_r, rwkv_w_k, rwkv_w_v, rwkv_w_o, rwkv_w0, rwkv_w1, rwkv_w2, rwkv_a0, rwkv_a1, rwkv_a2, rwkv_g1, rwkv_g2, rwkv_k_k, rwkv_k_a, rwkv_r_k, rwkv_lnx_g, rwkv_lnx_b, na_w_qkv, na_w_o, na_rpb, moe_w_grp, moe_b_grp, moe_w_exp, moe_b_exp, moe_w1, moe_w3, moe_w2, final_norm):
    y = jnp.concatenate([x_prompt.reshape(NPR, D), x_sample.reshape(NSR, D)], axis=0)
    c8 = jnp.concatenate([c_ctx[None, :], c, jnp.zeros((8 - 1 - NB_S, D), F32)], axis=0)
    mod = _adaln(c8, ada_w, ada_b)
    ri = jnp.arange(2 * LANES)[:, None] // HD
    bo = (ri == ri.T).astype(BF)
    row = lambda a: a.reshape(1, D)

    w1c = jnp.concatenate([rwkv_w1[0, 0], rwkv_w1[0, 1]], axis=1).astype(BF)
    a1c = jnp.concatenate([rwkv_a1[0, 0], rwkv_a1[0, 1]], axis=1).astype(BF)
    w2bd = _block_diag2(rwkv_w2[0, 0], rwkv_w2[0, 1]).astype(BF)
    a2bd = _block_diag2(rwkv_a2[0, 0], rwkv_a2[0, 1]).astype(BF)
    r, k, v, g, kk, lw0, lw1, ag0, ag1 = _rwkv_front(
        y, mod[0], row(norm_mix[0]), rwkv_mu[0], rwkv_w_r[0].astype(BF), rwkv_w_k[0].astype(BF),
        rwkv_w_v[0].astype(BF), w1c, a1c, rwkv_g1[0].astype(BF), w2bd, a2bd, rwkv_g2[0].astype(BF),
        rwkv_w0[0], rwkv_a0[0], row(rwkv_k_k[0]), bo)
    arrs = (r, k, v, kk, lw0, lw1, ag0, ag1)
    ka, rk = row(rwkv_k_a[0]), row(rwkv_r_k[0])
    ys_p, bon_p, st_bd = _wkv(arrs, ka, rk, T_P, NB_P, 0)
    s0 = state_rwkv[:, 0].reshape(NB_S, 2, NH // 2, 2, HD, HD)
    z = jnp.zeros_like(s0[:, :, :, 0])
    s0_bd = jnp.concatenate([jnp.concatenate([s0[:, :, :, 0], z], axis=-1),
                             jnp.concatenate([z, s0[:, :, :, 1]], axis=-1)], axis=-2)
    ys_s, bon_s = _wkv(arrs, ka, rk, T_S, NB_S, NPR // T_S, s0_bd=s0_bd)
    new_state = jnp.stack([st_bd[..., :HD, :HD], st_bd[..., HD:, HD:]], axis=3)
    new_state_rwkv = new_state.reshape(NB_P, 1, 2, NH, HD, HD)
    y = _proj_residual(y, ys_p, ys_s, mod[0], rwkv_w_o[0].astype(BF),
                       gn_args=(bon_p, bon_s, g, row(rwkv_lnx_g[0]), row(rwkv_lnx_b[0]), bo),
                       name="rwkv_out")

    def moe(y, i, final):
        wr = jnp.concatenate([moe_w_exp[i], moe_w_grp[i],
                              jnp.zeros((D, LANES - NE - NE // EPG), F32)], axis=1)
        br = jnp.concatenate([moe_b_exp[i], moe_b_grp[i],
                              jnp.zeros((LANES - NE - NE // EPG,), F32)]).reshape(1, LANES)
        xt, gate = _moe_router(y, mod[i], row(norm_ffn[i]), wr, br)
        w13 = jnp.concatenate([moe_w1[i], moe_w3[i]], axis=-1).astype(BF)
        return _moe_experts(xt, gate, w13, moe_w2[i].astype(BF), y, mod[i], row(final_norm), final)

    y = moe(y, 0, False)

    q, k, v = _na_qkv(y, mod[1], row(norm_mix[1]), na_w_qkv[0].astype(BF))
    o_p = _ctx_attn(q, k, v)
    kc = cache_na_k[:, 0].reshape(NB_S * cache_na_k.shape[2], D)
    vc = cache_na_v[:, 0].reshape(NB_S * cache_na_v.shape[2], D)
    o_s = _nbr_attn(q, k, v, kc, vc, _nbr_bias_table(na_rpb[0]))
    y = _proj_residual(y, o_p, o_s, mod[1], na_w_o[0].astype(BF), name="attn_out")
    y = moe(y, 1, True)

    y_prompt = y[:NPR].reshape(NB_P, T_P, D)
    y_sample = y[NPR:].reshape(NB_S, T_S, D)
    new_k = k[:NPR].reshape(NB_P, 1, T_P, NH, HD)
    new_v = v[:NPR].reshape(NB_P, 1, T_P, NH, HD)
    return (y_prompt, y_sample, new_state_rwkv, new_k, new_v)
```

```python
import functools
import math

import jax
import jax.numpy as jnp
from jax import lax
from jax.experimental import pallas as pl
from jax.experimental.pallas import tpu as pltpu

F32 = jnp.float32
BF = jnp.bfloat16

D = 1024
NH = 16
HD = 64
NB_P, T_P = 32, 256
NB_S, T_S = 2, 1024
NPR = NB_P * T_P
NSR = NB_S * T_S
NTOK = NPR + NSR
TM = 256
NT = NTOK // TM
NT_P = NPR // TM
TPS = T_S // TM
TMX = 1024
NMOD = 6
NE = 16
EPG = 4
DE = 256
CH = 64
SB = 16
LANES = 128
GRID_W = 64
WIN_R, WIN_C = 8, 16
NBR_UNROLL = 4
CTX_PAIRS = 2
NORM_EPS = 1e-6
GN_EPS = 64e-5
NEG = -1e30
VMEM_LIMIT = 56 * 1024 * 1024


def _cp(sem):
    return pltpu.CompilerParams(dimension_semantics=sem, vmem_limit_bytes=VMEM_LIMIT)


def _bdot(a, b):
    return jnp.dot(a.astype(BF), b.astype(BF), preferred_element_type=F32)


def _bdot_nt(a, b):
    return lax.dot_general(a.astype(BF), b.astype(BF), (((1,), (1,)), ((), ())),
                           preferred_element_type=F32)


def _split2(x):
    hi = x.astype(BF)
    lo = (x - hi.astype(F32)).astype(BF)
    return hi, lo


def _split3(x):
    hi = x.astype(BF)
    r1 = x - hi.astype(F32)
    mid = r1.astype(BF)
    lo = (r1 - mid.astype(F32)).astype(BF)
    return hi, mid, lo


def _seg_sum(x, bo):
    outs = []
    for c in range(x.shape[1] // 256):
        hi, lo = _split2(x[:, c * 256:(c + 1) * 256])
        outs.append(jnp.dot(hi, bo, preferred_element_type=F32)
                    + jnp.dot(lo, bo, preferred_element_type=F32))
    return outs[0] if len(outs) == 1 else jnp.concatenate(outs, axis=1)


def _tile_mod_row(i):
    return jnp.where(i < NT_P, 0, 1 + (i - NT_P) // TPS)


def _modulate(x, g, sh, sc):
    ms = jnp.mean(x * x, axis=-1, keepdims=True)
    return x * lax.rsqrt(ms + NORM_EPS) * g * (1.0 + sc) + sh


def _mod_vec(mod_ref, row, k):
    return mod_ref[pl.ds(row, 1), pl.ds(k * D, D)]


def _adaln_kernel(c_ref, w_ref, b_ref, o_ref):
    c = c_ref[...]
    s = c * jax.nn.sigmoid(c)
    o_ref[...] = _bdot(s, w_ref[...]) + b_ref[...]


def _adaln(c8, ada_w, ada_b):
    nl = ada_w.shape[0]
    tn = 1536
    return pl.pallas_call(
        _adaln_kernel,
        grid=(nl, NMOD * D // tn),
        in_specs=[pl.BlockSpec((8, D), lambda l, j: (0, 0)),
                  pl.BlockSpec((None, D, tn), lambda l, j: (l, 0, j)),
                  pl.BlockSpec((None, 1, tn), lambda l, j: (l, 0, j))],
        out_specs=pl.BlockSpec((None, 8, tn), lambda l, j: (l, 0, j)),
        out_shape=jax.ShapeDtypeStruct((nl, 8, NMOD * D), F32),
        compiler_params=_cp(("parallel", "parallel")),
        name="adaln",
    )(c8, ada_w, ada_b.reshape(nl, 1, NMOD * D))


def _front_kernel(y_ref, yp_ref, yn_ref, mod_ref, nrm_ref, mu_ref, wr_ref, wk_ref, wv_ref,
                  w1_ref, a1_ref, g1_ref, w2_ref, a2_ref, g2_ref, w0_ref, a0_ref, kkw_ref, bo_ref,
                  r_o, k_o, v_o, g_o, kk_o, lw0_o, lw1_o, ag0_o, ag1_o):
    i = pl.program_id(0)
    row = _tile_mod_row(i)
    q = (i - NT_P) % TPS
    first = jnp.logical_or(i < NT_P, q == 0)
    last = jnp.logical_or(i < NT_P, q == TPS - 1)
    sh = _mod_vec(mod_ref, row, 0)
    sc = _mod_vec(mod_ref, row, 1)
    g = nrm_ref[...]
    h = _modulate(y_ref[...], g, sh, sc)
    hp = _modulate(yp_ref[...], g, sh, sc)[7:8]
    hn = _modulate(yn_ref[...], g, sh, sc)[0:1]
    hp = jnp.where(first, 0.0, hp)
    hn = jnp.where(last, 0.0, hn)
    rid = lax.broadcasted_iota(jnp.int32, (TM, D), 0)
    prev = jnp.where(rid == 0, hp, pltpu.roll(h, 1, 0))
    nxt = jnp.where(rid == TM - 1, hn, pltpu.roll(h, TM - 1, 0))
    dx = 0.5 * (prev + nxt) - h

    def mix(n):
        return (h + dx * mu_ref[n:n + 1, :]).astype(BF)

    r_o[...] = jnp.dot(mix(0), wr_ref[...], preferred_element_type=F32)
    xw = mix(1)
    lora_w = _bdot(jnp.tanh(jnp.dot(xw, w1_ref[...], preferred_element_type=F32)), w2_ref[...])
    c05 = math.exp(-0.5)
    lw0_o[...] = -c05 * jax.nn.sigmoid(w0_ref[0:1, :] + lora_w[:, :D])
    lw1_o[...] = -c05 * jax.nn.sigmoid(w0_ref[1:2, :] + lora_w[:, D:])
    k_raw = jnp.dot(mix(2), wk_ref[...], preferred_element_type=F32)
    k_o[...] = k_raw
    kq = k_raw * kkw_ref[...]
    ss = _seg_sum(kq * kq, bo_ref[...])
    kk_o[...] = kq / jnp.maximum(jnp.sqrt(ss), 1e-12)
    v_o[...] = jnp.dot(mix(3), wv_ref[...], preferred_element_type=F32)
    xa = mix(4)
    lora_a = _bdot(jnp.dot(xa, a1_ref[...], preferred_element_type=F32), a2_ref[...])
    ag0_o[...] = jax.nn.sigmoid(a0_ref[0:1, :] + lora_a[:, :D])
    ag1_o[...] = jax.nn.sigmoid(a0_ref[1:2, :] + lora_a[:, D:])
    xg = mix(5)
    g_o[...] = _bdot(jax.nn.sigmoid(jnp.dot(xg, g1_ref[...], preferred_element_type=F32)), g2_ref[...])


def _rwkv_front(y, mod, nrm, mu, wr, wk, wv, w1c, a1c, g1, w2bd, a2bd, g2, w0, a0, kkw, bo):
    tile = pl.BlockSpec((TM, D), lambda i: (i, 0))
    nblk8 = NTOK // 8
    full = lambda a: pl.BlockSpec(a.shape, lambda i: (0,) * a.ndim)
    ins = [y, y, y, mod, nrm, mu, wr, wk, wv, w1c, a1c, g1, w2bd, a2bd, g2, w0, a0, kkw, bo]
    in_specs = [tile,
                pl.BlockSpec((8, D), lambda i: (jnp.maximum(i * (TM // 8) - 1, 0), 0)),
                pl.BlockSpec((8, D), lambda i: (jnp.minimum((i + 1) * (TM // 8), nblk8 - 1), 0))]
    in_specs += [full(a) for a in ins[3:]]
    out = jax.ShapeDtypeStruct((NTOK, D), F32)
    return pl.pallas_call(
        _front_kernel,
        grid=(NT,),
        in_specs=in_specs,
        out_specs=[tile] * 9,
        out_shape=[out] * 9,
        compiler_params=_cp(("parallel",)),
        name="rwkv_front",
    )(*ins)


def _wkv_chunk(d, S, r, k, v, kk, lw, ag, ka, rk, cst):
    tri4, strict, incl, h0, bd, bo2, same_blk, bd_sb, eye_c, lane_blk = cst
    dot = lambda a, b: jnp.dot(a, b, preferred_element_type=F32)
    nblk = CH // SB
    h0s = lax.broadcasted_iota(jnp.int32, (SB, LANES), 1) < HD

    def stack_heads(x, swap=False):
        zero = jnp.zeros_like(x)
        parts = [jnp.where(h0, x, zero), jnp.where(h0, zero, x)]
        return jnp.concatenate(parts[::-1] if swap else parts, axis=0)

    def dot_split(a, bm):
        a_hi, a_lo = _split2(a)
        b_hi, b_lo = _split2(bm)
        return dot(a_hi, b_hi) + (dot(a_hi, b_lo) + dot(a_lo, b_hi))

    def expand(mc):
        return jnp.where(bd_sb, jnp.concatenate([mc] * (LANES // SB), axis=0), 0.0)

    b = kk * ag
    kd = k * (1.0 + (ag - 1.0) * ka)
    hi, mid, lo = _split3(lw)
    cs = dot(tri4[d], jnp.concatenate([hi, mid, lo, jnp.zeros_like(lo)], axis=0))
    qh, ql = _split2(r * kd * rk)
    bonus = (dot(qh, bo2) + dot(ql, bo2)) * v
    yield
    tot = cs[CH - 1:CH, :] if d == 0 else cs[0:1, :]
    p_inv = jnp.exp(-cs)
    p_end = jnp.exp(tot - cs)
    at = -kk * jnp.exp(cs - lw)
    rt = r * jnp.exp(cs)
    bt = (b * p_inv).astype(BF)
    kt = (kd * p_inv).astype(BF)
    ar = jnp.concatenate([at, rt], axis=0)
    h02 = jnp.concatenate([h0, h0], axis=0)
    g0 = _bdot_nt(jnp.where(h02, ar, 0.0), jnp.concatenate([bt, kt], axis=0))
    g1 = _bdot_nt(jnp.where(h02, 0.0, ar), jnp.concatenate([kt, bt], axis=0))
    w0y0 = _bdot_nt(ar, S)
    yield
    w0, y0 = w0y0[:CH], w0y0[CH:]
    lab = jnp.where(strict[d], jnp.where(h0, g0[:CH], g1[:CH]), 0.0)
    lak = jnp.where(strict[d], jnp.where(h0, g1[:CH], g0[:CH]), 0.0)
    mrb = jnp.where(incl[d], jnp.where(h0, g0[CH:], g1[CH:]), 0.0)
    mrk = jnp.where(incl[d], jnp.where(h0, g1[CH:], g0[CH:]), 0.0)
    vstk_sw = stack_heads(v, swap=True).astype(BF)
    w = w0 + dot(lak.astype(BF), vstk_sw)
    ldiag = jnp.where(same_blk, lab, 0.0)
    loff = jnp.where(same_blk, 0.0, lab)
    mc = ldiag[0:SB]
    for i in range(1, nblk):
        mc = mc + ldiag[i * SB:(i + 1) * SB]
    xc = eye_c + mc
    mc = dot_split(mc, expand(mc))
    yield
    nsq = int(math.log2(SB)) - 1
    for it in range(nsq):
        last = it + 1 == nsq
        res = dot_split(xc if last else jnp.concatenate([xc, mc], axis=0), expand(mc))
        xc = xc + res[:SB]
        if not last:
            mc = res[SB:]
        yield
    ublk = [None] * nblk
    order = list(range(nblk)) if d == 0 else list(range(nblk - 1, -1, -1))
    for n, i in enumerate(order):
        wi = w[i * SB:(i + 1) * SB]
        if n > 0:
            ucur = jnp.concatenate([jnp.zeros((SB, LANES), F32) if ub is None else ub
                                    for ub in ublk], axis=0)
            wi = wi + dot(loff[i * SB:(i + 1) * SB].astype(BF), stack_heads(ucur).astype(BF))
            yield
        m0 = jnp.where(h0s, wi, 0.0).astype(BF)
        m1 = jnp.where(h0s, 0.0, wi).astype(BF)
        rhs = jnp.concatenate([m0] * nblk + [m1] * nblk, axis=0)
        ublk[i] = dot(jnp.where(lane_blk[i], xc, 0.0).astype(BF), rhs)
        yield
    u = jnp.concatenate(ublk, axis=0)
    y = dot(jnp.concatenate([mrb, mrk], axis=1).astype(BF),
            jnp.concatenate([stack_heads(u).astype(BF), vstk_sw], axis=0))
    uvt = jnp.concatenate([u, v], axis=0).T
    ds = _bdot(uvt, jnp.concatenate([b * p_end, kd * p_end], axis=0))
    yield
    s_new = S * jnp.exp(tot) + jnp.where(bd, ds, 0.0)
    return s_new, y0 + y, bonus


def _run_lockstep(gens):
    results = [None] * len(gens)
    pending = list(range(len(gens)))
    while pending:
        for i in list(pending):
            try:
                next(gens[i])
            except StopIteration as stop:
                results[i] = stop.value
                pending.remove(i)
    return results


def _wkv_consts():
    t = lax.broadcasted_iota(jnp.int32, (CH, 4 * CH), 0)
    s = lax.broadcasted_iota(jnp.int32, (CH, 4 * CH), 1) & (CH - 1)
    tri4 = [jnp.where(s <= t, 1.0, 0.0).astype(BF), jnp.where(s >= t, 1.0, 0.0).astype(BF)]
    t2 = lax.broadcasted_iota(jnp.int32, (CH, LANES), 0)
    l2 = lax.broadcasted_iota(jnp.int32, (CH, LANES), 1)
    s2 = l2 & (CH - 1)
    strict = [s2 < t2, s2 > t2]
    incl = [s2 <= t2, s2 >= t2]
    h0 = l2 < HD
    same_blk = (t2 // SB) == (s2 // SB)
    ri = lax.broadcasted_iota(jnp.int32, (LANES, LANES), 0)
    ci = lax.broadcasted_iota(jnp.int32, (LANES, LANES), 1)
    bd = (ri // HD) == (ci // HD)
    bo2 = jnp.where(bd, 1.0, 0.0).astype(BF)
    bd_sb = (ri // SB) == (ci // SB)
    t3 = lax.broadcasted_iota(jnp.int32, (SB, LANES), 0)
    l3 = lax.broadcasted_iota(jnp.int32, (SB, LANES), 1)
    eye_c = jnp.where((l3 & (SB - 1)) == t3, 1.0, 0.0)
    lane_blk = [((l3 & (CH - 1)) // SB) == i for i in range(CH // SB)]
    return tri4, strict, incl, h0, bd, bo2, same_blk, bd_sb, eye_c, lane_blk


def _wkv_kernel(T, has_s0, *refs):
    (r_ref, k_ref, v_ref, kk_ref, lw0_ref, lw1_ref, ag0_ref, ag1_ref, ka_ref, rk_ref), rest = refs[:10], refs[10:]
    if has_s0:
        s0_ref, y_ref, bon_ref, s_scr = rest
    else:
        y_ref, bon_ref, st_ref, s_scr = rest
    nch = T // CH
    npair = y_ref.shape[1] // LANES
    cst = _wkv_consts()
    if has_s0:
        s_scr[...] = s0_ref[...]
    else:
        s_scr[...] = jnp.zeros_like(s_scr)
    y_ref[...] = jnp.zeros_like(y_ref)
    bon_ref[...] = jnp.zeros_like(bon_ref)
    lw_refs = (lw0_ref, lw1_ref)
    ag_refs = (ag0_ref, ag1_ref)

    def body(c, carry):
        chains = [(p, d) for p in range(npair) for d in range(2)]
        sl = {}
        for p, d in chains:
            r0 = pl.multiple_of((c if d == 0 else nch - 1 - c) * CH, CH)
            sl[p, d] = (pl.ds(r0, CH), pl.ds(p * LANES, LANES))
        args = {}
        for p, d in chains:
            rs, ls = sl[p, d]
            args[p, d] = (s_scr[d, p], r_ref[rs, ls], k_ref[rs, ls], v_ref[rs, ls], kk_ref[rs, ls],
                          lw_refs[d][rs, ls], ag_refs[d][rs, ls], ka_ref[:, ls], rk_ref[:, ls],
                          y_ref[rs, ls], bon_ref[rs, ls])
        outs = _run_lockstep([_wkv_chunk(d, *args[p, d][:9], cst) for p, d in chains])
        for (p, d), (s_new, y, bon) in zip(chains, outs):
            rs, ls = sl[p, d]
            s_scr[d, p] = s_new
            y_ref[rs, ls] = args[p, d][9] + y
            bon_ref[rs, ls] = args[p, d][10] + bon
        return carry

    lax.fori_loop(0, nch, body, 0)
    if not has_s0:
        for d in range(2):
            for p in range(npair):
                s = s_scr[d, p]
                st_ref[d, 2 * p] = s[:HD, :HD]
                st_ref[d, 2 * p + 1] = s[HD:, HD:]


def _wkv(arrs, ka, rk, T, nb, row_blk0, s0_bd=None, lanes=512):
    ng = D // lanes
    npair = lanes // LANES
    seq = pl.BlockSpec((T, lanes), lambda b, g: (row_blk0 + b, g))
    vec = pl.BlockSpec((1, lanes), lambda b, g: (0, g))
    st = pl.BlockSpec((None, 2, npair, LANES, LANES), lambda b, g: (b, 0, g, 0, 0))
    out_seq = pl.BlockSpec((T, lanes), lambda b, g: (b, g))
    in_specs = [seq] * 8 + [vec, vec]
    ins = list(arrs) + [ka, rk]
    out_specs = [out_seq, out_seq]
    out_shape = [jax.ShapeDtypeStruct((nb * T, D), F32)] * 2
    if s0_bd is not None:
        in_specs.append(st)
        ins.append(s0_bd)
    else:
        out_specs.append(pl.BlockSpec((None, 2, 2 * npair, HD, HD), lambda b, g: (b, 0, g, 0, 0)))
        out_shape.append(jax.ShapeDtypeStruct((nb, 2, NH, HD, HD), F32))
    return pl.pallas_call(
        functools.partial(_wkv_kernel, T, s0_bd is not None),
        grid=(nb, ng),
        in_specs=in_specs,
        out_specs=out_specs,
        out_shape=out_shape,
        scratch_shapes=[pltpu.VMEM((2, npair, LANES, LANES), F32)],
        compiler_params=_cp(("parallel", "parallel")),
        name="wkv_T%d" % T,
    )(*ins)


def _proj_kernel(gn, y_ref, ap_ref, as_ref, *rest):
    if gn:
        bp_ref, bs_ref, g_ref, lng_ref, lnb_ref, bo_ref, mod_ref, w_ref, o_ref = rest
    else:
        mod_ref, w_ref, o_ref = rest
    i = pl.program_id(0)
    is_p = i < NT_P
    a = jnp.where(is_p, ap_ref[...], as_ref[...])
    if gn:
        bo = bo_ref[...]
        mean = _seg_sum(a, bo) * (1.0 / HD)
        cen = a - mean
        var = _seg_sum(cen * cen, bo) * (1.0 / HD)
        yn = cen * lax.rsqrt(var + GN_EPS) * lng_ref[...] + lnb_ref[...]
        a = (yn + jnp.where(is_p, bp_ref[...], bs_ref[...])) * g_ref[...]
    gt = _mod_vec(mod_ref, _tile_mod_row(i), 2)
    o_ref[...] = y_ref[...] + gt * _bdot(a, w_ref[...])


def _proj_residual(y, a_p, a_s, mod, w, gn_args=None, name="proj"):
    tile = pl.BlockSpec((TM, D), lambda i: (i, 0))
    tile_p = pl.BlockSpec((TM, D), lambda i: (jnp.minimum(i, NT_P - 1), 0))
    tile_s = pl.BlockSpec((TM, D), lambda i: (jnp.maximum(i - NT_P, 0), 0))
    full = lambda a: pl.BlockSpec(a.shape, lambda i: (0,) * a.ndim)
    ins = [y, a_p, a_s]
    in_specs = [tile, tile_p, tile_s]
    if gn_args is not None:
        b_p, b_s, g, lng, lnb, bo = gn_args
        ins += [b_p, b_s, g, lng, lnb, bo]
        in_specs += [tile_p, tile_s, tile, full(lng), full(lnb), full(bo)]
    ins += [mod, w]
    in_specs += [full(mod), full(w)]
    return pl.pallas_call(
        functools.partial(_proj_kernel, gn_args is not None),
        grid=(NT,),
        in_specs=in_specs,
        out_specs=tile,
        out_shape=jax.ShapeDtypeStruct((NTOK, D), F32),
        compiler_params=_cp(("parallel",)),
        name=name,
    )(*ins)


def _router_kernel(y_ref, mod_ref, nrm_ref, wr_ref, br_ref, xt_o, gate_o):
    i = pl.program_id(0)
    row = _tile_mod_row(i)
    xt = _modulate(y_ref[...], nrm_ref[...], _mod_vec(mod_ref, row, 3), _mod_vec(mod_ref, row, 4))
    xt_o[...] = xt.astype(BF)
    x1, x2, x3 = _split3(xt)
    w1, w2, w3 = _split3(wr_ref[...])
    dot = lambda a, b: jnp.dot(a, b, preferred_element_type=F32)
    logits = (dot(x3, w1) + dot(x2, w2) + dot(x1, w3)) + (dot(x2, w1) + dot(x1, w2)) + dot(x1, w1)
    logits = logits + br_ref[...]
    lane = lax.broadcasted_iota(jnp.int32, logits.shape, 1).astype(F32)
    isg = jnp.logical_and(lane >= NE, lane < NE + NE // EPG)
    mg = jnp.max(jnp.where(isg, logits, NEG), axis=-1, keepdims=True)
    eg = jnp.where(isg, jnp.exp(jnp.minimum(logits - mg, 0.0)), 0.0)
    p_sel = 1.0 / jnp.sum(eg, axis=-1, keepdims=True)
    gidx = jnp.min(jnp.where(jnp.logical_and(isg, logits == mg), lane, 1e3), axis=-1, keepdims=True) - NE
    lo_l = gidx * EPG
    ing = jnp.logical_and(lane >= lo_l, lane < lo_l + EPG)
    me = jnp.max(jnp.where(ing, logits, NEG), axis=-1, keepdims=True)
    ee = jnp.where(ing, jnp.exp(jnp.minimum(logits - me, 0.0)), 0.0)
    se = jnp.sum(ee, axis=-1, keepdims=True)
    pe = ee / se
    i1 = jnp.min(jnp.where(jnp.logical_and(ing, logits == me), lane, 1e3), axis=-1, keepdims=True)
    v1 = 1.0 / se
    rest = jnp.logical_and(ing, lane != i1)
    v2 = jnp.max(jnp.where(rest, pe, -1.0), axis=-1, keepdims=True)
    i2 = jnp.min(jnp.where(jnp.logical_and(rest, pe == v2), lane, 1e3), axis=-1, keepdims=True)
    den = v1 + v2
    gate_o[...] = (jnp.where(lane == i1, p_sel * v1 / den, 0.0)
                   + jnp.where(lane == i2, p_sel * v2 / den, 0.0))


def _moe_router(y, mod, nrm, wr, br):
    tile = pl.BlockSpec((TM, D), lambda i: (i, 0))
    full = lambda a: pl.BlockSpec(a.shape, lambda i: (0,) * a.ndim)
    return pl.pallas_call(
        _router_kernel,
        grid=(NT,),
        in_specs=[tile, full(mod), full(nrm), full(wr), full(br)],
        out_specs=[tile, pl.BlockSpec((TM, LANES), lambda i: (i, 0))],
        out_shape=[jax.ShapeDtypeStruct((NTOK, D), BF), jax.ShapeDtypeStruct((NTOK, LANES), F32)],
        compiler_params=_cp(("parallel",)),
        name="moe_router",
    )(y, mod, nrm, wr, br)


def _experts_kernel(final, xt_ref, gate_ref, w13_ref, w2_ref, y_ref, mod_ref, fin_ref, *rest):
    i = pl.program_id(0)
    e = pl.program_id(1)
    acc_ref = rest[-1]
    ntile_p = NPR // TMX

    h13 = jnp.dot(xt_ref[...], w13_ref[...], preferred_element_type=F32)
    h1, h3 = h13[:, :DE], h13[:, DE:]
    lane = lax.broadcasted_iota(jnp.int32, gate_ref.shape, 1)
    ge = jnp.sum(jnp.where(lane == e, gate_ref[...], 0.0), axis=-1, keepdims=True)
    hid = (h1 * jax.nn.sigmoid(h1)) * h3 * ge
    part = jnp.dot(hid.astype(BF), w2_ref[...], preferred_element_type=F32)

    @pl.when(e == 0)
    def _():
        acc_ref[...] = part

    @pl.when(jnp.logical_and(e > 0, e < NE - 1))
    def _():
        acc_ref[...] += part

    def result():
        row = jnp.where(i < ntile_p, 0, i - ntile_p + 1)
        out = y_ref[...] + _mod_vec(mod_ref, row, 5) * (acc_ref[...] + part)
        if final:
            ms = jnp.mean(out * out, axis=-1, keepdims=True)
            out = out * lax.rsqrt(ms + NORM_EPS) * fin_ref[...]
        return out

    if final:
        op_ref, os_ref = rest[:2]

        @pl.when(jnp.logical_and(e == NE - 1, i < ntile_p))
        def _():
            op_ref[...] = result()

        @pl.when(jnp.logical_and(e == NE - 1, i >= ntile_p))
        def _():
            os_ref[...] = result()
    else:
        @pl.when(e == NE - 1)
        def _():
            rest[0][...] = result()


def _moe_experts(xt, gate, w13, w2, y, mod, fin, final):
    tile = pl.BlockSpec((TMX, D), lambda i, e: (i, 0))
    full = lambda a: pl.BlockSpec(a.shape, lambda i, e: (0,) * a.ndim)
    ntile_p = NPR // TMX
    if final:
        out_specs = [pl.BlockSpec((TMX, D), lambda i, e: (jnp.minimum(i, ntile_p - 1), 0)),
                     pl.BlockSpec((TMX, D), lambda i, e: (jnp.maximum(i - ntile_p, 0), 0))]
        out_shape = [jax.ShapeDtypeStruct((NPR, D), F32), jax.ShapeDtypeStruct((NSR, D), F32)]
    else:
        out_specs = tile
        out_shape = jax.ShapeDtypeStruct((NTOK, D), F32)
    return pl.pallas_call(
        functools.partial(_experts_kernel, final),
        grid=(NTOK // TMX, NE),
        in_specs=[tile, pl.BlockSpec((TMX, LANES), lambda i, e: (i, 0)),
                  pl.BlockSpec((None, D, 2 * DE), lambda i, e: (e, 0, 0)),
                  pl.BlockSpec((None, DE, D), lambda i, e: (e, 0, 0)),
                  tile, full(mod), full(fin)],
        out_specs=out_specs,
        out_shape=out_shape,
        scratch_shapes=[pltpu.VMEM((TMX, D), F32)],
        compiler_params=_cp(("arbitrary", "arbitrary")),
        name="moe_experts",
    )(xt, gate, w13, w2, y, mod, fin)


def _qkv_kernel(y_ref, mod_ref, nrm_ref, w_ref, q_o, k_o, v_o, kc_o, vc_o):
    i = pl.program_id(0)
    row = _tile_mod_row(i)
    h = _modulate(y_ref[...], nrm_ref[...], _mod_vec(mod_ref, row, 0), _mod_vec(mod_ref, row, 1))
    qkv = _bdot(h, w_ref[...])
    q_o[...] = qkv[:, :D]
    kx = qkv[:, D:2 * D]
    vx = qkv[:, 2 * D:]
    k_o[...] = kx
    v_o[...] = vx

    @pl.when(i < NT_P)
    def _():
        for hh in range(NH):
            kc_o[:, hh, :] = kx[:, hh * HD:(hh + 1) * HD]
            vc_o[:, hh, :] = vx[:, hh * HD:(hh + 1) * HD]


def _na_qkv(y, mod, nrm, w):
    tile = pl.BlockSpec((TM, D), lambda i: (i, 0))
    full = lambda a: pl.BlockSpec(a.shape, lambda i: (0,) * a.ndim)
    out = jax.ShapeDtypeStruct((NTOK, D), F32)
    cache = jax.ShapeDtypeStruct((NPR, NH, HD), F32)
    cache_tile = pl.BlockSpec((TM, NH, HD), lambda i: (jnp.minimum(i, NT_P - 1), 0, 0))
    return pl.pallas_call(
        _qkv_kernel,
        grid=(NT,),
        in_specs=[tile, full(mod), full(nrm), full(w)],
        out_specs=[tile] * 3 + [cache_tile] * 2,
        out_shape=[out] * 3 + [cache] * 2,
        compiler_params=_cp(("arbitrary",)),
        name="na_qkv",
    )(y, mod, nrm, w)


def _softmax_rows(s):
    m = jnp.max(s, axis=-1, keepdims=True)
    e = jnp.exp(s - m)
    return e / jnp.sum(e, axis=-1, keepdims=True)


def _ctx_attn_kernel(q_ref, k_ref, v_ref, o_ref):
    scale = HD ** -0.5
    h0 = lax.broadcasted_iota(jnp.int32, (T_P, LANES), 1) < HD
    def head_chain(h, q, kb, vb):
        qm = jnp.where(h0 if h == 0 else jnp.logical_not(h0), q, 0.0)
        s = _bdot_nt(qm, kb)
        yield
        o = jnp.dot(_softmax_rows(s * scale).astype(BF), vb, preferred_element_type=F32)
        yield
        return o

    npair = D // LANES
    for p0 in range(0, npair, CTX_PAIRS):
        gens = []
        for p in range(p0, p0 + CTX_PAIRS):
            ls = pl.ds(p * LANES, LANES)
            q = q_ref[:, ls]
            kb = k_ref[:, ls].astype(BF)
            vb = v_ref[:, ls].astype(BF)
            gens += [head_chain(h, q, kb, vb) for h in range(2)]
        outs = _run_lockstep(gens)
        for j, p in enumerate(range(p0, p0 + CTX_PAIRS)):
            o_ref[:, pl.ds(p * LANES, LANES)] = jnp.where(h0, outs[2 * j], outs[2 * j + 1])


def _ctx_attn(q, k, v):
    blk = pl.BlockSpec((T_P, D), lambda b: (b, 0))
    return pl.pallas_call(
        _ctx_attn_kernel,
        grid=(NB_P,),
        in_specs=[blk] * 3,
        out_specs=blk,
        out_shape=jax.ShapeDtypeStruct((NPR, D), F32),
        compiler_params=_cp(("parallel",)),
        name="ctx_attn",
    )(q, k, v)


def _nbr_attn_kernel(q_ref, k_ref, v_ref, kc_ref, vc_ref, tz_ref, o_ref):
    scale = HD ** -0.5
    rows = T_S // GRID_W
    nloc = WIN_R * GRID_W
    h0 = lax.broadcasted_iota(jnp.int32, (GRID_W, LANES), 1) < HD
    kcb = kc_ref[...].astype(BF)
    vcb = vc_ref[...].astype(BF)

    def head_chain(h, q, klb, vlb, j0):
        qm = jnp.where(h0 if h == 0 else jnp.logical_not(h0), q, 0.0).astype(BF)
        sl = _bdot_nt(qm, klb)
        sc = _bdot_nt(qm, kcb)
        yield
        bias = jnp.concatenate([tz_ref[h, j0 + 2 * m] for m in range(WIN_R // 2)], axis=1)
        sl = sl * scale + bias
        sc = sc * scale
        m = jnp.maximum(jnp.max(sl, axis=-1, keepdims=True), jnp.max(sc, axis=-1, keepdims=True))
        el = jnp.exp(sl - m)
        ec = jnp.exp(sc - m)
        inv = 1.0 / (jnp.sum(el, axis=-1, keepdims=True) + jnp.sum(ec, axis=-1, keepdims=True))
        o = (jnp.dot((el * inv).astype(BF), vlb, preferred_element_type=F32)
             + jnp.dot((ec * inv).astype(BF), vcb, preferred_element_type=F32))
        yield
        return o

    def body(it, carry):
        gens, slices = [], []
        for j in range(NBR_UNROLL):
            r = it * NBR_UNROLL + j
            start = jnp.clip(r - WIN_R // 2, 0, rows - WIN_R)
            qs = pl.ds(pl.multiple_of(r * GRID_W, GRID_W), GRID_W)
            ks = pl.ds(pl.multiple_of(start * GRID_W, GRID_W), nloc)
            q = q_ref[qs, :]
            klb = k_ref[ks, :].astype(BF)
            vlb = v_ref[ks, :].astype(BF)
            j0 = start - r + WIN_R - 1
            slices.append(qs)
            gens += [head_chain(h, q, klb, vlb, j0) for h in range(2)]
        outs = _run_lockstep(gens)
        for j, qs in enumerate(slices):
            o_ref[qs, :] = jnp.where(h0, outs[2 * j], outs[2 * j + 1])
        return carry

    lax.fori_loop(0, rows // NBR_UNROLL, body, 0)


def _nbr_attn(q, k, v, kc, vc, tz2):
    npair = D // LANES
    seq = pl.BlockSpec((T_S, LANES), lambda b, p: (NPR // T_S + b, p))
    ctx = pl.BlockSpec((kc.shape[0] // NB_S, LANES), lambda b, p: (b, p))
    return pl.pallas_call(
        _nbr_attn_kernel,
        grid=(NB_S, npair),
        in_specs=[seq, seq, seq, ctx, ctx,
                  pl.BlockSpec((2,) + tz2.shape[1:], lambda b, p: (p, 0, 0, 0))],
        out_specs=pl.BlockSpec((T_S, LANES), lambda b, p: (b, p)),
        out_shape=jax.ShapeDtypeStruct((NSR, D), F32),
        compiler_params=_cp(("parallel", "parallel")),
        name="nbr_attn",
    )(q, k, v, kc, vc, tz2)


def _block_diag2(a, b):
    z = jnp.zeros_like(a)
    return jnp.concatenate([jnp.concatenate([a, z], axis=1), jnp.concatenate([z, b], axis=1)], axis=0)


def _nbr_bias_table(rpb):
    qc = jnp.arange(GRID_W)[:, None]
    kc = jnp.arange(GRID_W)[None, :]
    cstart = jnp.clip(qc - WIN_C // 2, 0, GRID_W - WIN_C)
    valid = jnp.logical_and(kc >= cstart, kc < cstart + WIN_C)
    dc = jnp.clip(kc - qc, -(WIN_C - 1), WIN_C - 1) + WIN_C - 1
    tz = jnp.where(valid, rpb[:, :, dc], NEG)
    return jnp.concatenate([tz[:, :-1], tz[:, 1:]], axis=-1)


def kernel(x_prompt, x_sample, c, state_rwkv, cache_na_k, cache_na_v, c_ctx, norm_mix, norm_ffn, ada_w, ada_b, rwkv_mu, rwkv_w_r, rwkv_w_k, rwkv_w_v, rwkv_w_o, rwkv_w0, rwkv_w1, rwkv_w2, rwkv_a0, rwkv_a1, rwkv_a2, rwkv_g1, rwkv_g2, rwkv_k_k, rwkv_k_a, rwkv_r_k, rwkv_lnx_g, rwkv_lnx_b, na_w_qkv, na_w_o, na_rpb, moe_w_grp, moe_b_grp, moe_w_exp, moe_b_exp, moe_w1, moe_w3, moe_w2, final_norm):
    y = jnp.concatenate([x_prompt.reshape(NPR, D), x_sample.reshape(NSR, D)], axis=0)
    c8 = jnp.concatenate([c_ctx[None, :], c, jnp.zeros((8 - 1 - NB_S, D), F32)], axis=0)
    mod = _adaln(c8, ada_w, ada_b)
    ri = jnp.arange(2 * LANES)[:, None] // HD
    bo = (ri == ri.T).astype(BF)
    row = lambda a: a.reshape(1, D)

    w1c = jnp.concatenate([rwkv_w1[0, 0], rwkv_w1[0, 1]], axis=1).astype(BF)
    a1c = jnp.concatenate([rwkv_a1[0, 0], rwkv_a1[0, 1]], axis=1).astype(BF)
    w2bd = _block_diag2(rwkv_w2[0, 0], rwkv_w2[0, 1]).astype(BF)
    a2bd = _block_diag2(rwkv_a2[0, 0], rwkv_a2[0, 1]).astype(BF)
    r, k, v, g, kk, lw0, lw1, ag0, ag1 = _rwkv_front(
        y, mod[0], row(norm_mix[0]), rwkv_mu[0], rwkv_w_r[0].astype(BF), rwkv_w_k[0].astype(BF),
        rwkv_w_v[0].astype(BF), w1c, a1c, rwkv_g1[0].astype(BF), w2bd, a2bd, rwkv_g2[0].astype(BF),
        rwkv_w0[0], rwkv_a0[0], row(rwkv_k_k[0]), bo)
    arrs = (r, k, v, kk, lw0, lw1, ag0, ag1)
    ka, rk = row(rwkv_k_a[0]), row(rwkv_r_k[0])
    ys_p, bon_p, new_state = _wkv(arrs, ka, rk, T_P, NB_P, 0)
    s0 = state_rwkv[:, 0].reshape(NB_S, 2, NH // 2, 2, HD, HD)
    z = jnp.zeros_like(s0[:, :, :, 0])
    s0_bd = jnp.concatenate([jnp.concatenate([s0[:, :, :, 0], z], axis=-1),
                             jnp.concatenate([z, s0[:, :, :, 1]], axis=-1)], axis=-2)
    ys_s, bon_s = _wkv(arrs, ka, rk, T_S, NB_S, NPR // T_S, s0_bd=s0_bd)
    new_state_rwkv = new_state.reshape(NB_P, 1, 2, NH, HD, HD)
    y = _proj_residual(y, ys_p, ys_s, mod[0], rwkv_w_o[0].astype(BF),
                       gn_args=(bon_p, bon_s, g, row(rwkv_lnx_g[0]), row(rwkv_lnx_b[0]), bo),
                       name="rwkv_out")

    def moe(y, i, final):
        wr = jnp.concatenate([moe_w_exp[i], moe_w_grp[i],
                              jnp.zeros((D, LANES - NE - NE // EPG), F32)], axis=1)
        br = jnp.concatenate([moe_b_exp[i], moe_b_grp[i],
                              jnp.zeros((LANES - NE - NE // EPG,), F32)]).reshape(1, LANES)
        xt, gate = _moe_router(y, mod[i], row(norm_ffn[i]), wr, br)
        w13 = jnp.concatenate([moe_w1[i], moe_w3[i]], axis=-1).astype(BF)
        return _moe_experts(xt, gate, w13, moe_w2[i].astype(BF), y, mod[i], row(final_norm), final)

    y = moe(y, 0, False)

    q, k, v, kcache, vcache = _na_qkv(y, mod[1], row(norm_mix[1]), na_w_qkv[0].astype(BF))
    o_p = _ctx_attn(q, k, v)
    kc = cache_na_k[:, 0].reshape(NB_S * cache_na_k.shape[2], D)
    vc = cache_na_v[:, 0].reshape(NB_S * cache_na_v.shape[2], D)
    o_s = _nbr_attn(q, k, v, kc, vc, _nbr_bias_table(na_rpb[0]))
    y = _proj_residual(y, o_p, o_s, mod[1], na_w_o[0].astype(BF), name="attn_out")
    y_p, y_s = moe(y, 1, True)

    y_prompt = y_p.reshape(NB_P, T_P, D)
    y_sample = y_s.reshape(NB_S, T_S, D)
    new_k = kcache.reshape(NB_P, 1, T_P, NH, HD)
    new_v = vcache.reshape(NB_P, 1, T_P, NH, HD)
    return (y_prompt, y_sample, new_state_rwkv, new_k, new_v)
```

```python
import functools
import math

import jax
import jax.numpy as jnp
from jax import lax
from jax.experimental import pallas as pl
from jax.experimental.pallas import tpu as pltpu

F32 = jnp.float32
BF = jnp.bfloat16

D = 1024
NH = 16
HD = 64
NB_P, T_P = 32, 256
NB_S, T_S = 2, 1024
NPR = NB_P * T_P
NSR = NB_S * T_S
NTOK = NPR + NSR
TM = 256
NT = NTOK // TM
NT_P = NPR // TM
TPS = T_S // TM
TMX = 1024
NMOD = 6
NE = 16
EPG = 4
DE = 256
CH = 64
SB = 16
LANES = 128
GRID_W = 64
WIN_R, WIN_C = 8, 16
NBR_UNROLL = 4
CTX_PAIRS = 2
NORM_EPS = 1e-6
GN_EPS = 64e-5
NEG = -1e30
VMEM_LIMIT = 56 * 1024 * 1024


def _cp(sem):
    return pltpu.CompilerParams(dimension_semantics=sem, vmem_limit_bytes=VMEM_LIMIT)


def _bdot(a, b):
    return jnp.dot(a.astype(BF), b.astype(BF), preferred_element_type=F32)


def _bdot_nt(a, b):
    return lax.dot_general(a.astype(BF), b.astype(BF), (((1,), (1,)), ((), ())),
                           preferred_element_type=F32)


def _split2(x):
    hi = x.astype(BF)
    lo = (x - hi.astype(F32)).astype(BF)
    return hi, lo


def _split3(x):
    hi = x.astype(BF)
    r1 = x - hi.astype(F32)
    mid = r1.astype(BF)
    lo = (r1 - mid.astype(F32)).astype(BF)
    return hi, mid, lo


def _seg_sum(x, bo):
    outs = []
    for c in range(x.shape[1] // 256):
        hi, lo = _split2(x[:, c * 256:(c + 1) * 256])
        outs.append(jnp.dot(hi, bo, preferred_element_type=F32)
                    + jnp.dot(lo, bo, preferred_element_type=F32))
    return outs[0] if len(outs) == 1 else jnp.concatenate(outs, axis=1)


def _tile_mod_row(i):
    return jnp.where(i < NT_P, 0, 1 + (i - NT_P) // TPS)


def _modulate(x, g, sh, sc):
    ms = jnp.mean(x * x, axis=-1, keepdims=True)
    return x * lax.rsqrt(ms + NORM_EPS) * g * (1.0 + sc) + sh


def _mod_vec(mod_ref, row, k):
    return mod_ref[pl.ds(row, 1), pl.ds(k * D, D)]


def _adaln_kernel(c_ref, w_ref, b_ref, o_ref):
    c = c_ref[...]
    s = c * jax.nn.sigmoid(c)
    o_ref[...] = _bdot(s, w_ref[...]) + b_ref[...]


def _adaln(c8, ada_w, ada_b):
    nl = ada_w.shape[0]
    tn = 1536
    return pl.pallas_call(
        _adaln_kernel,
        grid=(nl, NMOD * D // tn),
        in_specs=[pl.BlockSpec((8, D), lambda l, j: (0, 0)),
                  pl.BlockSpec((None, D, tn), lambda l, j: (l, 0, j)),
                  pl.BlockSpec((None, 1, tn), lambda l, j: (l, 0, j))],
        out_specs=pl.BlockSpec((None, 8, tn), lambda l, j: (l, 0, j)),
        out_shape=jax.ShapeDtypeStruct((nl, 8, NMOD * D), F32),
        compiler_params=_cp(("parallel", "parallel")),
        name="adaln",
    )(c8, ada_w, ada_b.reshape(nl, 1, NMOD * D))


def _front_kernel(xp_ref, xs_ref, yp_ref, yn_ref, mod_ref, nrm_ref, mu_ref, wr_ref, wk_ref, wv_ref,
                  w1_ref, a1_ref, g1_ref, w2_ref, a2_ref, g2_ref, w0_ref, a0_ref, kkw_ref, bo_ref,
                  r_o, k_o, v_o, g_o, kk_o, lw0_o, lw1_o, ag0_o, ag1_o):
    i = pl.program_id(0)
    row = _tile_mod_row(i)
    q = (i - NT_P) % TPS
    first = jnp.logical_or(i < NT_P, q == 0)
    last = jnp.logical_or(i < NT_P, q == TPS - 1)
    sh = _mod_vec(mod_ref, row, 0)
    sc = _mod_vec(mod_ref, row, 1)
    g = nrm_ref[...]
    h = _modulate(jnp.where(i < NT_P, xp_ref[...], xs_ref[...]), g, sh, sc)
    hp = _modulate(yp_ref[...], g, sh, sc)[7:8]
    hn = _modulate(yn_ref[...], g, sh, sc)[0:1]
    hp = jnp.where(first, 0.0, hp)
    hn = jnp.where(last, 0.0, hn)
    rid = lax.broadcasted_iota(jnp.int32, (TM, D), 0)
    prev = jnp.where(rid == 0, hp, pltpu.roll(h, 1, 0))
    nxt = jnp.where(rid == TM - 1, hn, pltpu.roll(h, TM - 1, 0))
    dx = 0.5 * (prev + nxt) - h

    def mix(n):
        return (h + dx * mu_ref[n:n + 1, :]).astype(BF)

    r_o[...] = jnp.dot(mix(0), wr_ref[...], preferred_element_type=F32)
    xw = mix(1)
    lora_w = _bdot(jnp.tanh(jnp.dot(xw, w1_ref[...], preferred_element_type=F32)), w2_ref[...])
    c05 = math.exp(-0.5)
    lw0_o[...] = -c05 * jax.nn.sigmoid(w0_ref[0:1, :] + lora_w[:, :D])
    lw1_o[...] = -c05 * jax.nn.sigmoid(w0_ref[1:2, :] + lora_w[:, D:])
    k_raw = jnp.dot(mix(2), wk_ref[...], preferred_element_type=F32)
    k_o[...] = k_raw
    kq = k_raw * kkw_ref[...]
    ss = _seg_sum(kq * kq, bo_ref[...])
    kk_o[...] = kq / jnp.maximum(jnp.sqrt(ss), 1e-12)
    v_o[...] = jnp.dot(mix(3), wv_ref[...], preferred_element_type=F32)
    xa = mix(4)
    lora_a = _bdot(jnp.dot(xa, a1_ref[...], preferred_element_type=F32), a2_ref[...])
    ag0_o[...] = jax.nn.sigmoid(a0_ref[0:1, :] + lora_a[:, :D])
    ag1_o[...] = jax.nn.sigmoid(a0_ref[1:2, :] + lora_a[:, D:])
    xg = mix(5)
    g_o[...] = _bdot(jax.nn.sigmoid(jnp.dot(xg, g1_ref[...], preferred_element_type=F32)), g2_ref[...])


def _rwkv_front(x_p, x_s, mod, nrm, mu, wr, wk, wv, w1c, a1c, g1, w2bd, a2bd, g2, w0, a0, kkw, bo):
    tile = pl.BlockSpec((TM, D), lambda i: (i, 0))
    nblk8 = NSR // 8
    bpt = TM // 8
    full = lambda a: pl.BlockSpec(a.shape, lambda i: (0,) * a.ndim)
    ins = [x_p, x_s, x_s, x_s, mod, nrm, mu, wr, wk, wv, w1c, a1c, g1, w2bd, a2bd, g2, w0, a0, kkw, bo]
    in_specs = [pl.BlockSpec((TM, D), lambda i: (jnp.minimum(i, NT_P - 1), 0)),
                pl.BlockSpec((TM, D), lambda i: (jnp.maximum(i - NT_P, 0), 0)),
                pl.BlockSpec((8, D), lambda i: (jnp.clip((i - NT_P) * bpt - 1, 0, nblk8 - 1), 0)),
                pl.BlockSpec((8, D), lambda i: (jnp.clip((i - NT_P + 1) * bpt, 0, nblk8 - 1), 0))]
    in_specs += [full(a) for a in ins[4:]]
    out = jax.ShapeDtypeStruct((NTOK, D), F32)
    return pl.pallas_call(
        _front_kernel,
        grid=(NT,),
        in_specs=in_specs,
        out_specs=[tile] * 9,
        out_shape=[out] * 9,
        compiler_params=_cp(("parallel",)),
        name="rwkv_front",
    )(*ins)


def _wkv_chunk(d, S, r, k, v, kk, lw, ag, ka, rk, cst):
    tri4, strict, incl, h0, bd, bo2, same_blk, bd_sb, eye_c, lane_blk = cst
    dot = lambda a, b: jnp.dot(a, b, preferred_element_type=F32)
    nblk = CH // SB
    h0s = lax.broadcasted_iota(jnp.int32, (SB, LANES), 1) < HD

    def stack_heads(x, swap=False):
        zero = jnp.zeros_like(x)
        parts = [jnp.where(h0, x, zero), jnp.where(h0, zero, x)]
        return jnp.concatenate(parts[::-1] if swap else parts, axis=0)

    def dot_split(a, bm):
        a_hi, a_lo = _split2(a)
        b_hi, b_lo = _split2(bm)
        return dot(a_hi, b_hi) + (dot(a_hi, b_lo) + dot(a_lo, b_hi))

    def expand(mc):
        return jnp.where(bd_sb, jnp.concatenate([mc] * (LANES // SB), axis=0), 0.0)

    b = kk * ag
    kd = k * (1.0 + (ag - 1.0) * ka)
    hi, mid, lo = _split3(lw)
    cs = dot(tri4[d], jnp.concatenate([hi, mid, lo, jnp.zeros_like(lo)], axis=0))
    qh, ql = _split2(r * kd * rk)
    bonus = (dot(qh, bo2) + dot(ql, bo2)) * v
    yield
    tot = cs[CH - 1:CH, :] if d == 0 else cs[0:1, :]
    p_inv = jnp.exp(-cs)
    p_end = jnp.exp(tot - cs)
    at = -kk * jnp.exp(cs - lw)
    rt = r * jnp.exp(cs)
    bt = (b * p_inv).astype(BF)
    kt = (kd * p_inv).astype(BF)
    ar = jnp.concatenate([at, rt], axis=0)
    h02 = jnp.concatenate([h0, h0], axis=0)
    g0 = _bdot_nt(jnp.where(h02, ar, 0.0), jnp.concatenate([bt, kt], axis=0))
    g1 = _bdot_nt(jnp.where(h02, 0.0, ar), jnp.concatenate([kt, bt], axis=0))
    w0y0 = _bdot_nt(ar, S)
    yield
    w0, y0 = w0y0[:CH], w0y0[CH:]
    lab = jnp.where(strict[d], jnp.where(h0, g0[:CH], g1[:CH]), 0.0)
    lak = jnp.where(strict[d], jnp.where(h0, g1[:CH], g0[:CH]), 0.0)
    mrb = jnp.where(incl[d], jnp.where(h0, g0[CH:], g1[CH:]), 0.0)
    mrk = jnp.where(incl[d], jnp.where(h0, g1[CH:], g0[CH:]), 0.0)
    vstk_sw = stack_heads(v, swap=True).astype(BF)
    w = w0 + dot(lak.astype(BF), vstk_sw)
    ldiag = jnp.where(same_blk, lab, 0.0)
    loff = jnp.where(same_blk, 0.0, lab)
    mc = ldiag[0:SB]
    for i in range(1, nblk):
        mc = mc + ldiag[i * SB:(i + 1) * SB]
    xc = eye_c + mc
    mc = dot_split(mc, expand(mc))
    yield
    nsq = int(math.log2(SB)) - 1
    for it in range(nsq):
        last = it + 1 == nsq
        res = dot_split(xc if last else jnp.concatenate([xc, mc], axis=0), expand(mc))
        xc = xc + res[:SB]
        if not last:
            mc = res[SB:]
        yield
    ublk = [None] * nblk
    order = list(range(nblk)) if d == 0 else list(range(nblk - 1, -1, -1))
    for n, i in enumerate(order):
        wi = w[i * SB:(i + 1) * SB]
        if n > 0:
            ucur = jnp.concatenate([jnp.zeros((SB, LANES), F32) if ub is None else ub
                                    for ub in ublk], axis=0)
            wi = wi + dot(loff[i * SB:(i + 1) * SB].astype(BF), stack_heads(ucur).astype(BF))
            yield
        m0 = jnp.where(h0s, wi, 0.0).astype(BF)
        m1 = jnp.where(h0s, 0.0, wi).astype(BF)
        rhs = jnp.concatenate([m0] * nblk + [m1] * nblk, axis=0)
        ublk[i] = dot(jnp.where(lane_blk[i], xc, 0.0).astype(BF), rhs)
        yield
    u = jnp.concatenate(ublk, axis=0)
    y = dot(jnp.concatenate([mrb, mrk], axis=1).astype(BF),
            jnp.concatenate([stack_heads(u).astype(BF), vstk_sw], axis=0))
    uvt = jnp.concatenate([u, v], axis=0).T
    ds = _bdot(uvt, jnp.concatenate([b * p_end, kd * p_end], axis=0))
    yield
    s_new = S * jnp.exp(tot) + jnp.where(bd, ds, 0.0)
    return s_new, y0 + y, bonus


def _run_lockstep(gens):
    results = [None] * len(gens)
    pending = list(range(len(gens)))
    while pending:
        for i in list(pending):
            try:
                next(gens[i])
            except StopIteration as stop:
                results[i] = stop.value
                pending.remove(i)
    return results


def _wkv_consts():
    t = lax.broadcasted_iota(jnp.int32, (CH, 4 * CH), 0)
    s = lax.broadcasted_iota(jnp.int32, (CH, 4 * CH), 1) & (CH - 1)
    tri4 = [jnp.where(s <= t, 1.0, 0.0).astype(BF), jnp.where(s >= t, 1.0, 0.0).astype(BF)]
    t2 = lax.broadcasted_iota(jnp.int32, (CH, LANES), 0)
    l2 = lax.broadcasted_iota(jnp.int32, (CH, LANES), 1)
    s2 = l2 & (CH - 1)
    strict = [s2 < t2, s2 > t2]
    incl = [s2 <= t2, s2 >= t2]
    h0 = l2 < HD
    same_blk = (t2 // SB) == (s2 // SB)
    ri = lax.broadcasted_iota(jnp.int32, (LANES, LANES), 0)
    ci = lax.broadcasted_iota(jnp.int32, (LANES, LANES), 1)
    bd = (ri // HD) == (ci // HD)
    bo2 = jnp.where(bd, 1.0, 0.0).astype(BF)
    bd_sb = (ri // SB) == (ci // SB)
    t3 = lax.broadcasted_iota(jnp.int32, (SB, LANES), 0)
    l3 = lax.broadcasted_iota(jnp.int32, (SB, LANES), 1)
    eye_c = jnp.where((l3 & (SB - 1)) == t3, 1.0, 0.0)
    lane_blk = [((l3 & (CH - 1)) // SB) == i for i in range(CH // SB)]
    return tri4, strict, incl, h0, bd, bo2, same_blk, bd_sb, eye_c, lane_blk


def _wkv_kernel(T, has_s0, *refs):
    (r_ref, k_ref, v_ref, kk_ref, lw0_ref, lw1_ref, ag0_ref, ag1_ref, ka_ref, rk_ref), rest = refs[:10], refs[10:]
    if has_s0:
        s0_ref, y_ref, bon_ref, s_scr = rest
    else:
        y_ref, bon_ref, st_ref, s_scr = rest
    nch = T // CH
    npair = y_ref.shape[1] // LANES
    cst = _wkv_consts()
    if has_s0:
        s_scr[...] = s0_ref[...]
    else:
        s_scr[...] = jnp.zeros_like(s_scr)
    y_ref[...] = jnp.zeros_like(y_ref)
    bon_ref[...] = jnp.zeros_like(bon_ref)
    lw_refs = (lw0_ref, lw1_ref)
    ag_refs = (ag0_ref, ag1_ref)

    def body(c, carry):
        chains = [(p, d) for p in range(npair) for d in range(2)]
        sl = {}
        for p, d in chains:
            r0 = pl.multiple_of((c if d == 0 else nch - 1 - c) * CH, CH)
            sl[p, d] = (pl.ds(r0, CH), pl.ds(p * LANES, LANES))
        args = {}
        for p, d in chains:
            rs, ls = sl[p, d]
            args[p, d] = (s_scr[d, p], r_ref[rs, ls], k_ref[rs, ls], v_ref[rs, ls], kk_ref[rs, ls],
                          lw_refs[d][rs, ls], ag_refs[d][rs, ls], ka_ref[:, ls], rk_ref[:, ls],
                          y_ref[rs, ls], bon_ref[rs, ls])
        outs = _run_lockstep([_wkv_chunk(d, *args[p, d][:9], cst) for p, d in chains])
        for (p, d), (s_new, y, bon) in zip(chains, outs):
            rs, ls = sl[p, d]
            s_scr[d, p] = s_new
            y_ref[rs, ls] = args[p, d][9] + y
            bon_ref[rs, ls] = args[p, d][10] + bon
        return carry

    lax.fori_loop(0, nch, body, 0)
    if not has_s0:
        for d in range(2):
            for p in range(npair):
                s = s_scr[d, p]
                st_ref[d, 2 * p] = s[:HD, :HD]
                st_ref[d, 2 * p + 1] = s[HD:, HD:]


def _wkv(arrs, ka, rk, T, nb, row_blk0, s0_bd=None, lanes=512):
    ng = D // lanes
    npair = lanes // LANES
    seq = pl.BlockSpec((T, lanes), lambda b, g: (row_blk0 + b, g))
    vec = pl.BlockSpec((1, lanes), lambda b, g: (0, g))
    st = pl.BlockSpec((None, 2, npair, LANES, LANES), lambda b, g: (b, 0, g, 0, 0))
    out_seq = pl.BlockSpec((T, lanes), lambda b, g: (b, g))
    in_specs = [seq] * 8 + [vec, vec]
    ins = list(arrs) + [ka, rk]
    out_specs = [out_seq, out_seq]
    out_shape = [jax.ShapeDtypeStruct((nb * T, D), F32)] * 2
    if s0_bd is not None:
        in_specs.append(st)
        ins.append(s0_bd)
    else:
        out_specs.append(pl.BlockSpec((None, 2, 2 * npair, HD, HD), lambda b, g: (b, 0, g, 0, 0)))
        out_shape.append(jax.ShapeDtypeStruct((nb, 2, NH, HD, HD), F32))
    return pl.pallas_call(
        functools.partial(_wkv_kernel, T, s0_bd is not None),
        grid=(nb, ng),
        in_specs=in_specs,
        out_specs=out_specs,
        out_shape=out_shape,
        scratch_shapes=[pltpu.VMEM((2, npair, LANES, LANES), F32)],
        compiler_params=_cp(("parallel", "parallel")),
        name="wkv_T%d" % T,
    )(*ins)


def _proj_kernel(gn, *refs):
    i = pl.program_id(0)
    is_p = i < NT_P
    if gn:
        (yp_ref, ys_ref, ap_ref, as_ref, bp_ref, bs_ref, g_ref, lng_ref, lnb_ref, bo_ref,
         mod_ref, w_ref, o_ref) = refs
        y = jnp.where(is_p, yp_ref[...], ys_ref[...])
    else:
        y_ref, ap_ref, as_ref, mod_ref, w_ref, o_ref = refs
        y = y_ref[...]
    a = jnp.where(is_p, ap_ref[...], as_ref[...])
    if gn:
        bo = bo_ref[...]
        mean = _seg_sum(a, bo) * (1.0 / HD)
        cen = a - mean
        var = _seg_sum(cen * cen, bo) * (1.0 / HD)
        yn = cen * lax.rsqrt(var + GN_EPS) * lng_ref[...] + lnb_ref[...]
        a = (yn + jnp.where(is_p, bp_ref[...], bs_ref[...])) * g_ref[...]
    gt = _mod_vec(mod_ref, _tile_mod_row(i), 2)
    o_ref[...] = y + gt * _bdot(a, w_ref[...])


def _proj_residual(y, a_p, a_s, mod, w, gn_args=None, name="proj"):
    tile = pl.BlockSpec((TM, D), lambda i: (i, 0))
    tile_p = pl.BlockSpec((TM, D), lambda i: (jnp.minimum(i, NT_P - 1), 0))
    tile_s = pl.BlockSpec((TM, D), lambda i: (jnp.maximum(i - NT_P, 0), 0))
    full = lambda a: pl.BlockSpec(a.shape, lambda i: (0,) * a.ndim)
    if gn_args is not None:
        ins = [y[0], y[1], a_p, a_s]
        in_specs = [tile_p, tile_s, tile_p, tile_s]
    else:
        ins = [y, a_p, a_s]
        in_specs = [tile, tile_p, tile_s]
    if gn_args is not None:
        b_p, b_s, g, lng, lnb, bo = gn_args
        ins += [b_p, b_s, g, lng, lnb, bo]
        in_specs += [tile_p, tile_s, tile, full(lng), full(lnb), full(bo)]
    ins += [mod, w]
    in_specs += [full(mod), full(w)]
    return pl.pallas_call(
        functools.partial(_proj_kernel, gn_args is not None),
        grid=(NT,),
        in_specs=in_specs,
        out_specs=tile,
        out_shape=jax.ShapeDtypeStruct((NTOK, D), F32),
        compiler_params=_cp(("parallel",)),
        name=name,
    )(*ins)


def _router_kernel(y_ref, mod_ref, nrm_ref, wr_ref, br_ref, xt_o, gate_o):
    i = pl.program_id(0)
    row = _tile_mod_row(i)
    xt = _modulate(y_ref[...], nrm_ref[...], _mod_vec(mod_ref, row, 3), _mod_vec(mod_ref, row, 4))
    xt_o[...] = xt.astype(BF)
    x1, x2, x3 = _split3(xt)
    w1, w2, w3 = _split3(wr_ref[...])
    dot = lambda a, b: jnp.dot(a, b, preferred_element_type=F32)
    logits = (dot(x3, w1) + dot(x2, w2) + dot(x1, w3)) + (dot(x2, w1) + dot(x1, w2)) + dot(x1, w1)
    logits = logits + br_ref[...]
    lane = lax.broadcasted_iota(jnp.int32, logits.shape, 1).astype(F32)
    isg = jnp.logical_and(lane >= NE, lane < NE + NE // EPG)
    mg = jnp.max(jnp.where(isg, logits, NEG), axis=-1, keepdims=True)
    eg = jnp.where(isg, jnp.exp(jnp.minimum(logits - mg, 0.0)), 0.0)
    p_sel = 1.0 / jnp.sum(eg, axis=-1, keepdims=True)
    gidx = jnp.min(jnp.where(jnp.logical_and(isg, logits == mg), lane, 1e3), axis=-1, keepdims=True) - NE
    lo_l = gidx * EPG
    ing = jnp.logical_and(lane >= lo_l, lane < lo_l + EPG)
    me = jnp.max(jnp.where(ing, logits, NEG), axis=-1, keepdims=True)
    ee = jnp.where(ing, jnp.exp(jnp.minimum(logits - me, 0.0)), 0.0)
    se = jnp.sum(ee, axis=-1, keepdims=True)
    pe = ee / se
    i1 = jnp.min(jnp.where(jnp.logical_and(ing, logits == me), lane, 1e3), axis=-1, keepdims=True)
    v1 = 1.0 / se
    rest = jnp.logical_and(ing, lane != i1)
    v2 = jnp.max(jnp.where(rest, pe, -1.0), axis=-1, keepdims=True)
    i2 = jnp.min(jnp.where(jnp.logical_and(rest, pe == v2), lane, 1e3), axis=-1, keepdims=True)
    den = v1 + v2
    gate_o[...] = (jnp.where(lane == i1, p_sel * v1 / den, 0.0)
                   + jnp.where(lane == i2, p_sel * v2 / den, 0.0))


def _moe_router(y, mod, nrm, wr, br):
    tile = pl.BlockSpec((TM, D), lambda i: (i, 0))
    full = lambda a: pl.BlockSpec(a.shape, lambda i: (0,) * a.ndim)
    return pl.pallas_call(
        _router_kernel,
        grid=(NT,),
        in_specs=[tile, full(mod), full(nrm), full(wr), full(br)],
        out_specs=[tile, pl.BlockSpec((TM, LANES), lambda i: (i, 0))],
        out_shape=[jax.ShapeDtypeStruct((NTOK, D), BF), jax.ShapeDtypeStruct((NTOK, LANES), F32)],
        compiler_params=_cp(("parallel",)),
        name="moe_router",
    )(y, mod, nrm, wr, br)


def _experts_kernel(final, xt_ref, gate_ref, w13_ref, w2_ref, y_ref, mod_ref, fin_ref, *rest):
    i = pl.program_id(0)
    e = pl.program_id(1)
    acc_ref = rest[-1]
    ntile_p = NPR // TMX

    h13 = jnp.dot(xt_ref[...], w13_ref[...], preferred_element_type=F32)
    h1, h3 = h13[:, :DE], h13[:, DE:]
    lane = lax.broadcasted_iota(jnp.int32, gate_ref.shape, 1)
    ge = jnp.sum(jnp.where(lane == e, gate_ref[...], 0.0), axis=-1, keepdims=True)
    hid = (h1 * jax.nn.sigmoid(h1)) * h3 * ge

    @pl.when(e == 0)
    def _():
        acc_ref[...] = jnp.zeros_like(acc_ref)

    acc_ref[...] += jnp.dot(hid.astype(BF), w2_ref[...], preferred_element_type=F32)

    def result():
        row = jnp.where(i < ntile_p, 0, i - ntile_p + 1)
        out = y_ref[...] + _mod_vec(mod_ref, row, 5) * acc_ref[...]
        if final:
            ms = jnp.mean(out * out, axis=-1, keepdims=True)
            out = out * lax.rsqrt(ms + NORM_EPS) * fin_ref[...]
        return out

    if final:
        op_ref, os_ref = rest[:2]

        @pl.when(jnp.logical_and(e == NE - 1, i < ntile_p))
        def _():
            op_ref[...] = result()

        @pl.when(jnp.logical_and(e == NE - 1, i >= ntile_p))
        def _():
            os_ref[...] = result()
    else:
        @pl.when(e == NE - 1)
        def _():
            rest[0][...] = result()


def _moe_experts(xt, gate, w13, w2, y, mod, fin, final):
    tile = pl.BlockSpec((TMX, D), lambda i, e: (i, 0))
    full = lambda a: pl.BlockSpec(a.shape, lambda i, e: (0,) * a.ndim)
    ntile_p = NPR // TMX
    if final:
        out_specs = [pl.BlockSpec((TMX, D), lambda i, e: (jnp.minimum(i, ntile_p - 1), 0)),
                     pl.BlockSpec((TMX, D), lambda i, e: (jnp.maximum(i - ntile_p, 0), 0))]
        out_shape = [jax.ShapeDtypeStruct((NPR, D), F32), jax.ShapeDtypeStruct((NSR, D), F32)]
    else:
        out_specs = tile
        out_shape = jax.ShapeDtypeStruct((NTOK, D), F32)
    return pl.pallas_call(
        functools.partial(_experts_kernel, final),
        grid=(NTOK // TMX, NE),
        in_specs=[tile, pl.BlockSpec((TMX, LANES), lambda i, e: (i, 0)),
                  pl.BlockSpec((None, D, 2 * DE), lambda i, e: (e, 0, 0)),
                  pl.BlockSpec((None, DE, D), lambda i, e: (e, 0, 0)),
                  tile, full(mod), full(fin)],
        out_specs=out_specs,
        out_shape=out_shape,
        scratch_shapes=[pltpu.VMEM((TMX, D), F32)] if final else [],
        compiler_params=_cp(("arbitrary", "arbitrary")),
        name="moe_experts",
    )(xt, gate, w13, w2, y, mod, fin)


def _qkv_kernel(y_ref, mod_ref, nrm_ref, w_ref, q_o, kp_o, vp_o, ks_o, vs_o):
    i = pl.program_id(0)
    row = _tile_mod_row(i)
    h = _modulate(y_ref[...], nrm_ref[...], _mod_vec(mod_ref, row, 0), _mod_vec(mod_ref, row, 1))
    qkv = _bdot(h, w_ref[...])
    q_o[...] = qkv[:, :D]

    @pl.when(i < NT_P)
    def _():
        kp_o[...] = qkv[:, D:2 * D]
        vp_o[...] = qkv[:, 2 * D:]

    @pl.when(i >= NT_P)
    def _():
        ks_o[...] = qkv[:, D:2 * D]
        vs_o[...] = qkv[:, 2 * D:]


def _na_qkv(y, mod, nrm, w):
    tile = pl.BlockSpec((TM, D), lambda i: (i, 0))
    tile_p = pl.BlockSpec((TM, D), lambda i: (jnp.minimum(i, NT_P - 1), 0))
    tile_s = pl.BlockSpec((TM, D), lambda i: (jnp.maximum(i - NT_P, 0), 0))
    full = lambda a: pl.BlockSpec(a.shape, lambda i: (0,) * a.ndim)
    out = lambda n: jax.ShapeDtypeStruct((n, D), F32)
    return pl.pallas_call(
        _qkv_kernel,
        grid=(NT,),
        in_specs=[tile, full(mod), full(nrm), full(w)],
        out_specs=[tile, tile_p, tile_p, tile_s, tile_s],
        out_shape=[out(NTOK), out(NPR), out(NPR), out(NSR), out(NSR)],
        compiler_params=_cp(("arbitrary",)),
        name="na_qkv",
    )(y, mod, nrm, w)


def _softmax_rows(s):
    m = jnp.max(s, axis=-1, keepdims=True)
    e = jnp.exp(s - m)
    return e / jnp.sum(e, axis=-1, keepdims=True)


def _ctx_attn_kernel(q_ref, k_ref, v_ref, o_ref):
    scale = HD ** -0.5
    h0 = lax.broadcasted_iota(jnp.int32, (T_P, LANES), 1) < HD
    def head_chain(h, q, kb, vb):
        qm = jnp.where(h0 if h == 0 else jnp.logical_not(h0), q, 0.0)
        s = _bdot_nt(qm, kb)
        yield
        o = jnp.dot(_softmax_rows(s * scale).astype(BF), vb, preferred_element_type=F32)
        yield
        return o

    npair = D // LANES
    for p0 in range(0, npair, CTX_PAIRS):
        gens = []
        for p in range(p0, p0 + CTX_PAIRS):
            ls = pl.ds(p * LANES, LANES)
            q = q_ref[:, ls]
            kb = k_ref[:, ls].astype(BF)
            vb = v_ref[:, ls].astype(BF)
            gens += [head_chain(h, q, kb, vb) for h in range(2)]
        outs = _run_lockstep(gens)
        for j, p in enumerate(range(p0, p0 + CTX_PAIRS)):
            o_ref[:, pl.ds(p * LANES, LANES)] = jnp.where(h0, outs[2 * j], outs[2 * j + 1])


def _ctx_attn(q, k, v):
    blk = pl.BlockSpec((T_P, D), lambda b: (b, 0))
    return pl.pallas_call(
        _ctx_attn_kernel,
        grid=(NB_P,),
        in_specs=[blk] * 3,
        out_specs=blk,
        out_shape=jax.ShapeDtypeStruct((NPR, D), F32),
        compiler_params=_cp(("parallel",)),
        name="ctx_attn",
    )(q, k, v)


def _nbr_attn_kernel(q_ref, k_ref, v_ref, kc_ref, vc_ref, tz_ref, o_ref):
    scale = HD ** -0.5
    rows = T_S // GRID_W
    nloc = WIN_R * GRID_W
    h0 = lax.broadcasted_iota(jnp.int32, (GRID_W, LANES), 1) < HD
    kcb = kc_ref[...].astype(BF)
    vcb = vc_ref[...].astype(BF)

    def head_chain(h, q, klb, vlb, j0):
        qm = jnp.where(h0 if h == 0 else jnp.logical_not(h0), q, 0.0).astype(BF)
        sl = _bdot_nt(qm, klb)
        sc = _bdot_nt(qm, kcb)
        yield
        bias = jnp.concatenate([tz_ref[h, j0 + 2 * m] for m in range(WIN_R // 2)], axis=1)
        sl = sl * scale + bias
        sc = sc * scale
        m = jnp.maximum(jnp.max(sl, axis=-1, keepdims=True), jnp.max(sc, axis=-1, keepdims=True))
        el = jnp.exp(sl - m)
        ec = jnp.exp(sc - m)
        inv = 1.0 / (jnp.sum(el, axis=-1, keepdims=True) + jnp.sum(ec, axis=-1, keepdims=True))
        o = (jnp.dot((el * inv).astype(BF), vlb, preferred_element_type=F32)
             + jnp.dot((ec * inv).astype(BF), vcb, preferred_element_type=F32))
        yield
        return o

    def body(it, carry):
        gens, slices = [], []
        for j in range(NBR_UNROLL):
            r = it * NBR_UNROLL + j
            start = jnp.clip(r - WIN_R // 2, 0, rows - WIN_R)
            qs = pl.ds(pl.multiple_of(r * GRID_W, GRID_W), GRID_W)
            ks = pl.ds(pl.multiple_of(start * GRID_W, GRID_W), nloc)
            q = q_ref[qs, :]
            klb = k_ref[ks, :].astype(BF)
            vlb = v_ref[ks, :].astype(BF)
            j0 = start - r + WIN_R - 1
            slices.append(qs)
            gens += [head_chain(h, q, klb, vlb, j0) for h in range(2)]
        outs = _run_lockstep(gens)
        for j, qs in enumerate(slices):
            o_ref[qs, :] = jnp.where(h0, outs[2 * j], outs[2 * j + 1])
        return carry

    lax.fori_loop(0, rows // NBR_UNROLL, body, 0)


def _nbr_attn(q, k, v, kc, vc, tz2):
    npair = D // LANES
    qseq = pl.BlockSpec((T_S, LANES), lambda b, p: (NPR // T_S + b, p))
    seq = pl.BlockSpec((T_S, LANES), lambda b, p: (b, p))
    ctx = pl.BlockSpec((kc.shape[0] // NB_S, LANES), lambda b, p: (b, p))
    return pl.pallas_call(
        _nbr_attn_kernel,
        grid=(NB_S, npair),
        in_specs=[qseq, seq, seq, ctx, ctx,
                  pl.BlockSpec((2,) + tz2.shape[1:], lambda b, p: (p, 0, 0, 0))],
        out_specs=pl.BlockSpec((T_S, LANES), lambda b, p: (b, p)),
        out_shape=jax.ShapeDtypeStruct((NSR, D), F32),
        compiler_params=_cp(("parallel", "parallel")),
        name="nbr_attn",
    )(q, k, v, kc, vc, tz2)


def _block_diag2(a, b):
    z = jnp.zeros_like(a)
    return jnp.concatenate([jnp.concatenate([a, z], axis=1), jnp.concatenate([z, b], axis=1)], axis=0)


def _nbr_bias_table(rpb):
    qc = jnp.arange(GRID_W)[:, None]
    kc = jnp.arange(GRID_W)[None, :]
    cstart = jnp.clip(qc - WIN_C // 2, 0, GRID_W - WIN_C)
    valid = jnp.logical_and(kc >= cstart, kc < cstart + WIN_C)
    nr = rpb.shape[1]
    period = 2 * GRID_W
    p = jnp.concatenate([rpb[:, :, WIN_C - 1:], jnp.zeros(rpb.shape[:2] + (period - 2 * WIN_C + 1,), F32),
                         rpb[:, :, :WIN_C - 1]], axis=-1)
    flat = jnp.tile(p, (1, 1, GRID_W))[:, :, :GRID_W * (period - 1)]
    toep = flat.reshape(rpb.shape[0], nr, GRID_W, period - 1)[..., :GRID_W]
    tz = jnp.where(valid, toep, NEG)
    return jnp.concatenate([tz[:, :-1], tz[:, 1:]], axis=-1)


def kernel(x_prompt, x_sample, c, state_rwkv, cache_na_k, cache_na_v, c_ctx, norm_mix, norm_ffn, ada_w, ada_b, rwkv_mu, rwkv_w_r, rwkv_w_k, rwkv_w_v, rwkv_w_o, rwkv_w0, rwkv_w1, rwkv_w2, rwkv_a0, rwkv_a1, rwkv_a2, rwkv_g1, rwkv_g2, rwkv_k_k, rwkv_k_a, rwkv_r_k, rwkv_lnx_g, rwkv_lnx_b, na_w_qkv, na_w_o, na_rpb, moe_w_grp, moe_b_grp, moe_w_exp, moe_b_exp, moe_w1, moe_w3, moe_w2, final_norm):
    x_p = x_prompt.reshape(NPR, D)
    x_s = x_sample.reshape(NSR, D)
    c8 = jnp.concatenate([c_ctx[None, :], c, jnp.zeros((8 - 1 - NB_S, D), F32)], axis=0)
    mod = _adaln(c8, ada_w, ada_b)
    ri = jnp.arange(2 * LANES)[:, None] // HD
    bo = (ri == ri.T).astype(BF)
    row = lambda a: a.reshape(1, D)

    w1c = jnp.concatenate([rwkv_w1[0, 0], rwkv_w1[0, 1]], axis=1).astype(BF)
    a1c = jnp.concatenate([rwkv_a1[0, 0], rwkv_a1[0, 1]], axis=1).astype(BF)
    w2bd = _block_diag2(rwkv_w2[0, 0], rwkv_w2[0, 1]).astype(BF)
    a2bd = _block_diag2(rwkv_a2[0, 0], rwkv_a2[0, 1]).astype(BF)
    r, k, v, g, kk, lw0, lw1, ag0, ag1 = _rwkv_front(
        x_p, x_s, mod[0], row(norm_mix[0]), rwkv_mu[0], rwkv_w_r[0].astype(BF), rwkv_w_k[0].astype(BF),
        rwkv_w_v[0].astype(BF), w1c, a1c, rwkv_g1[0].astype(BF), w2bd, a2bd, rwkv_g2[0].astype(BF),
        rwkv_w0[0], rwkv_a0[0], row(rwkv_k_k[0]), bo)
    arrs = (r, k, v, kk, lw0, lw1, ag0, ag1)
    ka, rk = row(rwkv_k_a[0]), row(rwkv_r_k[0])
    ys_p, bon_p, new_state = _wkv(arrs, ka, rk, T_P, NB_P, 0)
    s0 = state_rwkv[:, 0].reshape(NB_S, 2, NH // 2, 2, HD, HD)
    z = jnp.zeros_like(s0[:, :, :, 0])
    s0_bd = jnp.concatenate([jnp.concatenate([s0[:, :, :, 0], z], axis=-1),
                             jnp.concatenate([z, s0[:, :, :, 1]], axis=-1)], axis=-2)
    ys_s, bon_s = _wkv(arrs, ka, rk, T_S, NB_S, NPR // T_S, s0_bd=s0_bd)
    new_state_rwkv = new_state.reshape(NB_P, 1, 2, NH, HD, HD)
    y = _proj_residual((x_p, x_s), ys_p, ys_s, mod[0], rwkv_w_o[0].astype(BF),
                       gn_args=(bon_p, bon_s, g, row(rwkv_lnx_g[0]), row(rwkv_lnx_b[0]), bo),
                       name="rwkv_out")

    def moe(y, i, final):
        wr = jnp.concatenate([moe_w_exp[i], moe_w_grp[i],
                              jnp.zeros((D, LANES - NE - NE // EPG), F32)], axis=1)
        br = jnp.concatenate([moe_b_exp[i], moe_b_grp[i],
                              jnp.zeros((LANES - NE - NE // EPG,), F32)]).reshape(1, LANES)
        xt, gate = _moe_router(y, mod[i], row(norm_ffn[i]), wr, br)
        w13 = jnp.concatenate([moe_w1[i], moe_w3[i]], axis=-1).astype(BF)
        return _moe_experts(xt, gate, w13, moe_w2[i].astype(BF), y, mod[i], row(final_norm), final)

    y = moe(y, 0, False)

    q, k_p, v_p, k_s, v_s = _na_qkv(y, mod[1], row(norm_mix[1]), na_w_qkv[0].astype(BF))
    o_p = _ctx_attn(q, k_p, v_p)
    kc = cache_na_k[:, 0].reshape(NB_S * cache_na_k.shape[2], D)
    vc = cache_na_v[:, 0].reshape(NB_S * cache_na_v.shape[2], D)
    o_s = _nbr_attn(q, k_s, v_s, kc, vc, _nbr_bias_table(na_rpb[0]))
    y = _proj_residual(y, o_p, o_s, mod[1], na_w_o[0].astype(BF), name="attn_out")
    y_p, y_s = moe(y, 1, True)

    y_prompt = y_p.reshape(NB_P, T_P, D)
    y_sample = y_s.reshape(NB_S, T_S, D)
    new_k = k_p.reshape(NB_P, 1, T_P, NH, HD)
    new_v = v_p.reshape(NB_P, 1, T_P, NH, HD)
    return (y_prompt, y_sample, new_state_rwkv, new_k, new_v)
```

```python
import functools
import math

import jax
import jax.numpy as jnp
from jax import lax
from jax.experimental import pallas as pl
from jax.experimental.pallas import tpu as pltpu

F32 = jnp.float32
BF = jnp.bfloat16

D = 1024
NH = 16
HD = 64
NB_P, T_P = 32, 256
NB_S, T_S = 2, 1024
NPR = NB_P * T_P
NSR = NB_S * T_S
NTOK = NPR + NSR
TM = 256
NT = NTOK // TM
NT_P = NPR // TM
TPS = T_S // TM
TMX = 1024
NMOD = 6
NE = 16
EPG = 4
DE = 256
EPS = 4
CH = 64
SB = 16
LANES = 128
GRID_W = 64
WIN_R, WIN_C = 8, 16
NBR_UNROLL = 4
CTX_PAIRS = 2
NORM_EPS = 1e-6
GN_EPS = 64e-5
NEG = -1e30
VMEM_LIMIT = 56 * 1024 * 1024


def _cp(sem):
    return pltpu.CompilerParams(dimension_semantics=sem, vmem_limit_bytes=VMEM_LIMIT)


def _bdot(a, b):
    return jnp.dot(a.astype(BF), b.astype(BF), preferred_element_type=F32)


def _bdot_nt(a, b):
    return lax.dot_general(a.astype(BF), b.astype(BF), (((1,), (1,)), ((), ())),
                           preferred_element_type=F32)


def _split2(x):
    hi = x.astype(BF)
    lo = (x - hi.astype(F32)).astype(BF)
    return hi, lo


def _split3(x):
    hi = x.astype(BF)
    r1 = x - hi.astype(F32)
    mid = r1.astype(BF)
    lo = (r1 - mid.astype(F32)).astype(BF)
    return hi, mid, lo


def _seg_sum(x, bo):
    outs = []
    for c in range(x.shape[1] // 256):
        hi, lo = _split2(x[:, c * 256:(c + 1) * 256])
        outs.append(jnp.dot(hi, bo, preferred_element_type=F32)
                    + jnp.dot(lo, bo, preferred_element_type=F32))
    return outs[0] if len(outs) == 1 else jnp.concatenate(outs, axis=1)


def _tile_mod_row(i):
    return jnp.where(i < NT_P, 0, 1 + (i - NT_P) // TPS)


def _modulate(x, g, sh, sc):
    ms = jnp.mean(x * x, axis=-1, keepdims=True)
    return x * lax.rsqrt(ms + NORM_EPS) * g * (1.0 + sc) + sh


def _mod_vec(mod_ref, row, k):
    return mod_ref[pl.ds(row, 1), pl.ds(k * D, D)]


def _adaln_kernel(c_ref, w_ref, b_ref, o_ref):
    c = c_ref[...]
    s = c * jax.nn.sigmoid(c)
    o_ref[...] = _bdot(s, w_ref[...]) + b_ref[...]


def _adaln(c8, ada_w, ada_b):
    nl = ada_w.shape[0]
    tn = 1536
    return pl.pallas_call(
        _adaln_kernel,
        grid=(nl, NMOD * D // tn),
        in_specs=[pl.BlockSpec((8, D), lambda l, j: (0, 0)),
                  pl.BlockSpec((None, D, tn), lambda l, j: (l, 0, j)),
                  pl.BlockSpec((None, 1, tn), lambda l, j: (l, 0, j))],
        out_specs=pl.BlockSpec((None, 8, tn), lambda l, j: (l, 0, j)),
        out_shape=jax.ShapeDtypeStruct((nl, 8, NMOD * D), F32),
        compiler_params=_cp(("parallel", "parallel")),
        name="adaln",
    )(c8, ada_w, ada_b.reshape(nl, 1, NMOD * D))


def _front_kernel(xp_ref, xs_ref, yp_ref, yn_ref, mod_ref, nrm_ref, mu_ref, wr_ref, wk_ref, wv_ref,
                  w1_ref, a1_ref, g1_ref, w2_ref, a2_ref, g2_ref, w0_ref, a0_ref, kkw_ref, bo_ref,
                  r_o, k_o, v_o, g_o, kk_o, lw0_o, lw1_o, ag0_o, ag1_o):
    i = pl.program_id(0)
    row = _tile_mod_row(i)
    q = (i - NT_P) % TPS
    first = jnp.logical_or(i < NT_P, q == 0)
    last = jnp.logical_or(i < NT_P, q == TPS - 1)
    sh = _mod_vec(mod_ref, row, 0)
    sc = _mod_vec(mod_ref, row, 1)
    g = nrm_ref[...]
    h = _modulate(jnp.where(i < NT_P, xp_ref[...], xs_ref[...]), g, sh, sc)
    hp = _modulate(yp_ref[...], g, sh, sc)[7:8]
    hn = _modulate(yn_ref[...], g, sh, sc)[0:1]
    hp = jnp.where(first, 0.0, hp)
    hn = jnp.where(last, 0.0, hn)
    rid = lax.broadcasted_iota(jnp.int32, (TM, D), 0)
    prev = jnp.where(rid == 0, hp, pltpu.roll(h, 1, 0))
    nxt = jnp.where(rid == TM - 1, hn, pltpu.roll(h, TM - 1, 0))
    dx = 0.5 * (prev + nxt) - h

    def mix(n):
        return (h + dx * mu_ref[n:n + 1, :]).astype(BF)

    r_o[...] = jnp.dot(mix(0), wr_ref[...], preferred_element_type=F32)
    xw = mix(1)
    lora_w = _bdot(jnp.tanh(jnp.dot(xw, w1_ref[...], preferred_element_type=F32)), w2_ref[...])
    c05 = math.exp(-0.5)
    lw0_o[...] = -c05 * jax.nn.sigmoid(w0_ref[0:1, :] + lora_w[:, :D])
    lw1_o[...] = -c05 * jax.nn.sigmoid(w0_ref[1:2, :] + lora_w[:, D:])
    k_raw = jnp.dot(mix(2), wk_ref[...], preferred_element_type=F32)
    k_o[...] = k_raw
    kq = k_raw * kkw_ref[...]
    ss = _seg_sum(kq * kq, bo_ref[...])
    kk_o[...] = kq / jnp.maximum(jnp.sqrt(ss), 1e-12)
    v_o[...] = jnp.dot(mix(3), wv_ref[...], preferred_element_type=F32)
    xa = mix(4)
    lora_a = _bdot(jnp.dot(xa, a1_ref[...], preferred_element_type=F32), a2_ref[...])
    ag0_o[...] = jax.nn.sigmoid(a0_ref[0:1, :] + lora_a[:, :D])
    ag1_o[...] = jax.nn.sigmoid(a0_ref[1:2, :] + lora_a[:, D:])
    xg = mix(5)
    g_o[...] = _bdot(jax.nn.sigmoid(jnp.dot(xg, g1_ref[...], preferred_element_type=F32)), g2_ref[...])


def _rwkv_front(x_p, x_s, mod, nrm, mu, wr, wk, wv, w1c, a1c, g1, w2bd, a2bd, g2, w0, a0, kkw, bo):
    tile = pl.BlockSpec((TM, D), lambda i: (i, 0))
    nblk8 = NSR // 8
    bpt = TM // 8
    full = lambda a: pl.BlockSpec(a.shape, lambda i: (0,) * a.ndim)
    ins = [x_p, x_s, x_s, x_s, mod, nrm, mu, wr, wk, wv, w1c, a1c, g1, w2bd, a2bd, g2, w0, a0, kkw, bo]
    in_specs = [pl.BlockSpec((TM, D), lambda i: (jnp.minimum(i, NT_P - 1), 0)),
                pl.BlockSpec((TM, D), lambda i: (jnp.maximum(i - NT_P, 0), 0)),
                pl.BlockSpec((8, D), lambda i: (jnp.clip((i - NT_P) * bpt - 1, 0, nblk8 - 1), 0)),
                pl.BlockSpec((8, D), lambda i: (jnp.clip((i - NT_P + 1) * bpt, 0, nblk8 - 1), 0))]
    in_specs += [full(a) for a in ins[4:]]
    out = jax.ShapeDtypeStruct((NTOK, D), F32)
    return pl.pallas_call(
        _front_kernel,
        grid=(NT,),
        in_specs=in_specs,
        out_specs=[tile] * 9,
        out_shape=[out] * 9,
        compiler_params=_cp(("parallel",)),
        name="rwkv_front",
    )(*ins)


def _wkv_chunk(d, S, r, k, v, kk, lw, ag, ka, rk, cst):
    tri4, strict, incl, h0, bd, bo2, same_blk, bd_sb, eye_c, lane_blk = cst
    dot = lambda a, b: jnp.dot(a, b, preferred_element_type=F32)
    nblk = CH // SB
    h0s = lax.broadcasted_iota(jnp.int32, (SB, LANES), 1) < HD

    def stack_heads(x, swap=False):
        zero = jnp.zeros_like(x)
        parts = [jnp.where(h0, x, zero), jnp.where(h0, zero, x)]
        return jnp.concatenate(parts[::-1] if swap else parts, axis=0)

    def dot_split(a, bm):
        a_hi, a_lo = _split2(a)
        b_hi, b_lo = _split2(bm)
        return dot(a_hi, b_hi) + (dot(a_hi, b_lo) + dot(a_lo, b_hi))

    def expand(mc):
        return jnp.where(bd_sb, jnp.concatenate([mc] * (LANES // SB), axis=0), 0.0)

    b = kk * ag
    kd = k * (1.0 + (ag - 1.0) * ka)
    hi, mid, lo = _split3(lw)
    cs = dot(tri4[d], jnp.concatenate([hi, mid, lo, jnp.zeros_like(lo)], axis=0))
    qh, ql = _split2(r * kd * rk)
    bonus = (dot(qh, bo2) + dot(ql, bo2)) * v
    yield
    tot = cs[CH - 1:CH, :] if d == 0 else cs[0:1, :]
    p_inv = jnp.exp(-cs)
    p_end = jnp.exp(tot - cs)
    at = -kk * jnp.exp(cs - lw)
    rt = r * jnp.exp(cs)
    bt = (b * p_inv).astype(BF)
    kt = (kd * p_inv).astype(BF)
    ar = jnp.concatenate([at, rt], axis=0)
    h02 = jnp.concatenate([h0, h0], axis=0)
    g0 = _bdot_nt(jnp.where(h02, ar, 0.0), jnp.concatenate([bt, kt], axis=0))
    g1 = _bdot_nt(jnp.where(h02, 0.0, ar), jnp.concatenate([kt, bt], axis=0))
    w0y0 = _bdot_nt(ar, S)
    yield
    w0, y0 = w0y0[:CH], w0y0[CH:]
    lab = jnp.where(strict[d], jnp.where(h0, g0[:CH], g1[:CH]), 0.0)
    lak = jnp.where(strict[d], jnp.where(h0, g1[:CH], g0[:CH]), 0.0)
    mrb = jnp.where(incl[d], jnp.where(h0, g0[CH:], g1[CH:]), 0.0)
    mrk = jnp.where(incl[d], jnp.where(h0, g1[CH:], g0[CH:]), 0.0)
    vstk_sw = stack_heads(v, swap=True).astype(BF)
    w = w0 + dot(lak.astype(BF), vstk_sw)
    ldiag = jnp.where(same_blk, lab, 0.0)
    loff = jnp.where(same_blk, 0.0, lab)
    mc = ldiag[0:SB]
    for i in range(1, nblk):
        mc = mc + ldiag[i * SB:(i + 1) * SB]
    xc = eye_c + mc
    mc = dot_split(mc, expand(mc))
    yield
    nsq = int(math.log2(SB)) - 1
    for it in range(nsq):
        last = it + 1 == nsq
        res = dot_split(xc if last else jnp.concatenate([xc, mc], axis=0), expand(mc))
        xc = xc + res[:SB]
        if not last:
            mc = res[SB:]
        yield
    ublk = [None] * nblk
    order = list(range(nblk)) if d == 0 else list(range(nblk - 1, -1, -1))
    for n, i in enumerate(order):
        wi = w[i * SB:(i + 1) * SB]
        if n > 0:
            ucur = jnp.concatenate([jnp.zeros((SB, LANES), F32) if ub is None else ub
                                    for ub in ublk], axis=0)
            wi = wi + dot(loff[i * SB:(i + 1) * SB].astype(BF), stack_heads(ucur).astype(BF))
            yield
        m0 = jnp.where(h0s, wi, 0.0).astype(BF)
        m1 = jnp.where(h0s, 0.0, wi).astype(BF)
        rhs = jnp.concatenate([m0] * nblk + [m1] * nblk, axis=0)
        ublk[i] = dot(jnp.where(lane_blk[i], xc, 0.0).astype(BF), rhs)
        yield
    u = jnp.concatenate(ublk, axis=0)
    y = dot(jnp.concatenate([mrb, mrk], axis=1).astype(BF),
            jnp.concatenate([stack_heads(u).astype(BF), vstk_sw], axis=0))
    uvt = jnp.concatenate([u, v], axis=0).T
    ds = _bdot(uvt, jnp.concatenate([b * p_end, kd * p_end], axis=0))
    yield
    s_new = S * jnp.exp(tot) + jnp.where(bd, ds, 0.0)
    return s_new, y0 + y, bonus


def _run_lockstep(gens):
    results = [None] * len(gens)
    pending = list(range(len(gens)))
    while pending:
        for i in list(pending):
            try:
                next(gens[i])
            except StopIteration as stop:
                results[i] = stop.value
                pending.remove(i)
    return results


def _wkv_consts():
    t = lax.broadcasted_iota(jnp.int32, (CH, 4 * CH), 0)
    s = lax.broadcasted_iota(jnp.int32, (CH, 4 * CH), 1) & (CH - 1)
    tri4 = [jnp.where(s <= t, 1.0, 0.0).astype(BF), jnp.where(s >= t, 1.0, 0.0).astype(BF)]
    t2 = lax.broadcasted_iota(jnp.int32, (CH, LANES), 0)
    l2 = lax.broadcasted_iota(jnp.int32, (CH, LANES), 1)
    s2 = l2 & (CH - 1)
    strict = [s2 < t2, s2 > t2]
    incl = [s2 <= t2, s2 >= t2]
    h0 = l2 < HD
    same_blk = (t2 // SB) == (s2 // SB)
    ri = lax.broadcasted_iota(jnp.int32, (LANES, LANES), 0)
    ci = lax.broadcasted_iota(jnp.int32, (LANES, LANES), 1)
    bd = (ri // HD) == (ci // HD)
    bo2 = jnp.where(bd, 1.0, 0.0).astype(BF)
    bd_sb = (ri // SB) == (ci // SB)
    t3 = lax.broadcasted_iota(jnp.int32, (SB, LANES), 0)
    l3 = lax.broadcasted_iota(jnp.int32, (SB, LANES), 1)
    eye_c = jnp.where((l3 & (SB - 1)) == t3, 1.0, 0.0)
    lane_blk = [((l3 & (CH - 1)) // SB) == i for i in range(CH // SB)]
    return tri4, strict, incl, h0, bd, bo2, same_blk, bd_sb, eye_c, lane_blk


def _wkv_kernel(T, has_s0, *refs):
    (r_ref, k_ref, v_ref, kk_ref, lw0_ref, lw1_ref, ag0_ref, ag1_ref, ka_ref, rk_ref), rest = refs[:10], refs[10:]
    if has_s0:
        s0_ref, y_ref, bon_ref, s_scr = rest
    else:
        y_ref, bon_ref, st_ref, s_scr = rest
    nch = T // CH
    npair = y_ref.shape[1] // LANES
    cst = _wkv_consts()
    if has_s0:
        s_scr[...] = s0_ref[...]
    else:
        s_scr[...] = jnp.zeros_like(s_scr)
    y_ref[...] = jnp.zeros_like(y_ref)
    bon_ref[...] = jnp.zeros_like(bon_ref)
    lw_refs = (lw0_ref, lw1_ref)
    ag_refs = (ag0_ref, ag1_ref)

    def body(c, carry):
        chains = [(p, d) for p in range(npair) for d in range(2)]
        sl = {}
        for p, d in chains:
            r0 = pl.multiple_of((c if d == 0 else nch - 1 - c) * CH, CH)
            sl[p, d] = (pl.ds(r0, CH), pl.ds(p * LANES, LANES))
        args = {}
        for p, d in chains:
            rs, ls = sl[p, d]
            args[p, d] = (s_scr[d, p], r_ref[rs, ls], k_ref[rs, ls], v_ref[rs, ls], kk_ref[rs, ls],
                          lw_refs[d][rs, ls], ag_refs[d][rs, ls], ka_ref[:, ls], rk_ref[:, ls],
                          y_ref[rs, ls], bon_ref[rs, ls])
        outs = _run_lockstep([_wkv_chunk(d, *args[p, d][:9], cst) for p, d in chains])
        for (p, d), (s_new, y, bon) in zip(chains, outs):
            rs, ls = sl[p, d]
            s_scr[d, p] = s_new
            y_ref[rs, ls] = args[p, d][9] + y
            bon_ref[rs, ls] = args[p, d][10] + bon
        return carry

    lax.fori_loop(0, nch, body, 0)
    if not has_s0:
        for d in range(2):
            for p in range(npair):
                s = s_scr[d, p]
                st_ref[d, 2 * p] = s[:HD, :HD]
                st_ref[d, 2 * p + 1] = s[HD:, HD:]


def _wkv(arrs, ka, rk, T, nb, row_blk0, s0_bd=None, lanes=512):
    ng = D // lanes
    npair = lanes // LANES
    seq = pl.BlockSpec((T, lanes), lambda b, g: (row_blk0 + b, g))
    vec = pl.BlockSpec((1, lanes), lambda b, g: (0, g))
    st = pl.BlockSpec((None, 2, npair, LANES, LANES), lambda b, g: (b, 0, g, 0, 0))
    out_seq = pl.BlockSpec((T, lanes), lambda b, g: (b, g))
    in_specs = [seq] * 8 + [vec, vec]
    ins = list(arrs) + [ka, rk]
    out_specs = [out_seq, out_seq]
    out_shape = [jax.ShapeDtypeStruct((nb * T, D), F32)] * 2
    if s0_bd is not None:
        in_specs.append(st)
        ins.append(s0_bd)
    else:
        out_specs.append(pl.BlockSpec((None, 2, 2 * npair, HD, HD), lambda b, g: (b, 0, g, 0, 0)))
        out_shape.append(jax.ShapeDtypeStruct((nb, 2, NH, HD, HD), F32))
    return pl.pallas_call(
        functools.partial(_wkv_kernel, T, s0_bd is not None),
        grid=(nb, ng),
        in_specs=in_specs,
        out_specs=out_specs,
        out_shape=out_shape,
        scratch_shapes=[pltpu.VMEM((2, npair, LANES, LANES), F32)],
        compiler_params=_cp(("parallel", "parallel")),
        name="wkv_T%d" % T,
    )(*ins)


def _proj_kernel(gn, *refs):
    i = pl.program_id(0)
    is_p = i < NT_P
    if gn:
        (yp_ref, ys_ref, ap_ref, as_ref, bp_ref, bs_ref, g_ref, lng_ref, lnb_ref, bo_ref,
         mod_ref, w_ref, o_ref) = refs
        y = jnp.where(is_p, yp_ref[...], ys_ref[...])
    else:
        y_ref, ap_ref, as_ref, mod_ref, w_ref, o_ref = refs
        y = y_ref[...]
    a = jnp.where(is_p, ap_ref[...], as_ref[...])
    if gn:
        bo = bo_ref[...]
        mean = _seg_sum(a, bo) * (1.0 / HD)
        cen = a - mean
        var = _seg_sum(cen * cen, bo) * (1.0 / HD)
        yn = cen * lax.rsqrt(var + GN_EPS) * lng_ref[...] + lnb_ref[...]
        a = (yn + jnp.where(is_p, bp_ref[...], bs_ref[...])) * g_ref[...]
    gt = _mod_vec(mod_ref, _tile_mod_row(i), 2)
    o_ref[...] = y + gt * _bdot(a, w_ref[...])


def _proj_residual(y, a_p, a_s, mod, w, gn_args=None, name="proj"):
    tile = pl.BlockSpec((TM, D), lambda i: (i, 0))
    tile_p = pl.BlockSpec((TM, D), lambda i: (jnp.minimum(i, NT_P - 1), 0))
    tile_s = pl.BlockSpec((TM, D), lambda i: (jnp.maximum(i - NT_P, 0), 0))
    full = lambda a: pl.BlockSpec(a.shape, lambda i: (0,) * a.ndim)
    if gn_args is not None:
        ins = [y[0], y[1], a_p, a_s]
        in_specs = [tile_p, tile_s, tile_p, tile_s]
    else:
        ins = [y, a_p, a_s]
        in_specs = [tile, tile_p, tile_s]
    if gn_args is not None:
        b_p, b_s, g, lng, lnb, bo = gn_args
        ins += [b_p, b_s, g, lng, lnb, bo]
        in_specs += [tile_p, tile_s, tile, full(lng), full(lnb), full(bo)]
    ins += [mod, w]
    in_specs += [full(mod), full(w)]
    return pl.pallas_call(
        functools.partial(_proj_kernel, gn_args is not None),
        grid=(NT,),
        in_specs=in_specs,
        out_specs=tile,
        out_shape=jax.ShapeDtypeStruct((NTOK, D), F32),
        compiler_params=_cp(("parallel",)),
        name=name,
    )(*ins)


def _router_kernel(y_ref, mod_ref, nrm_ref, wr_ref, br_ref, xt_o, gate_o):
    i = pl.program_id(0)
    row = _tile_mod_row(i)
    xt = _modulate(y_ref[...], nrm_ref[...], _mod_vec(mod_ref, row, 3), _mod_vec(mod_ref, row, 4))
    xt_o[...] = xt.astype(BF)
    x1, x2, x3 = _split3(xt)
    w1, w2, w3 = _split3(wr_ref[...])
    dot = lambda a, b: jnp.dot(a, b, preferred_element_type=F32)
    logits = (dot(x3, w1) + dot(x2, w2) + dot(x1, w3)) + (dot(x2, w1) + dot(x1, w2)) + dot(x1, w1)
    logits = logits + br_ref[...]
    lane = lax.broadcasted_iota(jnp.int32, logits.shape, 1).astype(F32)
    isg = jnp.logical_and(lane >= NE, lane < NE + NE // EPG)
    mg = jnp.max(jnp.where(isg, logits, NEG), axis=-1, keepdims=True)
    eg = jnp.where(isg, jnp.exp(jnp.minimum(logits - mg, 0.0)), 0.0)
    p_sel = 1.0 / jnp.sum(eg, axis=-1, keepdims=True)
    gidx = jnp.min(jnp.where(jnp.logical_and(isg, logits == mg), lane, 1e3), axis=-1, keepdims=True) - NE
    lo_l = gidx * EPG
    ing = jnp.logical_and(lane >= lo_l, lane < lo_l + EPG)
    me = jnp.max(jnp.where(ing, logits, NEG), axis=-1, keepdims=True)
    ee = jnp.where(ing, jnp.exp(jnp.minimum(logits - me, 0.0)), 0.0)
    se = jnp.sum(ee, axis=-1, keepdims=True)
    pe = ee / se
    i1 = jnp.min(jnp.where(jnp.logical_and(ing, logits == me), lane, 1e3), axis=-1, keepdims=True)
    v1 = 1.0 / se
    rest = jnp.logical_and(ing, lane != i1)
    v2 = jnp.max(jnp.where(rest, pe, -1.0), axis=-1, keepdims=True)
    i2 = jnp.min(jnp.where(jnp.logical_and(rest, pe == v2), lane, 1e3), axis=-1, keepdims=True)
    den = v1 + v2
    gate_o[...] = (jnp.where(lane == i1, p_sel * v1 / den, 0.0)
                   + jnp.where(lane == i2, p_sel * v2 / den, 0.0))


def _moe_router(y, mod, nrm, wr, br):
    tile = pl.BlockSpec((TM, D), lambda i: (i, 0))
    full = lambda a: pl.BlockSpec(a.shape, lambda i: (0,) * a.ndim)
    return pl.pallas_call(
        _router_kernel,
        grid=(NT,),
        in_specs=[tile, full(mod), full(nrm), full(wr), full(br)],
        out_specs=[tile, pl.BlockSpec((TM, LANES), lambda i: (i, 0))],
        out_shape=[jax.ShapeDtypeStruct((NTOK, D), BF), jax.ShapeDtypeStruct((NTOK, LANES), F32)],
        compiler_params=_cp(("parallel",)),
        name="moe_router",
    )(y, mod, nrm, wr, br)


def _experts_kernel(final, xt_ref, gate_ref, w13_ref, w2_ref, y_ref, mod_ref, fin_ref, *rest):
    i = pl.program_id(0)
    e = pl.program_id(1)
    acc_ref = rest[-1]
    ntile_p = NPR // TMX

    x = xt_ref[...]
    gate = gate_ref[...]
    lane = lax.broadcasted_iota(jnp.int32, gate_ref.shape, 1)
    hids = []
    for j in range(EPS):
        h13 = jnp.dot(x, w13_ref[j], preferred_element_type=F32)
        h1, h3 = h13[:, :DE], h13[:, DE:]
        ge = jnp.sum(jnp.where(lane == e * EPS + j, gate, 0.0), axis=-1, keepdims=True)
        hids.append(((h1 * jax.nn.sigmoid(h1)) * h3 * ge).astype(BF))
    hid = jnp.concatenate(hids, axis=1)

    @pl.when(e == 0)
    def _():
        acc_ref[...] = jnp.zeros_like(acc_ref)

    acc_ref[...] += jnp.dot(hid, w2_ref[...].reshape(EPS * DE, D), preferred_element_type=F32)

    def result():
        row = jnp.where(i < ntile_p, 0, i - ntile_p + 1)
        out = y_ref[...] + _mod_vec(mod_ref, row, 5) * acc_ref[...]
        if final:
            ms = jnp.mean(out * out, axis=-1, keepdims=True)
            out = out * lax.rsqrt(ms + NORM_EPS) * fin_ref[...]
        return out

    if final:
        op_ref, os_ref = rest[:2]

        @pl.when(jnp.logical_and(e == NE // EPS - 1, i < ntile_p))
        def _():
            op_ref[...] = result()

        @pl.when(jnp.logical_and(e == NE // EPS - 1, i >= ntile_p))
        def _():
            os_ref[...] = result()
    else:
        @pl.when(e == NE // EPS - 1)
        def _():
            rest[0][...] = result()


def _moe_experts(xt, gate, w13, w2, y, mod, fin, final):
    tile = pl.BlockSpec((TMX, D), lambda i, e: (i, 0))
    full = lambda a: pl.BlockSpec(a.shape, lambda i, e: (0,) * a.ndim)
    ntile_p = NPR // TMX
    if final:
        out_specs = [pl.BlockSpec((TMX, D), lambda i, e: (jnp.minimum(i, ntile_p - 1), 0)),
                     pl.BlockSpec((TMX, D), lambda i, e: (jnp.maximum(i - ntile_p, 0), 0))]
        out_shape = [jax.ShapeDtypeStruct((NPR, D), F32), jax.ShapeDtypeStruct((NSR, D), F32)]
    else:
        out_specs = tile
        out_shape = jax.ShapeDtypeStruct((NTOK, D), F32)
    return pl.pallas_call(
        functools.partial(_experts_kernel, final),
        grid=(NTOK // TMX, NE // EPS),
        in_specs=[tile, pl.BlockSpec((TMX, LANES), lambda i, e: (i, 0)),
                  pl.BlockSpec((EPS, D, 2 * DE), lambda i, e: (e, 0, 0)),
                  pl.BlockSpec((EPS, DE, D), lambda i, e: (e, 0, 0)),
                  tile, full(mod), full(fin)],
        out_specs=out_specs,
        out_shape=out_shape,
        scratch_shapes=[pltpu.VMEM((TMX, D), F32)] if final else [],
        compiler_params=_cp(("arbitrary" if final else "parallel", "arbitrary")),
        name="moe_experts",
    )(xt, gate, w13, w2, y, mod, fin)


def _qkv_kernel(y_ref, mod_ref, nrm_ref, w_ref, q_o, kp_o, vp_o, ks_o, vs_o, kc_o, vc_o):
    i = pl.program_id(0)
    row = _tile_mod_row(i)
    h = _modulate(y_ref[...], nrm_ref[...], _mod_vec(mod_ref, row, 0), _mod_vec(mod_ref, row, 1))
    qkv = _bdot(h, w_ref[...])
    q_o[...] = qkv[:, :D]

    @pl.when(i < NT_P)
    def _():
        kx = qkv[:, D:2 * D]
        vx = qkv[:, 2 * D:]
        kp_o[...] = kx
        vp_o[...] = vx
        for hh in range(NH):
            kc_o[pl.ds(hh, TM, stride=NH), :] = kx[:, hh * HD:(hh + 1) * HD]
            vc_o[pl.ds(hh, TM, stride=NH), :] = vx[:, hh * HD:(hh + 1) * HD]

    @pl.when(i >= NT_P)
    def _():
        ks_o[...] = qkv[:, D:2 * D]
        vs_o[...] = qkv[:, 2 * D:]


def _na_qkv(y, mod, nrm, w):
    tile = pl.BlockSpec((TM, D), lambda i: (i, 0))
    tile_p = pl.BlockSpec((TM, D), lambda i: (jnp.minimum(i, NT_P - 1), 0))
    tile_s = pl.BlockSpec((TM, D), lambda i: (jnp.maximum(i - NT_P, 0), 0))
    full = lambda a: pl.BlockSpec(a.shape, lambda i: (0,) * a.ndim)
    out = lambda n: jax.ShapeDtypeStruct((n, D), F32)
    cache = jax.ShapeDtypeStruct((NPR * NH, HD), F32)
    cache_tile = pl.BlockSpec((TM * NH, HD), lambda i: (jnp.minimum(i, NT_P - 1), 0))
    return pl.pallas_call(
        _qkv_kernel,
        grid=(NT,),
        in_specs=[tile, full(mod), full(nrm), full(w)],
        out_specs=[tile, tile_p, tile_p, tile_s, tile_s, cache_tile, cache_tile],
        out_shape=[out(NTOK), out(NPR), out(NPR), out(NSR), out(NSR), cache, cache],
        compiler_params=_cp(("arbitrary",)),
        name="na_qkv",
    )(y, mod, nrm, w)


def _softmax_rows(s):
    m = jnp.max(s, axis=-1, keepdims=True)
    e = jnp.exp(s - m)
    return e / jnp.sum(e, axis=-1, keepdims=True)


def _ctx_attn_kernel(q_ref, k_ref, v_ref, o_ref):
    scale = HD ** -0.5
    h0 = lax.broadcasted_iota(jnp.int32, (T_P, LANES), 1) < HD
    def head_chain(h, q, kb, vb):
        qm = jnp.where(h0 if h == 0 else jnp.logical_not(h0), q, 0.0)
        s = _bdot_nt(qm, kb)
        yield
        o = jnp.dot(_softmax_rows(s * scale).astype(BF), vb, preferred_element_type=F32)
        yield
        return o

    npair = D // LANES
    for p0 in range(0, npair, CTX_PAIRS):
        gens = []
        for p in range(p0, p0 + CTX_PAIRS):
            ls = pl.ds(p * LANES, LANES)
            q = q_ref[:, ls]
            kb = k_ref[:, ls].astype(BF)
            vb = v_ref[:, ls].astype(BF)
            gens += [head_chain(h, q, kb, vb) for h in range(2)]
        outs = _run_lockstep(gens)
        for j, p in enumerate(range(p0, p0 + CTX_PAIRS)):
            o_ref[:, pl.ds(p * LANES, LANES)] = jnp.where(h0, outs[2 * j], outs[2 * j + 1])


def _ctx_attn(q, k, v):
    blk = pl.BlockSpec((T_P, D), lambda b: (b, 0))
    return pl.pallas_call(
        _ctx_attn_kernel,
        grid=(NB_P,),
        in_specs=[blk] * 3,
        out_specs=blk,
        out_shape=jax.ShapeDtypeStruct((NPR, D), F32),
        compiler_params=_cp(("parallel",)),
        name="ctx_attn",
    )(q, k, v)


def _nbr_attn_kernel(q_ref, k_ref, v_ref, kc_ref, vc_ref, tz_ref, o_ref):
    scale = HD ** -0.5
    rows = T_S // GRID_W
    nloc = WIN_R * GRID_W
    h0 = lax.broadcasted_iota(jnp.int32, (GRID_W, LANES), 1) < HD
    kcb = kc_ref[...].astype(BF)
    vcb = vc_ref[...].astype(BF)

    def head_chain(h, q, klb, vlb, j0):
        qm = jnp.where(h0 if h == 0 else jnp.logical_not(h0), q, 0.0).astype(BF)
        sl = _bdot_nt(qm, klb)
        sc = _bdot_nt(qm, kcb)
        yield
        bias = jnp.concatenate([tz_ref[h, j0 + 2 * m] for m in range(WIN_R // 2)], axis=1)
        sl = sl * scale + bias
        sc = sc * scale
        m = jnp.maximum(jnp.max(sl, axis=-1, keepdims=True), jnp.max(sc, axis=-1, keepdims=True))
        el = jnp.exp(sl - m)
        ec = jnp.exp(sc - m)
        inv = 1.0 / (jnp.sum(el, axis=-1, keepdims=True) + jnp.sum(ec, axis=-1, keepdims=True))
        o = (jnp.dot((el * inv).astype(BF), vlb, preferred_element_type=F32)
             + jnp.dot((ec * inv).astype(BF), vcb, preferred_element_type=F32))
        yield
        return o

    def body(it, carry):
        gens, slices = [], []
        for j in range(NBR_UNROLL):
            r = it * NBR_UNROLL + j
            start = jnp.clip(r - WIN_R // 2, 0, rows - WIN_R)
            qs = pl.ds(pl.multiple_of(r * GRID_W, GRID_W), GRID_W)
            ks = pl.ds(pl.multiple_of(start * GRID_W, GRID_W), nloc)
            q = q_ref[qs, :]
            klb = k_ref[ks, :].astype(BF)
            vlb = v_ref[ks, :].astype(BF)
            j0 = start - r + WIN_R - 1
            slices.append(qs)
            gens += [head_chain(h, q, klb, vlb, j0) for h in range(2)]
        outs = _run_lockstep(gens)
        for j, qs in enumerate(slices):
            o_ref[qs, :] = jnp.where(h0, outs[2 * j], outs[2 * j + 1])
        return carry

    lax.fori_loop(0, rows // NBR_UNROLL, body, 0)


def _nbr_attn(q, k, v, kc, vc, tz2):
    npair = D // LANES
    qseq = pl.BlockSpec((T_S, LANES), lambda b, p: (NPR // T_S + b, p))
    seq = pl.BlockSpec((T_S, LANES), lambda b, p: (b, p))
    ctx = pl.BlockSpec((kc.shape[0] // NB_S, LANES), lambda b, p: (b, p))
    return pl.pallas_call(
        _nbr_attn_kernel,
        grid=(NB_S, npair),
        in_specs=[qseq, seq, seq, ctx, ctx,
                  pl.BlockSpec((2,) + tz2.shape[1:], lambda b, p: (p, 0, 0, 0))],
        out_specs=pl.BlockSpec((T_S, LANES), lambda b, p: (b, p)),
        out_shape=jax.ShapeDtypeStruct((NSR, D), F32),
        compiler_params=_cp(("parallel", "parallel")),
        name="nbr_attn",
    )(q, k, v, kc, vc, tz2)


def _block_diag2(a, b):
    z = jnp.zeros_like(a)
    return jnp.concatenate([jnp.concatenate([a, z], axis=1), jnp.concatenate([z, b], axis=1)], axis=0)


def _nbr_bias_table(rpb):
    qc = jnp.arange(GRID_W)[:, None]
    kc = jnp.arange(GRID_W)[None, :]
    cstart = jnp.clip(qc - WIN_C // 2, 0, GRID_W - WIN_C)
    valid = jnp.logical_and(kc >= cstart, kc < cstart + WIN_C)
    nr = rpb.shape[1]
    period = 2 * GRID_W
    p = jnp.concatenate([rpb[:, :, WIN_C - 1:], jnp.zeros(rpb.shape[:2] + (period - 2 * WIN_C + 1,), F32),
                         rpb[:, :, :WIN_C - 1]], axis=-1)
    flat = jnp.tile(p, (1, 1, GRID_W))[:, :, :GRID_W * (period - 1)]
    toep = flat.reshape(rpb.shape[0], nr, GRID_W, period - 1)[..., :GRID_W]
    tz = jnp.where(valid, toep, NEG)
    return jnp.concatenate([tz[:, :-1], tz[:, 1:]], axis=-1)


def kernel(x_prompt, x_sample, c, state_rwkv, cache_na_k, cache_na_v, c_ctx, norm_mix, norm_ffn, ada_w, ada_b, rwkv_mu, rwkv_w_r, rwkv_w_k, rwkv_w_v, rwkv_w_o, rwkv_w0, rwkv_w1, rwkv_w2, rwkv_a0, rwkv_a1, rwkv_a2, rwkv_g1, rwkv_g2, rwkv_k_k, rwkv_k_a, rwkv_r_k, rwkv_lnx_g, rwkv_lnx_b, na_w_qkv, na_w_o, na_rpb, moe_w_grp, moe_b_grp, moe_w_exp, moe_b_exp, moe_w1, moe_w3, moe_w2, final_norm):
    x_p = x_prompt.reshape(NPR, D)
    x_s = x_sample.reshape(NSR, D)
    c8 = jnp.concatenate([c_ctx[None, :], c, jnp.zeros((8 - 1 - NB_S, D), F32)], axis=0)
    mod = _adaln(c8, ada_w, ada_b)
    ri = jnp.arange(2 * LANES)[:, None] // HD
    bo = (ri == ri.T).astype(BF)
    row = lambda a: a.reshape(1, D)

    w1c = jnp.concatenate([rwkv_w1[0, 0], rwkv_w1[0, 1]], axis=1).astype(BF)
    a1c = jnp.concatenate([rwkv_a1[0, 0], rwkv_a1[0, 1]], axis=1).astype(BF)
    w2bd = _block_diag2(rwkv_w2[0, 0], rwkv_w2[0, 1]).astype(BF)
    a2bd = _block_diag2(rwkv_a2[0, 0], rwkv_a2[0, 1]).astype(BF)
    r, k, v, g, kk, lw0, lw1, ag0, ag1 = _rwkv_front(
        x_p, x_s, mod[0], row(norm_mix[0]), rwkv_mu[0], rwkv_w_r[0].astype(BF), rwkv_w_k[0].astype(BF),
        rwkv_w_v[0].astype(BF), w1c, a1c, rwkv_g1[0].astype(BF), w2bd, a2bd, rwkv_g2[0].astype(BF),
        rwkv_w0[0], rwkv_a0[0], row(rwkv_k_k[0]), bo)
    arrs = (r, k, v, kk, lw0, lw1, ag0, ag1)
    ka, rk = row(rwkv_k_a[0]), row(rwkv_r_k[0])
    ys_p, bon_p, new_state = _wkv(arrs, ka, rk, T_P, NB_P, 0)
    s0 = state_rwkv[:, 0].reshape(NB_S, 2, NH // 2, 2, HD, HD)
    z = jnp.zeros_like(s0[:, :, :, 0])
    s0_bd = jnp.concatenate([jnp.concatenate([s0[:, :, :, 0], z], axis=-1),
                             jnp.concatenate([z, s0[:, :, :, 1]], axis=-1)], axis=-2)
    ys_s, bon_s = _wkv(arrs, ka, rk, T_S, NB_S, NPR // T_S, s0_bd=s0_bd)
    new_state_rwkv = new_state.reshape(NB_P, 1, 2, NH, HD, HD)
    y = _proj_residual((x_p, x_s), ys_p, ys_s, mod[0], rwkv_w_o[0].astype(BF),
                       gn_args=(bon_p, bon_s, g, row(rwkv_lnx_g[0]), row(rwkv_lnx_b[0]), bo),
                       name="rwkv_out")

    def moe(y, i, final):
        wr = jnp.concatenate([moe_w_exp[i], moe_w_grp[i],
                              jnp.zeros((D, LANES - NE - NE // EPG), F32)], axis=1)
        br = jnp.concatenate([moe_b_exp[i], moe_b_grp[i],
                              jnp.zeros((LANES - NE - NE // EPG,), F32)]).reshape(1, LANES)
        xt, gate = _moe_router(y, mod[i], row(norm_ffn[i]), wr, br)
        w13 = jnp.concatenate([moe_w1[i], moe_w3[i]], axis=-1).astype(BF)
        return _moe_experts(xt, gate, w13, moe_w2[i].astype(BF), y, mod[i], row(final_norm), final)

    y = moe(y, 0, False)

    q, k_p, v_p, k_s, v_s, kcache, vcache = _na_qkv(y, mod[1], row(norm_mix[1]), na_w_qkv[0].astype(BF))
    o_p = _ctx_attn(q, k_p, v_p)
    kc = cache_na_k[:, 0].reshape(NB_S * cache_na_k.shape[2], D)
    vc = cache_na_v[:, 0].reshape(NB_S * cache_na_v.shape[2], D)
    o_s = _nbr_attn(q, k_s, v_s, kc, vc, _nbr_bias_table(na_rpb[0]))
    y = _proj_residual(y, o_p, o_s, mod[1], na_w_o[0].astype(BF), name="attn_out")
    y_p, y_s = moe(y, 1, True)

    y_prompt = y_p.reshape(NB_P, T_P, D)
    y_sample = y_s.reshape(NB_S, T_S, D)
    new_k = kcache.reshape(NB_P, 1, T_P, NH, HD)
    new_v = vcache.reshape(NB_P, 1, T_P, NH, HD)
    return (y_prompt, y_sample, new_state_rwkv, new_k, new_v)
```

```python
import functools
import math

import jax
import jax.numpy as jnp
from jax import lax
from jax.experimental import pallas as pl
from jax.experimental.pallas import tpu as pltpu

F32 = jnp.float32
BF = jnp.bfloat16

D = 1024
NH = 16
HD = 64
NB_P, T_P = 32, 256
NB_S, T_S = 2, 1024
NPR = NB_P * T_P
NSR = NB_S * T_S
NTOK = NPR + NSR
TM = 256
NT = NTOK // TM
NT_P = NPR // TM
TPS = T_S // TM
TMX = 1024
NMOD = 6
NE = 16
EPG = 4
DE = 256
EPS = 4
CH = 64
SB = 32
LANES = 128
GRID_W = 64
WIN_R, WIN_C = 8, 16
NBR_UNROLL = 4
CTX_PAIRS = 2
NORM_EPS = 1e-6
GN_EPS = 64e-5
NEG = -1e30
VMEM_LIMIT = 56 * 1024 * 1024


def _cp(sem):
    return pltpu.CompilerParams(dimension_semantics=sem, vmem_limit_bytes=VMEM_LIMIT)


def _bdot(a, b):
    return jnp.dot(a.astype(BF), b.astype(BF), preferred_element_type=F32)


def _bdot_nt(a, b):
    return lax.dot_general(a.astype(BF), b.astype(BF), (((1,), (1,)), ((), ())),
                           preferred_element_type=F32)


def _split2(x):
    hi = x.astype(BF)
    lo = (x - hi.astype(F32)).astype(BF)
    return hi, lo


def _split3(x):
    hi = x.astype(BF)
    r1 = x - hi.astype(F32)
    mid = r1.astype(BF)
    lo = (r1 - mid.astype(F32)).astype(BF)
    return hi, mid, lo


def _seg_sum(x, bo):
    outs = []
    for c in range(x.shape[1] // 256):
        hi, lo = _split2(x[:, c * 256:(c + 1) * 256])
        outs.append(jnp.dot(hi, bo, preferred_element_type=F32)
                    + jnp.dot(lo, bo, preferred_element_type=F32))
    return outs[0] if len(outs) == 1 else jnp.concatenate(outs, axis=1)


def _tile_mod_row(i):
    return jnp.where(i < NT_P, 0, 1 + (i - NT_P) // TPS)


def _modulate(x, g, sh, sc):
    ms = jnp.mean(x * x, axis=-1, keepdims=True)
    return x * lax.rsqrt(ms + NORM_EPS) * g * (1.0 + sc) + sh


def _mod_vec(mod_ref, row, k):
    return mod_ref[pl.ds(row, 1), pl.ds(k * D, D)]


def _adaln_kernel(c_ref, w_ref, b_ref, o_ref):
    c = c_ref[...]
    s = c * jax.nn.sigmoid(c)
    o_ref[...] = _bdot(s, w_ref[...]) + b_ref[...]


def _adaln(c8, ada_w, ada_b):
    nl = ada_w.shape[0]
    tn = 1536
    return pl.pallas_call(
        _adaln_kernel,
        grid=(nl, NMOD * D // tn),
        in_specs=[pl.BlockSpec((8, D), lambda l, j: (0, 0)),
                  pl.BlockSpec((None, D, tn), lambda l, j: (l, 0, j)),
                  pl.BlockSpec((None, 1, tn), lambda l, j: (l, 0, j))],
        out_specs=pl.BlockSpec((None, 8, tn), lambda l, j: (l, 0, j)),
        out_shape=jax.ShapeDtypeStruct((nl, 8, NMOD * D), F32),
        compiler_params=_cp(("parallel", "parallel")),
        name="adaln",
    )(c8, ada_w, ada_b.reshape(nl, 1, NMOD * D))


def _front_kernel(xp_ref, xs_ref, yp_ref, yn_ref, mod_ref, nrm_ref, mu_ref, wr_ref, wk_ref, wv_ref,
                  w1_ref, a1_ref, g1_ref, w2_ref, a2_ref, g2_ref, w0_ref, a0_ref, kkw_ref, bo_ref,
                  r_o, k_o, v_o, g_o, kk_o, lw0_o, lw1_o, ag0_o, ag1_o):
    i = pl.program_id(0)
    row = _tile_mod_row(i)
    q = (i - NT_P) % TPS
    first = jnp.logical_or(i < NT_P, q == 0)
    last = jnp.logical_or(i < NT_P, q == TPS - 1)
    sh = _mod_vec(mod_ref, row, 0)
    sc = _mod_vec(mod_ref, row, 1)
    g = nrm_ref[...]
    h = _modulate(jnp.where(i < NT_P, xp_ref[...], xs_ref[...]), g, sh, sc)
    hp = _modulate(yp_ref[...], g, sh, sc)[7:8]
    hn = _modulate(yn_ref[...], g, sh, sc)[0:1]
    hp = jnp.where(first, 0.0, hp)
    hn = jnp.where(last, 0.0, hn)
    rid = lax.broadcasted_iota(jnp.int32, (TM, D), 0)
    prev = jnp.where(rid == 0, hp, pltpu.roll(h, 1, 0))
    nxt = jnp.where(rid == TM - 1, hn, pltpu.roll(h, TM - 1, 0))
    dx = 0.5 * (prev + nxt) - h

    def mix(n):
        return (h + dx * mu_ref[n:n + 1, :]).astype(BF)

    r_o[...] = jnp.dot(mix(0), wr_ref[...], preferred_element_type=F32)
    xw = mix(1)
    lora_w = _bdot(jnp.tanh(jnp.dot(xw, w1_ref[...], preferred_element_type=F32)), w2_ref[...])
    c05 = math.exp(-0.5)
    lw0_o[...] = -c05 * jax.nn.sigmoid(w0_ref[0:1, :] + lora_w[:, :D])
    lw1_o[...] = -c05 * jax.nn.sigmoid(w0_ref[1:2, :] + lora_w[:, D:])
    k_raw = jnp.dot(mix(2), wk_ref[...], preferred_element_type=F32)
    k_o[...] = k_raw
    kq = k_raw * kkw_ref[...]
    ss = _seg_sum(kq * kq, bo_ref[...])
    kk_o[...] = kq / jnp.maximum(jnp.sqrt(ss), 1e-12)
    v_o[...] = jnp.dot(mix(3), wv_ref[...], preferred_element_type=F32)
    xa = mix(4)
    lora_a = _bdot(jnp.dot(xa, a1_ref[...], preferred_element_type=F32), a2_ref[...])
    ag0_o[...] = jax.nn.sigmoid(a0_ref[0:1, :] + lora_a[:, :D])
    ag1_o[...] = jax.nn.sigmoid(a0_ref[1:2, :] + lora_a[:, D:])
    xg = mix(5)
    g_o[...] = _bdot(jax.nn.sigmoid(jnp.dot(xg, g1_ref[...], preferred_element_type=F32)), g2_ref[...])


def _rwkv_front(x_p, x_s, mod, nrm, mu, wr, wk, wv, w1c, a1c, g1, w2bd, a2bd, g2, w0, a0, kkw, bo):
    tile = pl.BlockSpec((TM, D), lambda i: (i, 0))
    nblk8 = NSR // 8
    bpt = TM // 8
    full = lambda a: pl.BlockSpec(a.shape, lambda i: (0,) * a.ndim)
    ins = [x_p, x_s, x_s, x_s, mod, nrm, mu, wr, wk, wv, w1c, a1c, g1, w2bd, a2bd, g2, w0, a0, kkw, bo]
    in_specs = [pl.BlockSpec((TM, D), lambda i: (jnp.minimum(i, NT_P - 1), 0)),
                pl.BlockSpec((TM, D), lambda i: (jnp.maximum(i - NT_P, 0), 0)),
                pl.BlockSpec((8, D), lambda i: (jnp.clip((i - NT_P) * bpt - 1, 0, nblk8 - 1), 0)),
                pl.BlockSpec((8, D), lambda i: (jnp.clip((i - NT_P + 1) * bpt, 0, nblk8 - 1), 0))]
    in_specs += [full(a) for a in ins[4:]]
    out = jax.ShapeDtypeStruct((NTOK, D), F32)
    return pl.pallas_call(
        _front_kernel,
        grid=(NT,),
        in_specs=in_specs,
        out_specs=[tile] * 9,
        out_shape=[out] * 9,
        compiler_params=_cp(("parallel",)),
        name="rwkv_front",
    )(*ins)


def _wkv_chunk(d, S, r, k, v, kk, lw, ag, ka, rk, cst):
    tri4, strict, incl, h0, bd, bo2, same_blk, bd_sb, eye_c, lane_blk = cst
    dot = lambda a, b: jnp.dot(a, b, preferred_element_type=F32)
    nblk = CH // SB
    h0s = lax.broadcasted_iota(jnp.int32, (SB, LANES), 1) < HD

    def stack_heads(x, swap=False):
        zero = jnp.zeros_like(x)
        parts = [jnp.where(h0, x, zero), jnp.where(h0, zero, x)]
        return jnp.concatenate(parts[::-1] if swap else parts, axis=0)

    def dot_split(a, bm):
        return dot(a.astype(BF), bm.astype(BF))

    def expand(mc):
        return jnp.where(bd_sb, jnp.concatenate([mc] * (LANES // SB), axis=0), 0.0)

    b = kk * ag
    kd = k * (1.0 + (ag - 1.0) * ka)
    hi, mid, lo = _split3(lw)
    cs = dot(tri4[d], jnp.concatenate([hi, mid, lo, jnp.zeros_like(lo)], axis=0))
    qh, ql = _split2(r * kd * rk)
    bonus = (dot(qh, bo2) + dot(ql, bo2)) * v
    yield
    tot = cs[CH - 1:CH, :] if d == 0 else cs[0:1, :]
    p_inv = jnp.exp(-cs)
    p_end = jnp.exp(tot - cs)
    at = -kk * jnp.exp(cs - lw)
    rt = r * jnp.exp(cs)
    bt = (b * p_inv).astype(BF)
    kt = (kd * p_inv).astype(BF)
    ar = jnp.concatenate([at, rt], axis=0)
    h02 = jnp.concatenate([h0, h0], axis=0)
    g0 = _bdot_nt(jnp.where(h02, ar, 0.0), jnp.concatenate([bt, kt], axis=0))
    g1 = _bdot_nt(jnp.where(h02, 0.0, ar), jnp.concatenate([kt, bt], axis=0))
    w0y0 = _bdot_nt(ar, S)
    yield
    w0, y0 = w0y0[:CH], w0y0[CH:]
    lab = jnp.where(strict[d], jnp.where(h0, g0[:CH], g1[:CH]), 0.0)
    lak = jnp.where(strict[d], jnp.where(h0, g1[:CH], g0[:CH]), 0.0)
    mrb = jnp.where(incl[d], jnp.where(h0, g0[CH:], g1[CH:]), 0.0)
    mrk = jnp.where(incl[d], jnp.where(h0, g1[CH:], g0[CH:]), 0.0)
    vstk_sw = stack_heads(v, swap=True).astype(BF)
    w = w0 + dot(lak.astype(BF), vstk_sw)
    ldiag = jnp.where(same_blk, lab, 0.0)
    loff = jnp.where(same_blk, 0.0, lab)
    mc = ldiag[0:SB]
    for i in range(1, nblk):
        mc = mc + ldiag[i * SB:(i + 1) * SB]
    xc = eye_c + mc
    mc = dot_split(mc, expand(mc))
    yield
    nsq = int(math.log2(SB)) - 1
    for it in range(nsq):
        last = it + 1 == nsq
        res = dot_split(xc if last else jnp.concatenate([xc, mc], axis=0), expand(mc))
        xc = xc + res[:SB]
        if not last:
            mc = res[SB:]
        yield
    ublk = [None] * nblk
    order = list(range(nblk)) if d == 0 else list(range(nblk - 1, -1, -1))
    for n, i in enumerate(order):
        wi = w[i * SB:(i + 1) * SB]
        if n > 0:
            ucur = jnp.concatenate([jnp.zeros((SB, LANES), F32) if ub is None else ub
                                    for ub in ublk], axis=0)
            wi = wi + dot(loff[i * SB:(i + 1) * SB].astype(BF), stack_heads(ucur).astype(BF))
            yield
        m0 = jnp.where(h0s, wi, 0.0).astype(BF)
        m1 = jnp.where(h0s, 0.0, wi).astype(BF)
        rhs = jnp.concatenate([m0] * nblk + [m1] * nblk, axis=0)
        ublk[i] = dot(jnp.where(lane_blk[i], xc, 0.0).astype(BF), rhs)
        yield
    u = jnp.concatenate(ublk, axis=0)
    y = dot(jnp.concatenate([mrb, mrk], axis=1).astype(BF),
            jnp.concatenate([stack_heads(u).astype(BF), vstk_sw], axis=0))
    uvt = jnp.concatenate([u, v], axis=0).T
    ds = _bdot(uvt, jnp.concatenate([b * p_end, kd * p_end], axis=0))
    yield
    s_new = S * jnp.exp(tot) + jnp.where(bd, ds, 0.0)
    return s_new, y0 + y, bonus


def _run_lockstep(gens):
    results = [None] * len(gens)
    pending = list(range(len(gens)))
    while pending:
        for i in list(pending):
            try:
                next(gens[i])
            except StopIteration as stop:
                results[i] = stop.value
                pending.remove(i)
    return results


def _wkv_consts():
    t = lax.broadcasted_iota(jnp.int32, (CH, 4 * CH), 0)
    s = lax.broadcasted_iota(jnp.int32, (CH, 4 * CH), 1) & (CH - 1)
    tri4 = [jnp.where(s <= t, 1.0, 0.0).astype(BF), jnp.where(s >= t, 1.0, 0.0).astype(BF)]
    t2 = lax.broadcasted_iota(jnp.int32, (CH, LANES), 0)
    l2 = lax.broadcasted_iota(jnp.int32, (CH, LANES), 1)
    s2 = l2 & (CH - 1)
    strict = [s2 < t2, s2 > t2]
    incl = [s2 <= t2, s2 >= t2]
    h0 = l2 < HD
    same_blk = (t2 // SB) == (s2 // SB)
    ri = lax.broadcasted_iota(jnp.int32, (LANES, LANES), 0)
    ci = lax.broadcasted_iota(jnp.int32, (LANES, LANES), 1)
    bd = (ri // HD) == (ci // HD)
    bo2 = jnp.where(bd, 1.0, 0.0).astype(BF)
    bd_sb = (ri // SB) == (ci // SB)
    t3 = lax.broadcasted_iota(jnp.int32, (SB, LANES), 0)
    l3 = lax.broadcasted_iota(jnp.int32, (SB, LANES), 1)
    eye_c = jnp.where((l3 & (SB - 1)) == t3, 1.0, 0.0)
    lane_blk = [((l3 & (CH - 1)) // SB) == i for i in range(CH // SB)]
    return tri4, strict, incl, h0, bd, bo2, same_blk, bd_sb, eye_c, lane_blk


def _wkv_kernel(T, has_s0, *refs):
    (r_ref, k_ref, v_ref, kk_ref, lw0_ref, lw1_ref, ag0_ref, ag1_ref, ka_ref, rk_ref), rest = refs[:10], refs[10:]
    if has_s0:
        s0_ref, y_ref, bon_ref, s_scr = rest
    else:
        y_ref, bon_ref, st_ref, s_scr = rest
    nch = T // CH
    npair = y_ref.shape[1] // LANES
    cst = _wkv_consts()
    if has_s0:
        s_scr[...] = s0_ref[...]
    else:
        s_scr[...] = jnp.zeros_like(s_scr)
    y_ref[...] = jnp.zeros_like(y_ref)
    bon_ref[...] = jnp.zeros_like(bon_ref)
    lw_refs = (lw0_ref, lw1_ref)
    ag_refs = (ag0_ref, ag1_ref)

    def body(c, carry):
        chains = [(p, d) for p in range(npair) for d in range(2)]
        sl = {}
        for p, d in chains:
            r0 = pl.multiple_of((c if d == 0 else nch - 1 - c) * CH, CH)
            sl[p, d] = (pl.ds(r0, CH), pl.ds(p * LANES, LANES))
        args = {}
        for p, d in chains:
            rs, ls = sl[p, d]
            args[p, d] = (s_scr[d, p], r_ref[rs, ls], k_ref[rs, ls], v_ref[rs, ls], kk_ref[rs, ls],
                          lw_refs[d][rs, ls], ag_refs[d][rs, ls], ka_ref[:, ls], rk_ref[:, ls],
                          y_ref[rs, ls], bon_ref[rs, ls])
        outs = _run_lockstep([_wkv_chunk(d, *args[p, d][:9], cst) for p, d in chains])
        for (p, d), (s_new, y, bon) in zip(chains, outs):
            rs, ls = sl[p, d]
            s_scr[d, p] = s_new
            y_ref[rs, ls] = args[p, d][9] + y
            bon_ref[rs, ls] = args[p, d][10] + bon
        return carry

    lax.fori_loop(0, nch, body, 0)
    if not has_s0:
        for d in range(2):
            for p in range(npair):
                s = s_scr[d, p]
                st_ref[d, 2 * p] = s[:HD, :HD]
                st_ref[d, 2 * p + 1] = s[HD:, HD:]


def _wkv(arrs, ka, rk, T, nb, row_blk0, s0_bd=None, lanes=512):
    ng = D // lanes
    npair = lanes // LANES
    seq = pl.BlockSpec((T, lanes), lambda b, g: (row_blk0 + b, g))
    vec = pl.BlockSpec((1, lanes), lambda b, g: (0, g))
    st = pl.BlockSpec((None, 2, npair, LANES, LANES), lambda b, g: (b, 0, g, 0, 0))
    out_seq = pl.BlockSpec((T, lanes), lambda b, g: (b, g))
    in_specs = [seq] * 8 + [vec, vec]
    ins = list(arrs) + [ka, rk]
    out_specs = [out_seq, out_seq]
    out_shape = [jax.ShapeDtypeStruct((nb * T, D), F32)] * 2
    if s0_bd is not None:
        in_specs.append(st)
        ins.append(s0_bd)
    else:
        out_specs.append(pl.BlockSpec((None, 2, 2 * npair, HD, HD), lambda b, g: (b, 0, g, 0, 0)))
        out_shape.append(jax.ShapeDtypeStruct((nb, 2, NH, HD, HD), F32))
    return pl.pallas_call(
        functools.partial(_wkv_kernel, T, s0_bd is not None),
        grid=(nb, ng),
        in_specs=in_specs,
        out_specs=out_specs,
        out_shape=out_shape,
        scratch_shapes=[pltpu.VMEM((2, npair, LANES, LANES), F32)],
        compiler_params=_cp(("parallel", "parallel")),
        name="wkv_T%d" % T,
    )(*ins)


def _proj_kernel(gn, *refs):
    i = pl.program_id(0)
    is_p = i < NT_P
    if gn:
        (yp_ref, ys_ref, ap_ref, as_ref, bp_ref, bs_ref, g_ref, lng_ref, lnb_ref, bo_ref,
         mod_ref, w_ref, o_ref) = refs
        y = jnp.where(is_p, yp_ref[...], ys_ref[...])
    else:
        y_ref, ap_ref, as_ref, mod_ref, w_ref, o_ref = refs
        y = y_ref[...]
    a = jnp.where(is_p, ap_ref[...], as_ref[...])
    if gn:
        bo = bo_ref[...]
        mean = _seg_sum(a, bo) * (1.0 / HD)
        cen = a - mean
        var = _seg_sum(cen * cen, bo) * (1.0 / HD)
        yn = cen * lax.rsqrt(var + GN_EPS) * lng_ref[...] + lnb_ref[...]
        a = (yn + jnp.where(is_p, bp_ref[...], bs_ref[...])) * g_ref[...]
    gt = _mod_vec(mod_ref, _tile_mod_row(i), 2)
    o_ref[...] = y + gt * _bdot(a, w_ref[...])


def _proj_residual(y, a_p, a_s, mod, w, gn_args=None, name="proj"):
    tile = pl.BlockSpec((TM, D), lambda i: (i, 0))
    tile_p = pl.BlockSpec((TM, D), lambda i: (jnp.minimum(i, NT_P - 1), 0))
    tile_s = pl.BlockSpec((TM, D), lambda i: (jnp.maximum(i - NT_P, 0), 0))
    full = lambda a: pl.BlockSpec(a.shape, lambda i: (0,) * a.ndim)
    if gn_args is not None:
        ins = [y[0], y[1], a_p, a_s]
        in_specs = [tile_p, tile_s, tile_p, tile_s]
    else:
        ins = [y, a_p, a_s]
        in_specs = [tile, tile_p, tile_s]
    if gn_args is not None:
        b_p, b_s, g, lng, lnb, bo = gn_args
        ins += [b_p, b_s, g, lng, lnb, bo]
        in_specs += [tile_p, tile_s, tile, full(lng), full(lnb), full(bo)]
    ins += [mod, w]
    in_specs += [full(mod), full(w)]
    return pl.pallas_call(
        functools.partial(_proj_kernel, gn_args is not None),
        grid=(NT,),
        in_specs=in_specs,
        out_specs=tile,
        out_shape=jax.ShapeDtypeStruct((NTOK, D), F32),
        compiler_params=_cp(("parallel",)),
        name=name,
    )(*ins)


def _router_kernel(y_ref, mod_ref, nrm_ref, wr_ref, br_ref, xt_o, gate_o):
    i = pl.program_id(0)
    row = _tile_mod_row(i)
    xt = _modulate(y_ref[...], nrm_ref[...], _mod_vec(mod_ref, row, 3), _mod_vec(mod_ref, row, 4))
    xt_o[...] = xt.astype(BF)
    x1, x2, x3 = _split3(xt)
    w1, w2, w3 = _split3(wr_ref[...])
    dot = lambda a, b: jnp.dot(a, b, preferred_element_type=F32)
    logits = (dot(x3, w1) + dot(x2, w2) + dot(x1, w3)) + (dot(x2, w1) + dot(x1, w2)) + dot(x1, w1)
    logits = logits + br_ref[...]
    lane = lax.broadcasted_iota(jnp.int32, logits.shape, 1).astype(F32)
    isg = jnp.logical_and(lane >= NE, lane < NE + NE // EPG)
    mg = jnp.max(jnp.where(isg, logits, NEG), axis=-1, keepdims=True)
    eg = jnp.where(isg, jnp.exp(jnp.minimum(logits - mg, 0.0)), 0.0)
    p_sel = 1.0 / jnp.sum(eg, axis=-1, keepdims=True)
    gidx = jnp.min(jnp.where(jnp.logical_and(isg, logits == mg), lane, 1e3), axis=-1, keepdims=True) - NE
    lo_l = gidx * EPG
    ing = jnp.logical_and(lane >= lo_l, lane < lo_l + EPG)
    me = jnp.max(jnp.where(ing, logits, NEG), axis=-1, keepdims=True)
    ee = jnp.where(ing, jnp.exp(jnp.minimum(logits - me, 0.0)), 0.0)
    se = jnp.sum(ee, axis=-1, keepdims=True)
    pe = ee / se
    i1 = jnp.min(jnp.where(jnp.logical_and(ing, logits == me), lane, 1e3), axis=-1, keepdims=True)
    v1 = 1.0 / se
    rest = jnp.logical_and(ing, lane != i1)
    v2 = jnp.max(jnp.where(rest, pe, -1.0), axis=-1, keepdims=True)
    i2 = jnp.min(jnp.where(jnp.logical_and(rest, pe == v2), lane, 1e3), axis=-1, keepdims=True)
    den = v1 + v2
    gate_o[...] = (jnp.where(lane == i1, p_sel * v1 / den, 0.0)
                   + jnp.where(lane == i2, p_sel * v2 / den, 0.0))


def _moe_router(y, mod, nrm, wr, br):
    tile = pl.BlockSpec((TM, D), lambda i: (i, 0))
    full = lambda a: pl.BlockSpec(a.shape, lambda i: (0,) * a.ndim)
    return pl.pallas_call(
        _router_kernel,
        grid=(NT,),
        in_specs=[tile, full(mod), full(nrm), full(wr), full(br)],
        out_specs=[tile, pl.BlockSpec((TM, LANES), lambda i: (i, 0))],
        out_shape=[jax.ShapeDtypeStruct((NTOK, D), BF), jax.ShapeDtypeStruct((NTOK, LANES), F32)],
        compiler_params=_cp(("parallel",)),
        name="moe_router",
    )(y, mod, nrm, wr, br)


def _experts_kernel(final, xt_ref, gate_ref, w13_ref, w2_ref, y_ref, mod_ref, fin_ref, *rest):
    i = pl.program_id(0)
    e = pl.program_id(1)
    acc_ref = rest[-1]
    ntile_p = NPR // TMX

    x = xt_ref[...]
    gate = gate_ref[...]
    lane = lax.broadcasted_iota(jnp.int32, gate_ref.shape, 1)
    hids = []
    for j in range(EPS):
        h13 = jnp.dot(x, w13_ref[j], preferred_element_type=F32)
        h1, h3 = h13[:, :DE], h13[:, DE:]
        ge = jnp.sum(jnp.where(lane == e * EPS + j, gate, 0.0), axis=-1, keepdims=True)
        hids.append(((h1 * jax.nn.sigmoid(h1)) * h3 * ge).astype(BF))
    hid = jnp.concatenate(hids, axis=1)

    @pl.when(e == 0)
    def _():
        acc_ref[...] = jnp.zeros_like(acc_ref)

    acc_ref[...] += jnp.dot(hid, w2_ref[...].reshape(EPS * DE, D), preferred_element_type=F32)

    def result():
        row = jnp.where(i < ntile_p, 0, i - ntile_p + 1)
        out = y_ref[...] + _mod_vec(mod_ref, row, 5) * acc_ref[...]
        if final:
            ms = jnp.mean(out * out, axis=-1, keepdims=True)
            out = out * lax.rsqrt(ms + NORM_EPS) * fin_ref[...]
        return out

    if final:
        op_ref, os_ref = rest[:2]

        @pl.when(jnp.logical_and(e == NE // EPS - 1, i < ntile_p))
        def _():
            op_ref[...] = result()

        @pl.when(jnp.logical_and(e == NE // EPS - 1, i >= ntile_p))
        def _():
            os_ref[...] = result()
    else:
        @pl.when(e == NE // EPS - 1)
        def _():
            rest[0][...] = result()


def _moe_experts(xt, gate, w13, w2, y, mod, fin, final):
    tile = pl.BlockSpec((TMX, D), lambda i, e: (i, 0))
    full = lambda a: pl.BlockSpec(a.shape, lambda i, e: (0,) * a.ndim)
    ntile_p = NPR // TMX
    if final:
        out_specs = [pl.BlockSpec((TMX, D), lambda i, e: (jnp.minimum(i, ntile_p - 1), 0)),
                     pl.BlockSpec((TMX, D), lambda i, e: (jnp.maximum(i - ntile_p, 0), 0))]
        out_shape = [jax.ShapeDtypeStruct((NPR, D), F32), jax.ShapeDtypeStruct((NSR, D), F32)]
    else:
        out_specs = tile
        out_shape = jax.ShapeDtypeStruct((NTOK, D), F32)
    return pl.pallas_call(
        functools.partial(_experts_kernel, final),
        grid=(NTOK // TMX, NE // EPS),
        in_specs=[tile, pl.BlockSpec((TMX, LANES), lambda i, e: (i, 0)),
                  pl.BlockSpec((EPS, D, 2 * DE), lambda i, e: (e, 0, 0)),
                  pl.BlockSpec((EPS, DE, D), lambda i, e: (e, 0, 0)),
                  tile, full(mod), full(fin)],
        out_specs=out_specs,
        out_shape=out_shape,
        scratch_shapes=[pltpu.VMEM((TMX, D), F32)] if final else [],
        compiler_params=_cp(("arbitrary" if final else "parallel", "arbitrary")),
        name="moe_experts",
    )(xt, gate, w13, w2, y, mod, fin)


def _qkv_kernel(y_ref, mod_ref, nrm_ref, w_ref, q_o, kp_o, vp_o, ks_o, vs_o, kc_o, vc_o):
    i = pl.program_id(0)
    row = _tile_mod_row(i)
    h = _modulate(y_ref[...], nrm_ref[...], _mod_vec(mod_ref, row, 0), _mod_vec(mod_ref, row, 1))
    qkv = _bdot(h, w_ref[...])
    q_o[...] = qkv[:, :D]

    @pl.when(i < NT_P)
    def _():
        kx = qkv[:, D:2 * D]
        vx = qkv[:, 2 * D:]
        kp_o[...] = kx
        vp_o[...] = vx
        for hh in range(NH):
            kc_o[pl.ds(hh, TM, stride=NH), :] = kx[:, hh * HD:(hh + 1) * HD]
            vc_o[pl.ds(hh, TM, stride=NH), :] = vx[:, hh * HD:(hh + 1) * HD]

    @pl.when(i >= NT_P)
    def _():
        ks_o[...] = qkv[:, D:2 * D]
        vs_o[...] = qkv[:, 2 * D:]


def _na_qkv(y, mod, nrm, w):
    tile = pl.BlockSpec((TM, D), lambda i: (i, 0))
    tile_p = pl.BlockSpec((TM, D), lambda i: (jnp.minimum(i, NT_P - 1), 0))
    tile_s = pl.BlockSpec((TM, D), lambda i: (jnp.maximum(i - NT_P, 0), 0))
    full = lambda a: pl.BlockSpec(a.shape, lambda i: (0,) * a.ndim)
    out = lambda n: jax.ShapeDtypeStruct((n, D), F32)
    cache = jax.ShapeDtypeStruct((NPR * NH, HD), F32)
    cache_tile = pl.BlockSpec((TM * NH, HD), lambda i: (jnp.minimum(i, NT_P - 1), 0))
    return pl.pallas_call(
        _qkv_kernel,
        grid=(NT,),
        in_specs=[tile, full(mod), full(nrm), full(w)],
        out_specs=[tile, tile_p, tile_p, tile_s, tile_s, cache_tile, cache_tile],
        out_shape=[out(NTOK), out(NPR), out(NPR), out(NSR), out(NSR), cache, cache],
        compiler_params=_cp(("arbitrary",)),
        name="na_qkv",
    )(y, mod, nrm, w)


def _softmax_rows(s):
    m = jnp.max(s, axis=-1, keepdims=True)
    e = jnp.exp(s - m)
    return e / jnp.sum(e, axis=-1, keepdims=True)


def _ctx_attn_kernel(q_ref, k_ref, v_ref, o_ref):
    scale = HD ** -0.5
    h0 = lax.broadcasted_iota(jnp.int32, (T_P, LANES), 1) < HD
    def head_chain(h, q, kb, vb):
        qm = jnp.where(h0 if h == 0 else jnp.logical_not(h0), q, 0.0)
        s = _bdot_nt(qm, kb)
        yield
        o = jnp.dot(_softmax_rows(s * scale).astype(BF), vb, preferred_element_type=F32)
        yield
        return o

    npair = D // LANES
    for p0 in range(0, npair, CTX_PAIRS):
        gens = []
        for p in range(p0, p0 + CTX_PAIRS):
            ls = pl.ds(p * LANES, LANES)
            q = q_ref[:, ls]
            kb = k_ref[:, ls].astype(BF)
            vb = v_ref[:, ls].astype(BF)
            gens += [head_chain(h, q, kb, vb) for h in range(2)]
        outs = _run_lockstep(gens)
        for j, p in enumerate(range(p0, p0 + CTX_PAIRS)):
            o_ref[:, pl.ds(p * LANES, LANES)] = jnp.where(h0, outs[2 * j], outs[2 * j + 1])


def _ctx_attn(q, k, v):
    blk = pl.BlockSpec((T_P, D), lambda b: (b, 0))
    return pl.pallas_call(
        _ctx_attn_kernel,
        grid=(NB_P,),
        in_specs=[blk] * 3,
        out_specs=blk,
        out_shape=jax.ShapeDtypeStruct((NPR, D), F32),
        compiler_params=_cp(("parallel",)),
        name="ctx_attn",
    )(q, k, v)


def _nbr_attn_kernel(q_ref, k_ref, v_ref, kc_ref, vc_ref, tz_ref, o_ref):
    scale = HD ** -0.5
    rows = T_S // GRID_W
    nloc = WIN_R * GRID_W
    h0 = lax.broadcasted_iota(jnp.int32, (GRID_W, LANES), 1) < HD
    kcb = kc_ref[...].astype(BF)
    vcb = vc_ref[...].astype(BF)

    def head_chain(h, q, klb, vlb, j0):
        qm = jnp.where(h0 if h == 0 else jnp.logical_not(h0), q, 0.0).astype(BF)
        sl = _bdot_nt(qm, klb)
        sc = _bdot_nt(qm, kcb)
        yield
        bias = jnp.concatenate([tz_ref[h, j0 + 2 * m] for m in range(WIN_R // 2)], axis=1)
        sl = sl * scale + bias
        sc = sc * scale
        m = jnp.maximum(jnp.max(sl, axis=-1, keepdims=True), jnp.max(sc, axis=-1, keepdims=True))
        el = jnp.exp(sl - m)
        ec = jnp.exp(sc - m)
        inv = 1.0 / (jnp.sum(el, axis=-1, keepdims=True) + jnp.sum(ec, axis=-1, keepdims=True))
        o = (jnp.dot((el * inv).astype(BF), vlb, preferred_element_type=F32)
             + jnp.dot((ec * inv).astype(BF), vcb, preferred_element_type=F32))
        yield
        return o

    def body(it, carry):
        gens, slices = [], []
        for j in range(NBR_UNROLL):
            r = it * NBR_UNROLL + j
            start = jnp.clip(r - WIN_R // 2, 0, rows - WIN_R)
            qs = pl.ds(pl.multiple_of(r * GRID_W, GRID_W), GRID_W)
            ks = pl.ds(pl.multiple_of(start * GRID_W, GRID_W), nloc)
            q = q_ref[qs, :]
            klb = k_ref[ks, :].astype(BF)
            vlb = v_ref[ks, :].astype(BF)
            j0 = start - r + WIN_R - 1
            slices.append(qs)
            gens += [head_chain(h, q, klb, vlb, j0) for h in range(2)]
        outs = _run_lockstep(gens)
        for j, qs in enumerate(slices):
            o_ref[qs, :] = jnp.where(h0, outs[2 * j], outs[2 * j + 1])
        return carry

    lax.fori_loop(0, rows // NBR_UNROLL, body, 0)


def _nbr_attn(q, k, v, kc, vc, tz2):
    npair = D // LANES
    qseq = pl.BlockSpec((T_S, LANES), lambda b, p: (NPR // T_S + b, p))
    seq = pl.BlockSpec((T_S, LANES), lambda b, p: (b, p))
    ctx = pl.BlockSpec((kc.shape[0] // NB_S, LANES), lambda b, p: (b, p))
    return pl.pallas_call(
        _nbr_attn_kernel,
        grid=(NB_S, npair),
        in_specs=[qseq, seq, seq, ctx, ctx,
                  pl.BlockSpec((2,) + tz2.shape[1:], lambda b, p: (p, 0, 0, 0))],
        out_specs=pl.BlockSpec((T_S, LANES), lambda b, p: (b, p)),
        out_shape=jax.ShapeDtypeStruct((NSR, D), F32),
        compiler_params=_cp(("parallel", "parallel")),
        name="nbr_attn",
    )(q, k, v, kc, vc, tz2)


def _block_diag2(a, b):
    z = jnp.zeros_like(a)
    return jnp.concatenate([jnp.concatenate([a, z], axis=1), jnp.concatenate([z, b], axis=1)], axis=0)


def _nbr_bias_table(rpb):
    qc = jnp.arange(GRID_W)[:, None]
    kc = jnp.arange(GRID_W)[None, :]
    cstart = jnp.clip(qc - WIN_C // 2, 0, GRID_W - WIN_C)
    valid = jnp.logical_and(kc >= cstart, kc < cstart + WIN_C)
    nr = rpb.shape[1]
    period = 2 * GRID_W
    p = jnp.concatenate([rpb[:, :, WIN_C - 1:], jnp.zeros(rpb.shape[:2] + (period - 2 * WIN_C + 1,), F32),
                         rpb[:, :, :WIN_C - 1]], axis=-1)
    flat = jnp.tile(p, (1, 1, GRID_W))[:, :, :GRID_W * (period - 1)]
    toep = flat.reshape(rpb.shape[0], nr, GRID_W, period - 1)[..., :GRID_W]
    tz = jnp.where(valid, toep, NEG)
    return jnp.concatenate([tz[:, :-1], tz[:, 1:]], axis=-1)


def kernel(x_prompt, x_sample, c, state_rwkv, cache_na_k, cache_na_v, c_ctx, norm_mix, norm_ffn, ada_w, ada_b, rwkv_mu, rwkv_w_r, rwkv_w_k, rwkv_w_v, rwkv_w_o, rwkv_w0, rwkv_w1, rwkv_w2, rwkv_a0, rwkv_a1, rwkv_a2, rwkv_g1, rwkv_g2, rwkv_k_k, rwkv_k_a, rwkv_r_k, rwkv_lnx_g, rwkv_lnx_b, na_w_qkv, na_w_o, na_rpb, moe_w_grp, moe_b_grp, moe_w_exp, moe_b_exp, moe_w1, moe_w3, moe_w2, final_norm):
    x_p = x_prompt.reshape(NPR, D)
    x_s = x_sample.reshape(NSR, D)
    c8 = jnp.concatenate([c_ctx[None, :], c, jnp.zeros((8 - 1 - NB_S, D), F32)], axis=0)
    mod = _adaln(c8, ada_w, ada_b)
    ri = jnp.arange(2 * LANES)[:, None] // HD
    bo = (ri == ri.T).astype(BF)
    row = lambda a: a.reshape(1, D)

    w1c = jnp.concatenate([rwkv_w1[0, 0], rwkv_w1[0, 1]], axis=1).astype(BF)
    a1c = jnp.concatenate([rwkv_a1[0, 0], rwkv_a1[0, 1]], axis=1).astype(BF)
    w2bd = _block_diag2(rwkv_w2[0, 0], rwkv_w2[0, 1]).astype(BF)
    a2bd = _block_diag2(rwkv_a2[0, 0], rwkv_a2[0, 1]).astype(BF)
    r, k, v, g, kk, lw0, lw1, ag0, ag1 = _rwkv_front(
        x_p, x_s, mod[0], row(norm_mix[0]), rwkv_mu[0], rwkv_w_r[0].astype(BF), rwkv_w_k[0].astype(BF),
        rwkv_w_v[0].astype(BF), w1c, a1c, rwkv_g1[0].astype(BF), w2bd, a2bd, rwkv_g2[0].astype(BF),
        rwkv_w0[0], rwkv_a0[0], row(rwkv_k_k[0]), bo)
    arrs = (r, k, v, kk, lw0, lw1, ag0, ag1)
    ka, rk = row(rwkv_k_a[0]), row(rwkv_r_k[0])
    ys_p, bon_p, new_state = _wkv(arrs, ka, rk, T_P, NB_P, 0, lanes=D)
    s0 = state_rwkv[:, 0].reshape(NB_S, 2, NH // 2, 2, HD, HD)
    z = jnp.zeros_like(s0[:, :, :, 0])
    s0_bd = jnp.concatenate([jnp.concatenate([s0[:, :, :, 0], z], axis=-1),
                             jnp.concatenate([z, s0[:, :, :, 1]], axis=-1)], axis=-2)
    ys_s, bon_s = _wkv(arrs, ka, rk, T_S, NB_S, NPR // T_S, s0_bd=s0_bd)
    new_state_rwkv = new_state.reshape(NB_P, 1, 2, NH, HD, HD)
    y = _proj_residual((x_p, x_s), ys_p, ys_s, mod[0], rwkv_w_o[0].astype(BF),
                       gn_args=(bon_p, bon_s, g, row(rwkv_lnx_g[0]), row(rwkv_lnx_b[0]), bo),
                       name="rwkv_out")

    def moe(y, i, final):
        wr = jnp.concatenate([moe_w_exp[i], moe_w_grp[i],
                              jnp.zeros((D, LANES - NE - NE // EPG), F32)], axis=1)
        br = jnp.concatenate([moe_b_exp[i], moe_b_grp[i],
                              jnp.zeros((LANES - NE - NE // EPG,), F32)]).reshape(1, LANES)
        xt, gate = _moe_router(y, mod[i], row(norm_ffn[i]), wr, br)
        w13 = jnp.concatenate([moe_w1[i], moe_w3[i]], axis=-1).astype(BF)
        return _moe_experts(xt, gate, w13, moe_w2[i].astype(BF), y, mod[i], row(final_norm), final)

    y = moe(y, 0, False)

    q, k_p, v_p, k_s, v_s, kcache, vcache = _na_qkv(y, mod[1], row(norm_mix[1]), na_w_qkv[0].astype(BF))
    o_p = _ctx_attn(q, k_p, v_p)
    kc = cache_na_k[:, 0].reshape(NB_S * cache_na_k.shape[2], D)
    vc = cache_na_v[:, 0].reshape(NB_S * cache_na_v.shape[2], D)
    o_s = _nbr_attn(q, k_s, v_s, kc, vc, _nbr_bias_table(na_rpb[0]))
    y = _proj_residual(y, o_p, o_s, mod[1], na_w_o[0].astype(BF), name="attn_out")
    y_p, y_s = moe(y, 1, True)

    y_prompt = y_p.reshape(NB_P, T_P, D)
    y_sample = y_s.reshape(NB_S, T_S, D)
    new_k = kcache.reshape(NB_P, 1, T_P, NH, HD)
    new_v = vcache.reshape(NB_P, 1, T_P, NH, HD)
    return (y_prompt, y_sample, new_state_rwkv, new_k, new_v)
```

```python
import functools
import math

import jax
import jax.numpy as jnp
from jax import lax
from jax.experimental import pallas as pl
from jax.experimental.pallas import tpu as pltpu

F32 = jnp.float32
BF = jnp.bfloat16

D = 1024
NH = 16
HD = 64
NB_P, T_P = 32, 256
NB_S, T_S = 2, 1024
NPR = NB_P * T_P
NSR = NB_S * T_S
NTOK = NPR + NSR
TM = 256
NT = NTOK // TM
NT_P = NPR // TM
TPS = T_S // TM
TMX = 1024
NMOD = 6
NE = 16
EPG = 4
DE = 256
EPS = 4
CH = 64
SB = 32
LANES = 128
GRID_W = 64
WIN_R, WIN_C = 8, 16
NBR_UNROLL = 4
CTX_PAIRS = 2
NORM_EPS = 1e-6
GN_EPS = 64e-5
NEG = -1e30
VMEM_LIMIT = 56 * 1024 * 1024


def _cp(sem):
    return pltpu.CompilerParams(dimension_semantics=sem, vmem_limit_bytes=VMEM_LIMIT)


def _bdot(a, b):
    return jnp.dot(a.astype(BF), b.astype(BF), preferred_element_type=F32)


def _bdot_nt(a, b):
    return lax.dot_general(a.astype(BF), b.astype(BF), (((1,), (1,)), ((), ())),
                           preferred_element_type=F32)


def _split2(x):
    hi = x.astype(BF)
    lo = (x - hi.astype(F32)).astype(BF)
    return hi, lo


def _split3(x):
    hi = x.astype(BF)
    r1 = x - hi.astype(F32)
    mid = r1.astype(BF)
    lo = (r1 - mid.astype(F32)).astype(BF)
    return hi, mid, lo


def _seg_sum(x, bo):
    outs = []
    for c in range(x.shape[1] // 256):
        hi, lo = _split2(x[:, c * 256:(c + 1) * 256])
        outs.append(jnp.dot(hi, bo, preferred_element_type=F32)
                    + jnp.dot(lo, bo, preferred_element_type=F32))
    return outs[0] if len(outs) == 1 else jnp.concatenate(outs, axis=1)


def _tile_mod_row(i):
    return jnp.where(i < NT_P, 0, 1 + (i - NT_P) // TPS)


def _modulate(x, g, sh, sc):
    ms = jnp.mean(x * x, axis=-1, keepdims=True)
    return x * lax.rsqrt(ms + NORM_EPS) * g * (1.0 + sc) + sh


def _mod_vec(mod_ref, row, k):
    return mod_ref[pl.ds(row, 1), pl.ds(k * D, D)]


def _adaln_kernel(c_ref, w_ref, b_ref, o_ref):
    c = c_ref[...]
    s = c * jax.nn.sigmoid(c)
    o_ref[...] = _bdot(s, w_ref[...]) + b_ref[...]


def _adaln(c8, ada_w, ada_b):
    nl = ada_w.shape[0]
    tn = 1536
    return pl.pallas_call(
        _adaln_kernel,
        grid=(nl, NMOD * D // tn),
        in_specs=[pl.BlockSpec((8, D), lambda l, j: (0, 0)),
                  pl.BlockSpec((None, D, tn), lambda l, j: (l, 0, j)),
                  pl.BlockSpec((None, 1, tn), lambda l, j: (l, 0, j))],
        out_specs=pl.BlockSpec((None, 8, tn), lambda l, j: (l, 0, j)),
        out_shape=jax.ShapeDtypeStruct((nl, 8, NMOD * D), F32),
        compiler_params=_cp(("parallel", "parallel")),
        name="adaln",
    )(c8, ada_w, ada_b.reshape(nl, 1, NMOD * D))


def _front_kernel(xp_ref, xs_ref, yp_ref, yn_ref, mod_ref, nrm_ref, mu_ref, wr_ref, wk_ref, wv_ref,
                  w1_ref, a1_ref, g1_ref, w2_ref, a2_ref, g2_ref, w0_ref, a0_ref, kkw_ref, bo_ref,
                  r_o, k_o, v_o, g_o, kk_o, lw0_o, lw1_o, ag0_o, ag1_o):
    i = pl.program_id(0)
    row = _tile_mod_row(i)
    q = (i - NT_P) % TPS
    first = jnp.logical_or(i < NT_P, q == 0)
    last = jnp.logical_or(i < NT_P, q == TPS - 1)
    sh = _mod_vec(mod_ref, row, 0)
    sc = _mod_vec(mod_ref, row, 1)
    g = nrm_ref[...]
    h = _modulate(jnp.where(i < NT_P, xp_ref[...], xs_ref[...]), g, sh, sc)
    hp = _modulate(yp_ref[...], g, sh, sc)[7:8]
    hn = _modulate(yn_ref[...], g, sh, sc)[0:1]
    hp = jnp.where(first, 0.0, hp)
    hn = jnp.where(last, 0.0, hn)
    rid = lax.broadcasted_iota(jnp.int32, (TM, D), 0)
    prev = jnp.where(rid == 0, hp, pltpu.roll(h, 1, 0))
    nxt = jnp.where(rid == TM - 1, hn, pltpu.roll(h, TM - 1, 0))
    dx = 0.5 * (prev + nxt) - h

    def mix(n):
        return (h + dx * mu_ref[n:n + 1, :]).astype(BF)

    r_o[...] = jnp.dot(mix(0), wr_ref[...], preferred_element_type=F32)
    xw = mix(1)
    lora_w = _bdot(jnp.tanh(jnp.dot(xw, w1_ref[...], preferred_element_type=F32)), w2_ref[...])
    c05 = math.exp(-0.5)
    lw0_o[...] = -c05 * jax.nn.sigmoid(w0_ref[0:1, :] + lora_w[:, :D])
    lw1_o[...] = -c05 * jax.nn.sigmoid(w0_ref[1:2, :] + lora_w[:, D:])
    k_raw = jnp.dot(mix(2), wk_ref[...], preferred_element_type=F32)
    k_o[...] = k_raw
    kq = k_raw * kkw_ref[...]
    ss = _seg_sum(kq * kq, bo_ref[...])
    kk_o[...] = kq / jnp.maximum(jnp.sqrt(ss), 1e-12)
    v_o[...] = jnp.dot(mix(3), wv_ref[...], preferred_element_type=F32)
    xa = mix(4)
    lora_a = _bdot(jnp.dot(xa, a1_ref[...], preferred_element_type=F32), a2_ref[...])
    ag0_o[...] = jax.nn.sigmoid(a0_ref[0:1, :] + lora_a[:, :D])
    ag1_o[...] = jax.nn.sigmoid(a0_ref[1:2, :] + lora_a[:, D:])
    xg = mix(5)
    g_o[...] = _bdot(jax.nn.sigmoid(jnp.dot(xg, g1_ref[...], preferred_element_type=F32)), g2_ref[...])


def _rwkv_front(x_p, x_s, mod, nrm, mu, wr, wk, wv, w1c, a1c, g1, w2bd, a2bd, g2, w0, a0, kkw, bo):
    tile = pl.BlockSpec((TM, D), lambda i: (i, 0))
    nblk8 = NSR // 8
    bpt = TM // 8
    full = lambda a: pl.BlockSpec(a.shape, lambda i: (0,) * a.ndim)
    ins = [x_p, x_s, x_s, x_s, mod, nrm, mu, wr, wk, wv, w1c, a1c, g1, w2bd, a2bd, g2, w0, a0, kkw, bo]
    in_specs = [pl.BlockSpec((TM, D), lambda i: (jnp.minimum(i, NT_P - 1), 0)),
                pl.BlockSpec((TM, D), lambda i: (jnp.maximum(i - NT_P, 0), 0)),
                pl.BlockSpec((8, D), lambda i: (jnp.clip((i - NT_P) * bpt - 1, 0, nblk8 - 1), 0)),
                pl.BlockSpec((8, D), lambda i: (jnp.clip((i - NT_P + 1) * bpt, 0, nblk8 - 1), 0))]
    in_specs += [full(a) for a in ins[4:]]
    out = jax.ShapeDtypeStruct((NTOK, D), F32)
    return pl.pallas_call(
        _front_kernel,
        grid=(NT,),
        in_specs=in_specs,
        out_specs=[tile] * 9,
        out_shape=[out] * 9,
        compiler_params=_cp(("parallel",)),
        name="rwkv_front",
    )(*ins)


def _wkv_chunk(d, S, r, k, v, kk, lw, ag, ka, rk, cst):
    tri4, strict, incl, h0, bd, bo2, same_blk, bd_sb, eye_c, lane_blk = cst
    dot = lambda a, b: jnp.dot(a, b, preferred_element_type=F32)
    nblk = CH // SB
    h0s = lax.broadcasted_iota(jnp.int32, (SB, LANES), 1) < HD

    def stack_heads(x, swap=False):
        zero = jnp.zeros_like(x)
        parts = [jnp.where(h0, x, zero), jnp.where(h0, zero, x)]
        return jnp.concatenate(parts[::-1] if swap else parts, axis=0)

    def dot_split(a, bm):
        return dot(a.astype(BF), bm.astype(BF))

    def expand(mc):
        return jnp.where(bd_sb, jnp.concatenate([mc] * (LANES // SB), axis=0), 0.0)

    b = kk * ag
    kd = k * (1.0 + (ag - 1.0) * ka)
    hi, mid, lo = _split3(lw)
    cs = dot(tri4[d], jnp.concatenate([hi, mid, lo, jnp.zeros_like(lo)], axis=0))
    qh, ql = _split2(r * kd * rk)
    qs = dot(jnp.concatenate([qh, ql], axis=0), bo2)
    bonus = (qs[:CH] + qs[CH:]) * v
    yield
    tot = cs[CH - 1:CH, :] if d == 0 else cs[0:1, :]
    p_inv = jnp.exp(-cs)
    p_end = jnp.exp(tot - cs)
    at = -kk * jnp.exp(cs - lw)
    rt = r * jnp.exp(cs)
    bt = (b * p_inv).astype(BF)
    kt = (kd * p_inv).astype(BF)
    ar = jnp.concatenate([at, rt], axis=0)
    h02 = jnp.concatenate([h0, h0], axis=0)
    g0 = _bdot_nt(jnp.where(h02, ar, 0.0), jnp.concatenate([bt, kt], axis=0))
    g1 = _bdot_nt(jnp.where(h02, 0.0, ar), jnp.concatenate([kt, bt], axis=0))
    w0y0 = _bdot_nt(ar, S)
    yield
    w0, y0 = w0y0[:CH], w0y0[CH:]
    lab = jnp.where(strict[d], jnp.where(h0, g0[:CH], g1[:CH]), 0.0)
    lak = jnp.where(strict[d], jnp.where(h0, g1[:CH], g0[:CH]), 0.0)
    mrb = jnp.where(incl[d], jnp.where(h0, g0[CH:], g1[CH:]), 0.0)
    mrk = jnp.where(incl[d], jnp.where(h0, g1[CH:], g0[CH:]), 0.0)
    vstk_sw = stack_heads(v, swap=True).astype(BF)
    w = w0 + dot(lak.astype(BF), vstk_sw)
    ldiag = jnp.where(same_blk, lab, 0.0)
    loff = jnp.where(same_blk, 0.0, lab)
    mc = ldiag[0:SB]
    for i in range(1, nblk):
        mc = mc + ldiag[i * SB:(i + 1) * SB]
    xc = eye_c + mc
    mc = dot_split(mc, expand(mc))
    yield
    nsq = int(math.log2(SB)) - 1
    for it in range(nsq):
        last = it + 1 == nsq
        res = dot_split(xc if last else jnp.concatenate([xc, mc], axis=0), expand(mc))
        xc = xc + res[:SB]
        if not last:
            mc = res[SB:]
        yield
    ublk = [None] * nblk
    order = list(range(nblk)) if d == 0 else list(range(nblk - 1, -1, -1))
    for n, i in enumerate(order):
        wi = w[i * SB:(i + 1) * SB]
        if n > 0:
            ucur = jnp.concatenate([jnp.zeros((SB, LANES), F32) if ub is None else ub
                                    for ub in ublk], axis=0)
            wi = wi + dot(loff[i * SB:(i + 1) * SB].astype(BF), stack_heads(ucur).astype(BF))
            yield
        m0 = jnp.where(h0s, wi, 0.0).astype(BF)
        m1 = jnp.where(h0s, 0.0, wi).astype(BF)
        rhs = jnp.concatenate([m0] * nblk + [m1] * nblk, axis=0)
        ublk[i] = dot(jnp.where(lane_blk[i], xc, 0.0).astype(BF), rhs)
        yield
    u = jnp.concatenate(ublk, axis=0)
    y = dot(jnp.concatenate([mrb, mrk], axis=1).astype(BF),
            jnp.concatenate([stack_heads(u).astype(BF), vstk_sw], axis=0))
    uvt = jnp.concatenate([u, v], axis=0).T
    ds = _bdot(uvt, jnp.concatenate([b * p_end, kd * p_end], axis=0))
    yield
    s_new = S * jnp.exp(tot) + jnp.where(bd, ds, 0.0)
    return s_new, y0 + y, bonus


def _run_lockstep(gens):
    results = [None] * len(gens)
    pending = list(range(len(gens)))
    while pending:
        for i in list(pending):
            try:
                next(gens[i])
            except StopIteration as stop:
                results[i] = stop.value
                pending.remove(i)
    return results


def _wkv_consts():
    t = lax.broadcasted_iota(jnp.int32, (CH, 4 * CH), 0)
    s = lax.broadcasted_iota(jnp.int32, (CH, 4 * CH), 1) & (CH - 1)
    tri4 = [jnp.where(s <= t, 1.0, 0.0).astype(BF), jnp.where(s >= t, 1.0, 0.0).astype(BF)]
    t2 = lax.broadcasted_iota(jnp.int32, (CH, LANES), 0)
    l2 = lax.broadcasted_iota(jnp.int32, (CH, LANES), 1)
    s2 = l2 & (CH - 1)
    strict = [s2 < t2, s2 > t2]
    incl = [s2 <= t2, s2 >= t2]
    h0 = l2 < HD
    same_blk = (t2 // SB) == (s2 // SB)
    ri = lax.broadcasted_iota(jnp.int32, (LANES, LANES), 0)
    ci = lax.broadcasted_iota(jnp.int32, (LANES, LANES), 1)
    bd = (ri // HD) == (ci // HD)
    bo2 = jnp.where(bd, 1.0, 0.0).astype(BF)
    bd_sb = (ri // SB) == (ci // SB)
    t3 = lax.broadcasted_iota(jnp.int32, (SB, LANES), 0)
    l3 = lax.broadcasted_iota(jnp.int32, (SB, LANES), 1)
    eye_c = jnp.where((l3 & (SB - 1)) == t3, 1.0, 0.0)
    lane_blk = [((l3 & (CH - 1)) // SB) == i for i in range(CH // SB)]
    return tri4, strict, incl, h0, bd, bo2, same_blk, bd_sb, eye_c, lane_blk


def _wkv_kernel(T, has_s0, *refs):
    (r_ref, k_ref, v_ref, kk_ref, lw0_ref, lw1_ref, ag0_ref, ag1_ref, ka_ref, rk_ref), rest = refs[:10], refs[10:]
    if has_s0:
        s0_ref, y_ref, bon_ref, s_scr = rest
    else:
        y_ref, bon_ref, st_ref, s_scr = rest
    nch = T // CH
    npair = y_ref.shape[1] // LANES
    cst = _wkv_consts()
    if has_s0:
        s_scr[...] = s0_ref[...]
    else:
        s_scr[...] = jnp.zeros_like(s_scr)
    y_ref[...] = jnp.zeros_like(y_ref)
    bon_ref[...] = jnp.zeros_like(bon_ref)
    lw_refs = (lw0_ref, lw1_ref)
    ag_refs = (ag0_ref, ag1_ref)

    def body(c, carry):
        chains = [(p, d) for p in range(npair) for d in range(2)]
        sl = {}
        for p, d in chains:
            r0 = pl.multiple_of((c if d == 0 else nch - 1 - c) * CH, CH)
            sl[p, d] = (pl.ds(r0, CH), pl.ds(p * LANES, LANES))
        args = {}
        for p, d in chains:
            rs, ls = sl[p, d]
            args[p, d] = (s_scr[d, p], r_ref[rs, ls], k_ref[rs, ls], v_ref[rs, ls], kk_ref[rs, ls],
                          lw_refs[d][rs, ls], ag_refs[d][rs, ls], ka_ref[:, ls], rk_ref[:, ls],
                          y_ref[rs, ls], bon_ref[rs, ls])
        outs = _run_lockstep([_wkv_chunk(d, *args[p, d][:9], cst) for p, d in chains])
        for (p, d), (s_new, y, bon) in zip(chains, outs):
            rs, ls = sl[p, d]
            s_scr[d, p] = s_new
            y_ref[rs, ls] = args[p, d][9] + y
            bon_ref[rs, ls] = args[p, d][10] + bon
        return carry

    lax.fori_loop(0, nch, body, 0)
    if not has_s0:
        for d in range(2):
            for p in range(npair):
                s = s_scr[d, p]
                st_ref[d, 2 * p] = s[:HD, :HD]
                st_ref[d, 2 * p + 1] = s[HD:, HD:]


def _wkv(arrs, ka, rk, T, nb, row_blk0, s0_bd=None, lanes=512):
    ng = D // lanes
    npair = lanes // LANES
    seq = pl.BlockSpec((T, lanes), lambda b, g: (row_blk0 + b, g))
    vec = pl.BlockSpec((1, lanes), lambda b, g: (0, g))
    st = pl.BlockSpec((None, 2, npair, LANES, LANES), lambda b, g: (b, 0, g, 0, 0))
    out_seq = pl.BlockSpec((T, lanes), lambda b, g: (b, g))
    in_specs = [seq] * 8 + [vec, vec]
    ins = list(arrs) + [ka, rk]
    out_specs = [out_seq, out_seq]
    out_shape = [jax.ShapeDtypeStruct((nb * T, D), F32)] * 2
    if s0_bd is not None:
        in_specs.append(st)
        ins.append(s0_bd)
    else:
        out_specs.append(pl.BlockSpec((None, 2, 2 * npair, HD, HD), lambda b, g: (b, 0, g, 0, 0)))
        out_shape.append(jax.ShapeDtypeStruct((nb, 2, NH, HD, HD), F32))
    return pl.pallas_call(
        functools.partial(_wkv_kernel, T, s0_bd is not None),
        grid=(nb, ng),
        in_specs=in_specs,
        out_specs=out_specs,
        out_shape=out_shape,
        scratch_shapes=[pltpu.VMEM((2, npair, LANES, LANES), F32)],
        compiler_params=_cp(("parallel", "parallel")),
        name="wkv_T%d" % T,
    )(*ins)


def _proj_kernel(gn, *refs):
    i = pl.program_id(0)
    is_p = i < NT_P
    if gn:
        (yp_ref, ys_ref, ap_ref, as_ref, bp_ref, bs_ref, g_ref, lng_ref, lnb_ref, bo_ref,
         mod_ref, w_ref, nrm_ref, wr_ref, br_ref, o_ref, xt_o, gate_o) = refs
        y = jnp.where(is_p, yp_ref[...], ys_ref[...])
    else:
        y_ref, ap_ref, as_ref, mod_ref, w_ref, nrm_ref, wr_ref, br_ref, o_ref, xt_o, gate_o = refs
        y = y_ref[...]
    a = jnp.where(is_p, ap_ref[...], as_ref[...])
    if gn:
        bo = bo_ref[...]
        mean = _seg_sum(a, bo) * (1.0 / HD)
        cen = a - mean
        var = _seg_sum(cen * cen, bo) * (1.0 / HD)
        yn = cen * lax.rsqrt(var + GN_EPS) * lng_ref[...] + lnb_ref[...]
        a = (yn + jnp.where(is_p, bp_ref[...], bs_ref[...])) * g_ref[...]
    gt = _mod_vec(mod_ref, _tile_mod_row(i), 2)
    y_new = y + gt * _bdot(a, w_ref[...])
    o_ref[...] = y_new
    row = _tile_mod_row(i)
    xt = _modulate(y_new, nrm_ref[...], _mod_vec(mod_ref, row, 3), _mod_vec(mod_ref, row, 4))
    xt_o[...] = xt.astype(BF)
    gate_o[...] = _route(xt, wr_ref[...], br_ref[...])


def _proj_residual(y, a_p, a_s, mod, w, router, gn_args=None, name="proj"):
    tile = pl.BlockSpec((TM, D), lambda i: (i, 0))
    tile_p = pl.BlockSpec((TM, D), lambda i: (jnp.minimum(i, NT_P - 1), 0))
    tile_s = pl.BlockSpec((TM, D), lambda i: (jnp.maximum(i - NT_P, 0), 0))
    full = lambda a: pl.BlockSpec(a.shape, lambda i: (0,) * a.ndim)
    if gn_args is not None:
        ins = [y[0], y[1], a_p, a_s]
        in_specs = [tile_p, tile_s, tile_p, tile_s]
    else:
        ins = [y, a_p, a_s]
        in_specs = [tile, tile_p, tile_s]
    if gn_args is not None:
        b_p, b_s, g, lng, lnb, bo = gn_args
        ins += [b_p, b_s, g, lng, lnb, bo]
        in_specs += [tile_p, tile_s, tile, full(lng), full(lnb), full(bo)]
    ins += [mod, w] + list(router)
    in_specs += [full(mod), full(w)] + [full(a) for a in router]
    return pl.pallas_call(
        functools.partial(_proj_kernel, gn_args is not None),
        grid=(NT,),
        in_specs=in_specs,
        out_specs=[tile, tile, pl.BlockSpec((TM, LANES), lambda i: (i, 0))],
        out_shape=[jax.ShapeDtypeStruct((NTOK, D), F32), jax.ShapeDtypeStruct((NTOK, D), BF),
                   jax.ShapeDtypeStruct((NTOK, LANES), F32)],
        compiler_params=_cp(("parallel",)),
        name=name,
    )(*ins)


def _route(xt, wr, br):
    x1, x2 = _split2(xt)
    w1, w2 = _split2(wr)
    dot = lambda a, b: jnp.dot(a, b, preferred_element_type=F32)
    logits = (dot(x2, w1) + dot(x1, w2)) + dot(x1, w1)
    logits = logits + br
    lane = lax.broadcasted_iota(jnp.int32, logits.shape, 1).astype(F32)
    isg = jnp.logical_and(lane >= NE, lane < NE + NE // EPG)
    mg = jnp.max(jnp.where(isg, logits, NEG), axis=-1, keepdims=True)
    eg = jnp.where(isg, jnp.exp(jnp.minimum(logits - mg, 0.0)), 0.0)
    p_sel = 1.0 / jnp.sum(eg, axis=-1, keepdims=True)
    gidx = jnp.min(jnp.where(jnp.logical_and(isg, logits == mg), lane, 1e3), axis=-1, keepdims=True) - NE
    lo_l = gidx * EPG
    ing = jnp.logical_and(lane >= lo_l, lane < lo_l + EPG)
    me = jnp.max(jnp.where(ing, logits, NEG), axis=-1, keepdims=True)
    ee = jnp.where(ing, jnp.exp(jnp.minimum(logits - me, 0.0)), 0.0)
    se = jnp.sum(ee, axis=-1, keepdims=True)
    pe = ee / se
    i1 = jnp.min(jnp.where(jnp.logical_and(ing, logits == me), lane, 1e3), axis=-1, keepdims=True)
    v1 = 1.0 / se
    rest = jnp.logical_and(ing, lane != i1)
    v2 = jnp.max(jnp.where(rest, pe, -1.0), axis=-1, keepdims=True)
    i2 = jnp.min(jnp.where(jnp.logical_and(rest, pe == v2), lane, 1e3), axis=-1, keepdims=True)
    den = v1 + v2
    return (jnp.where(lane == i1, p_sel * v1 / den, 0.0)
            + jnp.where(lane == i2, p_sel * v2 / den, 0.0))


def _experts_kernel(final, xt_ref, gate_ref, w13_ref, w2_ref, y_ref, mod_ref, fin_ref, *rest):
    i = pl.program_id(0)
    e = pl.program_id(1)
    acc_ref = rest[-1]
    ntile_p = NPR // TMX

    x = xt_ref[...]
    gate = gate_ref[...]
    lane = lax.broadcasted_iota(jnp.int32, gate_ref.shape, 1)
    hids = []
    for j in range(EPS):
        h13 = jnp.dot(x, w13_ref[j], preferred_element_type=F32)
        h1, h3 = h13[:, :DE], h13[:, DE:]
        ge = jnp.sum(jnp.where(lane == e * EPS + j, gate, 0.0), axis=-1, keepdims=True)
        hids.append(((h1 * jax.nn.sigmoid(h1)) * h3 * ge).astype(BF))
    hid = jnp.concatenate(hids, axis=1)

    @pl.when(e == 0)
    def _():
        acc_ref[...] = jnp.zeros_like(acc_ref)

    acc_ref[...] += jnp.dot(hid, w2_ref[...].reshape(EPS * DE, D), preferred_element_type=F32)

    def result():
        row = jnp.where(i < ntile_p, 0, i - ntile_p + 1)
        out = y_ref[...] + _mod_vec(mod_ref, row, 5) * acc_ref[...]
        if final:
            ms = jnp.mean(out * out, axis=-1, keepdims=True)
            out = out * lax.rsqrt(ms + NORM_EPS) * fin_ref[...]
        return out

    if final:
        op_ref, os_ref = rest[:2]

        @pl.when(jnp.logical_and(e == NE // EPS - 1, i < ntile_p))
        def _():
            op_ref[...] = result()

        @pl.when(jnp.logical_and(e == NE // EPS - 1, i >= ntile_p))
        def _():
            os_ref[...] = result()
    else:
        @pl.when(e == NE // EPS - 1)
        def _():
            rest[0][...] = result()


def _moe_experts(xt, gate, w13, w2, y, mod, fin, final):
    tile = pl.BlockSpec((TMX, D), lambda i, e: (i, 0))
    full = lambda a: pl.BlockSpec(a.shape, lambda i, e: (0,) * a.ndim)
    ntile_p = NPR // TMX
    if final:
        out_specs = [pl.BlockSpec((TMX, D), lambda i, e: (jnp.minimum(i, ntile_p - 1), 0)),
                     pl.BlockSpec((TMX, D), lambda i, e: (jnp.maximum(i - ntile_p, 0), 0))]
        out_shape = [jax.ShapeDtypeStruct((NPR, D), F32), jax.ShapeDtypeStruct((NSR, D), F32)]
    else:
        out_specs = tile
        out_shape = jax.ShapeDtypeStruct((NTOK, D), F32)
    return pl.pallas_call(
        functools.partial(_experts_kernel, final),
        grid=(NTOK // TMX, NE // EPS),
        in_specs=[tile, pl.BlockSpec((TMX, LANES), lambda i, e: (i, 0)),
                  pl.BlockSpec((EPS, D, 2 * DE), lambda i, e: (e, 0, 0)),
                  pl.BlockSpec((EPS, DE, D), lambda i, e: (e, 0, 0)),
                  tile, full(mod), full(fin)],
        out_specs=out_specs,
        out_shape=out_shape,
        scratch_shapes=[pltpu.VMEM((TMX, D), F32)] if final else [],
        compiler_params=_cp(("arbitrary" if final else "parallel", "arbitrary")),
        name="moe_experts",
    )(xt, gate, w13, w2, y, mod, fin)


def _qkv_kernel(y_ref, mod_ref, nrm_ref, w_ref, q_o, kp_o, vp_o, ks_o, vs_o, kc_o, vc_o):
    i = pl.program_id(0)
    row = _tile_mod_row(_qkv_tile(i))
    h = _modulate(y_ref[...], nrm_ref[...], _mod_vec(mod_ref, row, 0), _mod_vec(mod_ref, row, 1))
    hb = h.astype(BF)
    kx = jnp.dot(hb, w_ref[:, D:2 * D], preferred_element_type=F32)
    vx = jnp.dot(hb, w_ref[:, 2 * D:], preferred_element_type=F32)
    q_o[...] = jnp.dot(hb, w_ref[:, :D], preferred_element_type=F32)

    @pl.when(i < NT - NT_P)
    def _():
        ks_o[...] = kx
        vs_o[...] = vx

    @pl.when(i >= NT - NT_P)
    def _():
        kp_o[...] = kx
        vp_o[...] = vx
        for hh in range(NH):
            kc_o[pl.ds(hh, TM, stride=NH), :] = kx[:, hh * HD:(hh + 1) * HD]
            vc_o[pl.ds(hh, TM, stride=NH), :] = vx[:, hh * HD:(hh + 1) * HD]


def _qkv_tile(i):
    return jnp.where(i < NT - NT_P, NT_P + i, i - (NT - NT_P))


def _na_qkv(y, mod, nrm, w):
    nt_s = NT - NT_P
    tile = pl.BlockSpec((TM, D), lambda i: (_qkv_tile(i), 0))
    tile_p = pl.BlockSpec((TM, D), lambda i: (jnp.maximum(i - nt_s, 0), 0))
    tile_s = pl.BlockSpec((TM, D), lambda i: (jnp.minimum(i, nt_s - 1), 0))
    full = lambda a: pl.BlockSpec(a.shape, lambda i: (0,) * a.ndim)
    out = lambda n: jax.ShapeDtypeStruct((n, D), F32)
    cache = jax.ShapeDtypeStruct((NPR * NH, HD), F32)
    cache_tile = pl.BlockSpec((TM * NH, HD), lambda i: (jnp.maximum(i - nt_s, 0), 0))
    return pl.pallas_call(
        _qkv_kernel,
        grid=(NT,),
        in_specs=[tile, full(mod), full(nrm), full(w)],
        out_specs=[tile, tile_p, tile_p, tile_s, tile_s, cache_tile, cache_tile],
        out_shape=[out(NTOK), out(NPR), out(NPR), out(NSR), out(NSR), cache, cache],
        compiler_params=_cp(("arbitrary",)),
        name="na_qkv",
    )(y, mod, nrm, w)


def _softmax_rows(s):
    m = jnp.max(s, axis=-1, keepdims=True)
    e = jnp.exp(s - m)
    return e / jnp.sum(e, axis=-1, keepdims=True)


def _ctx_attn_kernel(q_ref, k_ref, v_ref, o_ref):
    scale = HD ** -0.5
    h0 = lax.broadcasted_iota(jnp.int32, (T_P, LANES), 1) < HD
    def head_chain(h, q, kb, vb):
        qm = jnp.where(h0 if h == 0 else jnp.logical_not(h0), q, 0.0)
        s = _bdot_nt(qm, kb)
        yield
        o = jnp.dot(_softmax_rows(s * scale).astype(BF), vb, preferred_element_type=F32)
        yield
        return o

    npair = D // LANES
    for p0 in range(0, npair, CTX_PAIRS):
        gens = []
        for p in range(p0, p0 + CTX_PAIRS):
            ls = pl.ds(p * LANES, LANES)
            q = q_ref[:, ls]
            kb = k_ref[:, ls].astype(BF)
            vb = v_ref[:, ls].astype(BF)
            gens += [head_chain(h, q, kb, vb) for h in range(2)]
        outs = _run_lockstep(gens)
        for j, p in enumerate(range(p0, p0 + CTX_PAIRS)):
            o_ref[:, pl.ds(p * LANES, LANES)] = jnp.where(h0, outs[2 * j], outs[2 * j + 1])


def _ctx_attn(q, k, v):
    blk = pl.BlockSpec((T_P, D), lambda b: (b, 0))
    return pl.pallas_call(
        _ctx_attn_kernel,
        grid=(NB_P,),
        in_specs=[blk] * 3,
        out_specs=blk,
        out_shape=jax.ShapeDtypeStruct((NPR, D), F32),
        compiler_params=_cp(("parallel",)),
        name="ctx_attn",
    )(q, k, v)


def _nbr_attn_kernel(q_ref, k_ref, v_ref, kc_ref, vc_ref, tz_ref, o_ref):
    scale = HD ** -0.5
    rows = T_S // GRID_W
    nloc = WIN_R * GRID_W
    h0 = lax.broadcasted_iota(jnp.int32, (GRID_W, LANES), 1) < HD
    kcb = kc_ref[...].astype(BF)
    vcb = vc_ref[...].astype(BF)

    def head_chain(h, q, klb, vlb, j0):
        qm = jnp.where(h0 if h == 0 else jnp.logical_not(h0), q, 0.0).astype(BF)
        sl = _bdot_nt(qm, klb)
        sc = _bdot_nt(qm, kcb)
        yield
        bias = jnp.concatenate([tz_ref[h, j0 + 2 * m] for m in range(WIN_R // 2)], axis=1)
        sl = sl * scale + bias
        sc = sc * scale
        m = jnp.maximum(jnp.max(sl, axis=-1, keepdims=True), jnp.max(sc, axis=-1, keepdims=True))
        el = jnp.exp(sl - m)
        ec = jnp.exp(sc - m)
        inv = 1.0 / (jnp.sum(el, axis=-1, keepdims=True) + jnp.sum(ec, axis=-1, keepdims=True))
        o = (jnp.dot((el * inv).astype(BF), vlb, preferred_element_type=F32)
             + jnp.dot((ec * inv).astype(BF), vcb, preferred_element_type=F32))
        yield
        return o

    def body(it, carry):
        gens, slices = [], []
        for j in range(NBR_UNROLL):
            r = it * NBR_UNROLL + j
            start = jnp.clip(r - WIN_R // 2, 0, rows - WIN_R)
            qs = pl.ds(pl.multiple_of(r * GRID_W, GRID_W), GRID_W)
            ks = pl.ds(pl.multiple_of(start * GRID_W, GRID_W), nloc)
            q = q_ref[qs, :]
            klb = k_ref[ks, :].astype(BF)
            vlb = v_ref[ks, :].astype(BF)
            j0 = start - r + WIN_R - 1
            slices.append(qs)
            gens += [head_chain(h, q, klb, vlb, j0) for h in range(2)]
        outs = _run_lockstep(gens)
        for j, qs in enumerate(slices):
            o_ref[qs, :] = jnp.where(h0, outs[2 * j], outs[2 * j + 1])
        return carry

    lax.fori_loop(0, rows // NBR_UNROLL, body, 0)


def _nbr_attn(q, k, v, kc, vc, tz2):
    npair = D // LANES
    qseq = pl.BlockSpec((T_S, LANES), lambda b, p: (NPR // T_S + b, p))
    seq = pl.BlockSpec((T_S, LANES), lambda b, p: (b, p))
    ctx = pl.BlockSpec((kc.shape[0] // NB_S, LANES), lambda b, p: (b, p))
    return pl.pallas_call(
        _nbr_attn_kernel,
        grid=(NB_S, npair),
        in_specs=[qseq, seq, seq, ctx, ctx,
                  pl.BlockSpec((2,) + tz2.shape[1:], lambda b, p: (p, 0, 0, 0))],
        out_specs=pl.BlockSpec((T_S, LANES), lambda b, p: (b, p)),
        out_shape=jax.ShapeDtypeStruct((NSR, D), F32),
        compiler_params=_cp(("parallel", "parallel")),
        name="nbr_attn",
    )(q, k, v, kc, vc, tz2)


def _block_diag2(a, b):
    z = jnp.zeros_like(a)
    return jnp.concatenate([jnp.concatenate([a, z], axis=1), jnp.concatenate([z, b], axis=1)], axis=0)


def _nbr_bias_table(rpb):
    qc = jnp.arange(GRID_W)[:, None]
    kc = jnp.arange(GRID_W)[None, :]
    cstart = jnp.clip(qc - WIN_C // 2, 0, GRID_W - WIN_C)
    valid = jnp.logical_and(kc >= cstart, kc < cstart + WIN_C)
    nr = rpb.shape[1]
    period = 2 * GRID_W
    p = jnp.concatenate([rpb[:, :, WIN_C - 1:], jnp.zeros(rpb.shape[:2] + (period - 2 * WIN_C + 1,), F32),
                         rpb[:, :, :WIN_C - 1]], axis=-1)
    flat = jnp.tile(p, (1, 1, GRID_W))[:, :, :GRID_W * (period - 1)]
    toep = flat.reshape(rpb.shape[0], nr, GRID_W, period - 1)[..., :GRID_W]
    tz = jnp.where(valid, toep, NEG)
    return jnp.concatenate([tz[:, :-1], tz[:, 1:]], axis=-1)


def kernel(x_prompt, x_sample, c, state_rwkv, cache_na_k, cache_na_v, c_ctx, norm_mix, norm_ffn, ada_w, ada_b, rwkv_mu, rwkv_w_r, rwkv_w_k, rwkv_w_v, rwkv_w_o, rwkv_w0, rwkv_w1, rwkv_w2, rwkv_a0, rwkv_a1, rwkv_a2, rwkv_g1, rwkv_g2, rwkv_k_k, rwkv_k_a, rwkv_r_k, rwkv_lnx_g, rwkv_lnx_b, na_w_qkv, na_w_o, na_rpb, moe_w_grp, moe_b_grp, moe_w_exp, moe_b_exp, moe_w1, moe_w3, moe_w2, final_norm):
    x_p = x_prompt.reshape(NPR, D)
    x_s = x_sample.reshape(NSR, D)
    c8 = jnp.concatenate([c_ctx[None, :], c, jnp.zeros((8 - 1 - NB_S, D), F32)], axis=0)
    mod = _adaln(c8, ada_w, ada_b)
    ri = jnp.arange(2 * LANES)[:, None] // HD
    bo = (ri == ri.T).astype(BF)
    row = lambda a: a.reshape(1, D)

    w1c = jnp.concatenate([rwkv_w1[0, 0], rwkv_w1[0, 1]], axis=1).astype(BF)
    a1c = jnp.concatenate([rwkv_a1[0, 0], rwkv_a1[0, 1]], axis=1).astype(BF)
    w2bd = _block_diag2(rwkv_w2[0, 0], rwkv_w2[0, 1]).astype(BF)
    a2bd = _block_diag2(rwkv_a2[0, 0], rwkv_a2[0, 1]).astype(BF)
    r, k, v, g, kk, lw0, lw1, ag0, ag1 = _rwkv_front(
        x_p, x_s, mod[0], row(norm_mix[0]), rwkv_mu[0], rwkv_w_r[0].astype(BF), rwkv_w_k[0].astype(BF),
        rwkv_w_v[0].astype(BF), w1c, a1c, rwkv_g1[0].astype(BF), w2bd, a2bd, rwkv_g2[0].astype(BF),
        rwkv_w0[0], rwkv_a0[0], row(rwkv_k_k[0]), bo)
    arrs = (r, k, v, kk, lw0, lw1, ag0, ag1)
    ka, rk = row(rwkv_k_a[0]), row(rwkv_r_k[0])
    ys_p, bon_p, new_state = _wkv(arrs, ka, rk, T_P, NB_P, 0, lanes=D)
    s0 = state_rwkv[:, 0].reshape(NB_S, 2, NH // 2, 2, HD, HD)
    z = jnp.zeros_like(s0[:, :, :, 0])
    s0_bd = jnp.concatenate([jnp.concatenate([s0[:, :, :, 0], z], axis=-1),
                             jnp.concatenate([z, s0[:, :, :, 1]], axis=-1)], axis=-2)
    ys_s, bon_s = _wkv(arrs, ka, rk, T_S, NB_S, NPR // T_S, s0_bd=s0_bd)
    new_state_rwkv = new_state.reshape(NB_P, 1, 2, NH, HD, HD)
    def router(i):
        wr = jnp.concatenate([moe_w_exp[i], moe_w_grp[i],
                              jnp.zeros((D, LANES - NE - NE // EPG), F32)], axis=1)
        br = jnp.concatenate([moe_b_exp[i], moe_b_grp[i],
                              jnp.zeros((LANES - NE - NE // EPG,), F32)]).reshape(1, LANES)
        return (row(norm_ffn[i]), wr, br)

    def experts(y, xt, gate, i, final):
        w13 = jnp.concatenate([moe_w1[i], moe_w3[i]], axis=-1).astype(BF)
        return _moe_experts(xt, gate, w13, moe_w2[i].astype(BF), y, mod[i], row(final_norm), final)

    y, xt, gate = _proj_residual((x_p, x_s), ys_p, ys_s, mod[0], rwkv_w_o[0].astype(BF), router(0),
                                 gn_args=(bon_p, bon_s, g, row(rwkv_lnx_g[0]), row(rwkv_lnx_b[0]), bo),
                                 name="rwkv_out")
    y = experts(y, xt, gate, 0, False)

    q, k_p, v_p, k_s, v_s, kcache, vcache = _na_qkv(y, mod[1], row(norm_mix[1]), na_w_qkv[0].astype(BF))
    o_p = _ctx_attn(q, k_p, v_p)
    kc = cache_na_k[:, 0].reshape(NB_S * cache_na_k.shape[2], D)
    vc = cache_na_v[:, 0].reshape(NB_S * cache_na_v.shape[2], D)
    o_s = _nbr_attn(q, k_s, v_s, kc, vc, _nbr_bias_table(na_rpb[0]))
    y, xt, gate = _proj_residual(y, o_p, o_s, mod[1], na_w_o[0].astype(BF), router(1), name="attn_out")
    y_p, y_s = experts(y, xt, gate, 1, True)

    y_prompt = y_p.reshape(NB_P, T_P, D)
    y_sample = y_s.reshape(NB_S, T_S, D)
    new_k = kcache.reshape(NB_P, 1, T_P, NH, HD)
    new_v = vcache.reshape(NB_P, 1, T_P, NH, HD)
    return (y_prompt, y_sample, new_state_rwkv, new_k, new_v)
```

```python
import functools
import math

import jax
import jax.numpy as jnp
from jax import lax
from jax.experimental import pallas as pl
from jax.experimental.pallas import tpu as pltpu

F32 = jnp.float32
BF = jnp.bfloat16

D = 1024
NH = 16
HD = 64
NB_P, T_P = 32, 256
NB_S, T_S = 2, 1024
NPR = NB_P * T_P
NSR = NB_S * T_S
NTOK = NPR + NSR
TM = 256
NT = NTOK // TM
NT_P = NPR // TM
TPS = T_S // TM
TMX = 1024
NMOD = 6
NE = 16
EPG = 4
DE = 256
EPS = 4
CH = 64
SB = 32
LANES = 128
GRID_W = 64
WIN_R, WIN_C = 8, 16
NBR_UNROLL = 4
CTX_PAIRS = 2
NORM_EPS = 1e-6
GN_EPS = 64e-5
NEG = -1e30
VMEM_LIMIT = 56 * 1024 * 1024


def _cp(sem):
    return pltpu.CompilerParams(dimension_semantics=sem, vmem_limit_bytes=VMEM_LIMIT)


def _bdot(a, b):
    return jnp.dot(a.astype(BF), b.astype(BF), preferred_element_type=F32)


def _bdot_nt(a, b):
    return lax.dot_general(a.astype(BF), b.astype(BF), (((1,), (1,)), ((), ())),
                           preferred_element_type=F32)


def _split2(x):
    hi = x.astype(BF)
    lo = (x - hi.astype(F32)).astype(BF)
    return hi, lo


def _split3(x):
    hi = x.astype(BF)
    r1 = x - hi.astype(F32)
    mid = r1.astype(BF)
    lo = (r1 - mid.astype(F32)).astype(BF)
    return hi, mid, lo


def _seg_sum(x, bo):
    outs = []
    for c in range(x.shape[1] // 256):
        hi, lo = _split2(x[:, c * 256:(c + 1) * 256])
        outs.append(jnp.dot(hi, bo, preferred_element_type=F32)
                    + jnp.dot(lo, bo, preferred_element_type=F32))
    return outs[0] if len(outs) == 1 else jnp.concatenate(outs, axis=1)


def _tile_mod_row(i):
    return jnp.where(i < NT_P, 0, 1 + (i - NT_P) // TPS)


def _modulate(x, g, sh, sc):
    ms = jnp.mean(x * x, axis=-1, keepdims=True)
    return x * lax.rsqrt(ms + NORM_EPS) * g * (1.0 + sc) + sh


def _mod_vec(mod_ref, row, k):
    return mod_ref[pl.ds(row, 1), pl.ds(k * D, D)]


def _adaln_kernel(c_ref, w_ref, b_ref, o_ref):
    c = c_ref[...]
    s = c * jax.nn.sigmoid(c)
    o_ref[...] = _bdot(s, w_ref[...]) + b_ref[...]


def _adaln(c8, ada_w, ada_b):
    nl = ada_w.shape[0]
    tn = 1536
    return pl.pallas_call(
        _adaln_kernel,
        grid=(nl, NMOD * D // tn),
        in_specs=[pl.BlockSpec((8, D), lambda l, j: (0, 0)),
                  pl.BlockSpec((None, D, tn), lambda l, j: (l, 0, j)),
                  pl.BlockSpec((None, 1, tn), lambda l, j: (l, 0, j))],
        out_specs=pl.BlockSpec((None, 8, tn), lambda l, j: (l, 0, j)),
        out_shape=jax.ShapeDtypeStruct((nl, 8, NMOD * D), F32),
        compiler_params=_cp(("parallel", "parallel")),
        name="adaln",
    )(c8, ada_w, ada_b.reshape(nl, 1, NMOD * D))


def _front_kernel(xp_ref, xs_ref, yp_ref, yn_ref, mod_ref, nrm_ref, mu_ref, wr_ref, wk_ref, wv_ref,
                  w1_ref, a1_ref, g1_ref, w2_ref, a2_ref, g2_ref, w0_ref, a0_ref, kkw_ref, bo_ref,
                  r_o, k_o, v_o, g_o, kk_o, lw0_o, lw1_o, ag0_o, ag1_o):
    i = pl.program_id(0)
    row = _tile_mod_row(i)
    q = (i - NT_P) % TPS
    first = jnp.logical_or(i < NT_P, q == 0)
    last = jnp.logical_or(i < NT_P, q == TPS - 1)
    sh = _mod_vec(mod_ref, row, 0)
    sc = _mod_vec(mod_ref, row, 1)
    g = nrm_ref[...]
    h = _modulate(jnp.where(i < NT_P, xp_ref[...], xs_ref[...]), g, sh, sc)
    hp = _modulate(yp_ref[...], g, sh, sc)[7:8]
    hn = _modulate(yn_ref[...], g, sh, sc)[0:1]
    hp = jnp.where(first, 0.0, hp)
    hn = jnp.where(last, 0.0, hn)
    rid = lax.broadcasted_iota(jnp.int32, (TM, D), 0)
    prev = jnp.where(rid == 0, hp, pltpu.roll(h, 1, 0))
    nxt = jnp.where(rid == TM - 1, hn, pltpu.roll(h, TM - 1, 0))
    dx = 0.5 * (prev + nxt) - h

    def mix(n):
        return (h + dx * mu_ref[n:n + 1, :]).astype(BF)

    r_o[...] = jnp.dot(mix(0), wr_ref[...], preferred_element_type=F32)
    xw = mix(1)
    lora_w = _bdot(jnp.tanh(jnp.dot(xw, w1_ref[...], preferred_element_type=F32)), w2_ref[...])
    hc = 0.5 * math.exp(-0.5)
    lw0_o[...] = -hc * jnp.tanh(0.5 * (w0_ref[0:1, :] + lora_w[:, :D])) - hc
    lw1_o[...] = -hc * jnp.tanh(0.5 * (w0_ref[1:2, :] + lora_w[:, D:])) - hc
    k_raw = jnp.dot(mix(2), wk_ref[...], preferred_element_type=F32)
    k_o[...] = k_raw
    kq = k_raw * kkw_ref[...]
    ss = _seg_sum(kq * kq, bo_ref[...])
    kk_o[...] = kq / jnp.maximum(jnp.sqrt(ss), 1e-12)
    v_o[...] = jnp.dot(mix(3), wv_ref[...], preferred_element_type=F32)
    xa = mix(4)
    lora_a = _bdot(jnp.dot(xa, a1_ref[...], preferred_element_type=F32), a2_ref[...])
    ag0_o[...] = 0.5 * jnp.tanh(0.5 * (a0_ref[0:1, :] + lora_a[:, :D])) + 0.5
    ag1_o[...] = 0.5 * jnp.tanh(0.5 * (a0_ref[1:2, :] + lora_a[:, D:])) + 0.5
    xg = mix(5)
    g_o[...] = _bdot(jax.nn.sigmoid(jnp.dot(xg, g1_ref[...], preferred_element_type=F32)), g2_ref[...])


def _rwkv_front(x_p, x_s, mod, nrm, mu, wr, wk, wv, w1c, a1c, g1, w2bd, a2bd, g2, w0, a0, kkw, bo):
    tile = pl.BlockSpec((TM, D), lambda i: (i, 0))
    nblk8 = NSR // 8
    bpt = TM // 8
    full = lambda a: pl.BlockSpec(a.shape, lambda i: (0,) * a.ndim)
    ins = [x_p, x_s, x_s, x_s, mod, nrm, mu, wr, wk, wv, w1c, a1c, g1, w2bd, a2bd, g2, w0, a0, kkw, bo]
    in_specs = [pl.BlockSpec((TM, D), lambda i: (jnp.minimum(i, NT_P - 1), 0)),
                pl.BlockSpec((TM, D), lambda i: (jnp.maximum(i - NT_P, 0), 0)),
                pl.BlockSpec((8, D), lambda i: (jnp.clip((i - NT_P) * bpt - 1, 0, nblk8 - 1), 0)),
                pl.BlockSpec((8, D), lambda i: (jnp.clip((i - NT_P + 1) * bpt, 0, nblk8 - 1), 0))]
    in_specs += [full(a) for a in ins[4:]]
    out = jax.ShapeDtypeStruct((NTOK, D), F32)
    return pl.pallas_call(
        _front_kernel,
        grid=(NT,),
        in_specs=in_specs,
        out_specs=[tile] * 9,
        out_shape=[out] * 9,
        compiler_params=_cp(("parallel",)),
        name="rwkv_front",
    )(*ins)


def _wkv_chunk(d, S, r, k, v, kk, lw, ag, ka, rk, cst):
    trow, strict, incl, h0, bd, bo2, same_blk, bd_sb, eye_c, lane_blk = cst
    dot = lambda a, b: jnp.dot(a, b, preferred_element_type=F32)
    nblk = CH // SB
    h0s = lax.broadcasted_iota(jnp.int32, (SB, LANES), 1) < HD

    def stack_heads(x, swap=False):
        zero = jnp.zeros_like(x)
        parts = [jnp.where(h0, x, zero), jnp.where(h0, zero, x)]
        return jnp.concatenate(parts[::-1] if swap else parts, axis=0)

    def dot_split(a, bm):
        return dot(a.astype(BF), bm.astype(BF))

    def expand(mc):
        return jnp.where(bd_sb, jnp.concatenate([mc] * (LANES // SB), axis=0), 0.0)

    b = kk * ag
    kd = k * (1.0 + (ag - 1.0) * ka)
    cs = lw
    for sh in [1 << n for n in range(int(math.log2(CH)))]:
        if d == 0:
            cs = cs + jnp.where(trow >= sh, pltpu.roll(cs, sh, 0), 0.0)
        else:
            cs = cs + jnp.where(trow < CH - sh, pltpu.roll(cs, CH - sh, 0), 0.0)
    qh, ql = _split2(r * kd * rk)
    qs = dot(jnp.concatenate([qh, ql], axis=0), bo2)
    bonus = (qs[:CH] + qs[CH:]) * v
    yield
    tot = cs[CH - 1:CH, :] if d == 0 else cs[0:1, :]
    p_inv = jnp.exp(-cs)
    p_end = jnp.exp(tot - cs)
    at = -kk * jnp.exp(cs - lw)
    rt = r * jnp.exp(cs)
    bt = (b * p_inv).astype(BF)
    kt = (kd * p_inv).astype(BF)
    ar = jnp.concatenate([at, rt], axis=0)
    h02 = jnp.concatenate([h0, h0], axis=0)
    g0 = _bdot_nt(jnp.where(h02, ar, 0.0), jnp.concatenate([bt, kt], axis=0))
    g1 = _bdot_nt(jnp.where(h02, 0.0, ar), jnp.concatenate([kt, bt], axis=0))
    w0y0 = _bdot_nt(ar, S)
    yield
    w0, y0 = w0y0[:CH], w0y0[CH:]
    lab = jnp.where(strict[d], jnp.where(h0, g0[:CH], g1[:CH]), 0.0)
    lak = jnp.where(strict[d], jnp.where(h0, g1[:CH], g0[:CH]), 0.0)
    mrb = jnp.where(incl[d], jnp.where(h0, g0[CH:], g1[CH:]), 0.0)
    mrk = jnp.where(incl[d], jnp.where(h0, g1[CH:], g0[CH:]), 0.0)
    vstk_sw = stack_heads(v, swap=True).astype(BF)
    w = w0 + dot(lak.astype(BF), vstk_sw)
    ldiag = jnp.where(same_blk, lab, 0.0)
    loff = jnp.where(same_blk, 0.0, lab)
    mc = ldiag[0:SB]
    for i in range(1, nblk):
        mc = mc + ldiag[i * SB:(i + 1) * SB]
    xc = eye_c + mc
    mc = dot_split(mc, expand(mc))
    yield
    nsq = int(math.log2(SB)) - 1
    for it in range(nsq):
        last = it + 1 == nsq
        res = dot_split(xc if last else jnp.concatenate([xc, mc], axis=0), expand(mc))
        xc = xc + res[:SB]
        if not last:
            mc = res[SB:]
        yield
    ublk = [None] * nblk
    order = list(range(nblk)) if d == 0 else list(range(nblk - 1, -1, -1))
    for n, i in enumerate(order):
        wi = w[i * SB:(i + 1) * SB]
        if n > 0:
            ucur = jnp.concatenate([jnp.zeros((SB, LANES), F32) if ub is None else ub
                                    for ub in ublk], axis=0)
            wi = wi + dot(loff[i * SB:(i + 1) * SB].astype(BF), stack_heads(ucur).astype(BF))
            yield
        m0 = jnp.where(h0s, wi, 0.0).astype(BF)
        m1 = jnp.where(h0s, 0.0, wi).astype(BF)
        rhs = jnp.concatenate([m0] * nblk + [m1] * nblk, axis=0)
        ublk[i] = dot(jnp.where(lane_blk[i], xc, 0.0).astype(BF), rhs)
        yield
    u = jnp.concatenate(ublk, axis=0)
    y = dot(jnp.concatenate([mrb, mrk], axis=1).astype(BF),
            jnp.concatenate([stack_heads(u).astype(BF), vstk_sw], axis=0))
    uvt = jnp.concatenate([u, v], axis=0).T
    ds = _bdot(uvt, jnp.concatenate([b * p_end, kd * p_end], axis=0))
    yield
    s_new = S * jnp.exp(tot) + jnp.where(bd, ds, 0.0)
    return s_new, y0 + y, bonus


def _run_lockstep(gens):
    results = [None] * len(gens)
    pending = list(range(len(gens)))
    while pending:
        for i in list(pending):
            try:
                next(gens[i])
            except StopIteration as stop:
                results[i] = stop.value
                pending.remove(i)
    return results


def _wkv_consts():
    t2 = lax.broadcasted_iota(jnp.int32, (CH, LANES), 0)
    l2 = lax.broadcasted_iota(jnp.int32, (CH, LANES), 1)
    s2 = l2 & (CH - 1)
    strict = [s2 < t2, s2 > t2]
    incl = [s2 <= t2, s2 >= t2]
    h0 = l2 < HD
    same_blk = (t2 // SB) == (s2 // SB)
    ri = lax.broadcasted_iota(jnp.int32, (LANES, LANES), 0)
    ci = lax.broadcasted_iota(jnp.int32, (LANES, LANES), 1)
    bd = (ri // HD) == (ci // HD)
    bo2 = jnp.where(bd, 1.0, 0.0).astype(BF)
    bd_sb = (ri // SB) == (ci // SB)
    t3 = lax.broadcasted_iota(jnp.int32, (SB, LANES), 0)
    l3 = lax.broadcasted_iota(jnp.int32, (SB, LANES), 1)
    eye_c = jnp.where((l3 & (SB - 1)) == t3, 1.0, 0.0)
    lane_blk = [((l3 & (CH - 1)) // SB) == i for i in range(CH // SB)]
    return t2, strict, incl, h0, bd, bo2, same_blk, bd_sb, eye_c, lane_blk


def _wkv_kernel(T, has_s0, *refs):
    (r_ref, k_ref, v_ref, kk_ref, lw0_ref, lw1_ref, ag0_ref, ag1_ref, ka_ref, rk_ref), rest = refs[:10], refs[10:]
    if has_s0:
        s0_ref, y_ref, bon_ref, s_scr = rest
    else:
        y_ref, bon_ref, st_ref, s_scr = rest
    nch = T // CH
    npair = y_ref.shape[1] // LANES
    cst = _wkv_consts()
    if has_s0:
        s_scr[...] = s0_ref[...]
    else:
        s_scr[...] = jnp.zeros_like(s_scr)
    y_ref[...] = jnp.zeros_like(y_ref)
    bon_ref[...] = jnp.zeros_like(bon_ref)
    lw_refs = (lw0_ref, lw1_ref)
    ag_refs = (ag0_ref, ag1_ref)

    def body(c, carry):
        chains = [(p, d) for p in range(npair) for d in range(2)]
        sl = {}
        for p, d in chains:
            r0 = pl.multiple_of((c if d == 0 else nch - 1 - c) * CH, CH)
            sl[p, d] = (pl.ds(r0, CH), pl.ds(p * LANES, LANES))
        args = {}
        for p, d in chains:
            rs, ls = sl[p, d]
            args[p, d] = (s_scr[d, p], r_ref[rs, ls], k_ref[rs, ls], v_ref[rs, ls], kk_ref[rs, ls],
                          lw_refs[d][rs, ls], ag_refs[d][rs, ls], ka_ref[:, ls], rk_ref[:, ls],
                          y_ref[rs, ls], bon_ref[rs, ls])
        outs = _run_lockstep([_wkv_chunk(d, *args[p, d][:9], cst) for p, d in chains])
        for (p, d), (s_new, y, bon) in zip(chains, outs):
            rs, ls = sl[p, d]
            s_scr[d, p] = s_new
            y_ref[rs, ls] = args[p, d][9] + y
            bon_ref[rs, ls] = args[p, d][10] + bon
        return carry

    lax.fori_loop(0, nch, body, 0)
    if not has_s0:
        for d in range(2):
            for p in range(npair):
                s = s_scr[d, p]
                st_ref[d, 2 * p] = s[:HD, :HD]
                st_ref[d, 2 * p + 1] = s[HD:, HD:]


def _wkv(arrs, ka, rk, T, nb, row_blk0, s0_bd=None, lanes=512):
    ng = D // lanes
    npair = lanes // LANES
    seq = pl.BlockSpec((T, lanes), lambda b, g: (row_blk0 + b, g))
    vec = pl.BlockSpec((1, lanes), lambda b, g: (0, g))
    st = pl.BlockSpec((None, 2, npair, LANES, LANES), lambda b, g: (b, 0, g, 0, 0))
    out_seq = pl.BlockSpec((T, lanes), lambda b, g: (b, g))
    in_specs = [seq] * 8 + [vec, vec]
    ins = list(arrs) + [ka, rk]
    out_specs = [out_seq, out_seq]
    out_shape = [jax.ShapeDtypeStruct((nb * T, D), F32)] * 2
    if s0_bd is not None:
        in_specs.append(st)
        ins.append(s0_bd)
    else:
        out_specs.append(pl.BlockSpec((None, 2, 2 * npair, HD, HD), lambda b, g: (b, 0, g, 0, 0)))
        out_shape.append(jax.ShapeDtypeStruct((nb, 2, NH, HD, HD), F32))
    return pl.pallas_call(
        functools.partial(_wkv_kernel, T, s0_bd is not None),
        grid=(nb, ng),
        in_specs=in_specs,
        out_specs=out_specs,
        out_shape=out_shape,
        scratch_shapes=[pltpu.VMEM((2, npair, LANES, LANES), F32)],
        compiler_params=_cp(("parallel", "parallel")),
        name="wkv_T%d" % T,
    )(*ins)


def _proj_kernel(gn, *refs):
    i = pl.program_id(0)
    is_p = i < NT_P
    if gn:
        (yp_ref, ys_ref, ap_ref, as_ref, bp_ref, bs_ref, g_ref, lng_ref, lnb_ref, bo_ref,
         mod_ref, w_ref, nrm_ref, wr_ref, br_ref, o_ref, xt_o, gate_o) = refs
        y = jnp.where(is_p, yp_ref[...], ys_ref[...])
    else:
        y_ref, ap_ref, as_ref, mod_ref, w_ref, nrm_ref, wr_ref, br_ref, o_ref, xt_o, gate_o = refs
        y = y_ref[...]
    a = jnp.where(is_p, ap_ref[...], as_ref[...])
    if gn:
        bo = bo_ref[...]
        mean = _seg_sum(a, bo) * (1.0 / HD)
        cen = a - mean
        var = _seg_sum(cen * cen, bo) * (1.0 / HD)
        yn = cen * lax.rsqrt(var + GN_EPS) * lng_ref[...] + lnb_ref[...]
        a = (yn + jnp.where(is_p, bp_ref[...], bs_ref[...])) * g_ref[...]
    gt = _mod_vec(mod_ref, _tile_mod_row(i), 2)
    y_new = y + gt * _bdot(a, w_ref[...])
    o_ref[...] = y_new
    row = _tile_mod_row(i)
    xt = _modulate(y_new, nrm_ref[...], _mod_vec(mod_ref, row, 3), _mod_vec(mod_ref, row, 4))
    xt_o[...] = xt.astype(BF)
    gate_o[...] = _route(xt, wr_ref[...], br_ref[...])


def _proj_residual(y, a_p, a_s, mod, w, router, gn_args=None, name="proj"):
    tile = pl.BlockSpec((TM, D), lambda i: (i, 0))
    tile_p = pl.BlockSpec((TM, D), lambda i: (jnp.minimum(i, NT_P - 1), 0))
    tile_s = pl.BlockSpec((TM, D), lambda i: (jnp.maximum(i - NT_P, 0), 0))
    full = lambda a: pl.BlockSpec(a.shape, lambda i: (0,) * a.ndim)
    if gn_args is not None:
        ins = [y[0], y[1], a_p, a_s]
        in_specs = [tile_p, tile_s, tile_p, tile_s]
    else:
        ins = [y, a_p, a_s]
        in_specs = [tile, tile_p, tile_s]
    if gn_args is not None:
        b_p, b_s, g, lng, lnb, bo = gn_args
        ins += [b_p, b_s, g, lng, lnb, bo]
        in_specs += [tile_p, tile_s, tile, full(lng), full(lnb), full(bo)]
    ins += [mod, w] + list(router)
    in_specs += [full(mod), full(w)] + [full(a) for a in router]
    return pl.pallas_call(
        functools.partial(_proj_kernel, gn_args is not None),
        grid=(NT,),
        in_specs=in_specs,
        out_specs=[tile, tile, pl.BlockSpec((TM, LANES), lambda i: (i, 0))],
        out_shape=[jax.ShapeDtypeStruct((NTOK, D), F32), jax.ShapeDtypeStruct((NTOK, D), BF),
                   jax.ShapeDtypeStruct((NTOK, LANES), F32)],
        compiler_params=_cp(("parallel",)),
        name=name,
    )(*ins)


def _route(xt, wr, br):
    x1, x2 = _split2(xt)
    w1, w2 = _split2(wr)
    dot = lambda a, b: jnp.dot(a, b, preferred_element_type=F32)
    logits = (dot(x2, w1) + dot(x1, w2)) + dot(x1, w1)
    logits = logits + br
    lane = lax.broadcasted_iota(jnp.int32, logits.shape, 1).astype(F32)
    isg = jnp.logical_and(lane >= NE, lane < NE + NE // EPG)
    mg = jnp.max(jnp.where(isg, logits, NEG), axis=-1, keepdims=True)
    eg = jnp.where(isg, jnp.exp(jnp.minimum(logits - mg, 0.0)), 0.0)
    p_sel = 1.0 / jnp.sum(eg, axis=-1, keepdims=True)
    gidx = jnp.min(jnp.where(jnp.logical_and(isg, logits == mg), lane, 1e3), axis=-1, keepdims=True) - NE
    lo_l = gidx * EPG
    ing = jnp.logical_and(lane >= lo_l, lane < lo_l + EPG)
    me = jnp.max(jnp.where(ing, logits, NEG), axis=-1, keepdims=True)
    ee = jnp.where(ing, jnp.exp(jnp.minimum(logits - me, 0.0)), 0.0)
    se = jnp.sum(ee, axis=-1, keepdims=True)
    pe = ee / se
    i1 = jnp.min(jnp.where(jnp.logical_and(ing, logits == me), lane, 1e3), axis=-1, keepdims=True)
    v1 = 1.0 / se
    rest = jnp.logical_and(ing, lane != i1)
    v2 = jnp.max(jnp.where(rest, pe, -1.0), axis=-1, keepdims=True)
    i2 = jnp.min(jnp.where(jnp.logical_and(rest, pe == v2), lane, 1e3), axis=-1, keepdims=True)
    den = v1 + v2
    return (jnp.where(lane == i1, p_sel * v1 / den, 0.0)
            + jnp.where(lane == i2, p_sel * v2 / den, 0.0))


def _experts_kernel(final, xt_ref, gate_ref, w13_ref, w2_ref, y_ref, mod_ref, fin_ref, *rest):
    i = pl.program_id(0)
    e = pl.program_id(1)
    acc_ref = rest[-1]
    ntile_p = NPR // TMX

    x = xt_ref[...]
    gate = gate_ref[...]
    lane = lax.broadcasted_iota(jnp.int32, gate_ref.shape, 1)
    hids = []
    for j in range(EPS):
        h13 = jnp.dot(x, w13_ref[j], preferred_element_type=F32)
        h1, h3 = h13[:, :DE], h13[:, DE:]
        ge = jnp.sum(jnp.where(lane == e * EPS + j, gate, 0.0), axis=-1, keepdims=True)
        hids.append(((h1 * jax.nn.sigmoid(h1)) * h3 * ge).astype(BF))
    hid = jnp.concatenate(hids, axis=1)

    @pl.when(e == 0)
    def _():
        acc_ref[...] = jnp.zeros_like(acc_ref)

    acc_ref[...] += jnp.dot(hid, w2_ref[...].reshape(EPS * DE, D), preferred_element_type=F32)

    def result():
        row = jnp.where(i < ntile_p, 0, i - ntile_p + 1)
        out = y_ref[...] + _mod_vec(mod_ref, row, 5) * acc_ref[...]
        if final:
            ms = jnp.mean(out * out, axis=-1, keepdims=True)
            out = out * lax.rsqrt(ms + NORM_EPS) * fin_ref[...]
        return out

    if final:
        op_ref, os_ref = rest[:2]

        @pl.when(jnp.logical_and(e == NE // EPS - 1, i < ntile_p))
        def _():
            op_ref[...] = result()

        @pl.when(jnp.logical_and(e == NE // EPS - 1, i >= ntile_p))
        def _():
            os_ref[...] = result()
    else:
        @pl.when(e == NE // EPS - 1)
        def _():
            rest[0][...] = result()


def _moe_experts(xt, gate, w13, w2, y, mod, fin, final):
    tile = pl.BlockSpec((TMX, D), lambda i, e: (i, 0))
    full = lambda a: pl.BlockSpec(a.shape, lambda i, e: (0,) * a.ndim)
    ntile_p = NPR // TMX
    if final:
        out_specs = [pl.BlockSpec((TMX, D), lambda i, e: (jnp.minimum(i, ntile_p - 1), 0)),
                     pl.BlockSpec((TMX, D), lambda i, e: (jnp.maximum(i - ntile_p, 0), 0))]
        out_shape = [jax.ShapeDtypeStruct((NPR, D), F32), jax.ShapeDtypeStruct((NSR, D), F32)]
    else:
        out_specs = tile
        out_shape = jax.ShapeDtypeStruct((NTOK, D), F32)
    return pl.pallas_call(
        functools.partial(_experts_kernel, final),
        grid=(NTOK // TMX, NE // EPS),
        in_specs=[tile, pl.BlockSpec((TMX, LANES), lambda i, e: (i, 0)),
                  pl.BlockSpec((EPS, D, 2 * DE), lambda i, e: (e, 0, 0)),
                  pl.BlockSpec((EPS, DE, D), lambda i, e: (e, 0, 0)),
                  tile, full(mod), full(fin)],
        out_specs=out_specs,
        out_shape=out_shape,
        scratch_shapes=[pltpu.VMEM((TMX, D), F32)] if final else [],
        compiler_params=_cp(("arbitrary" if final else "parallel", "arbitrary")),
        name="moe_experts",
    )(xt, gate, w13, w2, y, mod, fin)


def _qkv_kernel(y_ref, mod_ref, nrm_ref, w_ref, q_o, kp_o, vp_o, ks_o, vs_o, kc_o, vc_o):
    i = pl.program_id(0)
    row = _tile_mod_row(_qkv_tile(i))
    h = _modulate(y_ref[...], nrm_ref[...], _mod_vec(mod_ref, row, 0), _mod_vec(mod_ref, row, 1))
    hb = h.astype(BF)
    kx = jnp.dot(hb, w_ref[:, D:2 * D], preferred_element_type=F32)
    vx = jnp.dot(hb, w_ref[:, 2 * D:], preferred_element_type=F32)
    q_o[...] = jnp.dot(hb, w_ref[:, :D], preferred_element_type=F32)

    @pl.when(i < NT - NT_P)
    def _():
        ks_o[...] = kx
        vs_o[...] = vx

    @pl.when(i >= NT - NT_P)
    def _():
        kp_o[...] = kx
        vp_o[...] = vx
        for hh in range(NH):
            kc_o[pl.ds(hh, TM, stride=NH), :] = kx[:, hh * HD:(hh + 1) * HD]
            vc_o[pl.ds(hh, TM, stride=NH), :] = vx[:, hh * HD:(hh + 1) * HD]


def _qkv_tile(i):
    return jnp.where(i < NT - NT_P, NT_P + i, i - (NT - NT_P))


def _na_qkv(y, mod, nrm, w):
    nt_s = NT - NT_P
    tile = pl.BlockSpec((TM, D), lambda i: (_qkv_tile(i), 0))
    tile_p = pl.BlockSpec((TM, D), lambda i: (jnp.maximum(i - nt_s, 0), 0))
    tile_s = pl.BlockSpec((TM, D), lambda i: (jnp.minimum(i, nt_s - 1), 0))
    full = lambda a: pl.BlockSpec(a.shape, lambda i: (0,) * a.ndim)
    out = lambda n: jax.ShapeDtypeStruct((n, D), F32)
    cache = jax.ShapeDtypeStruct((NPR * NH, HD), F32)
    cache_tile = pl.BlockSpec((TM * NH, HD), lambda i: (jnp.maximum(i - nt_s, 0), 0))
    return pl.pallas_call(
        _qkv_kernel,
        grid=(NT,),
        in_specs=[tile, full(mod), full(nrm), full(w)],
        out_specs=[tile, tile_p, tile_p, tile_s, tile_s, cache_tile, cache_tile],
        out_shape=[out(NTOK), out(NPR), out(NPR), out(NSR), out(NSR), cache, cache],
        compiler_params=_cp(("arbitrary",)),
        name="na_qkv",
    )(y, mod, nrm, w)


def _softmax_rows(s):
    m = jnp.max(s, axis=-1, keepdims=True)
    e = jnp.exp(s - m)
    return e / jnp.sum(e, axis=-1, keepdims=True)


def _ctx_attn_kernel(q_ref, k_ref, v_ref, o_ref):
    scale = HD ** -0.5
    h0 = lax.broadcasted_iota(jnp.int32, (T_P, LANES), 1) < HD
    def head_chain(h, q, kb, vb):
        qm = jnp.where(h0 if h == 0 else jnp.logical_not(h0), q, 0.0)
        s = _bdot_nt(qm, kb)
        yield
        o = jnp.dot(_softmax_rows(s * scale).astype(BF), vb, preferred_element_type=F32)
        yield
        return o

    npair = D // LANES
    for p0 in range(0, npair, CTX_PAIRS):
        gens = []
        for p in range(p0, p0 + CTX_PAIRS):
            ls = pl.ds(p * LANES, LANES)
            q = q_ref[:, ls]
            kb = k_ref[:, ls].astype(BF)
            vb = v_ref[:, ls].astype(BF)
            gens += [head_chain(h, q, kb, vb) for h in range(2)]
        outs = _run_lockstep(gens)
        for j, p in enumerate(range(p0, p0 + CTX_PAIRS)):
            o_ref[:, pl.ds(p * LANES, LANES)] = jnp.where(h0, outs[2 * j], outs[2 * j + 1])


def _ctx_attn(q, k, v):
    blk = pl.BlockSpec((T_P, D), lambda b: (b, 0))
    return pl.pallas_call(
        _ctx_attn_kernel,
        grid=(NB_P,),
        in_specs=[blk] * 3,
        out_specs=blk,
        out_shape=jax.ShapeDtypeStruct((NPR, D), F32),
        compiler_params=_cp(("parallel",)),
        name="ctx_attn",
    )(q, k, v)


def _nbr_attn_kernel(q_ref, k_ref, v_ref, kc_ref, vc_ref, p_ref, o_ref, tz_ref):
    scale = HD ** -0.5
    rows = T_S // GRID_W
    nloc = WIN_R * GRID_W
    qc = lax.broadcasted_iota(jnp.int32, (GRID_W, LANES), 0)
    ln = lax.broadcasted_iota(jnp.int32, (GRID_W, LANES), 1)
    h0 = ln < HD
    kcb = kc_ref[...].astype(BF)
    vcb = vc_ref[...].astype(BF)
    kcol = ln & (GRID_W - 1)
    cstart = jnp.clip(qc - WIN_C // 2, 0, GRID_W - WIN_C)
    valid = jnp.logical_and(kcol >= cstart, kcol < cstart + WIN_C)
    for h in range(2):
        rolled = []
        for j in range(2 * WIN_R - 1):
            prow = jnp.broadcast_to(p_ref[h, j:j + 1, :], (GRID_W, LANES))
            rolled.append((pltpu.roll(prow, 0, 1, stride=1, stride_axis=0),
                           pltpu.roll(prow, GRID_W, 1, stride=1, stride_axis=0)))
        for j in range(2 * WIN_R - 2):
            tz_ref[h, j] = jnp.where(valid, jnp.where(h0, rolled[j][0], rolled[j + 1][1]), NEG)

    def head_chain(h, q, klb, vlb, j0):
        qm = jnp.where(h0 if h == 0 else jnp.logical_not(h0), q, 0.0).astype(BF)
        sl = _bdot_nt(qm, klb)
        sc = _bdot_nt(qm, kcb)
        yield
        bias = jnp.concatenate([tz_ref[h, j0 + 2 * m] for m in range(WIN_R // 2)], axis=1)
        sl = sl * scale + bias
        sc = sc * scale
        m = jnp.maximum(jnp.max(sl, axis=-1, keepdims=True), jnp.max(sc, axis=-1, keepdims=True))
        el = jnp.exp(sl - m)
        ec = jnp.exp(sc - m)
        inv = 1.0 / (jnp.sum(el, axis=-1, keepdims=True) + jnp.sum(ec, axis=-1, keepdims=True))
        o = (jnp.dot((el * inv).astype(BF), vlb, preferred_element_type=F32)
             + jnp.dot((ec * inv).astype(BF), vcb, preferred_element_type=F32))
        yield
        return o

    def body(it, carry):
        gens, slices = [], []
        for j in range(NBR_UNROLL):
            r = it * NBR_UNROLL + j
            start = jnp.clip(r - WIN_R // 2, 0, rows - WIN_R)
            qs = pl.ds(pl.multiple_of(r * GRID_W, GRID_W), GRID_W)
            ks = pl.ds(pl.multiple_of(start * GRID_W, GRID_W), nloc)
            q = q_ref[qs, :]
            klb = k_ref[ks, :].astype(BF)
            vlb = v_ref[ks, :].astype(BF)
            j0 = start - r + WIN_R - 1
            slices.append(qs)
            gens += [head_chain(h, q, klb, vlb, j0) for h in range(2)]
        outs = _run_lockstep(gens)
        for j, qs in enumerate(slices):
            o_ref[qs, :] = jnp.where(h0, outs[2 * j], outs[2 * j + 1])
        return carry

    lax.fori_loop(0, rows // NBR_UNROLL, body, 0)


def _nbr_attn(q, k, v, kc, vc, ptab):
    npair = D // LANES
    nrow = ptab.shape[1]
    qseq = pl.BlockSpec((T_S, LANES), lambda b, p: (NPR // T_S + b, p))
    seq = pl.BlockSpec((T_S, LANES), lambda b, p: (b, p))
    ctx = pl.BlockSpec((kc.shape[0] // NB_S, LANES), lambda b, p: (b, p))
    return pl.pallas_call(
        _nbr_attn_kernel,
        grid=(NB_S, npair),
        in_specs=[qseq, seq, seq, ctx, ctx,
                  pl.BlockSpec((2, nrow, LANES), lambda b, p: (p, 0, 0))],
        out_specs=pl.BlockSpec((T_S, LANES), lambda b, p: (b, p)),
        out_shape=jax.ShapeDtypeStruct((NSR, D), F32),
        scratch_shapes=[pltpu.VMEM((2, nrow - 1, GRID_W, LANES), F32)],
        compiler_params=_cp(("parallel", "parallel")),
        name="nbr_attn",
    )(q, k, v, kc, vc, ptab)


def _block_diag2(a, b):
    z = jnp.zeros_like(a)
    return jnp.concatenate([jnp.concatenate([a, z], axis=1), jnp.concatenate([z, b], axis=1)], axis=0)


def _nbr_bias_table(rpb):
    pad = jnp.zeros(rpb.shape[:2] + (LANES - 2 * WIN_C + 1,), F32)
    return jnp.concatenate([rpb[:, :, WIN_C - 1:], pad, rpb[:, :, :WIN_C - 1]], axis=-1)


def kernel(x_prompt, x_sample, c, state_rwkv, cache_na_k, cache_na_v, c_ctx, norm_mix, norm_ffn, ada_w, ada_b, rwkv_mu, rwkv_w_r, rwkv_w_k, rwkv_w_v, rwkv_w_o, rwkv_w0, rwkv_w1, rwkv_w2, rwkv_a0, rwkv_a1, rwkv_a2, rwkv_g1, rwkv_g2, rwkv_k_k, rwkv_k_a, rwkv_r_k, rwkv_lnx_g, rwkv_lnx_b, na_w_qkv, na_w_o, na_rpb, moe_w_grp, moe_b_grp, moe_w_exp, moe_b_exp, moe_w1, moe_w3, moe_w2, final_norm):
    x_p = x_prompt.reshape(NPR, D)
    x_s = x_sample.reshape(NSR, D)
    c8 = jnp.concatenate([c_ctx[None, :], c, jnp.zeros((8 - 1 - NB_S, D), F32)], axis=0)
    mod = _adaln(c8, ada_w, ada_b)
    ri = jnp.arange(2 * LANES)[:, None] // HD
    bo = (ri == ri.T).astype(BF)
    row = lambda a: a.reshape(1, D)

    w1c = jnp.concatenate([rwkv_w1[0, 0], rwkv_w1[0, 1]], axis=1).astype(BF)
    a1c = jnp.concatenate([rwkv_a1[0, 0], rwkv_a1[0, 1]], axis=1).astype(BF)
    w2bd = _block_diag2(rwkv_w2[0, 0], rwkv_w2[0, 1]).astype(BF)
    a2bd = _block_diag2(rwkv_a2[0, 0], rwkv_a2[0, 1]).astype(BF)
    r, k, v, g, kk, lw0, lw1, ag0, ag1 = _rwkv_front(
        x_p, x_s, mod[0], row(norm_mix[0]), rwkv_mu[0], rwkv_w_r[0].astype(BF), rwkv_w_k[0].astype(BF),
        rwkv_w_v[0].astype(BF), w1c, a1c, rwkv_g1[0].astype(BF), w2bd, a2bd, rwkv_g2[0].astype(BF),
        rwkv_w0[0], rwkv_a0[0], row(rwkv_k_k[0]), bo)
    arrs = (r, k, v, kk, lw0, lw1, ag0, ag1)
    ka, rk = row(rwkv_k_a[0]), row(rwkv_r_k[0])
    ys_p, bon_p, new_state = _wkv(arrs, ka, rk, T_P, NB_P, 0, lanes=D)
    s0 = state_rwkv[:, 0].reshape(NB_S, 2, NH // 2, 2, HD, HD)
    z = jnp.zeros_like(s0[:, :, :, 0])
    s0_bd = jnp.concatenate([jnp.concatenate([s0[:, :, :, 0], z], axis=-1),
                             jnp.concatenate([z, s0[:, :, :, 1]], axis=-1)], axis=-2)
    ys_s, bon_s = _wkv(arrs, ka, rk, T_S, NB_S, NPR // T_S, s0_bd=s0_bd)
    new_state_rwkv = new_state.reshape(NB_P, 1, 2, NH, HD, HD)
    def router(i):
        wr = jnp.concatenate([moe_w_exp[i], moe_w_grp[i],
                              jnp.zeros((D, LANES - NE - NE // EPG), F32)], axis=1)
        br = jnp.concatenate([moe_b_exp[i], moe_b_grp[i],
                              jnp.zeros((LANES - NE - NE // EPG,), F32)]).reshape(1, LANES)
        return (row(norm_ffn[i]), wr, br)

    def experts(y, xt, gate, i, final):
        w13 = jnp.concatenate([moe_w1[i], moe_w3[i]], axis=-1).astype(BF)
        return _moe_experts(xt, gate, w13, moe_w2[i].astype(BF), y, mod[i], row(final_norm), final)

    y, xt, gate = _proj_residual((x_p, x_s), ys_p, ys_s, mod[0], rwkv_w_o[0].astype(BF), router(0),
                                 gn_args=(bon_p, bon_s, g, row(rwkv_lnx_g[0]), row(rwkv_lnx_b[0]), bo),
                                 name="rwkv_out")
    y = experts(y, xt, gate, 0, False)

    q, k_p, v_p, k_s, v_s, kcache, vcache = _na_qkv(y, mod[1], row(norm_mix[1]), na_w_qkv[0].astype(BF))
    o_p = _ctx_attn(q, k_p, v_p)
    kc = cache_na_k[:, 0].reshape(NB_S * cache_na_k.shape[2], D)
    vc = cache_na_v[:, 0].reshape(NB_S * cache_na_v.shape[2], D)
    o_s = _nbr_attn(q, k_s, v_s, kc, vc, _nbr_bias_table(na_rpb[0]))
    y, xt, gate = _proj_residual(y, o_p, o_s, mod[1], na_w_o[0].astype(BF), router(1), name="attn_out")
    y_p, y_s = experts(y, xt, gate, 1, True)

    y_prompt = y_p.reshape(NB_P, T_P, D)
    y_sample = y_s.reshape(NB_S, T_S, D)
    new_k = kcache.reshape(NB_P, 1, T_P, NH, HD)
    new_v = vcache.reshape(NB_P, 1, T_P, NH, HD)
    return (y_prompt, y_sample, new_state_rwkv, new_k, new_v)
```

```python
import functools
import math

import jax
import jax.numpy as jnp
from jax import lax
from jax.experimental import pallas as pl
from jax.experimental.pallas import tpu as pltpu

F32 = jnp.float32
BF = jnp.bfloat16

D = 1024
NH = 16
HD = 64
NB_P, T_P = 32, 256
NB_S, T_S = 2, 1024
NPR = NB_P * T_P
NSR = NB_S * T_S
NTOK = NPR + NSR
TM = 256
NT = NTOK // TM
NT_P = NPR // TM
TPS = T_S // TM
TMX = 1024
NMOD = 6
NE = 16
EPG = 4
DE = 256
EPS = EPG
RC = 320
RCP = 384
GRP_LANE = 127
CH = 64
SB = 32
LANES = 128
GRID_W = 64
WIN_R, WIN_C = 8, 16
NBR_UNROLL = 4
CTX_PAIRS = 2
NORM_EPS = 1e-6
GN_EPS = 64e-5
NEG = -1e30
VMEM_LIMIT = 56 * 1024 * 1024


def _cp(sem):
    return pltpu.CompilerParams(dimension_semantics=sem, vmem_limit_bytes=VMEM_LIMIT)


def _bdot(a, b):
    return jnp.dot(a.astype(BF), b.astype(BF), preferred_element_type=F32)


def _bdot_nt(a, b):
    return lax.dot_general(a.astype(BF), b.astype(BF), (((1,), (1,)), ((), ())),
                           preferred_element_type=F32)


def _split2(x):
    hi = x.astype(BF)
    lo = (x - hi.astype(F32)).astype(BF)
    return hi, lo


def _split3(x):
    hi = x.astype(BF)
    r1 = x - hi.astype(F32)
    mid = r1.astype(BF)
    lo = (r1 - mid.astype(F32)).astype(BF)
    return hi, mid, lo


def _seg_sum(x, bo):
    outs = []
    for c in range(x.shape[1] // 256):
        hi, lo = _split2(x[:, c * 256:(c + 1) * 256])
        outs.append(jnp.dot(hi, bo, preferred_element_type=F32)
                    + jnp.dot(lo, bo, preferred_element_type=F32))
    return outs[0] if len(outs) == 1 else jnp.concatenate(outs, axis=1)


def _tile_mod_row(i):
    return jnp.where(i < NT_P, 0, 1 + (i - NT_P) // TPS)


def _modulate(x, g, sh, sc):
    ms = jnp.mean(x * x, axis=-1, keepdims=True)
    return x * lax.rsqrt(ms + NORM_EPS) * g * (1.0 + sc) + sh


def _mod_vec(mod_ref, row, k):
    return mod_ref[pl.ds(row, 1), pl.ds(k * D, D)]


def _adaln_kernel(c_ref, w_ref, b_ref, o_ref):
    c = c_ref[...]
    s = c * jax.nn.sigmoid(c)
    o_ref[...] = _bdot(s, w_ref[...]) + b_ref[...]


def _adaln(c8, ada_w, ada_b):
    nl = ada_w.shape[0]
    tn = 1536
    return pl.pallas_call(
        _adaln_kernel,
        grid=(nl, NMOD * D // tn),
        in_specs=[pl.BlockSpec((8, D), lambda l, j: (0, 0)),
                  pl.BlockSpec((None, D, tn), lambda l, j: (l, 0, j)),
                  pl.BlockSpec((None, 1, tn), lambda l, j: (l, 0, j))],
        out_specs=pl.BlockSpec((None, 8, tn), lambda l, j: (l, 0, j)),
        out_shape=jax.ShapeDtypeStruct((nl, 8, NMOD * D), F32),
        compiler_params=_cp(("parallel", "parallel")),
        name="adaln",
    )(c8, ada_w, ada_b.reshape(nl, 1, NMOD * D))


def _front_kernel(xp_ref, xs_ref, yp_ref, yn_ref, mod_ref, nrm_ref, mu_ref, wr_ref, wk_ref, wv_ref,
                  w1_ref, a1_ref, g1_ref, w2_ref, a2_ref, g2_ref, w0_ref, a0_ref, kkw_ref, bo_ref,
                  r_o, k_o, v_o, g_o, kk_o, lw0_o, lw1_o, ag0_o, ag1_o):
    i = pl.program_id(0)
    row = _tile_mod_row(i)
    q = (i - NT_P) % TPS
    first = jnp.logical_or(i < NT_P, q == 0)
    last = jnp.logical_or(i < NT_P, q == TPS - 1)
    sh = _mod_vec(mod_ref, row, 0)
    sc = _mod_vec(mod_ref, row, 1)
    g = nrm_ref[...]
    h = _modulate(jnp.where(i < NT_P, xp_ref[...], xs_ref[...]), g, sh, sc)
    hp = _modulate(yp_ref[...], g, sh, sc)[7:8]
    hn = _modulate(yn_ref[...], g, sh, sc)[0:1]
    hp = jnp.where(first, 0.0, hp)
    hn = jnp.where(last, 0.0, hn)
    rid = lax.broadcasted_iota(jnp.int32, (TM, D), 0)
    prev = jnp.where(rid == 0, hp, pltpu.roll(h, 1, 0))
    nxt = jnp.where(rid == TM - 1, hn, pltpu.roll(h, TM - 1, 0))
    dx = 0.5 * (prev + nxt) - h

    def mix(n):
        return (h + dx * mu_ref[n:n + 1, :]).astype(BF)

    r_o[...] = jnp.dot(mix(0), wr_ref[...], preferred_element_type=F32)
    xw = mix(1)
    lora_w = _bdot(jnp.tanh(jnp.dot(xw, w1_ref[...], preferred_element_type=F32)), w2_ref[...])
    hc = 0.5 * math.exp(-0.5)
    lw0_o[...] = -hc * jnp.tanh(0.5 * (w0_ref[0:1, :] + lora_w[:, :D])) - hc
    lw1_o[...] = -hc * jnp.tanh(0.5 * (w0_ref[1:2, :] + lora_w[:, D:])) - hc
    k_raw = jnp.dot(mix(2), wk_ref[...], preferred_element_type=F32)
    k_o[...] = k_raw
    kq = k_raw * kkw_ref[...]
    ss = _seg_sum(kq * kq, bo_ref[...])
    kk_o[...] = kq / jnp.maximum(jnp.sqrt(ss), 1e-12)
    v_o[...] = jnp.dot(mix(3), wv_ref[...], preferred_element_type=F32)
    xa = mix(4)
    lora_a = _bdot(jnp.dot(xa, a1_ref[...], preferred_element_type=F32), a2_ref[...])
    ag0_o[...] = 0.5 * jnp.tanh(0.5 * (a0_ref[0:1, :] + lora_a[:, :D])) + 0.5
    ag1_o[...] = 0.5 * jnp.tanh(0.5 * (a0_ref[1:2, :] + lora_a[:, D:])) + 0.5
    xg = mix(5)
    g_o[...] = _bdot(jax.nn.sigmoid(jnp.dot(xg, g1_ref[...], preferred_element_type=F32)), g2_ref[...])


def _rwkv_front(x_p, x_s, mod, nrm, mu, wr, wk, wv, w1c, a1c, g1, w2bd, a2bd, g2, w0, a0, kkw, bo):
    tile = pl.BlockSpec((TM, D), lambda i: (i, 0))
    nblk8 = NSR // 8
    bpt = TM // 8
    full = lambda a: pl.BlockSpec(a.shape, lambda i: (0,) * a.ndim)
    ins = [x_p, x_s, x_s, x_s, mod, nrm, mu, wr, wk, wv, w1c, a1c, g1, w2bd, a2bd, g2, w0, a0, kkw, bo]
    in_specs = [pl.BlockSpec((TM, D), lambda i: (jnp.minimum(i, NT_P - 1), 0)),
                pl.BlockSpec((TM, D), lambda i: (jnp.maximum(i - NT_P, 0), 0)),
                pl.BlockSpec((8, D), lambda i: (jnp.clip((i - NT_P) * bpt - 1, 0, nblk8 - 1), 0)),
                pl.BlockSpec((8, D), lambda i: (jnp.clip((i - NT_P + 1) * bpt, 0, nblk8 - 1), 0))]
    in_specs += [full(a) for a in ins[4:]]
    out = jax.ShapeDtypeStruct((NTOK, D), F32)
    return pl.pallas_call(
        _front_kernel,
        grid=(NT,),
        in_specs=in_specs,
        out_specs=[tile] * 9,
        out_shape=[out] * 9,
        compiler_params=_cp(("parallel",)),
        name="rwkv_front",
    )(*ins)


def _wkv_chunk(d, S, r, k, v, kk, lw, ag, ka, rk, cst):
    trow, strict, incl, h0, bd, bo2, same_blk, bd_sb, eye_c, lane_blk = cst
    dot = lambda a, b: jnp.dot(a, b, preferred_element_type=F32)
    nblk = CH // SB
    h0s = lax.broadcasted_iota(jnp.int32, (SB, LANES), 1) < HD

    def stack_heads(x, swap=False):
        zero = jnp.zeros_like(x)
        parts = [jnp.where(h0, x, zero), jnp.where(h0, zero, x)]
        return jnp.concatenate(parts[::-1] if swap else parts, axis=0)

    def dot_split(a, bm):
        return dot(a.astype(BF), bm.astype(BF))

    def expand(mc):
        return jnp.where(bd_sb, jnp.concatenate([mc] * (LANES // SB), axis=0), 0.0)

    b = kk * ag
    kd = k * (1.0 + (ag - 1.0) * ka)
    cs = lw
    for sh in [1 << n for n in range(int(math.log2(CH)))]:
        if d == 0:
            cs = cs + jnp.where(trow >= sh, pltpu.roll(cs, sh, 0), 0.0)
        else:
            cs = cs + jnp.where(trow < CH - sh, pltpu.roll(cs, CH - sh, 0), 0.0)
    qh, ql = _split2(r * kd * rk)
    qs = dot(jnp.concatenate([qh, ql], axis=0), bo2)
    bonus = (qs[:CH] + qs[CH:]) * v
    yield
    tot = cs[CH - 1:CH, :] if d == 0 else cs[0:1, :]
    p_inv = jnp.exp(-cs)
    p_end = jnp.exp(tot - cs)
    at = -kk * jnp.exp(cs - lw)
    rt = r * jnp.exp(cs)
    bt = (b * p_inv).astype(BF)
    kt = (kd * p_inv).astype(BF)
    ar = jnp.concatenate([at, rt], axis=0)
    h02 = jnp.concatenate([h0, h0], axis=0)
    g0 = _bdot_nt(jnp.where(h02, ar, 0.0), jnp.concatenate([bt, kt], axis=0))
    g1 = _bdot_nt(jnp.where(h02, 0.0, ar), jnp.concatenate([kt, bt], axis=0))
    w0y0 = _bdot_nt(ar, S)
    yield
    w0, y0 = w0y0[:CH], w0y0[CH:]
    lab = jnp.where(strict[d], jnp.where(h0, g0[:CH], g1[:CH]), 0.0)
    lak = jnp.where(strict[d], jnp.where(h0, g1[:CH], g0[:CH]), 0.0)
    mrb = jnp.where(incl[d], jnp.where(h0, g0[CH:], g1[CH:]), 0.0)
    mrk = jnp.where(incl[d], jnp.where(h0, g1[CH:], g0[CH:]), 0.0)
    vstk_sw = stack_heads(v, swap=True).astype(BF)
    w = w0 + dot(lak.astype(BF), vstk_sw)
    ldiag = jnp.where(same_blk, lab, 0.0)
    loff = jnp.where(same_blk, 0.0, lab)
    mc = ldiag[0:SB]
    for i in range(1, nblk):
        mc = mc + ldiag[i * SB:(i + 1) * SB]
    xc = eye_c + mc
    mc = dot_split(mc, expand(mc))
    yield
    nsq = int(math.log2(SB)) - 1
    for it in range(nsq):
        last = it + 1 == nsq
        res = dot_split(xc if last else jnp.concatenate([xc, mc], axis=0), expand(mc))
        xc = xc + res[:SB]
        if not last:
            mc = res[SB:]
        yield
    ublk = [None] * nblk
    order = list(range(nblk)) if d == 0 else list(range(nblk - 1, -1, -1))
    for n, i in enumerate(order):
        wi = w[i * SB:(i + 1) * SB]
        if n > 0:
            ucur = jnp.concatenate([jnp.zeros((SB, LANES), F32) if ub is None else ub
                                    for ub in ublk], axis=0)
            wi = wi + dot(loff[i * SB:(i + 1) * SB].astype(BF), stack_heads(ucur).astype(BF))
            yield
        m0 = jnp.where(h0s, wi, 0.0).astype(BF)
        m1 = jnp.where(h0s, 0.0, wi).astype(BF)
        rhs = jnp.concatenate([m0] * nblk + [m1] * nblk, axis=0)
        ublk[i] = dot(jnp.where(lane_blk[i], xc, 0.0).astype(BF), rhs)
        yield
    u = jnp.concatenate(ublk, axis=0)
    y = dot(jnp.concatenate([mrb, mrk], axis=1).astype(BF),
            jnp.concatenate([stack_heads(u).astype(BF), vstk_sw], axis=0))
    uvt = jnp.concatenate([u, v], axis=0).T
    ds = _bdot(uvt, jnp.concatenate([b * p_end, kd * p_end], axis=0))
    yield
    s_new = S * jnp.exp(tot) + jnp.where(bd, ds, 0.0)
    return s_new, y0 + y, bonus


def _run_lockstep(gens):
    results = [None] * len(gens)
    pending = list(range(len(gens)))
    while pending:
        for i in list(pending):
            try:
                next(gens[i])
            except StopIteration as stop:
                results[i] = stop.value
                pending.remove(i)
    return results


def _wkv_consts():
    t2 = lax.broadcasted_iota(jnp.int32, (CH, LANES), 0)
    l2 = lax.broadcasted_iota(jnp.int32, (CH, LANES), 1)
    s2 = l2 & (CH - 1)
    strict = [s2 < t2, s2 > t2]
    incl = [s2 <= t2, s2 >= t2]
    h0 = l2 < HD
    same_blk = (t2 // SB) == (s2 // SB)
    ri = lax.broadcasted_iota(jnp.int32, (LANES, LANES), 0)
    ci = lax.broadcasted_iota(jnp.int32, (LANES, LANES), 1)
    bd = (ri // HD) == (ci // HD)
    bo2 = jnp.where(bd, 1.0, 0.0).astype(BF)
    bd_sb = (ri // SB) == (ci // SB)
    t3 = lax.broadcasted_iota(jnp.int32, (SB, LANES), 0)
    l3 = lax.broadcasted_iota(jnp.int32, (SB, LANES), 1)
    eye_c = jnp.where((l3 & (SB - 1)) == t3, 1.0, 0.0)
    lane_blk = [((l3 & (CH - 1)) // SB) == i for i in range(CH // SB)]
    return t2, strict, incl, h0, bd, bo2, same_blk, bd_sb, eye_c, lane_blk


def _wkv_kernel(T, has_s0, *refs):
    (r_ref, k_ref, v_ref, kk_ref, lw0_ref, lw1_ref, ag0_ref, ag1_ref, ka_ref, rk_ref), rest = refs[:10], refs[10:]
    if has_s0:
        s0_ref, y_ref, bon_ref, s_scr = rest
    else:
        y_ref, bon_ref, st_ref, s_scr = rest
    nch = T // CH
    npair = y_ref.shape[1] // LANES
    cst = _wkv_consts()
    if has_s0:
        s_scr[...] = s0_ref[...]
    else:
        s_scr[...] = jnp.zeros_like(s_scr)
    y_ref[...] = jnp.zeros_like(y_ref)
    bon_ref[...] = jnp.zeros_like(bon_ref)
    lw_refs = (lw0_ref, lw1_ref)
    ag_refs = (ag0_ref, ag1_ref)

    def body(c, carry):
        chains = [(p, d) for p in range(npair) for d in range(2)]
        sl = {}
        for p, d in chains:
            r0 = pl.multiple_of((c if d == 0 else nch - 1 - c) * CH, CH)
            sl[p, d] = (pl.ds(r0, CH), pl.ds(p * LANES, LANES))
        args = {}
        for p, d in chains:
            rs, ls = sl[p, d]
            args[p, d] = (s_scr[d, p], r_ref[rs, ls], k_ref[rs, ls], v_ref[rs, ls], kk_ref[rs, ls],
                          lw_refs[d][rs, ls], ag_refs[d][rs, ls], ka_ref[:, ls], rk_ref[:, ls],
                          y_ref[rs, ls], bon_ref[rs, ls])
        outs = _run_lockstep([_wkv_chunk(d, *args[p, d][:9], cst) for p, d in chains])
        for (p, d), (s_new, y, bon) in zip(chains, outs):
            rs, ls = sl[p, d]
            s_scr[d, p] = s_new
            y_ref[rs, ls] = args[p, d][9] + y
            bon_ref[rs, ls] = args[p, d][10] + bon
        return carry

    lax.fori_loop(0, nch, body, 0)
    if not has_s0:
        for d in range(2):
            for p in range(npair):
                s = s_scr[d, p]
                st_ref[d, 2 * p] = s[:HD, :HD]
                st_ref[d, 2 * p + 1] = s[HD:, HD:]


def _wkv(arrs, ka, rk, T, nb, row_blk0, s0_bd=None, lanes=512):
    ng = D // lanes
    npair = lanes // LANES
    seq = pl.BlockSpec((T, lanes), lambda b, g: (row_blk0 + b, g))
    vec = pl.BlockSpec((1, lanes), lambda b, g: (0, g))
    st = pl.BlockSpec((None, 2, npair, LANES, LANES), lambda b, g: (b, 0, g, 0, 0))
    out_seq = pl.BlockSpec((T, lanes), lambda b, g: (b, g))
    in_specs = [seq] * 8 + [vec, vec]
    ins = list(arrs) + [ka, rk]
    out_specs = [out_seq, out_seq]
    out_shape = [jax.ShapeDtypeStruct((nb * T, D), F32)] * 2
    if s0_bd is not None:
        in_specs.append(st)
        ins.append(s0_bd)
    else:
        out_specs.append(pl.BlockSpec((None, 2, 2 * npair, HD, HD), lambda b, g: (b, 0, g, 0, 0)))
        out_shape.append(jax.ShapeDtypeStruct((nb, 2, NH, HD, HD), F32))
    return pl.pallas_call(
        functools.partial(_wkv_kernel, T, s0_bd is not None),
        grid=(nb, ng),
        in_specs=in_specs,
        out_specs=out_specs,
        out_shape=out_shape,
        scratch_shapes=[pltpu.VMEM((2, npair, LANES, LANES), F32)],
        compiler_params=_cp(("parallel", "parallel")),
        name="wkv_T%d" % T,
    )(*ins)


def _proj_kernel(gn, *refs):
    i = pl.program_id(0)
    is_p = i < NT_P
    if gn:
        (yp_ref, ys_ref, ap_ref, as_ref, bp_ref, bs_ref, g_ref, lng_ref, lnb_ref, bo_ref,
         mod_ref, w_ref, nrm_ref, wr_ref, br_ref, o_ref, xt_o, gate_o) = refs
        y = jnp.where(is_p, yp_ref[...], ys_ref[...])
    else:
        y_ref, ap_ref, as_ref, mod_ref, w_ref, nrm_ref, wr_ref, br_ref, o_ref, xt_o, gate_o = refs
        y = y_ref[...]
    a = jnp.where(is_p, ap_ref[...], as_ref[...])
    if gn:
        bo = bo_ref[...]
        mean = _seg_sum(a, bo) * (1.0 / HD)
        cen = a - mean
        var = _seg_sum(cen * cen, bo) * (1.0 / HD)
        yn = cen * lax.rsqrt(var + GN_EPS) * lng_ref[...] + lnb_ref[...]
        a = (yn + jnp.where(is_p, bp_ref[...], bs_ref[...])) * g_ref[...]
    gt = _mod_vec(mod_ref, _tile_mod_row(i), 2)
    y_new = y + gt * _bdot(a, w_ref[...])
    o_ref[...] = y_new
    row = _tile_mod_row(i)
    xt = _modulate(y_new, nrm_ref[...], _mod_vec(mod_ref, row, 3), _mod_vec(mod_ref, row, 4))
    xt_o[...] = xt.astype(BF)
    gate_o[...] = _route(xt, wr_ref[...], br_ref[...])


def _proj_residual(y, a_p, a_s, mod, w, router, gn_args=None, name="proj"):
    tile = pl.BlockSpec((TM, D), lambda i: (i, 0))
    tile_p = pl.BlockSpec((TM, D), lambda i: (jnp.minimum(i, NT_P - 1), 0))
    tile_s = pl.BlockSpec((TM, D), lambda i: (jnp.maximum(i - NT_P, 0), 0))
    full = lambda a: pl.BlockSpec(a.shape, lambda i: (0,) * a.ndim)
    if gn_args is not None:
        ins = [y[0], y[1], a_p, a_s]
        in_specs = [tile_p, tile_s, tile_p, tile_s]
    else:
        ins = [y, a_p, a_s]
        in_specs = [tile, tile_p, tile_s]
    if gn_args is not None:
        b_p, b_s, g, lng, lnb, bo = gn_args
        ins += [b_p, b_s, g, lng, lnb, bo]
        in_specs += [tile_p, tile_s, tile, full(lng), full(lnb), full(bo)]
    ins += [mod, w] + list(router)
    in_specs += [full(mod), full(w)] + [full(a) for a in router]
    return pl.pallas_call(
        functools.partial(_proj_kernel, gn_args is not None),
        grid=(NT,),
        in_specs=in_specs,
        out_specs=[tile, tile, pl.BlockSpec((TM, LANES), lambda i: (i, 0))],
        out_shape=[jax.ShapeDtypeStruct((NTOK, D), F32), jax.ShapeDtypeStruct((NTOK, D), BF),
                   jax.ShapeDtypeStruct((NTOK, LANES), F32)],
        compiler_params=_cp(("parallel",)),
        name=name,
    )(*ins)


def _route(xt, wr, br):
    x1, x2 = _split2(xt)
    w1, w2 = _split2(wr)
    dot = lambda a, b: jnp.dot(a, b, preferred_element_type=F32)
    logits = (dot(x2, w1) + dot(x1, w2)) + dot(x1, w1)
    logits = logits + br
    lane = lax.broadcasted_iota(jnp.int32, logits.shape, 1).astype(F32)
    isg = jnp.logical_and(lane >= NE, lane < NE + NE // EPG)
    mg = jnp.max(jnp.where(isg, logits, NEG), axis=-1, keepdims=True)
    eg = jnp.where(isg, jnp.exp(jnp.minimum(logits - mg, 0.0)), 0.0)
    p_sel = 1.0 / jnp.sum(eg, axis=-1, keepdims=True)
    gidx = jnp.min(jnp.where(jnp.logical_and(isg, logits == mg), lane, 1e3), axis=-1, keepdims=True) - NE
    lo_l = gidx * EPG
    ing = jnp.logical_and(lane >= lo_l, lane < lo_l + EPG)
    me = jnp.max(jnp.where(ing, logits, NEG), axis=-1, keepdims=True)
    ee = jnp.where(ing, jnp.exp(jnp.minimum(logits - me, 0.0)), 0.0)
    se = jnp.sum(ee, axis=-1, keepdims=True)
    pe = ee / se
    i1 = jnp.min(jnp.where(jnp.logical_and(ing, logits == me), lane, 1e3), axis=-1, keepdims=True)
    v1 = 1.0 / se
    rest = jnp.logical_and(ing, lane != i1)
    v2 = jnp.max(jnp.where(rest, pe, -1.0), axis=-1, keepdims=True)
    i2 = jnp.min(jnp.where(jnp.logical_and(rest, pe == v2), lane, 1e3), axis=-1, keepdims=True)
    den = v1 + v2
    return (jnp.where(lane == i1, p_sel * v1 / den, 0.0)
            + jnp.where(lane == i2, p_sel * v2 / den, 0.0)
            + jnp.where(lane == float(GRP_LANE), gidx, 0.0))


def _experts_kernel(final, nch_ref, xt_ref, gate_ref, w13_ref, w2_ref, y_ref, mod_ref, fin_ref, *rest):
    i = pl.program_id(0)
    g = pl.program_id(1)
    if final:
        op_ref, os_ref, acc_ref, rank_scr, rankt_scr, grpt_scr = rest
    else:
        o_ref, rank_scr, rankt_scr, grpt_scr = rest
        acc_ref = o_ref
    ntile_p = NPR // TMX
    dot = lambda a, b: jnp.dot(a, b, preferred_element_type=F32)
    gate = gate_ref[...]
    lane = lax.broadcasted_iota(jnp.int32, gate.shape, 1)
    grp = jnp.sum(jnp.where(lane == GRP_LANE, gate, 0.0), axis=-1, keepdims=True)

    @pl.when(g == 0)
    def _():
        acc_ref[...] = jnp.zeros_like(acc_ref)
        onehot = jnp.where(lane.astype(F32) == grp, 1.0, 0.0)
        tr = lax.broadcasted_iota(jnp.int32, (TMX, TMX), 0)
        tc = lax.broadcasted_iota(jnp.int32, (TMX, TMX), 1)
        rank_scr[...] = dot(jnp.where(tc < tr, 1.0, 0.0).astype(BF), onehot.astype(BF))
        onehot_t = onehot.T
        rankt = dot(onehot_t.astype(BF), jnp.where(tr < tc, 1.0, 0.0).astype(BF))
        rankt_scr[...] = rankt[:8]
        grpt_scr[...] = onehot_t[:8]

    x = xt_ref[...]
    g1, g2, g3 = _split3(gate)
    mine = grp == g.astype(F32)
    col_rank = jnp.sum(jnp.where(lane == g, rank_scr[...], 0.0), axis=-1, keepdims=True)
    row_rank = rankt_scr[pl.ds(g, 1), :]
    row_mine = grpt_scr[pl.ds(g, 1), :] > 0.5
    r_iota = lax.broadcasted_iota(jnp.int32, (RC, TMX), 0).astype(F32)
    c_iota = lax.broadcasted_iota(jnp.int32, (TMX, RCP), 1).astype(F32)
    nch = nch_ref[i * (NE // EPS) + g]

    for c in range(-(-TMX // RC)):
        @pl.when(c < nch)
        def _(c=c):
            sel = jnp.where(jnp.logical_and(r_iota + float(c * RC) == row_rank, row_mine), 1.0, 0.0).astype(BF)
            xs = dot(sel, x).astype(BF)
            gs = (dot(sel, g3) + dot(sel, g2)) + dot(sel, g1)
            lane_c = lax.broadcasted_iota(jnp.int32, gs.shape, 1)
            hids = []
            for j in range(EPS):
                h13 = dot(xs, w13_ref[j])
                h1, h3 = h13[:, :DE], h13[:, DE:]
                ge = jnp.sum(jnp.where(lane_c == g * EPS + j, gs, 0.0), axis=-1, keepdims=True)
                hids.append(((h1 * jax.nn.sigmoid(h1)) * h3 * ge).astype(BF))
            out = dot(jnp.concatenate(hids, axis=1), w2_ref[...].reshape(EPS * DE, D))
            outp = jnp.concatenate([out.astype(BF), jnp.zeros((RCP - RC, D), BF)], axis=0)
            back = jnp.logical_and(jnp.logical_and(c_iota + float(c * RC) == col_rank, c_iota < float(RC)), mine)
            acc_ref[...] += dot(jnp.where(back, 1.0, 0.0).astype(BF), outp)

    def result():
        row = jnp.where(i < ntile_p, 0, i - ntile_p + 1)
        out = y_ref[...] + _mod_vec(mod_ref, row, 5) * acc_ref[...]
        if final:
            ms = jnp.mean(out * out, axis=-1, keepdims=True)
            out = out * lax.rsqrt(ms + NORM_EPS) * fin_ref[...]
        return out

    if final:
        @pl.when(jnp.logical_and(g == NE // EPS - 1, i < ntile_p))
        def _():
            op_ref[...] = result()

        @pl.when(jnp.logical_and(g == NE // EPS - 1, i >= ntile_p))
        def _():
            os_ref[...] = result()
    else:
        @pl.when(g == NE // EPS - 1)
        def _():
            o_ref[...] = result()


def _moe_experts(nch, xt, gate, w13, w2, y, mod, fin, final):
    tile = pl.BlockSpec((TMX, D), lambda i, e, n: (i, 0))
    full = lambda a: pl.BlockSpec(a.shape, lambda i, e, n: (0,) * a.ndim)
    ntile_p = NPR // TMX
    if final:
        out_specs = [pl.BlockSpec((TMX, D), lambda i, e, n: (jnp.minimum(i, ntile_p - 1), 0)),
                     pl.BlockSpec((TMX, D), lambda i, e, n: (jnp.maximum(i - ntile_p, 0), 0))]
        out_shape = [jax.ShapeDtypeStruct((NPR, D), F32), jax.ShapeDtypeStruct((NSR, D), F32)]
    else:
        out_specs = tile
        out_shape = jax.ShapeDtypeStruct((NTOK, D), F32)
    scratch = [pltpu.VMEM((TMX, LANES), F32), pltpu.VMEM((8, TMX), F32), pltpu.VMEM((8, TMX), F32)]
    if final:
        scratch = [pltpu.VMEM((TMX, D), F32)] + scratch
    grid_spec = pltpu.PrefetchScalarGridSpec(
        num_scalar_prefetch=1,
        grid=(NTOK // TMX, NE // EPS),
        in_specs=[tile, pl.BlockSpec((TMX, LANES), lambda i, e, n: (i, 0)),
                  pl.BlockSpec((EPS, D, 2 * DE), lambda i, e, n: (e, 0, 0)),
                  pl.BlockSpec((EPS, DE, D), lambda i, e, n: (e, 0, 0)),
                  tile, full(mod), full(fin)],
        out_specs=out_specs,
        scratch_shapes=scratch)
    return pl.pallas_call(
        functools.partial(_experts_kernel, final),
        grid_spec=grid_spec,
        out_shape=out_shape,
        compiler_params=_cp(("arbitrary" if final else "parallel", "arbitrary")),
        name="moe_experts",
    )(nch, xt, gate, w13, w2, y, mod, fin)


def _qkv_kernel(y_ref, mod_ref, nrm_ref, w_ref, q_o, kp_o, vp_o, ks_o, vs_o, kc_o, vc_o):
    i = pl.program_id(0)
    row = _tile_mod_row(_qkv_tile(i))
    h = _modulate(y_ref[...], nrm_ref[...], _mod_vec(mod_ref, row, 0), _mod_vec(mod_ref, row, 1))
    hb = h.astype(BF)
    kx = jnp.dot(hb, w_ref[:, D:2 * D], preferred_element_type=F32)
    vx = jnp.dot(hb, w_ref[:, 2 * D:], preferred_element_type=F32)
    q_o[...] = jnp.dot(hb, w_ref[:, :D], preferred_element_type=F32)

    @pl.when(i < NT - NT_P)
    def _():
        ks_o[...] = kx
        vs_o[...] = vx

    @pl.when(i >= NT - NT_P)
    def _():
        kp_o[...] = kx
        vp_o[...] = vx
        for hh in range(NH):
            kc_o[pl.ds(hh, TM, stride=NH), :] = kx[:, hh * HD:(hh + 1) * HD]
            vc_o[pl.ds(hh, TM, stride=NH), :] = vx[:, hh * HD:(hh + 1) * HD]


def _qkv_tile(i):
    return jnp.where(i < NT - NT_P, NT_P + i, i - (NT - NT_P))


def _na_qkv(y, mod, nrm, w):
    nt_s = NT - NT_P
    tile = pl.BlockSpec((TM, D), lambda i: (_qkv_tile(i), 0))
    tile_p = pl.BlockSpec((TM, D), lambda i: (jnp.maximum(i - nt_s, 0), 0))
    tile_s = pl.BlockSpec((TM, D), lambda i: (jnp.minimum(i, nt_s - 1), 0))
    full = lambda a: pl.BlockSpec(a.shape, lambda i: (0,) * a.ndim)
    out = lambda n: jax.ShapeDtypeStruct((n, D), F32)
    cache = jax.ShapeDtypeStruct((NPR * NH, HD), F32)
    cache_tile = pl.BlockSpec((TM * NH, HD), lambda i: (jnp.maximum(i - nt_s, 0), 0))
    return pl.pallas_call(
        _qkv_kernel,
        grid=(NT,),
        in_specs=[tile, full(mod), full(nrm), full(w)],
        out_specs=[tile, tile_p, tile_p, tile_s, tile_s, cache_tile, cache_tile],
        out_shape=[out(NTOK), out(NPR), out(NPR), out(NSR), out(NSR), cache, cache],
        compiler_params=_cp(("arbitrary",)),
        name="na_qkv",
    )(y, mod, nrm, w)


def _softmax_rows(s):
    m = jnp.max(s, axis=-1, keepdims=True)
    e = jnp.exp(s - m)
    return e / jnp.sum(e, axis=-1, keepdims=True)


def _ctx_attn_kernel(q_ref, k_ref, v_ref, o_ref):
    scale = HD ** -0.5
    h0 = lax.broadcasted_iota(jnp.int32, (T_P, LANES), 1) < HD
    def head_chain(h, q, kb, vb):
        qm = jnp.where(h0 if h == 0 else jnp.logical_not(h0), q, 0.0)
        s = _bdot_nt(qm, kb)
        yield
        o = jnp.dot(_softmax_rows(s * scale).astype(BF), vb, preferred_element_type=F32)
        yield
        return o

    npair = D // LANES
    for p0 in range(0, npair, CTX_PAIRS):
        gens = []
        for p in range(p0, p0 + CTX_PAIRS):
            ls = pl.ds(p * LANES, LANES)
            q = q_ref[:, ls]
            kb = k_ref[:, ls].astype(BF)
            vb = v_ref[:, ls].astype(BF)
            gens += [head_chain(h, q, kb, vb) for h in range(2)]
        outs = _run_lockstep(gens)
        for j, p in enumerate(range(p0, p0 + CTX_PAIRS)):
            o_ref[:, pl.ds(p * LANES, LANES)] = jnp.where(h0, outs[2 * j], outs[2 * j + 1])


def _ctx_attn(q, k, v):
    blk = pl.BlockSpec((T_P, D), lambda b: (b, 0))
    return pl.pallas_call(
        _ctx_attn_kernel,
        grid=(NB_P,),
        in_specs=[blk] * 3,
        out_specs=blk,
        out_shape=jax.ShapeDtypeStruct((NPR, D), F32),
        compiler_params=_cp(("parallel",)),
        name="ctx_attn",
    )(q, k, v)


def _nbr_attn_kernel(q_ref, k_ref, v_ref, kc_ref, vc_ref, p_ref, o_ref, tz_ref):
    scale = HD ** -0.5
    rows = T_S // GRID_W
    nloc = WIN_R * GRID_W
    qc = lax.broadcasted_iota(jnp.int32, (GRID_W, LANES), 0)
    ln = lax.broadcasted_iota(jnp.int32, (GRID_W, LANES), 1)
    h0 = ln < HD
    kcb = kc_ref[...].astype(BF)
    vcb = vc_ref[...].astype(BF)
    kcol = ln & (GRID_W - 1)
    cstart = jnp.clip(qc - WIN_C // 2, 0, GRID_W - WIN_C)
    valid = jnp.logical_and(kcol >= cstart, kcol < cstart + WIN_C)
    for h in range(2):
        rolled = []
        for j in range(2 * WIN_R - 1):
            prow = jnp.broadcast_to(p_ref[h, j:j + 1, :], (GRID_W, LANES))
            rolled.append((pltpu.roll(prow, 0, 1, stride=1, stride_axis=0),
                           pltpu.roll(prow, GRID_W, 1, stride=1, stride_axis=0)))
        for j in range(2 * WIN_R - 2):
            tz_ref[h, j] = jnp.where(valid, jnp.where(h0, rolled[j][0], rolled[j + 1][1]), NEG)

    def head_chain(h, q, klb, vlb, j0):
        qm = jnp.where(h0 if h == 0 else jnp.logical_not(h0), q, 0.0).astype(BF)
        sl = _bdot_nt(qm, klb)
        sc = _bdot_nt(qm, kcb)
        yield
        bias = jnp.concatenate([tz_ref[h, j0 + 2 * m] for m in range(WIN_R // 2)], axis=1)
        sl = sl * scale + bias
        sc = sc * scale
        m = jnp.maximum(jnp.max(sl, axis=-1, keepdims=True), jnp.max(sc, axis=-1, keepdims=True))
        el = jnp.exp(sl - m)
        ec = jnp.exp(sc - m)
        inv = 1.0 / (jnp.sum(el, axis=-1, keepdims=True) + jnp.sum(ec, axis=-1, keepdims=True))
        o = (jnp.dot((el * inv).astype(BF), vlb, preferred_element_type=F32)
             + jnp.dot((ec * inv).astype(BF), vcb, preferred_element_type=F32))
        yield
        return o

    def body(it, carry):
        gens, slices = [], []
        for j in range(NBR_UNROLL):
            r = it * NBR_UNROLL + j
            start = jnp.clip(r - WIN_R // 2, 0, rows - WIN_R)
            qs = pl.ds(pl.multiple_of(r * GRID_W, GRID_W), GRID_W)
            ks = pl.ds(pl.multiple_of(start * GRID_W, GRID_W), nloc)
            q = q_ref[qs, :]
            klb = k_ref[ks, :].astype(BF)
            vlb = v_ref[ks, :].astype(BF)
            j0 = start - r + WIN_R - 1
            slices.append(qs)
            gens += [head_chain(h, q, klb, vlb, j0) for h in range(2)]
        outs = _run_lockstep(gens)
        for j, qs in enumerate(slices):
            o_ref[qs, :] = jnp.where(h0, outs[2 * j], outs[2 * j + 1])
        return carry

    lax.fori_loop(0, rows // NBR_UNROLL, body, 0)


def _nbr_attn(q, k, v, kc, vc, ptab):
    npair = D // LANES
    nrow = ptab.shape[1]
    qseq = pl.BlockSpec((T_S, LANES), lambda b, p: (NPR // T_S + b, p))
    seq = pl.BlockSpec((T_S, LANES), lambda b, p: (b, p))
    ctx = pl.BlockSpec((kc.shape[0] // NB_S, LANES), lambda b, p: (b, p))
    return pl.pallas_call(
        _nbr_attn_kernel,
        grid=(NB_S, npair),
        in_specs=[qseq, seq, seq, ctx, ctx,
                  pl.BlockSpec((2, nrow, LANES), lambda b, p: (p, 0, 0))],
        out_specs=pl.BlockSpec((T_S, LANES), lambda b, p: (b, p)),
        out_shape=jax.ShapeDtypeStruct((NSR, D), F32),
        scratch_shapes=[pltpu.VMEM((2, nrow - 1, GRID_W, LANES), F32)],
        compiler_params=_cp(("parallel", "parallel")),
        name="nbr_attn",
    )(q, k, v, kc, vc, ptab)


def _block_diag2(a, b):
    z = jnp.zeros_like(a)
    return jnp.concatenate([jnp.concatenate([a, z], axis=1), jnp.concatenate([z, b], axis=1)], axis=0)


def _nbr_bias_table(rpb):
    pad = jnp.zeros(rpb.shape[:2] + (LANES - 2 * WIN_C + 1,), F32)
    return jnp.concatenate([rpb[:, :, WIN_C - 1:], pad, rpb[:, :, :WIN_C - 1]], axis=-1)


def kernel(x_prompt, x_sample, c, state_rwkv, cache_na_k, cache_na_v, c_ctx, norm_mix, norm_ffn, ada_w, ada_b, rwkv_mu, rwkv_w_r, rwkv_w_k, rwkv_w_v, rwkv_w_o, rwkv_w0, rwkv_w1, rwkv_w2, rwkv_a0, rwkv_a1, rwkv_a2, rwkv_g1, rwkv_g2, rwkv_k_k, rwkv_k_a, rwkv_r_k, rwkv_lnx_g, rwkv_lnx_b, na_w_qkv, na_w_o, na_rpb, moe_w_grp, moe_b_grp, moe_w_exp, moe_b_exp, moe_w1, moe_w3, moe_w2, final_norm):
    x_p = x_prompt.reshape(NPR, D)
    x_s = x_sample.reshape(NSR, D)
    c8 = jnp.concatenate([c_ctx[None, :], c, jnp.zeros((8 - 1 - NB_S, D), F32)], axis=0)
    mod = _adaln(c8, ada_w, ada_b)
    ri = jnp.arange(2 * LANES)[:, None] // HD
    bo = (ri == ri.T).astype(BF)
    row = lambda a: a.reshape(1, D)

    w1c = jnp.concatenate([rwkv_w1[0, 0], rwkv_w1[0, 1]], axis=1).astype(BF)
    a1c = jnp.concatenate([rwkv_a1[0, 0], rwkv_a1[0, 1]], axis=1).astype(BF)
    w2bd = _block_diag2(rwkv_w2[0, 0], rwkv_w2[0, 1]).astype(BF)
    a2bd = _block_diag2(rwkv_a2[0, 0], rwkv_a2[0, 1]).astype(BF)
    r, k, v, g, kk, lw0, lw1, ag0, ag1 = _rwkv_front(
        x_p, x_s, mod[0], row(norm_mix[0]), rwkv_mu[0], rwkv_w_r[0].astype(BF), rwkv_w_k[0].astype(BF),
        rwkv_w_v[0].astype(BF), w1c, a1c, rwkv_g1[0].astype(BF), w2bd, a2bd, rwkv_g2[0].astype(BF),
        rwkv_w0[0], rwkv_a0[0], row(rwkv_k_k[0]), bo)
    arrs = (r, k, v, kk, lw0, lw1, ag0, ag1)
    ka, rk = row(rwkv_k_a[0]), row(rwkv_r_k[0])
    ys_p, bon_p, new_state = _wkv(arrs, ka, rk, T_P, NB_P, 0, lanes=D)
    s0 = state_rwkv[:, 0].reshape(NB_S, 2, NH // 2, 2, HD, HD)
    z = jnp.zeros_like(s0[:, :, :, 0])
    s0_bd = jnp.concatenate([jnp.concatenate([s0[:, :, :, 0], z], axis=-1),
                             jnp.concatenate([z, s0[:, :, :, 1]], axis=-1)], axis=-2)
    ys_s, bon_s = _wkv(arrs, ka, rk, T_S, NB_S, NPR // T_S, s0_bd=s0_bd)
    new_state_rwkv = new_state.reshape(NB_P, 1, 2, NH, HD, HD)
    def router(i):
        wr = jnp.concatenate([moe_w_exp[i], moe_w_grp[i],
                              jnp.zeros((D, LANES - NE - NE // EPG), F32)], axis=1)
        br = jnp.concatenate([moe_b_exp[i], moe_b_grp[i],
                              jnp.zeros((LANES - NE - NE // EPG,), F32)]).reshape(1, LANES)
        return (row(norm_ffn[i]), wr, br)

    def experts(y, xt, gate, i, final):
        w13 = jnp.concatenate([moe_w1[i], moe_w3[i]], axis=-1).astype(BF)
        grp = gate[:, GRP_LANE].astype(jnp.int32).reshape(NTOK // TMX, TMX)
        cnt = jnp.sum(grp[:, :, None] == jnp.arange(NE // EPG)[None, None, :], axis=1)
        nch = ((cnt + RC - 1) // RC).astype(jnp.int32).reshape(-1)
        return _moe_experts(nch, xt, gate, w13, moe_w2[i].astype(BF), y, mod[i], row(final_norm), final)

    y, xt, gate = _proj_residual((x_p, x_s), ys_p, ys_s, mod[0], rwkv_w_o[0].astype(BF), router(0),
                                 gn_args=(bon_p, bon_s, g, row(rwkv_lnx_g[0]), row(rwkv_lnx_b[0]), bo),
                                 name="rwkv_out")
    y = experts(y, xt, gate, 0, False)

    q, k_p, v_p, k_s, v_s, kcache, vcache = _na_qkv(y, mod[1], row(norm_mix[1]), na_w_qkv[0].astype(BF))
    o_p = _ctx_attn(q, k_p, v_p)
    kc = cache_na_k[:, 0].reshape(NB_S * cache_na_k.shape[2], D)
    vc = cache_na_v[:, 0].reshape(NB_S * cache_na_v.shape[2], D)
    o_s = _nbr_attn(q, k_s, v_s, kc, vc, _nbr_bias_table(na_rpb[0]))
    y, xt, gate = _proj_residual(y, o_p, o_s, mod[1], na_w_o[0].astype(BF), router(1), name="attn_out")
    y_p, y_s = experts(y, xt, gate, 1, True)

    y_prompt = y_p.reshape(NB_P, T_P, D)
    y_sample = y_s.reshape(NB_S, T_S, D)
    new_k = kcache.reshape(NB_P, 1, T_P, NH, HD)
    new_v = vcache.reshape(NB_P, 1, T_P, NH, HD)
    return (y_prompt, y_sample, new_state_rwkv, new_k, new_v)
```

```python
import functools
import math

import jax
import jax.numpy as jnp
from jax import lax
from jax.experimental import pallas as pl
from jax.experimental.pallas import tpu as pltpu

F32 = jnp.float32
BF = jnp.bfloat16

D = 1024
NH = 16
HD = 64
NB_P, T_P = 32, 256
NB_S, T_S = 2, 1024
NPR = NB_P * T_P
NSR = NB_S * T_S
NTOK = NPR + NSR
TM = 256
NT = NTOK // TM
NT_P = NPR // TM
TPS = T_S // TM
TMX = 1024
NMOD = 6
NE = 16
EPG = 4
DE = 256
EPS = EPG
SBR = 256
GRP_LANE = 127
CH = 64
SB = 32
LANES = 128
GRID_W = 64
WIN_R, WIN_C = 8, 16
NBR_UNROLL = 4
CTX_PAIRS = 2
NORM_EPS = 1e-6
GN_EPS = 64e-5
NEG = -1e30
VMEM_LIMIT = 56 * 1024 * 1024


def _cp(sem):
    return pltpu.CompilerParams(dimension_semantics=sem, vmem_limit_bytes=VMEM_LIMIT)


def _bdot(a, b):
    return jnp.dot(a.astype(BF), b.astype(BF), preferred_element_type=F32)


def _bdot_nt(a, b):
    return lax.dot_general(a.astype(BF), b.astype(BF), (((1,), (1,)), ((), ())),
                           preferred_element_type=F32)


def _split2(x):
    hi = x.astype(BF)
    lo = (x - hi.astype(F32)).astype(BF)
    return hi, lo


def _split3(x):
    hi = x.astype(BF)
    r1 = x - hi.astype(F32)
    mid = r1.astype(BF)
    lo = (r1 - mid.astype(F32)).astype(BF)
    return hi, mid, lo


def _seg_sum(x, bo):
    outs = []
    for c in range(x.shape[1] // 256):
        hi, lo = _split2(x[:, c * 256:(c + 1) * 256])
        outs.append(jnp.dot(hi, bo, preferred_element_type=F32)
                    + jnp.dot(lo, bo, preferred_element_type=F32))
    return outs[0] if len(outs) == 1 else jnp.concatenate(outs, axis=1)


def _tile_mod_row(i):
    return jnp.where(i < NT_P, 0, 1 + (i - NT_P) // TPS)


def _modulate(x, g, sh, sc):
    ms = jnp.mean(x * x, axis=-1, keepdims=True)
    return x * lax.rsqrt(ms + NORM_EPS) * g * (1.0 + sc) + sh


def _mod_vec(mod_ref, row, k):
    return mod_ref[pl.ds(row, 1), pl.ds(k * D, D)]


def _adaln_kernel(c_ref, w_ref, b_ref, o_ref):
    c = c_ref[...]
    s = c * jax.nn.sigmoid(c)
    o_ref[...] = _bdot(s, w_ref[...]) + b_ref[...]


def _adaln(c8, ada_w, ada_b):
    nl = ada_w.shape[0]
    tn = 1536
    return pl.pallas_call(
        _adaln_kernel,
        grid=(nl, NMOD * D // tn),
        in_specs=[pl.BlockSpec((8, D), lambda l, j: (0, 0)),
                  pl.BlockSpec((None, D, tn), lambda l, j: (l, 0, j)),
                  pl.BlockSpec((None, 1, tn), lambda l, j: (l, 0, j))],
        out_specs=pl.BlockSpec((None, 8, tn), lambda l, j: (l, 0, j)),
        out_shape=jax.ShapeDtypeStruct((nl, 8, NMOD * D), F32),
        compiler_params=_cp(("parallel", "parallel")),
        name="adaln",
    )(c8, ada_w, ada_b.reshape(nl, 1, NMOD * D))


def _front_kernel(xp_ref, xs_ref, yp_ref, yn_ref, mod_ref, nrm_ref, mu_ref, wr_ref, wk_ref, wv_ref,
                  w1_ref, a1_ref, g1_ref, w2_ref, a2_ref, g2_ref, w0_ref, a0_ref, kkw_ref, bo_ref,
                  r_o, k_o, v_o, g_o, kk_o, lw0_o, lw1_o, ag0_o, ag1_o):
    i = pl.program_id(0)
    row = _tile_mod_row(i)
    q = (i - NT_P) % TPS
    first = jnp.logical_or(i < NT_P, q == 0)
    last = jnp.logical_or(i < NT_P, q == TPS - 1)
    sh = _mod_vec(mod_ref, row, 0)
    sc = _mod_vec(mod_ref, row, 1)
    g = nrm_ref[...]
    h = _modulate(jnp.where(i < NT_P, xp_ref[...], xs_ref[...]), g, sh, sc)
    hp = _modulate(yp_ref[...], g, sh, sc)[7:8]
    hn = _modulate(yn_ref[...], g, sh, sc)[0:1]
    hp = jnp.where(first, 0.0, hp)
    hn = jnp.where(last, 0.0, hn)
    rid = lax.broadcasted_iota(jnp.int32, (TM, D), 0)
    prev = jnp.where(rid == 0, hp, pltpu.roll(h, 1, 0))
    nxt = jnp.where(rid == TM - 1, hn, pltpu.roll(h, TM - 1, 0))
    dx = 0.5 * (prev + nxt) - h

    def mix(n):
        return (h + dx * mu_ref[n:n + 1, :]).astype(BF)

    r_o[...] = jnp.dot(mix(0), wr_ref[...], preferred_element_type=F32)
    xw = mix(1)
    lora_w = _bdot(jnp.tanh(jnp.dot(xw, w1_ref[...], preferred_element_type=F32)), w2_ref[...])
    hc = 0.5 * math.exp(-0.5)
    lw0_o[...] = -hc * jnp.tanh(0.5 * (w0_ref[0:1, :] + lora_w[:, :D])) - hc
    lw1_o[...] = -hc * jnp.tanh(0.5 * (w0_ref[1:2, :] + lora_w[:, D:])) - hc
    k_raw = jnp.dot(mix(2), wk_ref[...], preferred_element_type=F32)
    k_o[...] = k_raw
    kq = k_raw * kkw_ref[...]
    ss = _seg_sum(kq * kq, bo_ref[...])
    kk_o[...] = kq / jnp.maximum(jnp.sqrt(ss), 1e-12)
    v_o[...] = jnp.dot(mix(3), wv_ref[...], preferred_element_type=F32)
    xa = mix(4)
    lora_a = _bdot(jnp.dot(xa, a1_ref[...], preferred_element_type=F32), a2_ref[...])
    ag0_o[...] = 0.5 * jnp.tanh(0.5 * (a0_ref[0:1, :] + lora_a[:, :D])) + 0.5
    ag1_o[...] = 0.5 * jnp.tanh(0.5 * (a0_ref[1:2, :] + lora_a[:, D:])) + 0.5
    xg = mix(5)
    g_o[...] = _bdot(jax.nn.sigmoid(jnp.dot(xg, g1_ref[...], preferred_element_type=F32)), g2_ref[...])


def _rwkv_front(x_p, x_s, mod, nrm, mu, wr, wk, wv, w1c, a1c, g1, w2bd, a2bd, g2, w0, a0, kkw, bo):
    tile = pl.BlockSpec((TM, D), lambda i: (i, 0))
    nblk8 = NSR // 8
    bpt = TM // 8
    full = lambda a: pl.BlockSpec(a.shape, lambda i: (0,) * a.ndim)
    ins = [x_p, x_s, x_s, x_s, mod, nrm, mu, wr, wk, wv, w1c, a1c, g1, w2bd, a2bd, g2, w0, a0, kkw, bo]
    in_specs = [pl.BlockSpec((TM, D), lambda i: (jnp.minimum(i, NT_P - 1), 0)),
                pl.BlockSpec((TM, D), lambda i: (jnp.maximum(i - NT_P, 0), 0)),
                pl.BlockSpec((8, D), lambda i: (jnp.clip((i - NT_P) * bpt - 1, 0, nblk8 - 1), 0)),
                pl.BlockSpec((8, D), lambda i: (jnp.clip((i - NT_P + 1) * bpt, 0, nblk8 - 1), 0))]
    in_specs += [full(a) for a in ins[4:]]
    out = jax.ShapeDtypeStruct((NTOK, D), F32)
    return pl.pallas_call(
        _front_kernel,
        grid=(NT,),
        in_specs=in_specs,
        out_specs=[tile] * 9,
        out_shape=[out] * 9,
        compiler_params=_cp(("parallel",)),
        name="rwkv_front",
    )(*ins)


def _wkv_chunk(d, S, r, k, v, kk, lw, ag, ka, rk, cst):
    trow, strict, incl, h0, bd, bo2, same_blk, bd_sb, eye_c, lane_blk = cst
    dot = lambda a, b: jnp.dot(a, b, preferred_element_type=F32)
    nblk = CH // SB
    h0s = lax.broadcasted_iota(jnp.int32, (SB, LANES), 1) < HD

    def stack_heads(x, swap=False):
        zero = jnp.zeros_like(x)
        parts = [jnp.where(h0, x, zero), jnp.where(h0, zero, x)]
        return jnp.concatenate(parts[::-1] if swap else parts, axis=0)

    def dot_split(a, bm):
        return dot(a.astype(BF), bm.astype(BF))

    def expand(mc):
        return jnp.where(bd_sb, jnp.concatenate([mc] * (LANES // SB), axis=0), 0.0)

    b = kk * ag
    kd = k * (1.0 + (ag - 1.0) * ka)
    cs = lw
    for sh in [1 << n for n in range(int(math.log2(CH)))]:
        if d == 0:
            cs = cs + jnp.where(trow >= sh, pltpu.roll(cs, sh, 0), 0.0)
        else:
            cs = cs + jnp.where(trow < CH - sh, pltpu.roll(cs, CH - sh, 0), 0.0)
    qh, ql = _split2(r * kd * rk)
    qs = dot(jnp.concatenate([qh, ql], axis=0), bo2)
    bonus = (qs[:CH] + qs[CH:]) * v
    yield
    tot = cs[CH - 1:CH, :] if d == 0 else cs[0:1, :]
    p_inv = jnp.exp(-cs)
    p_end = jnp.exp(tot - cs)
    at = -kk * jnp.exp(cs - lw)
    rt = r * jnp.exp(cs)
    bt = (b * p_inv).astype(BF)
    kt = (kd * p_inv).astype(BF)
    ar = jnp.concatenate([at, rt], axis=0)
    h02 = jnp.concatenate([h0, h0], axis=0)
    g0 = _bdot_nt(jnp.where(h02, ar, 0.0), jnp.concatenate([bt, kt], axis=0))
    g1 = _bdot_nt(jnp.where(h02, 0.0, ar), jnp.concatenate([kt, bt], axis=0))
    w0y0 = _bdot_nt(ar, S)
    yield
    w0, y0 = w0y0[:CH], w0y0[CH:]
    lab = jnp.where(strict[d], jnp.where(h0, g0[:CH], g1[:CH]), 0.0)
    lak = jnp.where(strict[d], jnp.where(h0, g1[:CH], g0[:CH]), 0.0)
    mrb = jnp.where(incl[d], jnp.where(h0, g0[CH:], g1[CH:]), 0.0)
    mrk = jnp.where(incl[d], jnp.where(h0, g1[CH:], g0[CH:]), 0.0)
    vstk_sw = stack_heads(v, swap=True).astype(BF)
    w = w0 + dot(lak.astype(BF), vstk_sw)
    ldiag = jnp.where(same_blk, lab, 0.0)
    loff = jnp.where(same_blk, 0.0, lab)
    mc = ldiag[0:SB]
    for i in range(1, nblk):
        mc = mc + ldiag[i * SB:(i + 1) * SB]
    xc = eye_c + mc
    mc = dot_split(mc, expand(mc))
    yield
    nsq = int(math.log2(SB)) - 1
    for it in range(nsq):
        last = it + 1 == nsq
        res = dot_split(xc if last else jnp.concatenate([xc, mc], axis=0), expand(mc))
        xc = xc + res[:SB]
        if not last:
            mc = res[SB:]
        yield
    ublk = [None] * nblk
    order = list(range(nblk)) if d == 0 else list(range(nblk - 1, -1, -1))
    for n, i in enumerate(order):
        wi = w[i * SB:(i + 1) * SB]
        if n > 0:
            ucur = jnp.concatenate([jnp.zeros((SB, LANES), F32) if ub is None else ub
                                    for ub in ublk], axis=0)
            wi = wi + dot(loff[i * SB:(i + 1) * SB].astype(BF), stack_heads(ucur).astype(BF))
            yield
        m0 = jnp.where(h0s, wi, 0.0).astype(BF)
        m1 = jnp.where(h0s, 0.0, wi).astype(BF)
        rhs = jnp.concatenate([m0] * nblk + [m1] * nblk, axis=0)
        ublk[i] = dot(jnp.where(lane_blk[i], xc, 0.0).astype(BF), rhs)
        yield
    u = jnp.concatenate(ublk, axis=0)
    y = dot(jnp.concatenate([mrb, mrk], axis=1).astype(BF),
            jnp.concatenate([stack_heads(u).astype(BF), vstk_sw], axis=0))
    uvt = jnp.concatenate([u, v], axis=0).T
    ds = _bdot(uvt, jnp.concatenate([b * p_end, kd * p_end], axis=0))
    yield
    s_new = S * jnp.exp(tot) + jnp.where(bd, ds, 0.0)
    return s_new, y0 + y, bonus


def _run_lockstep(gens):
    results = [None] * len(gens)
    pending = list(range(len(gens)))
    while pending:
        for i in list(pending):
            try:
                next(gens[i])
            except StopIteration as stop:
                results[i] = stop.value
                pending.remove(i)
    return results


def _wkv_consts():
    t2 = lax.broadcasted_iota(jnp.int32, (CH, LANES), 0)
    l2 = lax.broadcasted_iota(jnp.int32, (CH, LANES), 1)
    s2 = l2 & (CH - 1)
    strict = [s2 < t2, s2 > t2]
    incl = [s2 <= t2, s2 >= t2]
    h0 = l2 < HD
    same_blk = (t2 // SB) == (s2 // SB)
    ri = lax.broadcasted_iota(jnp.int32, (LANES, LANES), 0)
    ci = lax.broadcasted_iota(jnp.int32, (LANES, LANES), 1)
    bd = (ri // HD) == (ci // HD)
    bo2 = jnp.where(bd, 1.0, 0.0).astype(BF)
    bd_sb = (ri // SB) == (ci // SB)
    t3 = lax.broadcasted_iota(jnp.int32, (SB, LANES), 0)
    l3 = lax.broadcasted_iota(jnp.int32, (SB, LANES), 1)
    eye_c = jnp.where((l3 & (SB - 1)) == t3, 1.0, 0.0)
    lane_blk = [((l3 & (CH - 1)) // SB) == i for i in range(CH // SB)]
    return t2, strict, incl, h0, bd, bo2, same_blk, bd_sb, eye_c, lane_blk


def _wkv_kernel(T, has_s0, *refs):
    (r_ref, k_ref, v_ref, kk_ref, lw0_ref, lw1_ref, ag0_ref, ag1_ref, ka_ref, rk_ref), rest = refs[:10], refs[10:]
    if has_s0:
        s0_ref, y_ref, bon_ref, s_scr = rest
    else:
        y_ref, bon_ref, st_ref, s_scr = rest
    nch = T // CH
    npair = y_ref.shape[1] // LANES
    cst = _wkv_consts()
    if has_s0:
        s_scr[...] = s0_ref[...]
    else:
        s_scr[...] = jnp.zeros_like(s_scr)
    y_ref[...] = jnp.zeros_like(y_ref)
    bon_ref[...] = jnp.zeros_like(bon_ref)
    lw_refs = (lw0_ref, lw1_ref)
    ag_refs = (ag0_ref, ag1_ref)

    def body(c, carry):
        chains = [(p, d) for p in range(npair) for d in range(2)]
        sl = {}
        for p, d in chains:
            r0 = pl.multiple_of((c if d == 0 else nch - 1 - c) * CH, CH)
            sl[p, d] = (pl.ds(r0, CH), pl.ds(p * LANES, LANES))
        args = {}
        for p, d in chains:
            rs, ls = sl[p, d]
            args[p, d] = (s_scr[d, p], r_ref[rs, ls], k_ref[rs, ls], v_ref[rs, ls], kk_ref[rs, ls],
                          lw_refs[d][rs, ls], ag_refs[d][rs, ls], ka_ref[:, ls], rk_ref[:, ls],
                          y_ref[rs, ls], bon_ref[rs, ls])
        outs = _run_lockstep([_wkv_chunk(d, *args[p, d][:9], cst) for p, d in chains])
        for (p, d), (s_new, y, bon) in zip(chains, outs):
            rs, ls = sl[p, d]
            s_scr[d, p] = s_new
            y_ref[rs, ls] = args[p, d][9] + y
            bon_ref[rs, ls] = args[p, d][10] + bon
        return carry

    lax.fori_loop(0, nch, body, 0)
    if not has_s0:
        for d in range(2):
            for p in range(npair):
                s = s_scr[d, p]
                st_ref[d, 2 * p] = s[:HD, :HD]
                st_ref[d, 2 * p + 1] = s[HD:, HD:]


def _wkv(arrs, ka, rk, T, nb, row_blk0, s0_bd=None, lanes=512):
    ng = D // lanes
    npair = lanes // LANES
    seq = pl.BlockSpec((T, lanes), lambda b, g: (row_blk0 + b, g))
    vec = pl.BlockSpec((1, lanes), lambda b, g: (0, g))
    st = pl.BlockSpec((None, 2, npair, LANES, LANES), lambda b, g: (b, 0, g, 0, 0))
    out_seq = pl.BlockSpec((T, lanes), lambda b, g: (b, g))
    in_specs = [seq] * 8 + [vec, vec]
    ins = list(arrs) + [ka, rk]
    out_specs = [out_seq, out_seq]
    out_shape = [jax.ShapeDtypeStruct((nb * T, D), F32)] * 2
    if s0_bd is not None:
        in_specs.append(st)
        ins.append(s0_bd)
    else:
        out_specs.append(pl.BlockSpec((None, 2, 2 * npair, HD, HD), lambda b, g: (b, 0, g, 0, 0)))
        out_shape.append(jax.ShapeDtypeStruct((nb, 2, NH, HD, HD), F32))
    return pl.pallas_call(
        functools.partial(_wkv_kernel, T, s0_bd is not None),
        grid=(nb, ng),
        in_specs=in_specs,
        out_specs=out_specs,
        out_shape=out_shape,
        scratch_shapes=[pltpu.VMEM((2, npair, LANES, LANES), F32)],
        compiler_params=_cp(("parallel", "parallel")),
        name="wkv_T%d" % T,
    )(*ins)


def _proj_kernel(gn, *refs):
    i = pl.program_id(0)
    is_p = i < NT_P
    if gn:
        (yp_ref, ys_ref, ap_ref, as_ref, bp_ref, bs_ref, g_ref, lng_ref, lnb_ref, bo_ref,
         mod_ref, w_ref, nrm_ref, wr_ref, br_ref, o_ref, xt_o, gate_o) = refs
        y = jnp.where(is_p, yp_ref[...], ys_ref[...])
    else:
        y_ref, ap_ref, as_ref, mod_ref, w_ref, nrm_ref, wr_ref, br_ref, o_ref, xt_o, gate_o = refs
        y = y_ref[...]
    a = jnp.where(is_p, ap_ref[...], as_ref[...])
    if gn:
        bo = bo_ref[...]
        mean = _seg_sum(a, bo) * (1.0 / HD)
        cen = a - mean
        var = _seg_sum(cen * cen, bo) * (1.0 / HD)
        yn = cen * lax.rsqrt(var + GN_EPS) * lng_ref[...] + lnb_ref[...]
        a = (yn + jnp.where(is_p, bp_ref[...], bs_ref[...])) * g_ref[...]
    gt = _mod_vec(mod_ref, _tile_mod_row(i), 2)
    y_new = y + gt * _bdot(a, w_ref[...])
    o_ref[...] = y_new
    row = _tile_mod_row(i)
    xt = _modulate(y_new, nrm_ref[...], _mod_vec(mod_ref, row, 3), _mod_vec(mod_ref, row, 4))
    xt_o[...] = xt.astype(BF)
    gate_o[...] = _route(xt, wr_ref[...], br_ref[...])


def _proj_residual(y, a_p, a_s, mod, w, router, gn_args=None, name="proj"):
    tile = pl.BlockSpec((TM, D), lambda i: (i, 0))
    tile_p = pl.BlockSpec((TM, D), lambda i: (jnp.minimum(i, NT_P - 1), 0))
    tile_s = pl.BlockSpec((TM, D), lambda i: (jnp.maximum(i - NT_P, 0), 0))
    full = lambda a: pl.BlockSpec(a.shape, lambda i: (0,) * a.ndim)
    if gn_args is not None:
        ins = [y[0], y[1], a_p, a_s]
        in_specs = [tile_p, tile_s, tile_p, tile_s]
    else:
        ins = [y, a_p, a_s]
        in_specs = [tile, tile_p, tile_s]
    if gn_args is not None:
        b_p, b_s, g, lng, lnb, bo = gn_args
        ins += [b_p, b_s, g, lng, lnb, bo]
        in_specs += [tile_p, tile_s, tile, full(lng), full(lnb), full(bo)]
    ins += [mod, w] + list(router)
    in_specs += [full(mod), full(w)] + [full(a) for a in router]
    return pl.pallas_call(
        functools.partial(_proj_kernel, gn_args is not None),
        grid=(NT,),
        in_specs=in_specs,
        out_specs=[tile, tile, pl.BlockSpec((TM, LANES), lambda i: (i, 0))],
        out_shape=[jax.ShapeDtypeStruct((NTOK, D), F32), jax.ShapeDtypeStruct((NTOK, D), BF),
                   jax.ShapeDtypeStruct((NTOK, LANES), F32)],
        compiler_params=_cp(("parallel",)),
        name=name,
    )(*ins)


def _route(xt, wr, br):
    x1, x2 = _split2(xt)
    w1, w2 = _split2(wr)
    dot = lambda a, b: jnp.dot(a, b, preferred_element_type=F32)
    logits = (dot(x2, w1) + dot(x1, w2)) + dot(x1, w1)
    logits = logits + br
    lane = lax.broadcasted_iota(jnp.int32, logits.shape, 1).astype(F32)
    isg = jnp.logical_and(lane >= NE, lane < NE + NE // EPG)
    mg = jnp.max(jnp.where(isg, logits, NEG), axis=-1, keepdims=True)
    eg = jnp.where(isg, jnp.exp(jnp.minimum(logits - mg, 0.0)), 0.0)
    p_sel = 1.0 / jnp.sum(eg, axis=-1, keepdims=True)
    gidx = jnp.min(jnp.where(jnp.logical_and(isg, logits == mg), lane, 1e3), axis=-1, keepdims=True) - NE
    lo_l = gidx * EPG
    ing = jnp.logical_and(lane >= lo_l, lane < lo_l + EPG)
    me = jnp.max(jnp.where(ing, logits, NEG), axis=-1, keepdims=True)
    ee = jnp.where(ing, jnp.exp(jnp.minimum(logits - me, 0.0)), 0.0)
    se = jnp.sum(ee, axis=-1, keepdims=True)
    pe = ee / se
    i1 = jnp.min(jnp.where(jnp.logical_and(ing, logits == me), lane, 1e3), axis=-1, keepdims=True)
    v1 = 1.0 / se
    rest = jnp.logical_and(ing, lane != i1)
    v2 = jnp.max(jnp.where(rest, pe, -1.0), axis=-1, keepdims=True)
    i2 = jnp.min(jnp.where(jnp.logical_and(rest, pe == v2), lane, 1e3), axis=-1, keepdims=True)
    den = v1 + v2
    return (jnp.where(lane == i1, p_sel * v1 / den, 0.0)
            + jnp.where(lane == i2, p_sel * v2 / den, 0.0)
            + jnp.where(lane == float(GRP_LANE), gidx, 0.0))


def _experts_kernel(final, blk_ref, xt_ref, gate_ref, w13_ref, w2_ref, y_ref, mod_ref, fin_ref, *rest):
    i = pl.program_id(0)
    g = pl.program_id(1)
    ngrp = NE // EPS
    if final:
        op_ref, os_ref, xs_scr, gs_scr, outs_scr, pos_scr = rest
    else:
        o_ref, xs_scr, gs_scr, outs_scr, pos_scr = rest
    ntile_p = NPR // TMX
    dot = lambda a, b: jnp.dot(a, b, preferred_element_type=F32)

    @pl.when(g == 0)
    def _():
        gate = gate_ref[...]
        lane = lax.broadcasted_iota(jnp.int32, gate.shape, 1)
        grp = jnp.sum(jnp.where(lane == GRP_LANE, gate, 0.0), axis=-1, keepdims=True)
        onehot = jnp.where(lane.astype(F32) == grp, 1.0, 0.0)
        tr = lax.broadcasted_iota(jnp.int32, (TMX, TMX), 0)
        tc = lax.broadcasted_iota(jnp.int32, (TMX, TMX), 1)
        rank = dot(jnp.where(tc < tr, 1.0, 0.0).astype(BF), onehot.astype(BF))
        cnt = jnp.broadcast_to(jnp.sum(onehot, axis=0, keepdims=True), (8, LANES))
        lane8 = lax.broadcasted_iota(jnp.int32, (8, LANES), 1)
        off = jnp.zeros((8, LANES), F32)
        for k in range(1, ngrp):
            off = off + jnp.where(lane8 >= k, pltpu.roll(cnt, k, 1), 0.0)
        pos = jnp.sum(onehot * (off[0:1] + rank), axis=-1, keepdims=True)
        posb = jnp.broadcast_to(pos, (TMX, LANES))
        pos_scr[...] = posb
        pos_row = posb.T[0:1, :]
        perm = jnp.where(tr.astype(F32) == pos_row, 1.0, 0.0).astype(BF)
        xs_scr[...] = dot(perm, xt_ref[...]).astype(BF)
        g1, g2, g3 = _split3(gate)
        gs_scr[...] = (dot(perm, g3) + dot(perm, g2)) + dot(perm, g1)
        outs_scr[...] = jnp.zeros_like(outs_scr)

    blo = blk_ref[2 * (i * ngrp + g)]
    bhi = blk_ref[2 * (i * ngrp + g) + 1]
    for b in range(TMX // SBR):
        @pl.when(jnp.logical_and(b >= blo, b < bhi))
        def _(b=b):
            rows = pl.ds(b * SBR, SBR)
            xb = xs_scr[rows, :]
            gsb = gs_scr[rows, :]
            lane_c = lax.broadcasted_iota(jnp.int32, gsb.shape, 1)
            hids = []
            for j in range(EPS):
                h13 = dot(xb, w13_ref[j])
                h1, h3 = h13[:, :DE], h13[:, DE:]
                ge = jnp.sum(jnp.where(lane_c == g * EPS + j, gsb, 0.0), axis=-1, keepdims=True)
                hids.append(((h1 * jax.nn.sigmoid(h1)) * h3 * ge).astype(BF))
            outs_scr[rows, :] += dot(jnp.concatenate(hids, axis=1), w2_ref[...].reshape(EPS * DE, D))

    def result():
        row = jnp.where(i < ntile_p, 0, i - ntile_p + 1)
        tcol = lax.broadcasted_iota(jnp.int32, (TMX, TMX), 1).astype(F32)
        back = jnp.where(tcol == pos_scr[:, 0:1], 1.0, 0.0).astype(BF)
        out = y_ref[...] + _mod_vec(mod_ref, row, 5) * dot(back, outs_scr[...].astype(BF))
        if final:
            ms = jnp.mean(out * out, axis=-1, keepdims=True)
            out = out * lax.rsqrt(ms + NORM_EPS) * fin_ref[...]
        return out

    if final:
        @pl.when(jnp.logical_and(g == NE // EPS - 1, i < ntile_p))
        def _():
            op_ref[...] = result()

        @pl.when(jnp.logical_and(g == NE // EPS - 1, i >= ntile_p))
        def _():
            os_ref[...] = result()
    else:
        @pl.when(g == NE // EPS - 1)
        def _():
            o_ref[...] = result()


def _moe_experts(nch, xt, gate, w13, w2, y, mod, fin, final):
    tile = pl.BlockSpec((TMX, D), lambda i, e, n: (i, 0))
    full = lambda a: pl.BlockSpec(a.shape, lambda i, e, n: (0,) * a.ndim)
    ntile_p = NPR // TMX
    if final:
        out_specs = [pl.BlockSpec((TMX, D), lambda i, e, n: (jnp.minimum(i, ntile_p - 1), 0)),
                     pl.BlockSpec((TMX, D), lambda i, e, n: (jnp.maximum(i - ntile_p, 0), 0))]
        out_shape = [jax.ShapeDtypeStruct((NPR, D), F32), jax.ShapeDtypeStruct((NSR, D), F32)]
    else:
        out_specs = tile
        out_shape = jax.ShapeDtypeStruct((NTOK, D), F32)
    scratch = [pltpu.VMEM((TMX, D), BF), pltpu.VMEM((TMX, LANES), F32), pltpu.VMEM((TMX, D), F32),
               pltpu.VMEM((TMX, LANES), F32)]
    grid_spec = pltpu.PrefetchScalarGridSpec(
        num_scalar_prefetch=1,
        grid=(NTOK // TMX, NE // EPS),
        in_specs=[tile, pl.BlockSpec((TMX, LANES), lambda i, e, n: (i, 0)),
                  pl.BlockSpec((EPS, D, 2 * DE), lambda i, e, n: (e, 0, 0)),
                  pl.BlockSpec((EPS, DE, D), lambda i, e, n: (e, 0, 0)),
                  tile, full(mod), full(fin)],
        out_specs=out_specs,
        scratch_shapes=scratch)
    return pl.pallas_call(
        functools.partial(_experts_kernel, final),
        grid_spec=grid_spec,
        out_shape=out_shape,
        compiler_params=_cp(("arbitrary" if final else "parallel", "arbitrary")),
        name="moe_experts",
    )(nch, xt, gate, w13, w2, y, mod, fin)


def _qkv_kernel(y_ref, mod_ref, nrm_ref, w_ref, q_o, kp_o, vp_o, ks_o, vs_o, kc_o, vc_o):
    i = pl.program_id(0)
    row = _tile_mod_row(_qkv_tile(i))
    h = _modulate(y_ref[...], nrm_ref[...], _mod_vec(mod_ref, row, 0), _mod_vec(mod_ref, row, 1))
    hb = h.astype(BF)
    kx = jnp.dot(hb, w_ref[:, D:2 * D], preferred_element_type=F32)
    vx = jnp.dot(hb, w_ref[:, 2 * D:], preferred_element_type=F32)
    q_o[...] = jnp.dot(hb, w_ref[:, :D], preferred_element_type=F32)

    @pl.when(i < NT - NT_P)
    def _():
        ks_o[...] = kx
        vs_o[...] = vx

    @pl.when(i >= NT - NT_P)
    def _():
        kp_o[...] = kx
        vp_o[...] = vx
        for hh in range(NH):
            kc_o[pl.ds(hh, TM, stride=NH), :] = kx[:, hh * HD:(hh + 1) * HD]
            vc_o[pl.ds(hh, TM, stride=NH), :] = vx[:, hh * HD:(hh + 1) * HD]


def _qkv_tile(i):
    return jnp.where(i < NT - NT_P, NT_P + i, i - (NT - NT_P))


def _na_qkv(y, mod, nrm, w):
    nt_s = NT - NT_P
    tile = pl.BlockSpec((TM, D), lambda i: (_qkv_tile(i), 0))
    tile_p = pl.BlockSpec((TM, D), lambda i: (jnp.maximum(i - nt_s, 0), 0))
    tile_s = pl.BlockSpec((TM, D), lambda i: (jnp.minimum(i, nt_s - 1), 0))
    full = lambda a: pl.BlockSpec(a.shape, lambda i: (0,) * a.ndim)
    out = lambda n: jax.ShapeDtypeStruct((n, D), F32)
    cache = jax.ShapeDtypeStruct((NPR * NH, HD), F32)
    cache_tile = pl.BlockSpec((TM * NH, HD), lambda i: (jnp.maximum(i - nt_s, 0), 0))
    return pl.pallas_call(
        _qkv_kernel,
        grid=(NT,),
        in_specs=[tile, full(mod), full(nrm), full(w)],
        out_specs=[tile, tile_p, tile_p, tile_s, tile_s, cache_tile, cache_tile],
        out_shape=[out(NTOK), out(NPR), out(NPR), out(NSR), out(NSR), cache, cache],
        compiler_params=_cp(("arbitrary",)),
        name="na_qkv",
    )(y, mod, nrm, w)


def _softmax_rows(s):
    m = jnp.max(s, axis=-1, keepdims=True)
    e = jnp.exp(s - m)
    return e / jnp.sum(e, axis=-1, keepdims=True)


def _ctx_attn_kernel(q_ref, k_ref, v_ref, o_ref):
    scale = HD ** -0.5
    h0 = lax.broadcasted_iota(jnp.int32, (T_P, LANES), 1) < HD
    def head_chain(h, q, kb, vb):
        qm = jnp.where(h0 if h == 0 else jnp.logical_not(h0), q, 0.0)
        s = _bdot_nt(qm, kb)
        yield
        o = jnp.dot(_softmax_rows(s * scale).astype(BF), vb, preferred_element_type=F32)
        yield
        return o

    npair = D // LANES
    for p0 in range(0, npair, CTX_PAIRS):
        gens = []
        for p in range(p0, p0 + CTX_PAIRS):
            ls = pl.ds(p * LANES, LANES)
            q = q_ref[:, ls]
            kb = k_ref[:, ls].astype(BF)
            vb = v_ref[:, ls].astype(BF)
            gens += [head_chain(h, q, kb, vb) for h in range(2)]
        outs = _run_lockstep(gens)
        for j, p in enumerate(range(p0, p0 + CTX_PAIRS)):
            o_ref[:, pl.ds(p * LANES, LANES)] = jnp.where(h0, outs[2 * j], outs[2 * j + 1])


def _ctx_attn(q, k, v):
    blk = pl.BlockSpec((T_P, D), lambda b: (b, 0))
    return pl.pallas_call(
        _ctx_attn_kernel,
        grid=(NB_P,),
        in_specs=[blk] * 3,
        out_specs=blk,
        out_shape=jax.ShapeDtypeStruct((NPR, D), F32),
        compiler_params=_cp(("parallel",)),
        name="ctx_attn",
    )(q, k, v)


def _nbr_attn_kernel(q_ref, k_ref, v_ref, kc_ref, vc_ref, p_ref, o_ref, tz_ref):
    scale = HD ** -0.5
    rows = T_S // GRID_W
    nloc = WIN_R * GRID_W
    qc = lax.broadcasted_iota(jnp.int32, (GRID_W, LANES), 0)
    ln = lax.broadcasted_iota(jnp.int32, (GRID_W, LANES), 1)
    h0 = ln < HD
    kcb = kc_ref[...].astype(BF)
    vcb = vc_ref[...].astype(BF)
    kcol = ln & (GRID_W - 1)
    cstart = jnp.clip(qc - WIN_C // 2, 0, GRID_W - WIN_C)
    valid = jnp.logical_and(kcol >= cstart, kcol < cstart + WIN_C)
    for h in range(2):
        rolled = []
        for j in range(2 * WIN_R - 1):
            prow = jnp.broadcast_to(p_ref[h, j:j + 1, :], (GRID_W, LANES))
            rolled.append((pltpu.roll(prow, 0, 1, stride=1, stride_axis=0),
                           pltpu.roll(prow, GRID_W, 1, stride=1, stride_axis=0)))
        for j in range(2 * WIN_R - 2):
            tz_ref[h, j] = jnp.where(valid, jnp.where(h0, rolled[j][0], rolled[j + 1][1]), NEG)

    def head_chain(h, q, klb, vlb, j0):
        qm = jnp.where(h0 if h == 0 else jnp.logical_not(h0), q, 0.0).astype(BF)
        sl = _bdot_nt(qm, klb)
        sc = _bdot_nt(qm, kcb)
        yield
        bias = jnp.concatenate([tz_ref[h, j0 + 2 * m] for m in range(WIN_R // 2)], axis=1)
        sl = sl * scale + bias
        sc = sc * scale
        m = jnp.maximum(jnp.max(sl, axis=-1, keepdims=True), jnp.max(sc, axis=-1, keepdims=True))
        el = jnp.exp(sl - m)
        ec = jnp.exp(sc - m)
        inv = 1.0 / (jnp.sum(el, axis=-1, keepdims=True) + jnp.sum(ec, axis=-1, keepdims=True))
        o = (jnp.dot((el * inv).astype(BF), vlb, preferred_element_type=F32)
             + jnp.dot((ec * inv).astype(BF), vcb, preferred_element_type=F32))
        yield
        return o

    def body(it, carry):
        gens, slices = [], []
        for j in range(NBR_UNROLL):
            r = it * NBR_UNROLL + j
            start = jnp.clip(r - WIN_R // 2, 0, rows - WIN_R)
            qs = pl.ds(pl.multiple_of(r * GRID_W, GRID_W), GRID_W)
            ks = pl.ds(pl.multiple_of(start * GRID_W, GRID_W), nloc)
            q = q_ref[qs, :]
            klb = k_ref[ks, :].astype(BF)
            vlb = v_ref[ks, :].astype(BF)
            j0 = start - r + WIN_R - 1
            slices.append(qs)
            gens += [head_chain(h, q, klb, vlb, j0) for h in range(2)]
        outs = _run_lockstep(gens)
        for j, qs in enumerate(slices):
            o_ref[qs, :] = jnp.where(h0, outs[2 * j], outs[2 * j + 1])
        return carry

    lax.fori_loop(0, rows // NBR_UNROLL, body, 0)


def _nbr_attn(q, k, v, kc, vc, ptab):
    npair = D // LANES
    nrow = ptab.shape[1]
    qseq = pl.BlockSpec((T_S, LANES), lambda b, p: (NPR // T_S + b, p))
    seq = pl.BlockSpec((T_S, LANES), lambda b, p: (b, p))
    ctx = pl.BlockSpec((kc.shape[0] // NB_S, LANES), lambda b, p: (b, p))
    return pl.pallas_call(
        _nbr_attn_kernel,
        grid=(NB_S, npair),
        in_specs=[qseq, seq, seq, ctx, ctx,
                  pl.BlockSpec((2, nrow, LANES), lambda b, p: (p, 0, 0))],
        out_specs=pl.BlockSpec((T_S, LANES), lambda b, p: (b, p)),
        out_shape=jax.ShapeDtypeStruct((NSR, D), F32),
        scratch_shapes=[pltpu.VMEM((2, nrow - 1, GRID_W, LANES), F32)],
        compiler_params=_cp(("parallel", "parallel")),
        name="nbr_attn",
    )(q, k, v, kc, vc, ptab)


def _block_diag2(a, b):
    z = jnp.zeros_like(a)
    return jnp.concatenate([jnp.concatenate([a, z], axis=1), jnp.concatenate([z, b], axis=1)], axis=0)


def _nbr_bias_table(rpb):
    pad = jnp.zeros(rpb.shape[:2] + (LANES - 2 * WIN_C + 1,), F32)
    return jnp.concatenate([rpb[:, :, WIN_C - 1:], pad, rpb[:, :, :WIN_C - 1]], axis=-1)


def kernel(x_prompt, x_sample, c, state_rwkv, cache_na_k, cache_na_v, c_ctx, norm_mix, norm_ffn, ada_w, ada_b, rwkv_mu, rwkv_w_r, rwkv_w_k, rwkv_w_v, rwkv_w_o, rwkv_w0, rwkv_w1, rwkv_w2, rwkv_a0, rwkv_a1, rwkv_a2, rwkv_g1, rwkv_g2, rwkv_k_k, rwkv_k_a, rwkv_r_k, rwkv_lnx_g, rwkv_lnx_b, na_w_qkv, na_w_o, na_rpb, moe_w_grp, moe_b_grp, moe_w_exp, moe_b_exp, moe_w1, moe_w3, moe_w2, final_norm):
    x_p = x_prompt.reshape(NPR, D)
    x_s = x_sample.reshape(NSR, D)
    c8 = jnp.concatenate([c_ctx[None, :], c, jnp.zeros((8 - 1 - NB_S, D), F32)], axis=0)
    mod = _adaln(c8, ada_w, ada_b)
    ri = jnp.arange(2 * LANES)[:, None] // HD
    bo = (ri == ri.T).astype(BF)
    row = lambda a: a.reshape(1, D)

    w1c = jnp.concatenate([rwkv_w1[0, 0], rwkv_w1[0, 1]], axis=1).astype(BF)
    a1c = jnp.concatenate([rwkv_a1[0, 0], rwkv_a1[0, 1]], axis=1).astype(BF)
    w2bd = _block_diag2(rwkv_w2[0, 0], rwkv_w2[0, 1]).astype(BF)
    a2bd = _block_diag2(rwkv_a2[0, 0], rwkv_a2[0, 1]).astype(BF)
    r, k, v, g, kk, lw0, lw1, ag0, ag1 = _rwkv_front(
        x_p, x_s, mod[0], row(norm_mix[0]), rwkv_mu[0], rwkv_w_r[0].astype(BF), rwkv_w_k[0].astype(BF),
        rwkv_w_v[0].astype(BF), w1c, a1c, rwkv_g1[0].astype(BF), w2bd, a2bd, rwkv_g2[0].astype(BF),
        rwkv_w0[0], rwkv_a0[0], row(rwkv_k_k[0]), bo)
    arrs = (r, k, v, kk, lw0, lw1, ag0, ag1)
    ka, rk = row(rwkv_k_a[0]), row(rwkv_r_k[0])
    ys_p, bon_p, new_state = _wkv(arrs, ka, rk, T_P, NB_P, 0, lanes=D)
    s0 = state_rwkv[:, 0].reshape(NB_S, 2, NH // 2, 2, HD, HD)
    z = jnp.zeros_like(s0[:, :, :, 0])
    s0_bd = jnp.concatenate([jnp.concatenate([s0[:, :, :, 0], z], axis=-1),
                             jnp.concatenate([z, s0[:, :, :, 1]], axis=-1)], axis=-2)
    ys_s, bon_s = _wkv(arrs, ka, rk, T_S, NB_S, NPR // T_S, s0_bd=s0_bd)
    new_state_rwkv = new_state.reshape(NB_P, 1, 2, NH, HD, HD)
    def router(i):
        wr = jnp.concatenate([moe_w_exp[i], moe_w_grp[i],
                              jnp.zeros((D, LANES - NE - NE // EPG), F32)], axis=1)
        br = jnp.concatenate([moe_b_exp[i], moe_b_grp[i],
                              jnp.zeros((LANES - NE - NE // EPG,), F32)]).reshape(1, LANES)
        return (row(norm_ffn[i]), wr, br)

    def experts(y, xt, gate, i, final):
        w13 = jnp.concatenate([moe_w1[i], moe_w3[i]], axis=-1).astype(BF)
        grp = gate[:, GRP_LANE].astype(jnp.int32).reshape(NTOK // TMX, TMX)
        cnt = jnp.sum(grp[:, :, None] == jnp.arange(NE // EPG)[None, None, :], axis=1).astype(jnp.int32)
        off = jnp.cumsum(cnt, axis=1) - cnt
        blo = off // SBR
        bhi = jnp.where(cnt > 0, (off + cnt - 1) // SBR + 1, blo)
        blk = jnp.stack([blo, bhi], axis=-1).astype(jnp.int32).reshape(-1)
        return _moe_experts(blk, xt, gate, w13, moe_w2[i].astype(BF), y, mod[i], row(final_norm), final)

    y, xt, gate = _proj_residual((x_p, x_s), ys_p, ys_s, mod[0], rwkv_w_o[0].astype(BF), router(0),
                                 gn_args=(bon_p, bon_s, g, row(rwkv_lnx_g[0]), row(rwkv_lnx_b[0]), bo),
                                 name="rwkv_out")
    y = experts(y, xt, gate, 0, False)

    q, k_p, v_p, k_s, v_s, kcache, vcache = _na_qkv(y, mod[1], row(norm_mix[1]), na_w_qkv[0].astype(BF))
    o_p = _ctx_attn(q, k_p, v_p)
    kc = cache_na_k[:, 0].reshape(NB_S * cache_na_k.shape[2], D)
    vc = cache_na_v[:, 0].reshape(NB_S * cache_na_v.shape[2], D)
    o_s = _nbr_attn(q, k_s, v_s, kc, vc, _nbr_bias_table(na_rpb[0]))
    y, xt, gate = _proj_residual(y, o_p, o_s, mod[1], na_w_o[0].astype(BF), router(1), name="attn_out")
    y_p, y_s = experts(y, xt, gate, 1, True)

    y_prompt = y_p.reshape(NB_P, T_P, D)
    y_sample = y_s.reshape(NB_S, T_S, D)
    new_k = kcache.reshape(NB_P, 1, T_P, NH, HD)
    new_v = vcache.reshape(NB_P, 1, T_P, NH, HD)
    return (y_prompt, y_sample, new_state_rwkv, new_k, new_v)
```

```python
import functools
import math

import jax
import jax.numpy as jnp
from jax import lax
from jax.experimental import pallas as pl
from jax.experimental.pallas import tpu as pltpu

F32 = jnp.float32
BF = jnp.bfloat16

D = 1024
NH = 16
HD = 64
NB_P, T_P = 32, 256
NB_S, T_S = 2, 1024
NPR = NB_P * T_P
NSR = NB_S * T_S
NTOK = NPR + NSR
TM = 256
NT = NTOK // TM
NT_P = NPR // TM
TPS = T_S // TM
TMX = 1024
NMOD = 6
NE = 16
EPG = 4
DE = 256
EPS = EPG
SBR = 256
GRP_LANE = 127
CH = 64
SB = 32
LANES = 128
GRID_W = 64
WIN_R, WIN_C = 8, 16
NBR_UNROLL = 4
CTX_PAIRS = 2
NORM_EPS = 1e-6
GN_EPS = 64e-5
NEG = -1e30
VMEM_LIMIT = 56 * 1024 * 1024


def _cp(sem):
    return pltpu.CompilerParams(dimension_semantics=sem, vmem_limit_bytes=VMEM_LIMIT)


def _bdot(a, b):
    return jnp.dot(a.astype(BF), b.astype(BF), preferred_element_type=F32)


def _bdot_nt(a, b):
    return lax.dot_general(a.astype(BF), b.astype(BF), (((1,), (1,)), ((), ())),
                           preferred_element_type=F32)


def _split2(x):
    hi = x.astype(BF)
    lo = (x - hi.astype(F32)).astype(BF)
    return hi, lo


def _split3(x):
    hi = x.astype(BF)
    r1 = x - hi.astype(F32)
    mid = r1.astype(BF)
    lo = (r1 - mid.astype(F32)).astype(BF)
    return hi, mid, lo


def _seg_sum(x, bo):
    outs = []
    for c in range(x.shape[1] // 256):
        hi, lo = _split2(x[:, c * 256:(c + 1) * 256])
        outs.append(jnp.dot(hi, bo, preferred_element_type=F32)
                    + jnp.dot(lo, bo, preferred_element_type=F32))
    return outs[0] if len(outs) == 1 else jnp.concatenate(outs, axis=1)


def _tile_mod_row(i):
    return jnp.where(i < NT_P, 0, 1 + (i - NT_P) // TPS)


def _modulate(x, g, sh, sc):
    ms = jnp.mean(x * x, axis=-1, keepdims=True)
    return x * lax.rsqrt(ms + NORM_EPS) * g * (1.0 + sc) + sh


def _mod_vec(mod_ref, row, k):
    return mod_ref[pl.ds(row, 1), pl.ds(k * D, D)]


def _adaln_kernel(c_ref, w_ref, b_ref, o_ref):
    c = c_ref[...]
    s = c * jax.nn.sigmoid(c)
    o_ref[...] = _bdot(s, w_ref[...]) + b_ref[...]


def _adaln(c8, ada_w, ada_b):
    nl = ada_w.shape[0]
    tn = 1536
    return pl.pallas_call(
        _adaln_kernel,
        grid=(nl, NMOD * D // tn),
        in_specs=[pl.BlockSpec((8, D), lambda l, j: (0, 0)),
                  pl.BlockSpec((None, D, tn), lambda l, j: (l, 0, j)),
                  pl.BlockSpec((None, 1, tn), lambda l, j: (l, 0, j))],
        out_specs=pl.BlockSpec((None, 8, tn), lambda l, j: (l, 0, j)),
        out_shape=jax.ShapeDtypeStruct((nl, 8, NMOD * D), F32),
        compiler_params=_cp(("parallel", "parallel")),
        name="adaln",
    )(c8, ada_w, ada_b.reshape(nl, 1, NMOD * D))


def _front_kernel(xp_ref, xs_ref, yp_ref, yn_ref, mod_ref, nrm_ref, mu_ref, wr_ref, wk_ref, wv_ref,
                  w1_ref, a1_ref, g1_ref, w2_ref, a2_ref, g2_ref, w0_ref, a0_ref, kkw_ref, bo_ref,
                  r_o, k_o, v_o, g_o, kk_o, lw0_o, lw1_o, ag0_o, ag1_o):
    i = pl.program_id(0)
    row = _tile_mod_row(i)
    q = (i - NT_P) % TPS
    first = jnp.logical_or(i < NT_P, q == 0)
    last = jnp.logical_or(i < NT_P, q == TPS - 1)
    sh = _mod_vec(mod_ref, row, 0)
    sc = _mod_vec(mod_ref, row, 1)
    g = nrm_ref[...]
    h = _modulate(jnp.where(i < NT_P, xp_ref[...], xs_ref[...]), g, sh, sc)
    hp = _modulate(yp_ref[...], g, sh, sc)[7:8]
    hn = _modulate(yn_ref[...], g, sh, sc)[0:1]
    hp = jnp.where(first, 0.0, hp)
    hn = jnp.where(last, 0.0, hn)
    rid = lax.broadcasted_iota(jnp.int32, (TM, D), 0)
    prev = jnp.where(rid == 0, hp, pltpu.roll(h, 1, 0))
    nxt = jnp.where(rid == TM - 1, hn, pltpu.roll(h, TM - 1, 0))
    dx = 0.5 * (prev + nxt) - h

    def mix(n):
        return (h + dx * mu_ref[n:n + 1, :]).astype(BF)

    r_o[...] = jnp.dot(mix(0), wr_ref[...], preferred_element_type=F32)
    xw = mix(1)
    lora_w = _bdot(jnp.tanh(jnp.dot(xw, w1_ref[...], preferred_element_type=F32)), w2_ref[...])
    hc = 0.5 * math.exp(-0.5)
    lw0_o[...] = -hc * jnp.tanh(0.5 * (w0_ref[0:1, :] + lora_w[:, :D])) - hc
    lw1_o[...] = -hc * jnp.tanh(0.5 * (w0_ref[1:2, :] + lora_w[:, D:])) - hc
    k_raw = jnp.dot(mix(2), wk_ref[...], preferred_element_type=F32)
    k_o[...] = k_raw
    kq = k_raw * kkw_ref[...]
    ss = _seg_sum(kq * kq, bo_ref[...])
    kk_o[...] = kq / jnp.maximum(jnp.sqrt(ss), 1e-12)
    v_o[...] = jnp.dot(mix(3), wv_ref[...], preferred_element_type=F32)
    xa = mix(4)
    lora_a = _bdot(jnp.dot(xa, a1_ref[...], preferred_element_type=F32), a2_ref[...])
    ag0_o[...] = 0.5 * jnp.tanh(0.5 * (a0_ref[0:1, :] + lora_a[:, :D])) + 0.5
    ag1_o[...] = 0.5 * jnp.tanh(0.5 * (a0_ref[1:2, :] + lora_a[:, D:])) + 0.5
    xg = mix(5)
    g_o[...] = _bdot(jax.nn.sigmoid(jnp.dot(xg, g1_ref[...], preferred_element_type=F32)), g2_ref[...])


def _rwkv_front(x_p, x_s, mod, nrm, mu, wr, wk, wv, w1c, a1c, g1, w2bd, a2bd, g2, w0, a0, kkw, bo):
    tile = pl.BlockSpec((TM, D), lambda i: (i, 0))
    nblk8 = NSR // 8
    bpt = TM // 8
    full = lambda a: pl.BlockSpec(a.shape, lambda i: (0,) * a.ndim)
    ins = [x_p, x_s, x_s, x_s, mod, nrm, mu, wr, wk, wv, w1c, a1c, g1, w2bd, a2bd, g2, w0, a0, kkw, bo]
    in_specs = [pl.BlockSpec((TM, D), lambda i: (jnp.minimum(i, NT_P - 1), 0)),
                pl.BlockSpec((TM, D), lambda i: (jnp.maximum(i - NT_P, 0), 0)),
                pl.BlockSpec((8, D), lambda i: (jnp.clip((i - NT_P) * bpt - 1, 0, nblk8 - 1), 0)),
                pl.BlockSpec((8, D), lambda i: (jnp.clip((i - NT_P + 1) * bpt, 0, nblk8 - 1), 0))]
    in_specs += [full(a) for a in ins[4:]]
    out = jax.ShapeDtypeStruct((NTOK, D), F32)
    return pl.pallas_call(
        _front_kernel,
        grid=(NT,),
        in_specs=in_specs,
        out_specs=[tile] * 9,
        out_shape=[out] * 9,
        compiler_params=_cp(("parallel",)),
        name="rwkv_front",
    )(*ins)


def _wkv_chunk(d, S, r, k, v, kk, lw, ag, ka, rk, cst):
    trow, strict, incl, h0, bd, bo2, same_blk, bd_sb, eye_c, lane_blk = cst
    dot = lambda a, b: jnp.dot(a, b, preferred_element_type=F32)
    nblk = CH // SB
    h0s = lax.broadcasted_iota(jnp.int32, (SB, LANES), 1) < HD

    def stack_heads(x, swap=False):
        zero = jnp.zeros_like(x)
        parts = [jnp.where(h0, x, zero), jnp.where(h0, zero, x)]
        return jnp.concatenate(parts[::-1] if swap else parts, axis=0)

    def dot_split(a, bm):
        return dot(a.astype(BF), bm.astype(BF))

    def expand(mc):
        return jnp.where(bd_sb, jnp.concatenate([mc] * (LANES // SB), axis=0), 0.0)

    b = kk * ag
    kd = k * (1.0 + (ag - 1.0) * ka)
    cs = lw
    for sh in [1 << n for n in range(int(math.log2(CH)))]:
        if d == 0:
            cs = cs + jnp.where(trow >= sh, pltpu.roll(cs, sh, 0), 0.0)
        else:
            cs = cs + jnp.where(trow < CH - sh, pltpu.roll(cs, CH - sh, 0), 0.0)
    qh, ql = _split2(r * kd * rk)
    qs = dot(jnp.concatenate([qh, ql], axis=0), bo2)
    bonus = (qs[:CH] + qs[CH:]) * v
    yield
    tot = cs[CH - 1:CH, :] if d == 0 else cs[0:1, :]
    p_inv = jnp.exp(-cs)
    p_end = jnp.exp(tot - cs)
    at = -kk * jnp.exp(cs - lw)
    rt = r * jnp.exp(cs)
    bt = (b * p_inv).astype(BF)
    kt = (kd * p_inv).astype(BF)
    ar = jnp.concatenate([at, rt], axis=0)
    h02 = jnp.concatenate([h0, h0], axis=0)
    g0 = _bdot_nt(jnp.where(h02, ar, 0.0), jnp.concatenate([bt, kt], axis=0))
    g1 = _bdot_nt(jnp.where(h02, 0.0, ar), jnp.concatenate([kt, bt], axis=0))
    w0y0 = _bdot_nt(ar, S)
    yield
    w0, y0 = w0y0[:CH], w0y0[CH:]
    lab = jnp.where(strict[d], jnp.where(h0, g0[:CH], g1[:CH]), 0.0)
    lak = jnp.where(strict[d], jnp.where(h0, g1[:CH], g0[:CH]), 0.0)
    mrb = jnp.where(incl[d], jnp.where(h0, g0[CH:], g1[CH:]), 0.0)
    mrk = jnp.where(incl[d], jnp.where(h0, g1[CH:], g0[CH:]), 0.0)
    vstk_sw = stack_heads(v, swap=True).astype(BF)
    w = w0 + dot(lak.astype(BF), vstk_sw)
    ldiag = jnp.where(same_blk, lab, 0.0)
    loff = jnp.where(same_blk, 0.0, lab)
    mc = ldiag[0:SB]
    for i in range(1, nblk):
        mc = mc + ldiag[i * SB:(i + 1) * SB]
    xc = eye_c + mc
    mc = dot_split(mc, expand(mc))
    yield
    nsq = int(math.log2(SB)) - 1
    for it in range(nsq):
        last = it + 1 == nsq
        res = dot_split(xc if last else jnp.concatenate([xc, mc], axis=0), expand(mc))
        xc = xc + res[:SB]
        if not last:
            mc = res[SB:]
        yield
    ublk = [None] * nblk
    order = list(range(nblk)) if d == 0 else list(range(nblk - 1, -1, -1))
    for n, i in enumerate(order):
        wi = w[i * SB:(i + 1) * SB]
        if n > 0:
            ucur = jnp.concatenate([jnp.zeros((SB, LANES), F32) if ub is None else ub
                                    for ub in ublk], axis=0)
            wi = wi + dot(loff[i * SB:(i + 1) * SB].astype(BF), stack_heads(ucur).astype(BF))
            yield
        m0 = jnp.where(h0s, wi, 0.0).astype(BF)
        m1 = jnp.where(h0s, 0.0, wi).astype(BF)
        rhs = jnp.concatenate([m0] * nblk + [m1] * nblk, axis=0)
        ublk[i] = dot(jnp.where(lane_blk[i], xc, 0.0).astype(BF), rhs)
        yield
    u = jnp.concatenate(ublk, axis=0)
    y = dot(jnp.concatenate([mrb, mrk], axis=1).astype(BF),
            jnp.concatenate([stack_heads(u).astype(BF), vstk_sw], axis=0))
    uvt = jnp.concatenate([u, v], axis=0).T
    ds = _bdot(uvt, jnp.concatenate([b * p_end, kd * p_end], axis=0))
    yield
    s_new = S * jnp.exp(tot) + jnp.where(bd, ds, 0.0)
    return s_new, y0 + y, bonus


def _run_lockstep(gens):
    results = [None] * len(gens)
    pending = list(range(len(gens)))
    while pending:
        for i in list(pending):
            try:
                next(gens[i])
            except StopIteration as stop:
                results[i] = stop.value
                pending.remove(i)
    return results


def _wkv_consts():
    t2 = lax.broadcasted_iota(jnp.int32, (CH, LANES), 0)
    l2 = lax.broadcasted_iota(jnp.int32, (CH, LANES), 1)
    s2 = l2 & (CH - 1)
    strict = [s2 < t2, s2 > t2]
    incl = [s2 <= t2, s2 >= t2]
    h0 = l2 < HD
    same_blk = (t2 // SB) == (s2 // SB)
    ri = lax.broadcasted_iota(jnp.int32, (LANES, LANES), 0)
    ci = lax.broadcasted_iota(jnp.int32, (LANES, LANES), 1)
    bd = (ri // HD) == (ci // HD)
    bo2 = jnp.where(bd, 1.0, 0.0).astype(BF)
    bd_sb = (ri // SB) == (ci // SB)
    t3 = lax.broadcasted_iota(jnp.int32, (SB, LANES), 0)
    l3 = lax.broadcasted_iota(jnp.int32, (SB, LANES), 1)
    eye_c = jnp.where((l3 & (SB - 1)) == t3, 1.0, 0.0)
    lane_blk = [((l3 & (CH - 1)) // SB) == i for i in range(CH // SB)]
    return t2, strict, incl, h0, bd, bo2, same_blk, bd_sb, eye_c, lane_blk


def _wkv_kernel(T, has_s0, *refs):
    (r_ref, k_ref, v_ref, kk_ref, lw0_ref, lw1_ref, ag0_ref, ag1_ref, ka_ref, rk_ref), rest = refs[:10], refs[10:]
    if has_s0:
        s0_ref, y_ref, bon_ref, s_scr = rest
    else:
        y_ref, bon_ref, st_ref, s_scr = rest
    nch = T // CH
    npair = y_ref.shape[1] // LANES
    cst = _wkv_consts()
    if has_s0:
        s_scr[...] = s0_ref[...]
    else:
        s_scr[...] = jnp.zeros_like(s_scr)
    y_ref[...] = jnp.zeros_like(y_ref)
    bon_ref[...] = jnp.zeros_like(bon_ref)
    lw_refs = (lw0_ref, lw1_ref)
    ag_refs = (ag0_ref, ag1_ref)

    def body(c, carry):
        chains = [(p, d) for p in range(npair) for d in range(2)]
        sl = {}
        for p, d in chains:
            r0 = pl.multiple_of((c if d == 0 else nch - 1 - c) * CH, CH)
            sl[p, d] = (pl.ds(r0, CH), pl.ds(p * LANES, LANES))
        args = {}
        for p, d in chains:
            rs, ls = sl[p, d]
            args[p, d] = (s_scr[d, p], r_ref[rs, ls], k_ref[rs, ls], v_ref[rs, ls], kk_ref[rs, ls],
                          lw_refs[d][rs, ls], ag_refs[d][rs, ls], ka_ref[:, ls], rk_ref[:, ls],
                          y_ref[rs, ls], bon_ref[rs, ls])
        outs = _run_lockstep([_wkv_chunk(d, *args[p, d][:9], cst) for p, d in chains])
        for (p, d), (s_new, y, bon) in zip(chains, outs):
            rs, ls = sl[p, d]
            s_scr[d, p] = s_new
            y_ref[rs, ls] = args[p, d][9] + y
            bon_ref[rs, ls] = args[p, d][10] + bon
        return carry

    lax.fori_loop(0, nch, body, 0)
    if not has_s0:
        for d in range(2):
            for p in range(npair):
                s = s_scr[d, p]
                st_ref[d, 2 * p] = s[:HD, :HD]
                st_ref[d, 2 * p + 1] = s[HD:, HD:]


def _wkv(arrs, ka, rk, T, nb, row_blk0, s0_bd=None, lanes=512):
    ng = D // lanes
    npair = lanes // LANES
    seq = pl.BlockSpec((T, lanes), lambda b, g: (row_blk0 + b, g))
    vec = pl.BlockSpec((1, lanes), lambda b, g: (0, g))
    st = pl.BlockSpec((None, 2, npair, LANES, LANES), lambda b, g: (b, 0, g, 0, 0))
    out_seq = pl.BlockSpec((T, lanes), lambda b, g: (b, g))
    in_specs = [seq] * 8 + [vec, vec]
    ins = list(arrs) + [ka, rk]
    out_specs = [out_seq, out_seq]
    out_shape = [jax.ShapeDtypeStruct((nb * T, D), F32)] * 2
    if s0_bd is not None:
        in_specs.append(st)
        ins.append(s0_bd)
    else:
        out_specs.append(pl.BlockSpec((None, 2, 2 * npair, HD, HD), lambda b, g: (b, 0, g, 0, 0)))
        out_shape.append(jax.ShapeDtypeStruct((nb, 2, NH, HD, HD), F32))
    return pl.pallas_call(
        functools.partial(_wkv_kernel, T, s0_bd is not None),
        grid=(nb, ng),
        in_specs=in_specs,
        out_specs=out_specs,
        out_shape=out_shape,
        scratch_shapes=[pltpu.VMEM((2, npair, LANES, LANES), F32)],
        compiler_params=_cp(("parallel", "parallel")),
        name="wkv_T%d" % T,
    )(*ins)


def _proj_kernel(gn, *refs):
    i = pl.program_id(0)
    is_p = i < NT_P
    if gn:
        (yp_ref, ys_ref, ap_ref, as_ref, bp_ref, bs_ref, g_ref, lng_ref, lnb_ref, bo_ref,
         mod_ref, w_ref, nrm_ref, wr_ref, br_ref, o_ref, xt_o, gate_o) = refs
        y = jnp.where(is_p, yp_ref[...], ys_ref[...])
    else:
        y_ref, ap_ref, as_ref, mod_ref, w_ref, nrm_ref, wr_ref, br_ref, o_ref, xt_o, gate_o = refs
        y = y_ref[...]
    a = jnp.where(is_p, ap_ref[...], as_ref[...])
    if gn:
        bo = bo_ref[...]
        mean = _seg_sum(a, bo) * (1.0 / HD)
        cen = a - mean
        var = _seg_sum(cen * cen, bo) * (1.0 / HD)
        yn = cen * lax.rsqrt(var + GN_EPS) * lng_ref[...] + lnb_ref[...]
        a = (yn + jnp.where(is_p, bp_ref[...], bs_ref[...])) * g_ref[...]
    gt = _mod_vec(mod_ref, _tile_mod_row(i), 2)
    y_new = y + gt * _bdot(a, w_ref[...])
    o_ref[...] = y_new
    row = _tile_mod_row(i)
    xt = _modulate(y_new, nrm_ref[...], _mod_vec(mod_ref, row, 3), _mod_vec(mod_ref, row, 4))
    xt_o[...] = xt.astype(BF)
    gate_o[...] = _route(xt, wr_ref[...], br_ref[...])


def _proj_residual(y, a_p, a_s, mod, w, router, gn_args=None, name="proj"):
    tile = pl.BlockSpec((TM, D), lambda i: (i, 0))
    tile_p = pl.BlockSpec((TM, D), lambda i: (jnp.minimum(i, NT_P - 1), 0))
    tile_s = pl.BlockSpec((TM, D), lambda i: (jnp.maximum(i - NT_P, 0), 0))
    full = lambda a: pl.BlockSpec(a.shape, lambda i: (0,) * a.ndim)
    if gn_args is not None:
        ins = [y[0], y[1], a_p, a_s]
        in_specs = [tile_p, tile_s, tile_p, tile_s]
    else:
        ins = [y, a_p, a_s]
        in_specs = [tile, tile_p, tile_s]
    if gn_args is not None:
        b_p, b_s, g, lng, lnb, bo = gn_args
        ins += [b_p, b_s, g, lng, lnb, bo]
        in_specs += [tile_p, tile_s, tile, full(lng), full(lnb), full(bo)]
    ins += [mod, w] + list(router)
    in_specs += [full(mod), full(w)] + [full(a) for a in router]
    return pl.pallas_call(
        functools.partial(_proj_kernel, gn_args is not None),
        grid=(NT,),
        in_specs=in_specs,
        out_specs=[tile, tile, pl.BlockSpec((TM, LANES), lambda i: (i, 0))],
        out_shape=[jax.ShapeDtypeStruct((NTOK, D), F32), jax.ShapeDtypeStruct((NTOK, D), BF),
                   jax.ShapeDtypeStruct((NTOK, LANES), F32)],
        compiler_params=_cp(("parallel",)),
        name=name,
    )(*ins)


def _route(xt, wr, br):
    x1, x2 = _split2(xt)
    w1, w2 = _split2(wr)
    dot = lambda a, b: jnp.dot(a, b, preferred_element_type=F32)
    logits = (dot(x2, w1) + dot(x1, w2)) + dot(x1, w1)
    logits = logits + br
    lane = lax.broadcasted_iota(jnp.int32, logits.shape, 1).astype(F32)
    isg = jnp.logical_and(lane >= NE, lane < NE + NE // EPG)
    mg = jnp.max(jnp.where(isg, logits, NEG), axis=-1, keepdims=True)
    eg = jnp.where(isg, jnp.exp(jnp.minimum(logits - mg, 0.0)), 0.0)
    p_sel = 1.0 / jnp.sum(eg, axis=-1, keepdims=True)
    gidx = jnp.min(jnp.where(jnp.logical_and(isg, logits == mg), lane, 1e3), axis=-1, keepdims=True) - NE
    lo_l = gidx * EPG
    ing = jnp.logical_and(lane >= lo_l, lane < lo_l + EPG)
    me = jnp.max(jnp.where(ing, logits, NEG), axis=-1, keepdims=True)
    ee = jnp.where(ing, jnp.exp(jnp.minimum(logits - me, 0.0)), 0.0)
    se = jnp.sum(ee, axis=-1, keepdims=True)
    pe = ee / se
    i1 = jnp.min(jnp.where(jnp.logical_and(ing, logits == me), lane, 1e3), axis=-1, keepdims=True)
    v1 = 1.0 / se
    rest = jnp.logical_and(ing, lane != i1)
    v2 = jnp.max(jnp.where(rest, pe, -1.0), axis=-1, keepdims=True)
    i2 = jnp.min(jnp.where(jnp.logical_and(rest, pe == v2), lane, 1e3), axis=-1, keepdims=True)
    den = v1 + v2
    return (jnp.where(lane == i1, p_sel * v1 / den, 0.0)
            + jnp.where(lane == i2, p_sel * v2 / den, 0.0)
            + jnp.where(lane == float(GRP_LANE), gidx, 0.0))


def _experts_kernel(final, blk_ref, xt_ref, gate_ref, w13_ref, w2_ref, y_ref, mod_ref, fin_ref, *rest):
    i = pl.program_id(0)
    g = pl.program_id(1)
    ngrp = NE // EPS
    if final:
        op_ref, os_ref, xs_scr, gs_scr, outs_scr, pos_scr = rest
    else:
        o_ref, xs_scr, gs_scr, outs_scr, pos_scr = rest
    ntile_p = NPR // TMX
    dot = lambda a, b: jnp.dot(a, b, preferred_element_type=F32)

    @pl.when(g == 0)
    def _():
        gate = gate_ref[...]
        lane = lax.broadcasted_iota(jnp.int32, gate.shape, 1)
        grp = jnp.sum(jnp.where(lane == GRP_LANE, gate, 0.0), axis=-1, keepdims=True)
        onehot = jnp.where(lane.astype(F32) == grp, 1.0, 0.0)
        tr = lax.broadcasted_iota(jnp.int32, (TMX, TMX), 0)
        br = lax.broadcasted_iota(jnp.int32, (LANES, LANES), 0)
        bc = lax.broadcasted_iota(jnp.int32, (LANES, LANES), 1)
        ltri = jnp.where(bc < br, 1.0, 0.0).astype(BF)
        carry = jnp.zeros((1, LANES), F32)
        ranks = []
        for blk in range(TMX // LANES):
            oh = onehot[blk * LANES:(blk + 1) * LANES]
            ranks.append(dot(ltri, oh.astype(BF)) + carry)
            carry = carry + jnp.sum(oh, axis=0, keepdims=True)
        rank = jnp.concatenate(ranks, axis=0)
        cnt = jnp.broadcast_to(carry, (8, LANES))
        lane8 = lax.broadcasted_iota(jnp.int32, (8, LANES), 1)
        off = jnp.zeros((8, LANES), F32)
        for k in range(1, ngrp):
            off = off + jnp.where(lane8 >= k, pltpu.roll(cnt, k, 1), 0.0)
        pos = jnp.sum(onehot * (off[0:1] + rank), axis=-1, keepdims=True)
        posb = jnp.broadcast_to(pos, (TMX, LANES))
        pos_scr[...] = posb
        pos_row = posb.T[0:1, :]
        perm = jnp.where(tr.astype(F32) == pos_row, 1.0, 0.0).astype(BF)
        xs_scr[...] = dot(perm, xt_ref[...]).astype(BF)
        g1, g2 = _split2(gate)
        gsel = dot(perm, jnp.concatenate([g1, g2], axis=1))
        gs_scr[...] = gsel[:, :LANES] + gsel[:, LANES:]
        outs_scr[...] = jnp.zeros_like(outs_scr)

    blo = blk_ref[2 * (i * ngrp + g)]
    bhi = blk_ref[2 * (i * ngrp + g) + 1]
    for b in range(TMX // SBR):
        @pl.when(jnp.logical_and(b >= blo, b < bhi))
        def _(b=b):
            rows = pl.ds(b * SBR, SBR)
            xb = xs_scr[rows, :]
            gsb = gs_scr[rows, :]
            lane_c = lax.broadcasted_iota(jnp.int32, gsb.shape, 1)
            hids = []
            for j in range(EPS):
                h13 = dot(xb, w13_ref[j])
                h1, h3 = h13[:, :DE], h13[:, DE:]
                ge = jnp.sum(jnp.where(lane_c == g * EPS + j, gsb, 0.0), axis=-1, keepdims=True)
                hids.append(((h1 * jax.nn.sigmoid(h1)) * h3 * ge).astype(BF))
            outs_scr[rows, :] += dot(jnp.concatenate(hids, axis=1), w2_ref[...].reshape(EPS * DE, D))

    def result():
        row = jnp.where(i < ntile_p, 0, i - ntile_p + 1)
        tcol = lax.broadcasted_iota(jnp.int32, (TMX, TMX), 1).astype(F32)
        back = jnp.where(tcol == pos_scr[:, 0:1], 1.0, 0.0).astype(BF)
        out = y_ref[...] + _mod_vec(mod_ref, row, 5) * dot(back, outs_scr[...].astype(BF))
        if final:
            ms = jnp.mean(out * out, axis=-1, keepdims=True)
            out = out * lax.rsqrt(ms + NORM_EPS) * fin_ref[...]
        return out

    if final:
        @pl.when(jnp.logical_and(g == NE // EPS - 1, i < ntile_p))
        def _():
            op_ref[...] = result()

        @pl.when(jnp.logical_and(g == NE // EPS - 1, i >= ntile_p))
        def _():
            os_ref[...] = result()
    else:
        @pl.when(g == NE // EPS - 1)
        def _():
            o_ref[...] = result()


def _moe_experts(nch, xt, gate, w13, w2, y, mod, fin, final):
    tile = pl.BlockSpec((TMX, D), lambda i, e, n: (i, 0))
    full = lambda a: pl.BlockSpec(a.shape, lambda i, e, n: (0,) * a.ndim)
    ntile_p = NPR // TMX
    if final:
        out_specs = [pl.BlockSpec((TMX, D), lambda i, e, n: (jnp.minimum(i, ntile_p - 1), 0)),
                     pl.BlockSpec((TMX, D), lambda i, e, n: (jnp.maximum(i - ntile_p, 0), 0))]
        out_shape = [jax.ShapeDtypeStruct((NPR, D), F32), jax.ShapeDtypeStruct((NSR, D), F32)]
    else:
        out_specs = tile
        out_shape = jax.ShapeDtypeStruct((NTOK, D), F32)
    scratch = [pltpu.VMEM((TMX, D), BF), pltpu.VMEM((TMX, LANES), F32), pltpu.VMEM((TMX, D), F32),
               pltpu.VMEM((TMX, LANES), F32)]
    grid_spec = pltpu.PrefetchScalarGridSpec(
        num_scalar_prefetch=1,
        grid=(NTOK // TMX, NE // EPS),
        in_specs=[tile, pl.BlockSpec((TMX, LANES), lambda i, e, n: (i, 0)),
                  pl.BlockSpec((EPS, D, 2 * DE), lambda i, e, n: (e, 0, 0)),
                  pl.BlockSpec((EPS, DE, D), lambda i, e, n: (e, 0, 0)),
                  tile, full(mod), full(fin)],
        out_specs=out_specs,
        scratch_shapes=scratch)
    return pl.pallas_call(
        functools.partial(_experts_kernel, final),
        grid_spec=grid_spec,
        out_shape=out_shape,
        compiler_params=_cp(("arbitrary" if final else "parallel", "arbitrary")),
        name="moe_experts",
    )(nch, xt, gate, w13, w2, y, mod, fin)


def _qkv_kernel(y_ref, mod_ref, nrm_ref, w_ref, q_o, kp_o, vp_o, ks_o, vs_o, kc_o, vc_o):
    i = pl.program_id(0)
    row = _tile_mod_row(_qkv_tile(i))
    h = _modulate(y_ref[...], nrm_ref[...], _mod_vec(mod_ref, row, 0), _mod_vec(mod_ref, row, 1))
    hb = h.astype(BF)
    kx = jnp.dot(hb, w_ref[:, D:2 * D], preferred_element_type=F32)
    vx = jnp.dot(hb, w_ref[:, 2 * D:], preferred_element_type=F32)
    q_o[...] = jnp.dot(hb, w_ref[:, :D], preferred_element_type=F32)

    @pl.when(i < NT - NT_P)
    def _():
        ks_o[...] = kx
        vs_o[...] = vx

    @pl.when(i >= NT - NT_P)
    def _():
        kp_o[...] = kx
        vp_o[...] = vx
        for hh in range(NH):
            kc_o[pl.ds(hh, TM, stride=NH), :] = kx[:, hh * HD:(hh + 1) * HD]
            vc_o[pl.ds(hh, TM, stride=NH), :] = vx[:, hh * HD:(hh + 1) * HD]


def _qkv_tile(i):
    return jnp.where(i < NT - NT_P, NT_P + i, i - (NT - NT_P))


def _na_qkv(y, mod, nrm, w):
    nt_s = NT - NT_P
    tile = pl.BlockSpec((TM, D), lambda i: (_qkv_tile(i), 0))
    tile_p = pl.BlockSpec((TM, D), lambda i: (jnp.maximum(i - nt_s, 0), 0))
    tile_s = pl.BlockSpec((TM, D), lambda i: (jnp.minimum(i, nt_s - 1), 0))
    full = lambda a: pl.BlockSpec(a.shape, lambda i: (0,) * a.ndim)
    out = lambda n: jax.ShapeDtypeStruct((n, D), F32)
    cache = jax.ShapeDtypeStruct((NPR * NH, HD), F32)
    cache_tile = pl.BlockSpec((TM * NH, HD), lambda i: (jnp.maximum(i - nt_s, 0), 0))
    return pl.pallas_call(
        _qkv_kernel,
        grid=(NT,),
        in_specs=[tile, full(mod), full(nrm), full(w)],
        out_specs=[tile, tile_p, tile_p, tile_s, tile_s, cache_tile, cache_tile],
        out_shape=[out(NTOK), out(NPR), out(NPR), out(NSR), out(NSR), cache, cache],
        compiler_params=_cp(("arbitrary",)),
        name="na_qkv",
    )(y, mod, nrm, w)


def _softmax_rows(s):
    m = jnp.max(s, axis=-1, keepdims=True)
    e = jnp.exp(s - m)
    return e / jnp.sum(e, axis=-1, keepdims=True)


def _ctx_attn_kernel(q_ref, k_ref, v_ref, o_ref):
    scale = HD ** -0.5
    h0 = lax.broadcasted_iota(jnp.int32, (T_P, LANES), 1) < HD
    def head_chain(h, q, kb, vb):
        qm = jnp.where(h0 if h == 0 else jnp.logical_not(h0), q, 0.0)
        s = _bdot_nt(qm, kb)
        yield
        o = jnp.dot(_softmax_rows(s * scale).astype(BF), vb, preferred_element_type=F32)
        yield
        return o

    npair = D // LANES
    for p0 in range(0, npair, CTX_PAIRS):
        gens = []
        for p in range(p0, p0 + CTX_PAIRS):
            ls = pl.ds(p * LANES, LANES)
            q = q_ref[:, ls]
            kb = k_ref[:, ls].astype(BF)
            vb = v_ref[:, ls].astype(BF)
            gens += [head_chain(h, q, kb, vb) for h in range(2)]
        outs = _run_lockstep(gens)
        for j, p in enumerate(range(p0, p0 + CTX_PAIRS)):
            o_ref[:, pl.ds(p * LANES, LANES)] = jnp.where(h0, outs[2 * j], outs[2 * j + 1])


def _ctx_attn(q, k, v):
    blk = pl.BlockSpec((T_P, D), lambda b: (b, 0))
    return pl.pallas_call(
        _ctx_attn_kernel,
        grid=(NB_P,),
        in_specs=[blk] * 3,
        out_specs=blk,
        out_shape=jax.ShapeDtypeStruct((NPR, D), F32),
        compiler_params=_cp(("parallel",)),
        name="ctx_attn",
    )(q, k, v)


def _nbr_attn_kernel(q_ref, k_ref, v_ref, kc_ref, vc_ref, p_ref, o_ref, tz_ref):
    scale = HD ** -0.5
    rows = T_S // GRID_W
    nloc = WIN_R * GRID_W
    qc = lax.broadcasted_iota(jnp.int32, (GRID_W, LANES), 0)
    ln = lax.broadcasted_iota(jnp.int32, (GRID_W, LANES), 1)
    h0 = ln < HD
    kcb = kc_ref[...].astype(BF)
    vcb = vc_ref[...].astype(BF)
    kcol = ln & (GRID_W - 1)
    cstart = jnp.clip(qc - WIN_C // 2, 0, GRID_W - WIN_C)
    valid = jnp.logical_and(kcol >= cstart, kcol < cstart + WIN_C)
    for h in range(2):
        rolled = []
        for j in range(2 * WIN_R - 1):
            prow = jnp.broadcast_to(p_ref[h, j:j + 1, :], (GRID_W, LANES))
            rolled.append((pltpu.roll(prow, 0, 1, stride=1, stride_axis=0),
                           pltpu.roll(prow, GRID_W, 1, stride=1, stride_axis=0)))
        for j in range(2 * WIN_R - 2):
            tz_ref[h, j] = jnp.where(valid, jnp.where(h0, rolled[j][0], rolled[j + 1][1]), NEG)

    def head_chain(h, q, klb, vlb, j0):
        qm = jnp.where(h0 if h == 0 else jnp.logical_not(h0), q, 0.0).astype(BF)
        sl = _bdot_nt(qm, klb)
        sc = _bdot_nt(qm, kcb)
        yield
        bias = jnp.concatenate([tz_ref[h, j0 + 2 * m] for m in range(WIN_R // 2)], axis=1)
        sl = sl * scale + bias
        sc = sc * scale
        m = jnp.maximum(jnp.max(sl, axis=-1, keepdims=True), jnp.max(sc, axis=-1, keepdims=True))
        el = jnp.exp(sl - m)
        ec = jnp.exp(sc - m)
        inv = 1.0 / (jnp.sum(el, axis=-1, keepdims=True) + jnp.sum(ec, axis=-1, keepdims=True))
        o = (jnp.dot((el * inv).astype(BF), vlb, preferred_element_type=F32)
             + jnp.dot((ec * inv).astype(BF), vcb, preferred_element_type=F32))
        yield
        return o

    def body(it, carry):
        gens, slices = [], []
        for j in range(NBR_UNROLL):
            r = it * NBR_UNROLL + j
            start = jnp.clip(r - WIN_R // 2, 0, rows - WIN_R)
            qs = pl.ds(pl.multiple_of(r * GRID_W, GRID_W), GRID_W)
            ks = pl.ds(pl.multiple_of(start * GRID_W, GRID_W), nloc)
            q = q_ref[qs, :]
            klb = k_ref[ks, :].astype(BF)
            vlb = v_ref[ks, :].astype(BF)
            j0 = start - r + WIN_R - 1
            slices.append(qs)
            gens += [head_chain(h, q, klb, vlb, j0) for h in range(2)]
        outs = _run_lockstep(gens)
        for j, qs in enumerate(slices):
            o_ref[qs, :] = jnp.where(h0, outs[2 * j], outs[2 * j + 1])
        return carry

    lax.fori_loop(0, rows // NBR_UNROLL, body, 0)


def _nbr_attn(q, k, v, kc, vc, ptab):
    npair = D // LANES
    nrow = ptab.shape[1]
    qseq = pl.BlockSpec((T_S, LANES), lambda b, p: (NPR // T_S + b, p))
    seq = pl.BlockSpec((T_S, LANES), lambda b, p: (b, p))
    ctx = pl.BlockSpec((kc.shape[0] // NB_S, LANES), lambda b, p: (b, p))
    return pl.pallas_call(
        _nbr_attn_kernel,
        grid=(NB_S, npair),
        in_specs=[qseq, seq, seq, ctx, ctx,
                  pl.BlockSpec((2, nrow, LANES), lambda b, p: (p, 0, 0))],
        out_specs=pl.BlockSpec((T_S, LANES), lambda b, p: (b, p)),
        out_shape=jax.ShapeDtypeStruct((NSR, D), F32),
        scratch_shapes=[pltpu.VMEM((2, nrow - 1, GRID_W, LANES), F32)],
        compiler_params=_cp(("parallel", "parallel")),
        name="nbr_attn",
    )(q, k, v, kc, vc, ptab)


def _block_diag2(a, b):
    z = jnp.zeros_like(a)
    return jnp.concatenate([jnp.concatenate([a, z], axis=1), jnp.concatenate([z, b], axis=1)], axis=0)


def _nbr_bias_table(rpb):
    pad = jnp.zeros(rpb.shape[:2] + (LANES - 2 * WIN_C + 1,), F32)
    return jnp.concatenate([rpb[:, :, WIN_C - 1:], pad, rpb[:, :, :WIN_C - 1]], axis=-1)


def kernel(x_prompt, x_sample, c, state_rwkv, cache_na_k, cache_na_v, c_ctx, norm_mix, norm_ffn, ada_w, ada_b, rwkv_mu, rwkv_w_r, rwkv_w_k, rwkv_w_v, rwkv_w_o, rwkv_w0, rwkv_w1, rwkv_w2, rwkv_a0, rwkv_a1, rwkv_a2, rwkv_g1, rwkv_g2, rwkv_k_k, rwkv_k_a, rwkv_r_k, rwkv_lnx_g, rwkv_lnx_b, na_w_qkv, na_w_o, na_rpb, moe_w_grp, moe_b_grp, moe_w_exp, moe_b_exp, moe_w1, moe_w3, moe_w2, final_norm):
    x_p = x_prompt.reshape(NPR, D)
    x_s = x_sample.reshape(NSR, D)
    c8 = jnp.concatenate([c_ctx[None, :], c, jnp.zeros((8 - 1 - NB_S, D), F32)], axis=0)
    mod = _adaln(c8, ada_w, ada_b)
    ri = jnp.arange(2 * LANES)[:, None] // HD
    bo = (ri == ri.T).astype(BF)
    row = lambda a: a.reshape(1, D)

    w1c = jnp.concatenate([rwkv_w1[0, 0], rwkv_w1[0, 1]], axis=1).astype(BF)
    a1c = jnp.concatenate([rwkv_a1[0, 0], rwkv_a1[0, 1]], axis=1).astype(BF)
    w2bd = _block_diag2(rwkv_w2[0, 0], rwkv_w2[0, 1]).astype(BF)
    a2bd = _block_diag2(rwkv_a2[0, 0], rwkv_a2[0, 1]).astype(BF)
    r, k, v, g, kk, lw0, lw1, ag0, ag1 = _rwkv_front(
        x_p, x_s, mod[0], row(norm_mix[0]), rwkv_mu[0], rwkv_w_r[0].astype(BF), rwkv_w_k[0].astype(BF),
        rwkv_w_v[0].astype(BF), w1c, a1c, rwkv_g1[0].astype(BF), w2bd, a2bd, rwkv_g2[0].astype(BF),
        rwkv_w0[0], rwkv_a0[0], row(rwkv_k_k[0]), bo)
    arrs = (r, k, v, kk, lw0, lw1, ag0, ag1)
    ka, rk = row(rwkv_k_a[0]), row(rwkv_r_k[0])
    ys_p, bon_p, new_state = _wkv(arrs, ka, rk, T_P, NB_P, 0, lanes=D)
    s0 = state_rwkv[:, 0].reshape(NB_S, 2, NH // 2, 2, HD, HD)
    z = jnp.zeros_like(s0[:, :, :, 0])
    s0_bd = jnp.concatenate([jnp.concatenate([s0[:, :, :, 0], z], axis=-1),
                             jnp.concatenate([z, s0[:, :, :, 1]], axis=-1)], axis=-2)
    ys_s, bon_s = _wkv(arrs, ka, rk, T_S, NB_S, NPR // T_S, s0_bd=s0_bd)
    new_state_rwkv = new_state.reshape(NB_P, 1, 2, NH, HD, HD)
    def router(i):
        wr = jnp.concatenate([moe_w_exp[i], moe_w_grp[i],
                              jnp.zeros((D, LANES - NE - NE // EPG), F32)], axis=1)
        br = jnp.concatenate([moe_b_exp[i], moe_b_grp[i],
                              jnp.zeros((LANES - NE - NE // EPG,), F32)]).reshape(1, LANES)
        return (row(norm_ffn[i]), wr, br)

    def experts(y, xt, gate, i, final):
        w13 = jnp.concatenate([moe_w1[i], moe_w3[i]], axis=-1).astype(BF)
        grp = gate[:, GRP_LANE].astype(jnp.int32).reshape(NTOK // TMX, TMX)
        cnt = jnp.sum(grp[:, :, None] == jnp.arange(NE // EPG)[None, None, :], axis=1).astype(jnp.int32)
        off = jnp.cumsum(cnt, axis=1) - cnt
        blo = off // SBR
        bhi = jnp.where(cnt > 0, (off + cnt - 1) // SBR + 1, blo)
        blk = jnp.stack([blo, bhi], axis=-1).astype(jnp.int32).reshape(-1)
        return _moe_experts(blk, xt, gate, w13, moe_w2[i].astype(BF), y, mod[i], row(final_norm), final)

    y, xt, gate = _proj_residual((x_p, x_s), ys_p, ys_s, mod[0], rwkv_w_o[0].astype(BF), router(0),
                                 gn_args=(bon_p, bon_s, g, row(rwkv_lnx_g[0]), row(rwkv_lnx_b[0]), bo),
                                 name="rwkv_out")
    y = experts(y, xt, gate, 0, False)

    q, k_p, v_p, k_s, v_s, kcache, vcache = _na_qkv(y, mod[1], row(norm_mix[1]), na_w_qkv[0].astype(BF))
    o_p = _ctx_attn(q, k_p, v_p)
    kc = cache_na_k[:, 0].reshape(NB_S * cache_na_k.shape[2], D)
    vc = cache_na_v[:, 0].reshape(NB_S * cache_na_v.shape[2], D)
    o_s = _nbr_attn(q, k_s, v_s, kc, vc, _nbr_bias_table(na_rpb[0]))
    y, xt, gate = _proj_residual(y, o_p, o_s, mod[1], na_w_o[0].astype(BF), router(1), name="attn_out")
    y_p, y_s = experts(y, xt, gate, 1, True)

    y_prompt = y_p.reshape(NB_P, T_P, D)
    y_sample = y_s.reshape(NB_S, T_S, D)
    new_k = kcache.reshape(NB_P, 1, T_P, NH, HD)
    new_v = vcache.reshape(NB_P, 1, T_P, NH, HD)
    return (y_prompt, y_sample, new_state_rwkv, new_k, new_v)
```

```python
import functools
import math

import jax
import jax.numpy as jnp
from jax import lax
from jax.experimental import pallas as pl
from jax.experimental.pallas import tpu as pltpu

F32 = jnp.float32
BF = jnp.bfloat16

D = 1024
NH = 16
HD = 64
NB_P, T_P = 32, 256
NB_S, T_S = 2, 1024
NPR = NB_P * T_P
NSR = NB_S * T_S
NTOK = NPR + NSR
TM = 256
NT = NTOK // TM
NT_P = NPR // TM
TPS = T_S // TM
TMX = 1024
NMOD = 6
NE = 16
EPG = 4
DE = 256
EPS = EPG
SBR = 128
GRP_LANE = 127
CH = 64
SB = 32
LANES = 128
GRID_W = 64
WIN_R, WIN_C = 8, 16
NBR_UNROLL = 4
CTX_PAIRS = 2
NORM_EPS = 1e-6
GN_EPS = 64e-5
NEG = -1e30
VMEM_LIMIT = 56 * 1024 * 1024


def _cp(sem):
    return pltpu.CompilerParams(dimension_semantics=sem, vmem_limit_bytes=VMEM_LIMIT)


def _bdot(a, b):
    return jnp.dot(a.astype(BF), b.astype(BF), preferred_element_type=F32)


def _bdot_nt(a, b):
    return lax.dot_general(a.astype(BF), b.astype(BF), (((1,), (1,)), ((), ())),
                           preferred_element_type=F32)


def _split2(x):
    hi = x.astype(BF)
    lo = (x - hi.astype(F32)).astype(BF)
    return hi, lo


def _split3(x):
    hi = x.astype(BF)
    r1 = x - hi.astype(F32)
    mid = r1.astype(BF)
    lo = (r1 - mid.astype(F32)).astype(BF)
    return hi, mid, lo


def _seg_sum(x, bo):
    outs = []
    for c in range(x.shape[1] // 256):
        hi, lo = _split2(x[:, c * 256:(c + 1) * 256])
        outs.append(jnp.dot(hi, bo, preferred_element_type=F32)
                    + jnp.dot(lo, bo, preferred_element_type=F32))
    return outs[0] if len(outs) == 1 else jnp.concatenate(outs, axis=1)


def _tile_mod_row(i):
    return jnp.where(i < NT_P, 0, 1 + (i - NT_P) // TPS)


def _modulate(x, g, sh, sc):
    ms = jnp.mean(x * x, axis=-1, keepdims=True)
    return x * lax.rsqrt(ms + NORM_EPS) * g * (1.0 + sc) + sh


def _mod_vec(mod_ref, row, k):
    return mod_ref[pl.ds(row, 1), pl.ds(k * D, D)]


def _adaln_kernel(c_ref, w_ref, b_ref, o_ref):
    c = c_ref[...]
    s = c * jax.nn.sigmoid(c)
    o_ref[...] = _bdot(s, w_ref[...]) + b_ref[...]


def _adaln(c8, ada_w, ada_b):
    nl = ada_w.shape[0]
    tn = 1536
    return pl.pallas_call(
        _adaln_kernel,
        grid=(nl, NMOD * D // tn),
        in_specs=[pl.BlockSpec((8, D), lambda l, j: (0, 0)),
                  pl.BlockSpec((None, D, tn), lambda l, j: (l, 0, j)),
                  pl.BlockSpec((None, 1, tn), lambda l, j: (l, 0, j))],
        out_specs=pl.BlockSpec((None, 8, tn), lambda l, j: (l, 0, j)),
        out_shape=jax.ShapeDtypeStruct((nl, 8, NMOD * D), F32),
        compiler_params=_cp(("parallel", "parallel")),
        name="adaln",
    )(c8, ada_w, ada_b.reshape(nl, 1, NMOD * D))


def _front_kernel(xp_ref, xs_ref, yp_ref, yn_ref, mod_ref, nrm_ref, mu_ref, wr_ref, wk_ref, wv_ref,
                  w1_ref, a1_ref, g1_ref, w2_ref, a2_ref, g2_ref, w0_ref, a0_ref, kkw_ref, bo_ref,
                  r_o, k_o, v_o, g_o, kk_o, lw0_o, lw1_o, ag0_o, ag1_o):
    i = pl.program_id(0)
    row = _tile_mod_row(i)
    q = (i - NT_P) % TPS
    first = jnp.logical_or(i < NT_P, q == 0)
    last = jnp.logical_or(i < NT_P, q == TPS - 1)
    sh = _mod_vec(mod_ref, row, 0)
    sc = _mod_vec(mod_ref, row, 1)
    g = nrm_ref[...]
    h = _modulate(jnp.where(i < NT_P, xp_ref[...], xs_ref[...]), g, sh, sc)
    hp = _modulate(yp_ref[...], g, sh, sc)[7:8]
    hn = _modulate(yn_ref[...], g, sh, sc)[0:1]
    hp = jnp.where(first, 0.0, hp)
    hn = jnp.where(last, 0.0, hn)
    rid = lax.broadcasted_iota(jnp.int32, (TM, D), 0)
    prev = jnp.where(rid == 0, hp, pltpu.roll(h, 1, 0))
    nxt = jnp.where(rid == TM - 1, hn, pltpu.roll(h, TM - 1, 0))
    dx = 0.5 * (prev + nxt) - h

    def mix(n):
        return (h + dx * mu_ref[n:n + 1, :]).astype(BF)

    r_o[...] = jnp.dot(mix(0), wr_ref[...], preferred_element_type=F32)
    xw = mix(1)
    lora_w = _bdot(jnp.tanh(jnp.dot(xw, w1_ref[...], preferred_element_type=F32)), w2_ref[...])
    hc = 0.5 * math.exp(-0.5)
    lw0_o[...] = -hc * jnp.tanh(0.5 * (w0_ref[0:1, :] + lora_w[:, :D])) - hc
    lw1_o[...] = -hc * jnp.tanh(0.5 * (w0_ref[1:2, :] + lora_w[:, D:])) - hc
    k_raw = jnp.dot(mix(2), wk_ref[...], preferred_element_type=F32)
    k_o[...] = k_raw
    kq = k_raw * kkw_ref[...]
    ss = _seg_sum(kq * kq, bo_ref[...])
    kk_o[...] = kq / jnp.maximum(jnp.sqrt(ss), 1e-12)
    v_o[...] = jnp.dot(mix(3), wv_ref[...], preferred_element_type=F32)
    xa = mix(4)
    lora_a = _bdot(jnp.dot(xa, a1_ref[...], preferred_element_type=F32), a2_ref[...])
    ag0_o[...] = 0.5 * jnp.tanh(0.5 * (a0_ref[0:1, :] + lora_a[:, :D])) + 0.5
    ag1_o[...] = 0.5 * jnp.tanh(0.5 * (a0_ref[1:2, :] + lora_a[:, D:])) + 0.5
    xg = mix(5)
    g_o[...] = _bdot(jax.nn.sigmoid(jnp.dot(xg, g1_ref[...], preferred_element_type=F32)), g2_ref[...])


def _rwkv_front(x_p, x_s, mod, nrm, mu, wr, wk, wv, w1c, a1c, g1, w2bd, a2bd, g2, w0, a0, kkw, bo):
    tile = pl.BlockSpec((TM, D), lambda i: (i, 0))
    nblk8 = NSR // 8
    bpt = TM // 8
    full = lambda a: pl.BlockSpec(a.shape, lambda i: (0,) * a.ndim)
    ins = [x_p, x_s, x_s, x_s, mod, nrm, mu, wr, wk, wv, w1c, a1c, g1, w2bd, a2bd, g2, w0, a0, kkw, bo]
    in_specs = [pl.BlockSpec((TM, D), lambda i: (jnp.minimum(i, NT_P - 1), 0)),
                pl.BlockSpec((TM, D), lambda i: (jnp.maximum(i - NT_P, 0), 0)),
                pl.BlockSpec((8, D), lambda i: (jnp.clip((i - NT_P) * bpt - 1, 0, nblk8 - 1), 0)),
                pl.BlockSpec((8, D), lambda i: (jnp.clip((i - NT_P + 1) * bpt, 0, nblk8 - 1), 0))]
    in_specs += [full(a) for a in ins[4:]]
    out = jax.ShapeDtypeStruct((NTOK, D), F32)
    return pl.pallas_call(
        _front_kernel,
        grid=(NT,),
        in_specs=in_specs,
        out_specs=[tile] * 9,
        out_shape=[out] * 9,
        compiler_params=_cp(("parallel",)),
        name="rwkv_front",
    )(*ins)


def _wkv_chunk(d, S, r, k, v, kk, lw, ag, ka, rk, cst):
    trow, strict, incl, h0, bd, bo2, same_blk, bd_sb, eye_c, lane_blk = cst
    dot = lambda a, b: jnp.dot(a, b, preferred_element_type=F32)
    nblk = CH // SB
    h0s = lax.broadcasted_iota(jnp.int32, (SB, LANES), 1) < HD

    def stack_heads(x, swap=False):
        zero = jnp.zeros_like(x)
        parts = [jnp.where(h0, x, zero), jnp.where(h0, zero, x)]
        return jnp.concatenate(parts[::-1] if swap else parts, axis=0)

    def dot_split(a, bm):
        return dot(a.astype(BF), bm.astype(BF))

    def expand(mc):
        return jnp.where(bd_sb, jnp.concatenate([mc] * (LANES // SB), axis=0), 0.0)

    b = kk * ag
    kd = k * (1.0 + (ag - 1.0) * ka)
    cs = lw
    for sh in [1 << n for n in range(int(math.log2(CH)))]:
        if d == 0:
            cs = cs + jnp.where(trow >= sh, pltpu.roll(cs, sh, 0), 0.0)
        else:
            cs = cs + jnp.where(trow < CH - sh, pltpu.roll(cs, CH - sh, 0), 0.0)
    qh, ql = _split2(r * kd * rk)
    qs = dot(jnp.concatenate([qh, ql], axis=0), bo2)
    bonus = (qs[:CH] + qs[CH:]) * v
    yield
    tot = cs[CH - 1:CH, :] if d == 0 else cs[0:1, :]
    p_inv = jnp.exp(-cs)
    p_end = jnp.exp(tot - cs)
    at = -kk * jnp.exp(cs - lw)
    rt = r * jnp.exp(cs)
    bt = (b * p_inv).astype(BF)
    kt = (kd * p_inv).astype(BF)
    ar = jnp.concatenate([at, rt], axis=0)
    h02 = jnp.concatenate([h0, h0], axis=0)
    g0 = _bdot_nt(jnp.where(h02, ar, 0.0), jnp.concatenate([bt, kt], axis=0))
    g1 = _bdot_nt(jnp.where(h02, 0.0, ar), jnp.concatenate([kt, bt], axis=0))
    w0y0 = _bdot_nt(ar, S)
    yield
    w0, y0 = w0y0[:CH], w0y0[CH:]
    lab = jnp.where(strict[d], jnp.where(h0, g0[:CH], g1[:CH]), 0.0)
    lak = jnp.where(strict[d], jnp.where(h0, g1[:CH], g0[:CH]), 0.0)
    mrb = jnp.where(incl[d], jnp.where(h0, g0[CH:], g1[CH:]), 0.0)
    mrk = jnp.where(incl[d], jnp.where(h0, g1[CH:], g0[CH:]), 0.0)
    vstk_sw = stack_heads(v, swap=True).astype(BF)
    w = w0 + dot(lak.astype(BF), vstk_sw)
    ldiag = jnp.where(same_blk, lab, 0.0)
    loff = jnp.where(same_blk, 0.0, lab)
    mc = ldiag[0:SB]
    for i in range(1, nblk):
        mc = mc + ldiag[i * SB:(i + 1) * SB]
    xc = eye_c + mc
    mc = dot_split(mc, expand(mc))
    yield
    nsq = int(math.log2(SB)) - 1
    for it in range(nsq):
        last = it + 1 == nsq
        res = dot_split(xc if last else jnp.concatenate([xc, mc], axis=0), expand(mc))
        xc = xc + res[:SB]
        if not last:
            mc = res[SB:]
        yield
    ublk = [None] * nblk
    order = list(range(nblk)) if d == 0 else list(range(nblk - 1, -1, -1))
    for n, i in enumerate(order):
        wi = w[i * SB:(i + 1) * SB]
        if n > 0:
            ucur = jnp.concatenate([jnp.zeros((SB, LANES), F32) if ub is None else ub
                                    for ub in ublk], axis=0)
            wi = wi + dot(loff[i * SB:(i + 1) * SB].astype(BF), stack_heads(ucur).astype(BF))
            yield
        m0 = jnp.where(h0s, wi, 0.0).astype(BF)
        m1 = jnp.where(h0s, 0.0, wi).astype(BF)
        rhs = jnp.concatenate([m0] * nblk + [m1] * nblk, axis=0)
        ublk[i] = dot(jnp.where(lane_blk[i], xc, 0.0).astype(BF), rhs)
        yield
    u = jnp.concatenate(ublk, axis=0)
    y = dot(jnp.concatenate([mrb, mrk], axis=1).astype(BF),
            jnp.concatenate([stack_heads(u).astype(BF), vstk_sw], axis=0))
    uvt = jnp.concatenate([u, v], axis=0).T
    ds = _bdot(uvt, jnp.concatenate([b * p_end, kd * p_end], axis=0))
    yield
    s_new = S * jnp.exp(tot) + jnp.where(bd, ds, 0.0)
    return s_new, y0 + y, bonus


def _run_lockstep(gens):
    results = [None] * len(gens)
    pending = list(range(len(gens)))
    while pending:
        for i in list(pending):
            try:
                next(gens[i])
            except StopIteration as stop:
                results[i] = stop.value
                pending.remove(i)
    return results


def _wkv_consts():
    t2 = lax.broadcasted_iota(jnp.int32, (CH, LANES), 0)
    l2 = lax.broadcasted_iota(jnp.int32, (CH, LANES), 1)
    s2 = l2 & (CH - 1)
    strict = [s2 < t2, s2 > t2]
    incl = [s2 <= t2, s2 >= t2]
    h0 = l2 < HD
    same_blk = (t2 // SB) == (s2 // SB)
    ri = lax.broadcasted_iota(jnp.int32, (LANES, LANES), 0)
    ci = lax.broadcasted_iota(jnp.int32, (LANES, LANES), 1)
    bd = (ri // HD) == (ci // HD)
    bo2 = jnp.where(bd, 1.0, 0.0).astype(BF)
    bd_sb = (ri // SB) == (ci // SB)
    t3 = lax.broadcasted_iota(jnp.int32, (SB, LANES), 0)
    l3 = lax.broadcasted_iota(jnp.int32, (SB, LANES), 1)
    eye_c = jnp.where((l3 & (SB - 1)) == t3, 1.0, 0.0)
    lane_blk = [((l3 & (CH - 1)) // SB) == i for i in range(CH // SB)]
    return t2, strict, incl, h0, bd, bo2, same_blk, bd_sb, eye_c, lane_blk


def _wkv_kernel(T, has_s0, *refs):
    (r_ref, k_ref, v_ref, kk_ref, lw0_ref, lw1_ref, ag0_ref, ag1_ref, ka_ref, rk_ref), rest = refs[:10], refs[10:]
    if has_s0:
        s0_ref, y_ref, bon_ref, s_scr = rest
    else:
        y_ref, bon_ref, st_ref, s_scr = rest
    nch = T // CH
    npair = y_ref.shape[1] // LANES
    cst = _wkv_consts()
    if has_s0:
        s_scr[...] = s0_ref[...]
    else:
        s_scr[...] = jnp.zeros_like(s_scr)
    y_ref[...] = jnp.zeros_like(y_ref)
    bon_ref[...] = jnp.zeros_like(bon_ref)
    lw_refs = (lw0_ref, lw1_ref)
    ag_refs = (ag0_ref, ag1_ref)

    def body(c, carry):
        chains = [(p, d) for p in range(npair) for d in range(2)]
        sl = {}
        for p, d in chains:
            r0 = pl.multiple_of((c if d == 0 else nch - 1 - c) * CH, CH)
            sl[p, d] = (pl.ds(r0, CH), pl.ds(p * LANES, LANES))
        args = {}
        for p, d in chains:
            rs, ls = sl[p, d]
            args[p, d] = (s_scr[d, p], r_ref[rs, ls], k_ref[rs, ls], v_ref[rs, ls], kk_ref[rs, ls],
                          lw_refs[d][rs, ls], ag_refs[d][rs, ls], ka_ref[:, ls], rk_ref[:, ls],
                          y_ref[rs, ls], bon_ref[rs, ls])
        outs = _run_lockstep([_wkv_chunk(d, *args[p, d][:9], cst) for p, d in chains])
        for (p, d), (s_new, y, bon) in zip(chains, outs):
            rs, ls = sl[p, d]
            s_scr[d, p] = s_new
            y_ref[rs, ls] = args[p, d][9] + y
            bon_ref[rs, ls] = args[p, d][10] + bon
        return carry

    lax.fori_loop(0, nch, body, 0)
    if not has_s0:
        for d in range(2):
            for p in range(npair):
                s = s_scr[d, p]
                st_ref[d, 2 * p] = s[:HD, :HD]
                st_ref[d, 2 * p + 1] = s[HD:, HD:]


def _wkv(arrs, ka, rk, T, nb, row_blk0, s0_bd=None, lanes=512):
    ng = D // lanes
    npair = lanes // LANES
    seq = pl.BlockSpec((T, lanes), lambda b, g: (row_blk0 + b, g))
    vec = pl.BlockSpec((1, lanes), lambda b, g: (0, g))
    st = pl.BlockSpec((None, 2, npair, LANES, LANES), lambda b, g: (b, 0, g, 0, 0))
    out_seq = pl.BlockSpec((T, lanes), lambda b, g: (b, g))
    in_specs = [seq] * 8 + [vec, vec]
    ins = list(arrs) + [ka, rk]
    out_specs = [out_seq, out_seq]
    out_shape = [jax.ShapeDtypeStruct((nb * T, D), F32)] * 2
    if s0_bd is not None:
        in_specs.append(st)
        ins.append(s0_bd)
    else:
        out_specs.append(pl.BlockSpec((None, 2, 2 * npair, HD, HD), lambda b, g: (b, 0, g, 0, 0)))
        out_shape.append(jax.ShapeDtypeStruct((nb, 2, NH, HD, HD), F32))
    return pl.pallas_call(
        functools.partial(_wkv_kernel, T, s0_bd is not None),
        grid=(nb, ng),
        in_specs=in_specs,
        out_specs=out_specs,
        out_shape=out_shape,
        scratch_shapes=[pltpu.VMEM((2, npair, LANES, LANES), F32)],
        compiler_params=_cp(("parallel", "parallel")),
        name="wkv_T%d" % T,
    )(*ins)


def _proj_kernel(gn, *refs):
    i = pl.program_id(0)
    is_p = i < NT_P
    if gn:
        (yp_ref, ys_ref, ap_ref, as_ref, bp_ref, bs_ref, g_ref, lng_ref, lnb_ref, bo_ref,
         mod_ref, w_ref, nrm_ref, wr_ref, br_ref, o_ref, xt_o, gate_o) = refs
        y = jnp.where(is_p, yp_ref[...], ys_ref[...])
    else:
        y_ref, ap_ref, as_ref, mod_ref, w_ref, nrm_ref, wr_ref, br_ref, o_ref, xt_o, gate_o = refs
        y = y_ref[...]
    a = jnp.where(is_p, ap_ref[...], as_ref[...])
    if gn:
        bo = bo_ref[...]
        mean = _seg_sum(a, bo) * (1.0 / HD)
        cen = a - mean
        var = _seg_sum(cen * cen, bo) * (1.0 / HD)
        yn = cen * lax.rsqrt(var + GN_EPS) * lng_ref[...] + lnb_ref[...]
        a = (yn + jnp.where(is_p, bp_ref[...], bs_ref[...])) * g_ref[...]
    gt = _mod_vec(mod_ref, _tile_mod_row(i), 2)
    y_new = y + gt * _bdot(a, w_ref[...])
    o_ref[...] = y_new
    row = _tile_mod_row(i)
    xt = _modulate(y_new, nrm_ref[...], _mod_vec(mod_ref, row, 3), _mod_vec(mod_ref, row, 4))
    xt_o[...] = xt.astype(BF)
    gate_o[...] = _route(xt, wr_ref[...], br_ref[...])


def _proj_residual(y, a_p, a_s, mod, w, router, gn_args=None, name="proj"):
    tile = pl.BlockSpec((TM, D), lambda i: (i, 0))
    tile_p = pl.BlockSpec((TM, D), lambda i: (jnp.minimum(i, NT_P - 1), 0))
    tile_s = pl.BlockSpec((TM, D), lambda i: (jnp.maximum(i - NT_P, 0), 0))
    full = lambda a: pl.BlockSpec(a.shape, lambda i: (0,) * a.ndim)
    if gn_args is not None:
        ins = [y[0], y[1], a_p, a_s]
        in_specs = [tile_p, tile_s, tile_p, tile_s]
    else:
        ins = [y, a_p, a_s]
        in_specs = [tile, tile_p, tile_s]
    if gn_args is not None:
        b_p, b_s, g, lng, lnb, bo = gn_args
        ins += [b_p, b_s, g, lng, lnb, bo]
        in_specs += [tile_p, tile_s, tile, full(lng), full(lnb), full(bo)]
    ins += [mod, w] + list(router)
    in_specs += [full(mod), full(w)] + [full(a) for a in router]
    return pl.pallas_call(
        functools.partial(_proj_kernel, gn_args is not None),
        grid=(NT,),
        in_specs=in_specs,
        out_specs=[tile, tile, pl.BlockSpec((TM, LANES), lambda i: (i, 0))],
        out_shape=[jax.ShapeDtypeStruct((NTOK, D), F32), jax.ShapeDtypeStruct((NTOK, D), BF),
                   jax.ShapeDtypeStruct((NTOK, LANES), F32)],
        compiler_params=_cp(("parallel",)),
        name=name,
    )(*ins)


def _route(xt, wr, br):
    x1, x2 = _split2(xt)
    w1, w2 = _split2(wr)
    dot = lambda a, b: jnp.dot(a, b, preferred_element_type=F32)
    logits = (dot(x2, w1) + dot(x1, w2)) + dot(x1, w1)
    logits = logits + br
    lane = lax.broadcasted_iota(jnp.int32, logits.shape, 1).astype(F32)
    isg = jnp.logical_and(lane >= NE, lane < NE + NE // EPG)
    mg = jnp.max(jnp.where(isg, logits, NEG), axis=-1, keepdims=True)
    eg = jnp.where(isg, jnp.exp(jnp.minimum(logits - mg, 0.0)), 0.0)
    p_sel = 1.0 / jnp.sum(eg, axis=-1, keepdims=True)
    gidx = jnp.min(jnp.where(jnp.logical_and(isg, logits == mg), lane, 1e3), axis=-1, keepdims=True) - NE
    lo_l = gidx * EPG
    ing = jnp.logical_and(lane >= lo_l, lane < lo_l + EPG)
    me = jnp.max(jnp.where(ing, logits, NEG), axis=-1, keepdims=True)
    ee = jnp.where(ing, jnp.exp(jnp.minimum(logits - me, 0.0)), 0.0)
    se = jnp.sum(ee, axis=-1, keepdims=True)
    pe = ee / se
    i1 = jnp.min(jnp.where(jnp.logical_and(ing, logits == me), lane, 1e3), axis=-1, keepdims=True)
    v1 = 1.0 / se
    rest = jnp.logical_and(ing, lane != i1)
    v2 = jnp.max(jnp.where(rest, pe, -1.0), axis=-1, keepdims=True)
    i2 = jnp.min(jnp.where(jnp.logical_and(rest, pe == v2), lane, 1e3), axis=-1, keepdims=True)
    den = v1 + v2
    return (jnp.where(lane == i1, p_sel * v1 / den, 0.0)
            + jnp.where(lane == i2, p_sel * v2 / den, 0.0)
            + jnp.where(lane == float(GRP_LANE), gidx, 0.0))


def _experts_kernel(final, blk_ref, xt_ref, gate_ref, w1_ref, w3_ref, w2_ref, y_ref, mod_ref, fin_ref, *rest):
    i = pl.program_id(0)
    g = pl.program_id(1)
    ngrp = NE // EPS
    if final:
        op_ref, os_ref, xs_scr, gs_scr, outs_scr, pos_scr = rest
    else:
        o_ref, xs_scr, gs_scr, outs_scr, pos_scr = rest
    ntile_p = NPR // TMX
    dot = lambda a, b: jnp.dot(a, b, preferred_element_type=F32)

    @pl.when(g == 0)
    def _():
        gate = gate_ref[...]
        lane = lax.broadcasted_iota(jnp.int32, gate.shape, 1)
        grp = jnp.sum(jnp.where(lane == GRP_LANE, gate, 0.0), axis=-1, keepdims=True)
        onehot = jnp.where(lane.astype(F32) == grp, 1.0, 0.0)
        tr = lax.broadcasted_iota(jnp.int32, (TMX, TMX), 0)
        br = lax.broadcasted_iota(jnp.int32, (LANES, LANES), 0)
        bc = lax.broadcasted_iota(jnp.int32, (LANES, LANES), 1)
        ltri = jnp.where(bc < br, 1.0, 0.0).astype(BF)
        carry = jnp.zeros((1, LANES), F32)
        ranks = []
        for blk in range(TMX // LANES):
            oh = onehot[blk * LANES:(blk + 1) * LANES]
            ranks.append(dot(ltri, oh.astype(BF)) + carry)
            carry = carry + jnp.sum(oh, axis=0, keepdims=True)
        rank = jnp.concatenate(ranks, axis=0)
        cnt = jnp.broadcast_to(carry, (8, LANES))
        lane8 = lax.broadcasted_iota(jnp.int32, (8, LANES), 1)
        off = jnp.zeros((8, LANES), F32)
        for k in range(1, ngrp):
            off = off + jnp.where(lane8 >= k, pltpu.roll(cnt, k, 1), 0.0)
        pos = jnp.sum(onehot * (off[0:1] + rank), axis=-1, keepdims=True)
        posb = jnp.broadcast_to(pos, (TMX, LANES))
        pos_scr[...] = posb
        pos_row = posb.T[0:1, :]
        perm = jnp.where(tr.astype(F32) == pos_row, 1.0, 0.0).astype(BF)
        xs_scr[...] = dot(perm, xt_ref[...]).astype(BF)
        g1, g2 = _split2(gate)
        gsel = dot(perm, jnp.concatenate([g1, g2], axis=1))
        gs_scr[...] = gsel[:, :LANES] + gsel[:, LANES:]
        outs_scr[...] = jnp.zeros_like(outs_scr)

    blo = blk_ref[2 * (i * ngrp + g)]
    bhi = blk_ref[2 * (i * ngrp + g) + 1]
    for b in range(TMX // SBR):
        @pl.when(jnp.logical_and(b >= blo, b < bhi))
        def _(b=b):
            rows = pl.ds(b * SBR, SBR)
            xb = xs_scr[rows, :]
            gsb = gs_scr[rows, :]
            lane_c = lax.broadcasted_iota(jnp.int32, gsb.shape, 1)
            hids = []
            for j in range(EPS):
                h1 = dot(xb, w1_ref[j])
                h3 = dot(xb, w3_ref[j])
                ge = jnp.sum(jnp.where(lane_c == g * EPS + j, gsb, 0.0), axis=-1, keepdims=True)
                hids.append(((h1 * jax.nn.sigmoid(h1)) * h3 * ge).astype(BF))
            outs_scr[rows, :] += dot(jnp.concatenate(hids, axis=1), w2_ref[...].reshape(EPS * DE, D))

    def result():
        row = jnp.where(i < ntile_p, 0, i - ntile_p + 1)
        tcol = lax.broadcasted_iota(jnp.int32, (TMX, TMX), 1).astype(F32)
        back = jnp.where(tcol == pos_scr[:, 0:1], 1.0, 0.0).astype(BF)
        out = y_ref[...] + _mod_vec(mod_ref, row, 5) * dot(back, outs_scr[...].astype(BF))
        if final:
            ms = jnp.mean(out * out, axis=-1, keepdims=True)
            out = out * lax.rsqrt(ms + NORM_EPS) * fin_ref[...]
        return out

    if final:
        @pl.when(jnp.logical_and(g == NE // EPS - 1, i < ntile_p))
        def _():
            op_ref[...] = result()

        @pl.when(jnp.logical_and(g == NE // EPS - 1, i >= ntile_p))
        def _():
            os_ref[...] = result()
    else:
        @pl.when(g == NE // EPS - 1)
        def _():
            o_ref[...] = result()


def _moe_experts(nch, xt, gate, w1, w3, w2, y, mod, fin, final):
    tile = pl.BlockSpec((TMX, D), lambda i, e, n: (i, 0))
    full = lambda a: pl.BlockSpec(a.shape, lambda i, e, n: (0,) * a.ndim)
    ntile_p = NPR // TMX
    if final:
        out_specs = [pl.BlockSpec((TMX, D), lambda i, e, n: (jnp.minimum(i, ntile_p - 1), 0)),
                     pl.BlockSpec((TMX, D), lambda i, e, n: (jnp.maximum(i - ntile_p, 0), 0))]
        out_shape = [jax.ShapeDtypeStruct((NPR, D), F32), jax.ShapeDtypeStruct((NSR, D), F32)]
    else:
        out_specs = tile
        out_shape = jax.ShapeDtypeStruct((NTOK, D), F32)
    scratch = [pltpu.VMEM((TMX, D), BF), pltpu.VMEM((TMX, LANES), F32), pltpu.VMEM((TMX, D), F32),
               pltpu.VMEM((TMX, LANES), F32)]
    grid_spec = pltpu.PrefetchScalarGridSpec(
        num_scalar_prefetch=1,
        grid=(NTOK // TMX, NE // EPS),
        in_specs=[tile, pl.BlockSpec((TMX, LANES), lambda i, e, n: (i, 0)),
                  pl.BlockSpec((EPS, D, DE), lambda i, e, n: (e, 0, 0)),
                  pl.BlockSpec((EPS, D, DE), lambda i, e, n: (e, 0, 0)),
                  pl.BlockSpec((EPS, DE, D), lambda i, e, n: (e, 0, 0)),
                  tile, full(mod), full(fin)],
        out_specs=out_specs,
        scratch_shapes=scratch)
    return pl.pallas_call(
        functools.partial(_experts_kernel, final),
        grid_spec=grid_spec,
        out_shape=out_shape,
        compiler_params=_cp(("arbitrary" if final else "parallel", "arbitrary")),
        name="moe_experts",
    )(nch, xt, gate, w1, w3, w2, y, mod, fin)


def _qkv_kernel(y_ref, mod_ref, nrm_ref, w_ref, q_o, kp_o, vp_o, ks_o, vs_o, kc_o, vc_o):
    i = pl.program_id(0)
    row = _tile_mod_row(_qkv_tile(i))
    h = _modulate(y_ref[...], nrm_ref[...], _mod_vec(mod_ref, row, 0), _mod_vec(mod_ref, row, 1))
    hb = h.astype(BF)
    kx = jnp.dot(hb, w_ref[:, D:2 * D], preferred_element_type=F32)
    vx = jnp.dot(hb, w_ref[:, 2 * D:], preferred_element_type=F32)
    q_o[...] = jnp.dot(hb, w_ref[:, :D], preferred_element_type=F32)

    @pl.when(i < NT - NT_P)
    def _():
        ks_o[...] = kx
        vs_o[...] = vx

    @pl.when(i >= NT - NT_P)
    def _():
        kp_o[...] = kx
        vp_o[...] = vx
        for hh in range(NH):
            kc_o[pl.ds(hh, TM, stride=NH), :] = kx[:, hh * HD:(hh + 1) * HD]
            vc_o[pl.ds(hh, TM, stride=NH), :] = vx[:, hh * HD:(hh + 1) * HD]


def _qkv_tile(i):
    return jnp.where(i < NT - NT_P, NT_P + i, i - (NT - NT_P))


def _na_qkv(y, mod, nrm, w):
    nt_s = NT - NT_P
    tile = pl.BlockSpec((TM, D), lambda i: (_qkv_tile(i), 0))
    tile_p = pl.BlockSpec((TM, D), lambda i: (jnp.maximum(i - nt_s, 0), 0))
    tile_s = pl.BlockSpec((TM, D), lambda i: (jnp.minimum(i, nt_s - 1), 0))
    full = lambda a: pl.BlockSpec(a.shape, lambda i: (0,) * a.ndim)
    out = lambda n: jax.ShapeDtypeStruct((n, D), F32)
    cache = jax.ShapeDtypeStruct((NPR * NH, HD), F32)
    cache_tile = pl.BlockSpec((TM * NH, HD), lambda i: (jnp.maximum(i - nt_s, 0), 0))
    return pl.pallas_call(
        _qkv_kernel,
        grid=(NT,),
        in_specs=[tile, full(mod), full(nrm), full(w)],
        out_specs=[tile, tile_p, tile_p, tile_s, tile_s, cache_tile, cache_tile],
        out_shape=[out(NTOK), out(NPR), out(NPR), out(NSR), out(NSR), cache, cache],
        compiler_params=_cp(("arbitrary",)),
        name="na_qkv",
    )(y, mod, nrm, w)


def _softmax_rows(s):
    m = jnp.max(s, axis=-1, keepdims=True)
    e = jnp.exp(s - m)
    return e / jnp.sum(e, axis=-1, keepdims=True)


def _ctx_attn_kernel(q_ref, k_ref, v_ref, o_ref):
    scale = HD ** -0.5
    h0 = lax.broadcasted_iota(jnp.int32, (T_P, LANES), 1) < HD
    def head_chain(h, q, kb, vb):
        qm = jnp.where(h0 if h == 0 else jnp.logical_not(h0), q, 0.0)
        s = _bdot_nt(qm, kb)
        yield
        o = jnp.dot(_softmax_rows(s * scale).astype(BF), vb, preferred_element_type=F32)
        yield
        return o

    npair = D // LANES
    for p0 in range(0, npair, CTX_PAIRS):
        gens = []
        for p in range(p0, p0 + CTX_PAIRS):
            ls = pl.ds(p * LANES, LANES)
            q = q_ref[:, ls]
            kb = k_ref[:, ls].astype(BF)
            vb = v_ref[:, ls].astype(BF)
            gens += [head_chain(h, q, kb, vb) for h in range(2)]
        outs = _run_lockstep(gens)
        for j, p in enumerate(range(p0, p0 + CTX_PAIRS)):
            o_ref[:, pl.ds(p * LANES, LANES)] = jnp.where(h0, outs[2 * j], outs[2 * j + 1])


def _ctx_attn(q, k, v):
    blk = pl.BlockSpec((T_P, D), lambda b: (b, 0))
    return pl.pallas_call(
        _ctx_attn_kernel,
        grid=(NB_P,),
        in_specs=[blk] * 3,
        out_specs=blk,
        out_shape=jax.ShapeDtypeStruct((NPR, D), F32),
        compiler_params=_cp(("parallel",)),
        name="ctx_attn",
    )(q, k, v)


def _nbr_attn_kernel(q_ref, k_ref, v_ref, kc_ref, vc_ref, p_ref, o_ref, tz_ref):
    scale = HD ** -0.5
    rows = T_S // GRID_W
    nloc = WIN_R * GRID_W
    qc = lax.broadcasted_iota(jnp.int32, (GRID_W, LANES), 0)
    ln = lax.broadcasted_iota(jnp.int32, (GRID_W, LANES), 1)
    h0 = ln < HD
    kcb = kc_ref[...].astype(BF)
    vcb = vc_ref[...].astype(BF)
    kcol = ln & (GRID_W - 1)
    cstart = jnp.clip(qc - WIN_C // 2, 0, GRID_W - WIN_C)
    valid = jnp.logical_and(kcol >= cstart, kcol < cstart + WIN_C)
    for h in range(2):
        rolled = []
        for j in range(2 * WIN_R - 1):
            prow = jnp.broadcast_to(p_ref[h, j:j + 1, :], (GRID_W, LANES))
            rolled.append((pltpu.roll(prow, 0, 1, stride=1, stride_axis=0),
                           pltpu.roll(prow, GRID_W, 1, stride=1, stride_axis=0)))
        for j in range(2 * WIN_R - 2):
            tz_ref[h, j] = jnp.where(valid, jnp.where(h0, rolled[j][0], rolled[j + 1][1]), NEG)

    def head_chain(h, q, klb, vlb, j0):
        qm = jnp.where(h0 if h == 0 else jnp.logical_not(h0), q, 0.0).astype(BF)
        sl = _bdot_nt(qm, klb)
        sc = _bdot_nt(qm, kcb)
        yield
        bias = jnp.concatenate([tz_ref[h, j0 + 2 * m] for m in range(WIN_R // 2)], axis=1)
        sl = sl * scale + bias
        sc = sc * scale
        m = jnp.maximum(jnp.max(sl, axis=-1, keepdims=True), jnp.max(sc, axis=-1, keepdims=True))
        el = jnp.exp(sl - m)
        ec = jnp.exp(sc - m)
        inv = 1.0 / (jnp.sum(el, axis=-1, keepdims=True) + jnp.sum(ec, axis=-1, keepdims=True))
        o = (jnp.dot((el * inv).astype(BF), vlb, preferred_element_type=F32)
             + jnp.dot((ec * inv).astype(BF), vcb, preferred_element_type=F32))
        yield
        return o

    def body(it, carry):
        gens, slices = [], []
        for j in range(NBR_UNROLL):
            r = it * NBR_UNROLL + j
            start = jnp.clip(r - WIN_R // 2, 0, rows - WIN_R)
            qs = pl.ds(pl.multiple_of(r * GRID_W, GRID_W), GRID_W)
            ks = pl.ds(pl.multiple_of(start * GRID_W, GRID_W), nloc)
            q = q_ref[qs, :]
            klb = k_ref[ks, :].astype(BF)
            vlb = v_ref[ks, :].astype(BF)
            j0 = start - r + WIN_R - 1
            slices.append(qs)
            gens += [head_chain(h, q, klb, vlb, j0) for h in range(2)]
        outs = _run_lockstep(gens)
        for j, qs in enumerate(slices):
            o_ref[qs, :] = jnp.where(h0, outs[2 * j], outs[2 * j + 1])
        return carry

    lax.fori_loop(0, rows // NBR_UNROLL, body, 0)


def _nbr_attn(q, k, v, kc, vc, ptab):
    npair = D // LANES
    nrow = ptab.shape[1]
    qseq = pl.BlockSpec((T_S, LANES), lambda b, p: (NPR // T_S + b, p))
    seq = pl.BlockSpec((T_S, LANES), lambda b, p: (b, p))
    ctx = pl.BlockSpec((kc.shape[0] // NB_S, LANES), lambda b, p: (b, p))
    return pl.pallas_call(
        _nbr_attn_kernel,
        grid=(NB_S, npair),
        in_specs=[qseq, seq, seq, ctx, ctx,
                  pl.BlockSpec((2, nrow, LANES), lambda b, p: (p, 0, 0))],
        out_specs=pl.BlockSpec((T_S, LANES), lambda b, p: (b, p)),
        out_shape=jax.ShapeDtypeStruct((NSR, D), F32),
        scratch_shapes=[pltpu.VMEM((2, nrow - 1, GRID_W, LANES), F32)],
        compiler_params=_cp(("parallel", "parallel")),
        name="nbr_attn",
    )(q, k, v, kc, vc, ptab)


def _block_diag2(a, b):
    z = jnp.zeros_like(a)
    return jnp.concatenate([jnp.concatenate([a, z], axis=1), jnp.concatenate([z, b], axis=1)], axis=0)


def _nbr_bias_table(rpb):
    pad = jnp.zeros(rpb.shape[:2] + (LANES - 2 * WIN_C + 1,), F32)
    return jnp.concatenate([rpb[:, :, WIN_C - 1:], pad, rpb[:, :, :WIN_C - 1]], axis=-1)


def kernel(x_prompt, x_sample, c, state_rwkv, cache_na_k, cache_na_v, c_ctx, norm_mix, norm_ffn, ada_w, ada_b, rwkv_mu, rwkv_w_r, rwkv_w_k, rwkv_w_v, rwkv_w_o, rwkv_w0, rwkv_w1, rwkv_w2, rwkv_a0, rwkv_a1, rwkv_a2, rwkv_g1, rwkv_g2, rwkv_k_k, rwkv_k_a, rwkv_r_k, rwkv_lnx_g, rwkv_lnx_b, na_w_qkv, na_w_o, na_rpb, moe_w_grp, moe_b_grp, moe_w_exp, moe_b_exp, moe_w1, moe_w3, moe_w2, final_norm):
    x_p = x_prompt.reshape(NPR, D)
    x_s = x_sample.reshape(NSR, D)
    c8 = jnp.concatenate([c_ctx[None, :], c, jnp.zeros((8 - 1 - NB_S, D), F32)], axis=0)
    mod = _adaln(c8, ada_w, ada_b)
    ri = jnp.arange(2 * LANES)[:, None] // HD
    bo = (ri == ri.T).astype(BF)
    row = lambda a: a.reshape(1, D)

    w1c = jnp.concatenate([rwkv_w1[0, 0], rwkv_w1[0, 1]], axis=1).astype(BF)
    a1c = jnp.concatenate([rwkv_a1[0, 0], rwkv_a1[0, 1]], axis=1).astype(BF)
    w2bd = _block_diag2(rwkv_w2[0, 0], rwkv_w2[0, 1]).astype(BF)
    a2bd = _block_diag2(rwkv_a2[0, 0], rwkv_a2[0, 1]).astype(BF)
    r, k, v, g, kk, lw0, lw1, ag0, ag1 = _rwkv_front(
        x_p, x_s, mod[0], row(norm_mix[0]), rwkv_mu[0], rwkv_w_r[0].astype(BF), rwkv_w_k[0].astype(BF),
        rwkv_w_v[0].astype(BF), w1c, a1c, rwkv_g1[0].astype(BF), w2bd, a2bd, rwkv_g2[0].astype(BF),
        rwkv_w0[0], rwkv_a0[0], row(rwkv_k_k[0]), bo)
    arrs = (r, k, v, kk, lw0, lw1, ag0, ag1)
    ka, rk = row(rwkv_k_a[0]), row(rwkv_r_k[0])
    ys_p, bon_p, new_state = _wkv(arrs, ka, rk, T_P, NB_P, 0, lanes=D)
    s0 = state_rwkv[:, 0].reshape(NB_S, 2, NH // 2, 2, HD, HD)
    z = jnp.zeros_like(s0[:, :, :, 0])
    s0_bd = jnp.concatenate([jnp.concatenate([s0[:, :, :, 0], z], axis=-1),
                             jnp.concatenate([z, s0[:, :, :, 1]], axis=-1)], axis=-2)
    ys_s, bon_s = _wkv(arrs, ka, rk, T_S, NB_S, NPR // T_S, s0_bd=s0_bd)
    new_state_rwkv = new_state.reshape(NB_P, 1, 2, NH, HD, HD)
    def router(i):
        wr = jnp.concatenate([moe_w_exp[i], moe_w_grp[i],
                              jnp.zeros((D, LANES - NE - NE // EPG), F32)], axis=1)
        br = jnp.concatenate([moe_b_exp[i], moe_b_grp[i],
                              jnp.zeros((LANES - NE - NE // EPG,), F32)]).reshape(1, LANES)
        return (row(norm_ffn[i]), wr, br)

    def experts(y, xt, gate, i, final):
        grp = gate[:, GRP_LANE].astype(jnp.int32).reshape(NTOK // TMX, TMX)
        cnt = jnp.sum(grp[:, :, None] == jnp.arange(NE // EPG)[None, None, :], axis=1).astype(jnp.int32)
        off = jnp.cumsum(cnt, axis=1) - cnt
        blo = off // SBR
        bhi = jnp.where(cnt > 0, (off + cnt - 1) // SBR + 1, blo)
        blk = jnp.stack([blo, bhi], axis=-1).astype(jnp.int32).reshape(-1)
        return _moe_experts(blk, xt, gate, moe_w1[i].astype(BF), moe_w3[i].astype(BF), moe_w2[i].astype(BF),
                            y, mod[i], row(final_norm), final)

    y, xt, gate = _proj_residual((x_p, x_s), ys_p, ys_s, mod[0], rwkv_w_o[0].astype(BF), router(0),
                                 gn_args=(bon_p, bon_s, g, row(rwkv_lnx_g[0]), row(rwkv_lnx_b[0]), bo),
                                 name="rwkv_out")
    y = experts(y, xt, gate, 0, False)

    q, k_p, v_p, k_s, v_s, kcache, vcache = _na_qkv(y, mod[1], row(norm_mix[1]), na_w_qkv[0].astype(BF))
    o_p = _ctx_attn(q, k_p, v_p)
    kc = cache_na_k[:, 0].reshape(NB_S * cache_na_k.shape[2], D)
    vc = cache_na_v[:, 0].reshape(NB_S * cache_na_v.shape[2], D)
    o_s = _nbr_attn(q, k_s, v_s, kc, vc, _nbr_bias_table(na_rpb[0]))
    y, xt, gate = _proj_residual(y, o_p, o_s, mod[1], na_w_o[0].astype(BF), router(1), name="attn_out")
    y_p, y_s = experts(y, xt, gate, 1, True)

    y_prompt = y_p.reshape(NB_P, T_P, D)
    y_sample = y_s.reshape(NB_S, T_S, D)
    new_k = kcache.reshape(NB_P, 1, T_P, NH, HD)
    new_v = vcache.reshape(NB_P, 1, T_P, NH, HD)
    return (y_prompt, y_sample, new_state_rwkv, new_k, new_v)
```

```python
import functools
import math

import jax
import jax.numpy as jnp
from jax import lax
from jax.experimental import pallas as pl
from jax.experimental.pallas import tpu as pltpu

F32 = jnp.float32
BF = jnp.bfloat16

D = 1024
NH = 16
HD = 64
NB_P, T_P = 32, 256
NB_S, T_S = 2, 1024
NPR = NB_P * T_P
NSR = NB_S * T_S
NTOK = NPR + NSR
TM = 256
NT = NTOK // TM
NT_P = NPR // TM
TPS = T_S // TM
TMX = 1024
NMOD = 6
NE = 16
EPG = 4
DE = 256
EPS = EPG
SBR = 128
GRP_LANE = 127
CH = 64
SB = 32
LANES = 128
GRID_W = 64
WIN_R, WIN_C = 8, 16
NBR_UNROLL = 4
CTX_PAIRS = 2
NORM_EPS = 1e-6
GN_EPS = 64e-5
NEG = -1e30
VMEM_LIMIT = 56 * 1024 * 1024


def _cp(sem):
    return pltpu.CompilerParams(dimension_semantics=sem, vmem_limit_bytes=VMEM_LIMIT)


def _bdot(a, b):
    return jnp.dot(a.astype(BF), b.astype(BF), preferred_element_type=F32)


def _bdot_nt(a, b):
    return lax.dot_general(a.astype(BF), b.astype(BF), (((1,), (1,)), ((), ())),
                           preferred_element_type=F32)


def _split2(x):
    hi = x.astype(BF)
    lo = (x - hi.astype(F32)).astype(BF)
    return hi, lo


def _split3(x):
    hi = x.astype(BF)
    r1 = x - hi.astype(F32)
    mid = r1.astype(BF)
    lo = (r1 - mid.astype(F32)).astype(BF)
    return hi, mid, lo


def _seg_sum(x, bo):
    outs = []
    for c in range(x.shape[1] // 256):
        hi, lo = _split2(x[:, c * 256:(c + 1) * 256])
        outs.append(jnp.dot(hi, bo, preferred_element_type=F32)
                    + jnp.dot(lo, bo, preferred_element_type=F32))
    return outs[0] if len(outs) == 1 else jnp.concatenate(outs, axis=1)


def _tile_mod_row(i):
    return jnp.where(i < NT_P, 0, 1 + (i - NT_P) // TPS)


def _modulate(x, g, sh, sc):
    ms = jnp.mean(x * x, axis=-1, keepdims=True)
    return x * lax.rsqrt(ms + NORM_EPS) * g * (1.0 + sc) + sh


def _mod_vec(mod_ref, row, k):
    return mod_ref[pl.ds(row, 1), pl.ds(k * D, D)]


def _adaln_kernel(c_ref, w_ref, b_ref, o_ref):
    c = c_ref[...]
    s = c * jax.nn.sigmoid(c)
    o_ref[...] = _bdot(s, w_ref[...]) + b_ref[...]


def _adaln(c8, ada_w, ada_b):
    nl = ada_w.shape[0]
    tn = 1536
    return pl.pallas_call(
        _adaln_kernel,
        grid=(nl, NMOD * D // tn),
        in_specs=[pl.BlockSpec((8, D), lambda l, j: (0, 0)),
                  pl.BlockSpec((None, D, tn), lambda l, j: (l, 0, j)),
                  pl.BlockSpec((None, 1, tn), lambda l, j: (l, 0, j))],
        out_specs=pl.BlockSpec((None, 8, tn), lambda l, j: (l, 0, j)),
        out_shape=jax.ShapeDtypeStruct((nl, 8, NMOD * D), F32),
        compiler_params=_cp(("parallel", "parallel")),
        name="adaln",
    )(c8, ada_w, ada_b.reshape(nl, 1, NMOD * D))


def _front_kernel(xp_ref, xs_ref, yp_ref, yn_ref, mod_ref, nrm_ref, mu_ref, wr_ref, wk_ref, wv_ref,
                  w1_ref, a1_ref, g1_ref, w2_ref, a2_ref, g2_ref, w0_ref, a0_ref, kkw_ref, bo_ref,
                  r_o, k_o, v_o, g_o, kk_o, lw0_o, lw1_o, ag0_o, ag1_o):
    i = pl.program_id(0)
    row = _tile_mod_row(i)
    q = (i - NT_P) % TPS
    first = jnp.logical_or(i < NT_P, q == 0)
    last = jnp.logical_or(i < NT_P, q == TPS - 1)
    sh = _mod_vec(mod_ref, row, 0)
    sc = _mod_vec(mod_ref, row, 1)
    g = nrm_ref[...]
    h = _modulate(jnp.where(i < NT_P, xp_ref[...], xs_ref[...]), g, sh, sc)
    hp = _modulate(yp_ref[...], g, sh, sc)[7:8]
    hn = _modulate(yn_ref[...], g, sh, sc)[0:1]
    hp = jnp.where(first, 0.0, hp)
    hn = jnp.where(last, 0.0, hn)
    rid = lax.broadcasted_iota(jnp.int32, (TM, D), 0)
    prev = jnp.where(rid == 0, hp, pltpu.roll(h, 1, 0))
    nxt = jnp.where(rid == TM - 1, hn, pltpu.roll(h, TM - 1, 0))
    dx = 0.5 * (prev + nxt) - h

    def mix(n):
        return (h + dx * mu_ref[n:n + 1, :]).astype(BF)

    r_o[...] = jnp.dot(mix(0), wr_ref[...], preferred_element_type=F32)
    xw = mix(1)
    lora_w = _bdot(jnp.tanh(jnp.dot(xw, w1_ref[...], preferred_element_type=F32)), w2_ref[...])
    hc = 0.5 * math.exp(-0.5)
    lw0_o[...] = -hc * jnp.tanh(0.5 * (w0_ref[0:1, :] + lora_w[:, :D])) - hc
    lw1_o[...] = -hc * jnp.tanh(0.5 * (w0_ref[1:2, :] + lora_w[:, D:])) - hc
    k_raw = jnp.dot(mix(2), wk_ref[...], preferred_element_type=F32)
    k_o[...] = k_raw
    kq = k_raw * kkw_ref[...]
    ss = _seg_sum(kq * kq, bo_ref[...])
    kk_o[...] = kq / jnp.maximum(jnp.sqrt(ss), 1e-12)
    v_o[...] = jnp.dot(mix(3), wv_ref[...], preferred_element_type=F32)
    xa = mix(4)
    lora_a = _bdot(jnp.dot(xa, a1_ref[...], preferred_element_type=F32), a2_ref[...])
    ag0_o[...] = 0.5 * jnp.tanh(0.5 * (a0_ref[0:1, :] + lora_a[:, :D])) + 0.5
    ag1_o[...] = 0.5 * jnp.tanh(0.5 * (a0_ref[1:2, :] + lora_a[:, D:])) + 0.5
    xg = mix(5)
    g_o[...] = _bdot(jax.nn.sigmoid(jnp.dot(xg, g1_ref[...], preferred_element_type=F32)), g2_ref[...])


def _rwkv_front(x_p, x_s, mod, nrm, mu, wr, wk, wv, w1c, a1c, g1, w2bd, a2bd, g2, w0, a0, kkw, bo):
    tile = pl.BlockSpec((TM, D), lambda i: (i, 0))
    nblk8 = NSR // 8
    bpt = TM // 8
    full = lambda a: pl.BlockSpec(a.shape, lambda i: (0,) * a.ndim)
    ins = [x_p, x_s, x_s, x_s, mod, nrm, mu, wr, wk, wv, w1c, a1c, g1, w2bd, a2bd, g2, w0, a0, kkw, bo]
    in_specs = [pl.BlockSpec((TM, D), lambda i: (jnp.minimum(i, NT_P - 1), 0)),
                pl.BlockSpec((TM, D), lambda i: (jnp.maximum(i - NT_P, 0), 0)),
                pl.BlockSpec((8, D), lambda i: (jnp.clip((i - NT_P) * bpt - 1, 0, nblk8 - 1), 0)),
                pl.BlockSpec((8, D), lambda i: (jnp.clip((i - NT_P + 1) * bpt, 0, nblk8 - 1), 0))]
    in_specs += [full(a) for a in ins[4:]]
    out = jax.ShapeDtypeStruct((NTOK, D), F32)
    return pl.pallas_call(
        _front_kernel,
        grid=(NT,),
        in_specs=in_specs,
        out_specs=[tile] * 9,
        out_shape=[out] * 9,
        compiler_params=_cp(("parallel",)),
        name="rwkv_front",
    )(*ins)


def _wkv_chunk(d, S, r, k, v, kk, lw, ag, ka, rk, cst):
    trow, strict, incl, h0, bd, bo2, same_blk, bd_sb, eye_c, lane_blk = cst
    dot = lambda a, b: jnp.dot(a, b, preferred_element_type=F32)
    nblk = CH // SB
    h0s = lax.broadcasted_iota(jnp.int32, (SB, LANES), 1) < HD

    def stack_heads(x, swap=False):
        zero = jnp.zeros_like(x)
        parts = [jnp.where(h0, x, zero), jnp.where(h0, zero, x)]
        return jnp.concatenate(parts[::-1] if swap else parts, axis=0)

    def dot_split(a, bm):
        return dot(a.astype(BF), bm.astype(BF))

    def expand(mc):
        return jnp.where(bd_sb, jnp.concatenate([mc] * (LANES // SB), axis=0), 0.0)

    b = kk * ag
    kd = k * (1.0 + (ag - 1.0) * ka)
    cs = lw
    for sh in [1 << n for n in range(int(math.log2(CH)))]:
        if d == 0:
            cs = cs + jnp.where(trow >= sh, pltpu.roll(cs, sh, 0), 0.0)
        else:
            cs = cs + jnp.where(trow < CH - sh, pltpu.roll(cs, CH - sh, 0), 0.0)
    qh, ql = _split2(r * kd * rk)
    qs = dot(jnp.concatenate([qh, ql], axis=0), bo2)
    bonus = (qs[:CH] + qs[CH:]) * v
    yield
    tot = cs[CH - 1:CH, :] if d == 0 else cs[0:1, :]
    p_inv = jnp.exp(-cs)
    p_end = jnp.exp(tot - cs)
    at = -kk * jnp.exp(cs - lw)
    rt = r * jnp.exp(cs)
    bt = (b * p_inv).astype(BF)
    kt = (kd * p_inv).astype(BF)
    ar = jnp.concatenate([at, rt], axis=0)
    h02 = jnp.concatenate([h0, h0], axis=0)
    g0 = _bdot_nt(jnp.where(h02, ar, 0.0), jnp.concatenate([bt, kt], axis=0))
    g1 = _bdot_nt(jnp.where(h02, 0.0, ar), jnp.concatenate([kt, bt], axis=0))
    w0y0 = _bdot_nt(ar, S)
    yield
    w0, y0 = w0y0[:CH], w0y0[CH:]
    lab = jnp.where(strict[d], jnp.where(h0, g0[:CH], g1[:CH]), 0.0)
    lak = jnp.where(strict[d], jnp.where(h0, g1[:CH], g0[:CH]), 0.0)
    mrb = jnp.where(incl[d], jnp.where(h0, g0[CH:], g1[CH:]), 0.0)
    mrk = jnp.where(incl[d], jnp.where(h0, g1[CH:], g0[CH:]), 0.0)
    vstk_sw = stack_heads(v, swap=True).astype(BF)
    w = w0 + dot(lak.astype(BF), vstk_sw)
    ldiag = jnp.where(same_blk, lab, 0.0)
    loff = jnp.where(same_blk, 0.0, lab)
    mc = ldiag[0:SB]
    for i in range(1, nblk):
        mc = mc + ldiag[i * SB:(i + 1) * SB]
    xc = eye_c + mc
    mc = dot_split(mc, expand(mc))
    yield
    nsq = int(math.log2(SB)) - 1
    for it in range(nsq):
        last = it + 1 == nsq
        res = dot_split(xc if last else jnp.concatenate([xc, mc], axis=0), expand(mc))
        xc = xc + res[:SB]
        if not last:
            mc = res[SB:]
        yield
    ublk = [None] * nblk
    order = list(range(nblk)) if d == 0 else list(range(nblk - 1, -1, -1))
    for n, i in enumerate(order):
        wi = w[i * SB:(i + 1) * SB]
        if n > 0:
            ucur = jnp.concatenate([jnp.zeros((SB, LANES), F32) if ub is None else ub
                                    for ub in ublk], axis=0)
            wi = wi + dot(loff[i * SB:(i + 1) * SB].astype(BF), stack_heads(ucur).astype(BF))
            yield
        m0 = jnp.where(h0s, wi, 0.0).astype(BF)
        m1 = jnp.where(h0s, 0.0, wi).astype(BF)
        rhs = jnp.concatenate([m0] * nblk + [m1] * nblk, axis=0)
        ublk[i] = dot(jnp.where(lane_blk[i], xc, 0.0).astype(BF), rhs)
        yield
    u = jnp.concatenate(ublk, axis=0)
    y = dot(jnp.concatenate([mrb, mrk], axis=1).astype(BF),
            jnp.concatenate([stack_heads(u).astype(BF), vstk_sw], axis=0))
    uvt = jnp.concatenate([u, v], axis=0).T
    ds = _bdot(uvt, jnp.concatenate([b * p_end, kd * p_end], axis=0))
    yield
    s_new = S * jnp.exp(tot) + jnp.where(bd, ds, 0.0)
    return s_new, y0 + y, bonus


def _run_lockstep(gens):
    results = [None] * len(gens)
    pending = list(range(len(gens)))
    while pending:
        for i in list(pending):
            try:
                next(gens[i])
            except StopIteration as stop:
                results[i] = stop.value
                pending.remove(i)
    return results


def _wkv_consts():
    t2 = lax.broadcasted_iota(jnp.int32, (CH, LANES), 0)
    l2 = lax.broadcasted_iota(jnp.int32, (CH, LANES), 1)
    s2 = l2 & (CH - 1)
    strict = [s2 < t2, s2 > t2]
    incl = [s2 <= t2, s2 >= t2]
    h0 = l2 < HD
    same_blk = (t2 // SB) == (s2 // SB)
    ri = lax.broadcasted_iota(jnp.int32, (LANES, LANES), 0)
    ci = lax.broadcasted_iota(jnp.int32, (LANES, LANES), 1)
    bd = (ri // HD) == (ci // HD)
    bo2 = jnp.where(bd, 1.0, 0.0).astype(BF)
    bd_sb = (ri // SB) == (ci // SB)
    t3 = lax.broadcasted_iota(jnp.int32, (SB, LANES), 0)
    l3 = lax.broadcasted_iota(jnp.int32, (SB, LANES), 1)
    eye_c = jnp.where((l3 & (SB - 1)) == t3, 1.0, 0.0)
    lane_blk = [((l3 & (CH - 1)) // SB) == i for i in range(CH // SB)]
    return t2, strict, incl, h0, bd, bo2, same_blk, bd_sb, eye_c, lane_blk


def _wkv_kernel(T, has_s0, *refs):
    (r_ref, k_ref, v_ref, kk_ref, lw0_ref, lw1_ref, ag0_ref, ag1_ref, ka_ref, rk_ref), rest = refs[:10], refs[10:]
    if has_s0:
        s0_ref, y_ref, bon_ref, s_scr = rest
    else:
        y_ref, bon_ref, st_ref, s_scr = rest
    nch = T // CH
    npair = y_ref.shape[1] // LANES
    cst = _wkv_consts()
    if has_s0:
        s_scr[...] = s0_ref[...]
    else:
        s_scr[...] = jnp.zeros_like(s_scr)
    y_ref[...] = jnp.zeros_like(y_ref)
    bon_ref[...] = jnp.zeros_like(bon_ref)
    lw_refs = (lw0_ref, lw1_ref)
    ag_refs = (ag0_ref, ag1_ref)

    def body(c, carry):
        chains = [(p, d) for p in range(npair) for d in range(2)]
        sl = {}
        for p, d in chains:
            r0 = pl.multiple_of((c if d == 0 else nch - 1 - c) * CH, CH)
            sl[p, d] = (pl.ds(r0, CH), pl.ds(p * LANES, LANES))
        args = {}
        for p, d in chains:
            rs, ls = sl[p, d]
            args[p, d] = (s_scr[d, p], r_ref[rs, ls], k_ref[rs, ls], v_ref[rs, ls], kk_ref[rs, ls],
                          lw_refs[d][rs, ls], ag_refs[d][rs, ls], ka_ref[:, ls], rk_ref[:, ls],
                          y_ref[rs, ls], bon_ref[rs, ls])
        outs = _run_lockstep([_wkv_chunk(d, *args[p, d][:9], cst) for p, d in chains])
        for (p, d), (s_new, y, bon) in zip(chains, outs):
            rs, ls = sl[p, d]
            s_scr[d, p] = s_new
            y_ref[rs, ls] = args[p, d][9] + y
            bon_ref[rs, ls] = args[p, d][10] + bon
        return carry

    lax.fori_loop(0, nch, body, 0)
    if not has_s0:
        for d in range(2):
            for p in range(npair):
                s = s_scr[d, p]
                st_ref[d, 2 * p] = s[:HD, :HD]
                st_ref[d, 2 * p + 1] = s[HD:, HD:]


def _wkv(arrs, ka, rk, T, nb, row_blk0, s0_bd=None, lanes=512):
    ng = D // lanes
    npair = lanes // LANES
    seq = pl.BlockSpec((T, lanes), lambda b, g: (row_blk0 + b, g))
    vec = pl.BlockSpec((1, lanes), lambda b, g: (0, g))
    st = pl.BlockSpec((None, 2, npair, LANES, LANES), lambda b, g: (b, 0, g, 0, 0))
    out_seq = pl.BlockSpec((T, lanes), lambda b, g: (b, g))
    in_specs = [seq] * 8 + [vec, vec]
    ins = list(arrs) + [ka, rk]
    out_specs = [out_seq, out_seq]
    out_shape = [jax.ShapeDtypeStruct((nb * T, D), F32)] * 2
    if s0_bd is not None:
        in_specs.append(st)
        ins.append(s0_bd)
    else:
        out_specs.append(pl.BlockSpec((None, 2, 2 * npair, HD, HD), lambda b, g: (b, 0, g, 0, 0)))
        out_shape.append(jax.ShapeDtypeStruct((nb, 2, NH, HD, HD), F32))
    return pl.pallas_call(
        functools.partial(_wkv_kernel, T, s0_bd is not None),
        grid=(nb, ng),
        in_specs=in_specs,
        out_specs=out_specs,
        out_shape=out_shape,
        scratch_shapes=[pltpu.VMEM((2, npair, LANES, LANES), F32)],
        compiler_params=_cp(("parallel", "parallel")),
        name="wkv_T%d" % T,
    )(*ins)


def _proj_kernel(gn, *refs):
    i = pl.program_id(0)
    is_p = i < NT_P
    if gn:
        (yp_ref, ys_ref, ap_ref, as_ref, bp_ref, bs_ref, g_ref, lng_ref, lnb_ref, bo_ref,
         mod_ref, w_ref, nrm_ref, wr_ref, br_ref, o_ref, xt_o, gate_o) = refs
        y = jnp.where(is_p, yp_ref[...], ys_ref[...])
    else:
        y_ref, ap_ref, as_ref, mod_ref, w_ref, nrm_ref, wr_ref, br_ref, o_ref, xt_o, gate_o = refs
        y = y_ref[...]
    a = jnp.where(is_p, ap_ref[...], as_ref[...])
    if gn:
        bo = bo_ref[...]
        mean = _seg_sum(a, bo) * (1.0 / HD)
        cen = a - mean
        var = _seg_sum(cen * cen, bo) * (1.0 / HD)
        yn = cen * lax.rsqrt(var + GN_EPS) * lng_ref[...] + lnb_ref[...]
        a = (yn + jnp.where(is_p, bp_ref[...], bs_ref[...])) * g_ref[...]
    gt = _mod_vec(mod_ref, _tile_mod_row(i), 2)
    y_new = y + gt * _bdot(a, w_ref[...])
    o_ref[...] = y_new
    row = _tile_mod_row(i)
    xt = _modulate(y_new, nrm_ref[...], _mod_vec(mod_ref, row, 3), _mod_vec(mod_ref, row, 4))
    xt_o[...] = xt.astype(BF)
    gate_o[...] = _route(xt, wr_ref[...], br_ref[...])


def _proj_residual(y, a_p, a_s, mod, w, router, gn_args=None, name="proj"):
    tile = pl.BlockSpec((TM, D), lambda i: (i, 0))
    tile_p = pl.BlockSpec((TM, D), lambda i: (jnp.minimum(i, NT_P - 1), 0))
    tile_s = pl.BlockSpec((TM, D), lambda i: (jnp.maximum(i - NT_P, 0), 0))
    full = lambda a: pl.BlockSpec(a.shape, lambda i: (0,) * a.ndim)
    if gn_args is not None:
        ins = [y[0], y[1], a_p, a_s]
        in_specs = [tile_p, tile_s, tile_p, tile_s]
    else:
        ins = [y, a_p, a_s]
        in_specs = [tile, tile_p, tile_s]
    if gn_args is not None:
        b_p, b_s, g, lng, lnb, bo = gn_args
        ins += [b_p, b_s, g, lng, lnb, bo]
        in_specs += [tile_p, tile_s, tile, full(lng), full(lnb), full(bo)]
    ins += [mod, w] + list(router)
    in_specs += [full(mod), full(w)] + [full(a) for a in router]
    return pl.pallas_call(
        functools.partial(_proj_kernel, gn_args is not None),
        grid=(NT,),
        in_specs=in_specs,
        out_specs=[tile, tile, pl.BlockSpec((TM, LANES), lambda i: (i, 0))],
        out_shape=[jax.ShapeDtypeStruct((NTOK, D), F32), jax.ShapeDtypeStruct((NTOK, D), BF),
                   jax.ShapeDtypeStruct((NTOK, LANES), F32)],
        compiler_params=_cp(("parallel",)),
        name=name,
    )(*ins)


def _route(xt, wr, br):
    x1, x2 = _split2(xt)
    w1, w2 = _split2(wr)
    dot = lambda a, b: jnp.dot(a, b, preferred_element_type=F32)
    logits = (dot(x2, w1) + dot(x1, w2)) + dot(x1, w1)
    logits = logits + br
    lane = lax.broadcasted_iota(jnp.int32, logits.shape, 1).astype(F32)
    isg = jnp.logical_and(lane >= NE, lane < NE + NE // EPG)
    mg = jnp.max(jnp.where(isg, logits, NEG), axis=-1, keepdims=True)
    eg = jnp.where(isg, jnp.exp(jnp.minimum(logits - mg, 0.0)), 0.0)
    p_sel = 1.0 / jnp.sum(eg, axis=-1, keepdims=True)
    gidx = jnp.min(jnp.where(jnp.logical_and(isg, logits == mg), lane, 1e3), axis=-1, keepdims=True) - NE
    lo_l = gidx * EPG
    ing = jnp.logical_and(lane >= lo_l, lane < lo_l + EPG)
    me = jnp.max(jnp.where(ing, logits, NEG), axis=-1, keepdims=True)
    ee = jnp.where(ing, jnp.exp(jnp.minimum(logits - me, 0.0)), 0.0)
    se = jnp.sum(ee, axis=-1, keepdims=True)
    pe = ee / se
    i1 = jnp.min(jnp.where(jnp.logical_and(ing, logits == me), lane, 1e3), axis=-1, keepdims=True)
    v1 = 1.0 / se
    rest = jnp.logical_and(ing, lane != i1)
    v2 = jnp.max(jnp.where(rest, pe, -1.0), axis=-1, keepdims=True)
    i2 = jnp.min(jnp.where(jnp.logical_and(rest, pe == v2), lane, 1e3), axis=-1, keepdims=True)
    den = v1 + v2
    return (jnp.where(lane == i1, p_sel * v1 / den, 0.0)
            + jnp.where(lane == i2, p_sel * v2 / den, 0.0)
            + jnp.where(lane == float(GRP_LANE), gidx, 0.0))


def _experts_kernel(final, blk_ref, xt_ref, gate_ref, w1_ref, w3_ref, w2_ref, y_ref, mod_ref, fin_ref, *rest):
    i = pl.program_id(0)
    g = pl.program_id(1)
    ngrp = NE // EPS
    if final:
        op_ref, os_ref, xs_scr, gs_scr, outs_scr, pos_scr = rest
    else:
        o_ref, xs_scr, gs_scr, outs_scr, pos_scr = rest
    ntile_p = NPR // TMX
    dot = lambda a, b: jnp.dot(a, b, preferred_element_type=F32)

    @pl.when(g == 0)
    def _():
        gate = gate_ref[...]
        lane = lax.broadcasted_iota(jnp.int32, gate.shape, 1)
        grp = jnp.sum(jnp.where(lane == GRP_LANE, gate, 0.0), axis=-1, keepdims=True)
        onehot = jnp.where(lane.astype(F32) == grp, 1.0, 0.0)
        tr = lax.broadcasted_iota(jnp.int32, (TMX, TMX), 0)
        br = lax.broadcasted_iota(jnp.int32, (LANES, LANES), 0)
        bc = lax.broadcasted_iota(jnp.int32, (LANES, LANES), 1)
        ltri = jnp.where(bc < br, 1.0, 0.0).astype(BF)
        carry = jnp.zeros((1, LANES), F32)
        ranks = []
        for blk in range(TMX // LANES):
            oh = onehot[blk * LANES:(blk + 1) * LANES]
            ranks.append(dot(ltri, oh.astype(BF)) + carry)
            carry = carry + jnp.sum(oh, axis=0, keepdims=True)
        rank = jnp.concatenate(ranks, axis=0)
        cnt = jnp.broadcast_to(carry, (8, LANES))
        lane8 = lax.broadcasted_iota(jnp.int32, (8, LANES), 1)
        off = jnp.zeros((8, LANES), F32)
        for k in range(1, ngrp):
            off = off + jnp.where(lane8 >= k, pltpu.roll(cnt, k, 1), 0.0)
        pos = jnp.sum(onehot * (off[0:1] + rank), axis=-1, keepdims=True)
        posb = jnp.broadcast_to(pos, (TMX, LANES))
        pos_scr[...] = posb
        pos_row = posb.T[0:1, :]
        perm = jnp.where(tr.astype(F32) == pos_row, 1.0, 0.0).astype(BF)
        xs_scr[...] = dot(perm, xt_ref[...]).astype(BF)
        g1, g2 = _split2(gate)
        gsel = dot(perm, jnp.concatenate([g1, g2], axis=1))
        gs_scr[...] = gsel[:, :LANES] + gsel[:, LANES:]
        outs_scr[...] = jnp.zeros_like(outs_scr)

    blo = blk_ref[2 * (i * ngrp + g)]
    bhi = blk_ref[2 * (i * ngrp + g) + 1]
    for b in range(TMX // SBR):
        @pl.when(jnp.logical_and(b >= blo, b < bhi))
        def _(b=b):
            rows = pl.ds(b * SBR, SBR)
            xb = xs_scr[rows, :]
            gsb = gs_scr[rows, :]
            lane_c = lax.broadcasted_iota(jnp.int32, gsb.shape, 1)
            hids = []
            for j in range(EPS):
                h1 = dot(xb, w1_ref[j])
                h3 = dot(xb, w3_ref[j])
                ge = jnp.sum(jnp.where(lane_c == g * EPS + j, gsb, 0.0), axis=-1, keepdims=True)
                hids.append(((h1 * jax.nn.sigmoid(h1)) * h3 * ge).astype(BF))
            outs_scr[rows, :] += dot(jnp.concatenate(hids, axis=1), w2_ref[...].reshape(EPS * DE, D))

    def result():
        row = jnp.where(i < ntile_p, 0, i - ntile_p + 1)
        tcol = lax.broadcasted_iota(jnp.int32, (TMX, TMX), 1).astype(F32)
        back = jnp.where(tcol == pos_scr[:, 0:1], 1.0, 0.0).astype(BF)
        out = y_ref[...] + _mod_vec(mod_ref, row, 5) * dot(back, outs_scr[...].astype(BF))
        if final:
            ms = jnp.mean(out * out, axis=-1, keepdims=True)
            out = out * lax.rsqrt(ms + NORM_EPS) * fin_ref[...]
        return out

    if final:
        @pl.when(jnp.logical_and(g == NE // EPS - 1, i < ntile_p))
        def _():
            op_ref[...] = result()

        @pl.when(jnp.logical_and(g == NE // EPS - 1, i >= ntile_p))
        def _():
            os_ref[...] = result()
    else:
        @pl.when(g == NE // EPS - 1)
        def _():
            o_ref[...] = result()


def _moe_experts(nch, xt, gate, w1, w3, w2, layer, y, mod, fin, final):
    tile = pl.BlockSpec((TMX, D), lambda i, e, n: (i, 0))
    full = lambda a: pl.BlockSpec(a.shape, lambda i, e, n: (0,) * a.ndim)
    ntile_p = NPR // TMX
    if final:
        out_specs = [pl.BlockSpec((TMX, D), lambda i, e, n: (jnp.minimum(i, ntile_p - 1), 0)),
                     pl.BlockSpec((TMX, D), lambda i, e, n: (jnp.maximum(i - ntile_p, 0), 0))]
        out_shape = [jax.ShapeDtypeStruct((NPR, D), F32), jax.ShapeDtypeStruct((NSR, D), F32)]
    else:
        out_specs = tile
        out_shape = jax.ShapeDtypeStruct((NTOK, D), F32)
    scratch = [pltpu.VMEM((TMX, D), BF), pltpu.VMEM((TMX, LANES), F32), pltpu.VMEM((TMX, D), F32),
               pltpu.VMEM((TMX, LANES), F32)]
    grid_spec = pltpu.PrefetchScalarGridSpec(
        num_scalar_prefetch=1,
        grid=(NTOK // TMX, NE // EPS),
        in_specs=[tile, pl.BlockSpec((TMX, LANES), lambda i, e, n: (i, 0)),
                  pl.BlockSpec((None, EPS, D, DE), lambda i, e, n: (layer, e, 0, 0)),
                  pl.BlockSpec((None, EPS, D, DE), lambda i, e, n: (layer, e, 0, 0)),
                  pl.BlockSpec((None, EPS, DE, D), lambda i, e, n: (layer, e, 0, 0)),
                  tile, full(mod), full(fin)],
        out_specs=out_specs,
        scratch_shapes=scratch)
    return pl.pallas_call(
        functools.partial(_experts_kernel, final),
        grid_spec=grid_spec,
        out_shape=out_shape,
        compiler_params=_cp(("arbitrary" if final else "parallel", "arbitrary")),
        name="moe_experts",
    )(nch, xt, gate, w1, w3, w2, y, mod, fin)


def _qkv_kernel(y_ref, mod_ref, nrm_ref, w_ref, q_o, kp_o, vp_o, ks_o, vs_o, kc_o, vc_o):
    i = pl.program_id(0)
    row = _tile_mod_row(_qkv_tile(i))
    h = _modulate(y_ref[...], nrm_ref[...], _mod_vec(mod_ref, row, 0), _mod_vec(mod_ref, row, 1))
    hb = h.astype(BF)
    kx = jnp.dot(hb, w_ref[:, D:2 * D], preferred_element_type=F32)
    vx = jnp.dot(hb, w_ref[:, 2 * D:], preferred_element_type=F32)
    q_o[...] = jnp.dot(hb, w_ref[:, :D], preferred_element_type=F32)

    @pl.when(i < NT - NT_P)
    def _():
        ks_o[...] = kx
        vs_o[...] = vx

    @pl.when(i >= NT - NT_P)
    def _():
        kp_o[...] = kx
        vp_o[...] = vx
        for hh in range(NH):
            kc_o[pl.ds(hh, TM, stride=NH), :] = kx[:, hh * HD:(hh + 1) * HD]
            vc_o[pl.ds(hh, TM, stride=NH), :] = vx[:, hh * HD:(hh + 1) * HD]


def _qkv_tile(i):
    return jnp.where(i < NT - NT_P, NT_P + i, i - (NT - NT_P))


def _na_qkv(y, mod, nrm, w):
    nt_s = NT - NT_P
    tile = pl.BlockSpec((TM, D), lambda i: (_qkv_tile(i), 0))
    tile_p = pl.BlockSpec((TM, D), lambda i: (jnp.maximum(i - nt_s, 0), 0))
    tile_s = pl.BlockSpec((TM, D), lambda i: (jnp.minimum(i, nt_s - 1), 0))
    full = lambda a: pl.BlockSpec(a.shape, lambda i: (0,) * a.ndim)
    out = lambda n: jax.ShapeDtypeStruct((n, D), F32)
    cache = jax.ShapeDtypeStruct((NPR * NH, HD), F32)
    cache_tile = pl.BlockSpec((TM * NH, HD), lambda i: (jnp.maximum(i - nt_s, 0), 0))
    return pl.pallas_call(
        _qkv_kernel,
        grid=(NT,),
        in_specs=[tile, full(mod), full(nrm), full(w)],
        out_specs=[tile, tile_p, tile_p, tile_s, tile_s, cache_tile, cache_tile],
        out_shape=[out(NTOK), out(NPR), out(NPR), out(NSR), out(NSR), cache, cache],
        compiler_params=_cp(("arbitrary",)),
        name="na_qkv",
    )(y, mod, nrm, w)


def _softmax_rows(s):
    m = jnp.max(s, axis=-1, keepdims=True)
    e = jnp.exp(s - m)
    return e / jnp.sum(e, axis=-1, keepdims=True)


def _ctx_attn_kernel(q_ref, k_ref, v_ref, o_ref):
    scale = HD ** -0.5
    h0 = lax.broadcasted_iota(jnp.int32, (T_P, LANES), 1) < HD
    def head_chain(h, q, kb, vb):
        qm = jnp.where(h0 if h == 0 else jnp.logical_not(h0), q, 0.0)
        s = _bdot_nt(qm, kb)
        yield
        o = jnp.dot(_softmax_rows(s * scale).astype(BF), vb, preferred_element_type=F32)
        yield
        return o

    npair = D // LANES
    for p0 in range(0, npair, CTX_PAIRS):
        gens = []
        for p in range(p0, p0 + CTX_PAIRS):
            ls = pl.ds(p * LANES, LANES)
            q = q_ref[:, ls]
            kb = k_ref[:, ls].astype(BF)
            vb = v_ref[:, ls].astype(BF)
            gens += [head_chain(h, q, kb, vb) for h in range(2)]
        outs = _run_lockstep(gens)
        for j, p in enumerate(range(p0, p0 + CTX_PAIRS)):
            o_ref[:, pl.ds(p * LANES, LANES)] = jnp.where(h0, outs[2 * j], outs[2 * j + 1])


def _ctx_attn(q, k, v):
    blk = pl.BlockSpec((T_P, D), lambda b: (b, 0))
    return pl.pallas_call(
        _ctx_attn_kernel,
        grid=(NB_P,),
        in_specs=[blk] * 3,
        out_specs=blk,
        out_shape=jax.ShapeDtypeStruct((NPR, D), F32),
        compiler_params=_cp(("parallel",)),
        name="ctx_attn",
    )(q, k, v)


def _nbr_attn_kernel(q_ref, k_ref, v_ref, kc_ref, vc_ref, p_ref, o_ref, tz_ref):
    scale = HD ** -0.5
    rows = T_S // GRID_W
    nloc = WIN_R * GRID_W
    qc = lax.broadcasted_iota(jnp.int32, (GRID_W, LANES), 0)
    ln = lax.broadcasted_iota(jnp.int32, (GRID_W, LANES), 1)
    h0 = ln < HD
    kcb = kc_ref[...].astype(BF)
    vcb = vc_ref[...].astype(BF)
    kcol = ln & (GRID_W - 1)
    cstart = jnp.clip(qc - WIN_C // 2, 0, GRID_W - WIN_C)
    valid = jnp.logical_and(kcol >= cstart, kcol < cstart + WIN_C)
    for h in range(2):
        rolled = []
        for j in range(2 * WIN_R - 1):
            prow = jnp.broadcast_to(p_ref[h, j:j + 1, :], (GRID_W, LANES))
            rolled.append((pltpu.roll(prow, 0, 1, stride=1, stride_axis=0),
                           pltpu.roll(prow, GRID_W, 1, stride=1, stride_axis=0)))
        for j in range(2 * WIN_R - 2):
            tz_ref[h, j] = jnp.where(valid, jnp.where(h0, rolled[j][0], rolled[j + 1][1]), NEG)

    def head_chain(h, q, klb, vlb, j0):
        qm = jnp.where(h0 if h == 0 else jnp.logical_not(h0), q, 0.0).astype(BF)
        sl = _bdot_nt(qm, klb)
        sc = _bdot_nt(qm, kcb)
        yield
        bias = jnp.concatenate([tz_ref[h, j0 + 2 * m] for m in range(WIN_R // 2)], axis=1)
        sl = sl * scale + bias
        sc = sc * scale
        m = jnp.maximum(jnp.max(sl, axis=-1, keepdims=True), jnp.max(sc, axis=-1, keepdims=True))
        el = jnp.exp(sl - m)
        ec = jnp.exp(sc - m)
        inv = 1.0 / (jnp.sum(el, axis=-1, keepdims=True) + jnp.sum(ec, axis=-1, keepdims=True))
        o = (jnp.dot((el * inv).astype(BF), vlb, preferred_element_type=F32)
             + jnp.dot((ec * inv).astype(BF), vcb, preferred_element_type=F32))
        yield
        return o

    def body(it, carry):
        gens, slices = [], []
        for j in range(NBR_UNROLL):
            r = it * NBR_UNROLL + j
            start = jnp.clip(r - WIN_R // 2, 0, rows - WIN_R)
            qs = pl.ds(pl.multiple_of(r * GRID_W, GRID_W), GRID_W)
            ks = pl.ds(pl.multiple_of(start * GRID_W, GRID_W), nloc)
            q = q_ref[qs, :]
            klb = k_ref[ks, :].astype(BF)
            vlb = v_ref[ks, :].astype(BF)
            j0 = start - r + WIN_R - 1
            slices.append(qs)
            gens += [head_chain(h, q, klb, vlb, j0) for h in range(2)]
        outs = _run_lockstep(gens)
        for j, qs in enumerate(slices):
            o_ref[qs, :] = jnp.where(h0, outs[2 * j], outs[2 * j + 1])
        return carry

    lax.fori_loop(0, rows // NBR_UNROLL, body, 0)


def _nbr_attn(q, k, v, kc, vc, ptab):
    npair = D // LANES
    nrow = ptab.shape[1]
    qseq = pl.BlockSpec((T_S, LANES), lambda b, p: (NPR // T_S + b, p))
    seq = pl.BlockSpec((T_S, LANES), lambda b, p: (b, p))
    ctx = pl.BlockSpec((kc.shape[0] // NB_S, LANES), lambda b, p: (b, p))
    return pl.pallas_call(
        _nbr_attn_kernel,
        grid=(NB_S, npair),
        in_specs=[qseq, seq, seq, ctx, ctx,
                  pl.BlockSpec((2, nrow, LANES), lambda b, p: (p, 0, 0))],
        out_specs=pl.BlockSpec((T_S, LANES), lambda b, p: (b, p)),
        out_shape=jax.ShapeDtypeStruct((NSR, D), F32),
        scratch_shapes=[pltpu.VMEM((2, nrow - 1, GRID_W, LANES), F32)],
        compiler_params=_cp(("parallel", "parallel")),
        name="nbr_attn",
    )(q, k, v, kc, vc, ptab)


def _block_diag2(a, b):
    z = jnp.zeros_like(a)
    return jnp.concatenate([jnp.concatenate([a, z], axis=1), jnp.concatenate([z, b], axis=1)], axis=0)


def _nbr_bias_table(rpb):
    pad = jnp.zeros(rpb.shape[:2] + (LANES - 2 * WIN_C + 1,), F32)
    return jnp.concatenate([rpb[:, :, WIN_C - 1:], pad, rpb[:, :, :WIN_C - 1]], axis=-1)


def kernel(x_prompt, x_sample, c, state_rwkv, cache_na_k, cache_na_v, c_ctx, norm_mix, norm_ffn, ada_w, ada_b, rwkv_mu, rwkv_w_r, rwkv_w_k, rwkv_w_v, rwkv_w_o, rwkv_w0, rwkv_w1, rwkv_w2, rwkv_a0, rwkv_a1, rwkv_a2, rwkv_g1, rwkv_g2, rwkv_k_k, rwkv_k_a, rwkv_r_k, rwkv_lnx_g, rwkv_lnx_b, na_w_qkv, na_w_o, na_rpb, moe_w_grp, moe_b_grp, moe_w_exp, moe_b_exp, moe_w1, moe_w3, moe_w2, final_norm):
    x_p = x_prompt.reshape(NPR, D)
    x_s = x_sample.reshape(NSR, D)
    c8 = jnp.concatenate([c_ctx[None, :], c, jnp.zeros((8 - 1 - NB_S, D), F32)], axis=0)
    mod = _adaln(c8, ada_w, ada_b)
    ri = jnp.arange(2 * LANES)[:, None] // HD
    bo = (ri == ri.T).astype(BF)
    row = lambda a: a.reshape(1, D)

    w1c = jnp.concatenate([rwkv_w1[0, 0], rwkv_w1[0, 1]], axis=1).astype(BF)
    a1c = jnp.concatenate([rwkv_a1[0, 0], rwkv_a1[0, 1]], axis=1).astype(BF)
    w2bd = _block_diag2(rwkv_w2[0, 0], rwkv_w2[0, 1]).astype(BF)
    a2bd = _block_diag2(rwkv_a2[0, 0], rwkv_a2[0, 1]).astype(BF)
    r, k, v, g, kk, lw0, lw1, ag0, ag1 = _rwkv_front(
        x_p, x_s, mod[0], row(norm_mix[0]), rwkv_mu[0], rwkv_w_r[0].astype(BF), rwkv_w_k[0].astype(BF),
        rwkv_w_v[0].astype(BF), w1c, a1c, rwkv_g1[0].astype(BF), w2bd, a2bd, rwkv_g2[0].astype(BF),
        rwkv_w0[0], rwkv_a0[0], row(rwkv_k_k[0]), bo)
    arrs = (r, k, v, kk, lw0, lw1, ag0, ag1)
    ka, rk = row(rwkv_k_a[0]), row(rwkv_r_k[0])
    ys_p, bon_p, new_state = _wkv(arrs, ka, rk, T_P, NB_P, 0, lanes=D)
    s0 = state_rwkv[:, 0].reshape(NB_S, 2, NH // 2, 2, HD, HD)
    z = jnp.zeros_like(s0[:, :, :, 0])
    s0_bd = jnp.concatenate([jnp.concatenate([s0[:, :, :, 0], z], axis=-1),
                             jnp.concatenate([z, s0[:, :, :, 1]], axis=-1)], axis=-2)
    ys_s, bon_s = _wkv(arrs, ka, rk, T_S, NB_S, NPR // T_S, s0_bd=s0_bd)
    new_state_rwkv = new_state.reshape(NB_P, 1, 2, NH, HD, HD)
    def router(i):
        wr = jnp.concatenate([moe_w_exp[i], moe_w_grp[i],
                              jnp.zeros((D, LANES - NE - NE // EPG), F32)], axis=1)
        br = jnp.concatenate([moe_b_exp[i], moe_b_grp[i],
                              jnp.zeros((LANES - NE - NE // EPG,), F32)]).reshape(1, LANES)
        return (row(norm_ffn[i]), wr, br)

    def experts(y, xt, gate, i, final):
        grp = gate[:, GRP_LANE].astype(jnp.int32).reshape(NTOK // TMX, TMX)
        cnt = jnp.sum(grp[:, :, None] == jnp.arange(NE // EPG)[None, None, :], axis=1).astype(jnp.int32)
        off = jnp.cumsum(cnt, axis=1) - cnt
        blo = off // SBR
        bhi = jnp.where(cnt > 0, (off + cnt - 1) // SBR + 1, blo)
        blk = jnp.stack([blo, bhi], axis=-1).astype(jnp.int32).reshape(-1)
        return _moe_experts(blk, xt, gate, w1b, w3b, w2b, i, y, mod[i], row(final_norm), final)

    w1b, w3b, w2b = moe_w1.astype(BF), moe_w3.astype(BF), moe_w2.astype(BF)

    y, xt, gate = _proj_residual((x_p, x_s), ys_p, ys_s, mod[0], rwkv_w_o[0].astype(BF), router(0),
                                 gn_args=(bon_p, bon_s, g, row(rwkv_lnx_g[0]), row(rwkv_lnx_b[0]), bo),
                                 name="rwkv_out")
    y = experts(y, xt, gate, 0, False)

    q, k_p, v_p, k_s, v_s, kcache, vcache = _na_qkv(y, mod[1], row(norm_mix[1]), na_w_qkv[0].astype(BF))
    o_p = _ctx_attn(q, k_p, v_p)
    kc = cache_na_k[:, 0].reshape(NB_S * cache_na_k.shape[2], D)
    vc = cache_na_v[:, 0].reshape(NB_S * cache_na_v.shape[2], D)
    o_s = _nbr_attn(q, k_s, v_s, kc, vc, _nbr_bias_table(na_rpb[0]))
    y, xt, gate = _proj_residual(y, o_p, o_s, mod[1], na_w_o[0].astype(BF), router(1), name="attn_out")
    y_p, y_s = experts(y, xt, gate, 1, True)

    y_prompt = y_p.reshape(NB_P, T_P, D)
    y_sample = y_s.reshape(NB_S, T_S, D)
    new_k = kcache.reshape(NB_P, 1, T_P, NH, HD)
    new_v = vcache.reshape(NB_P, 1, T_P, NH, HD)
    return (y_prompt, y_sample, new_state_rwkv, new_k, new_v)
```

```python
import functools
import math

import jax
import jax.numpy as jnp
from jax import lax
from jax.experimental import pallas as pl
from jax.experimental.pallas import tpu as pltpu

F32 = jnp.float32
BF = jnp.bfloat16

D = 1024
NH = 16
HD = 64
NB_P, T_P = 32, 256
NB_S, T_S = 2, 1024
NPR = NB_P * T_P
NSR = NB_S * T_S
NTOK = NPR + NSR
TM = 256
NT = NTOK // TM
NT_P = NPR // TM
TPS = T_S // TM
TMX = 1024
NMOD = 6
NE = 16
EPG = 4
DE = 256
EPS = EPG
SBR = 128
GRP_LANE = 127
CH = 64
SB = 32
LANES = 128
GRID_W = 64
WIN_R, WIN_C = 8, 16
NBR_UNROLL = 4
CTX_PAIRS = 4
NORM_EPS = 1e-6
GN_EPS = 64e-5
NEG = -1e30
VMEM_LIMIT = 56 * 1024 * 1024


def _cp(sem):
    return pltpu.CompilerParams(dimension_semantics=sem, vmem_limit_bytes=VMEM_LIMIT)


def _bdot(a, b):
    return jnp.dot(a.astype(BF), b.astype(BF), preferred_element_type=F32)


def _bdot_nt(a, b):
    return lax.dot_general(a.astype(BF), b.astype(BF), (((1,), (1,)), ((), ())),
                           preferred_element_type=F32)


def _split2(x):
    hi = x.astype(BF)
    lo = (x - hi.astype(F32)).astype(BF)
    return hi, lo


def _split3(x):
    hi = x.astype(BF)
    r1 = x - hi.astype(F32)
    mid = r1.astype(BF)
    lo = (r1 - mid.astype(F32)).astype(BF)
    return hi, mid, lo


def _seg_sum(x, bo):
    outs = []
    for c in range(x.shape[1] // 256):
        hi, lo = _split2(x[:, c * 256:(c + 1) * 256])
        outs.append(jnp.dot(hi, bo, preferred_element_type=F32)
                    + jnp.dot(lo, bo, preferred_element_type=F32))
    return outs[0] if len(outs) == 1 else jnp.concatenate(outs, axis=1)


def _tile_mod_row(i):
    return jnp.where(i < NT_P, 0, 1 + (i - NT_P) // TPS)


def _modulate(x, g, sh, sc):
    ms = jnp.mean(x * x, axis=-1, keepdims=True)
    return x * lax.rsqrt(ms + NORM_EPS) * g * (1.0 + sc) + sh


def _mod_vec(mod_ref, row, k):
    return mod_ref[pl.ds(row, 1), pl.ds(k * D, D)]


def _adaln_kernel(c_ref, w_ref, b_ref, o_ref):
    c = c_ref[...]
    s = c * jax.nn.sigmoid(c)
    o_ref[...] = _bdot(s, w_ref[...]) + b_ref[...]


def _adaln(c8, ada_w, ada_b):
    nl = ada_w.shape[0]
    tn = 1536
    return pl.pallas_call(
        _adaln_kernel,
        grid=(nl, NMOD * D // tn),
        in_specs=[pl.BlockSpec((8, D), lambda l, j: (0, 0)),
                  pl.BlockSpec((None, D, tn), lambda l, j: (l, 0, j)),
                  pl.BlockSpec((None, 1, tn), lambda l, j: (l, 0, j))],
        out_specs=pl.BlockSpec((None, 8, tn), lambda l, j: (l, 0, j)),
        out_shape=jax.ShapeDtypeStruct((nl, 8, NMOD * D), F32),
        compiler_params=_cp(("parallel", "parallel")),
        name="adaln",
    )(c8, ada_w, ada_b.reshape(nl, 1, NMOD * D))


def _front_kernel(xp_ref, xs_ref, yp_ref, yn_ref, mod_ref, nrm_ref, mu_ref, wr_ref, wk_ref, wv_ref,
                  w1_ref, a1_ref, g1_ref, w2_ref, a2_ref, g2_ref, w0_ref, a0_ref, kkw_ref, bo_ref,
                  r_o, k_o, v_o, g_o, kk_o, lw0_o, lw1_o, ag0_o, ag1_o):
    i = pl.program_id(0)
    row = _tile_mod_row(i)
    q = (i - NT_P) % TPS
    first = jnp.logical_or(i < NT_P, q == 0)
    last = jnp.logical_or(i < NT_P, q == TPS - 1)
    sh = _mod_vec(mod_ref, row, 0)
    sc = _mod_vec(mod_ref, row, 1)
    g = nrm_ref[...]
    h = _modulate(jnp.where(i < NT_P, xp_ref[...], xs_ref[...]), g, sh, sc)
    hp = _modulate(yp_ref[...], g, sh, sc)[7:8]
    hn = _modulate(yn_ref[...], g, sh, sc)[0:1]
    hp = jnp.where(first, 0.0, hp)
    hn = jnp.where(last, 0.0, hn)
    rid = lax.broadcasted_iota(jnp.int32, (TM, D), 0)
    prev = jnp.where(rid == 0, hp, pltpu.roll(h, 1, 0))
    nxt = jnp.where(rid == TM - 1, hn, pltpu.roll(h, TM - 1, 0))
    dx = 0.5 * (prev + nxt) - h

    def mix(n):
        return (h + dx * mu_ref[n:n + 1, :]).astype(BF)

    r_o[...] = jnp.dot(mix(0), wr_ref[...], preferred_element_type=F32)
    xw = mix(1)
    lora_w = _bdot(jnp.tanh(jnp.dot(xw, w1_ref[...], preferred_element_type=F32)), w2_ref[...])
    hc = 0.5 * math.exp(-0.5)
    lw0_o[...] = -hc * jnp.tanh(0.5 * (w0_ref[0:1, :] + lora_w[:, :D])) - hc
    lw1_o[...] = -hc * jnp.tanh(0.5 * (w0_ref[1:2, :] + lora_w[:, D:])) - hc
    k_raw = jnp.dot(mix(2), wk_ref[...], preferred_element_type=F32)
    k_o[...] = k_raw
    kq = k_raw * kkw_ref[...]
    ss = _seg_sum(kq * kq, bo_ref[...])
    kk_o[...] = kq / jnp.maximum(jnp.sqrt(ss), 1e-12)
    v_o[...] = jnp.dot(mix(3), wv_ref[...], preferred_element_type=F32)
    xa = mix(4)
    lora_a = _bdot(jnp.dot(xa, a1_ref[...], preferred_element_type=F32), a2_ref[...])
    ag0_o[...] = 0.5 * jnp.tanh(0.5 * (a0_ref[0:1, :] + lora_a[:, :D])) + 0.5
    ag1_o[...] = 0.5 * jnp.tanh(0.5 * (a0_ref[1:2, :] + lora_a[:, D:])) + 0.5
    xg = mix(5)
    g_o[...] = _bdot(jax.nn.sigmoid(jnp.dot(xg, g1_ref[...], preferred_element_type=F32)), g2_ref[...])


def _rwkv_front(x_p, x_s, mod, nrm, mu, wr, wk, wv, w1c, a1c, g1, w2bd, a2bd, g2, w0, a0, kkw, bo):
    tile = pl.BlockSpec((TM, D), lambda i: (i, 0))
    nblk8 = NSR // 8
    bpt = TM // 8
    full = lambda a: pl.BlockSpec(a.shape, lambda i: (0,) * a.ndim)
    ins = [x_p, x_s, x_s, x_s, mod, nrm, mu, wr, wk, wv, w1c, a1c, g1, w2bd, a2bd, g2, w0, a0, kkw, bo]
    in_specs = [pl.BlockSpec((TM, D), lambda i: (jnp.minimum(i, NT_P - 1), 0)),
                pl.BlockSpec((TM, D), lambda i: (jnp.maximum(i - NT_P, 0), 0)),
                pl.BlockSpec((8, D), lambda i: (jnp.clip((i - NT_P) * bpt - 1, 0, nblk8 - 1), 0)),
                pl.BlockSpec((8, D), lambda i: (jnp.clip((i - NT_P + 1) * bpt, 0, nblk8 - 1), 0))]
    in_specs += [full(a) for a in ins[4:]]
    out = jax.ShapeDtypeStruct((NTOK, D), F32)
    return pl.pallas_call(
        _front_kernel,
        grid=(NT,),
        in_specs=in_specs,
        out_specs=[tile] * 9,
        out_shape=[out] * 9,
        compiler_params=_cp(("parallel",)),
        name="rwkv_front",
    )(*ins)


def _wkv_chunk(d, S, r, k, v, kk, lw, ag, ka, rk, cst):
    trow, strict, incl, h0, bd, bo2, same_blk, bd_sb, eye_c, lane_blk = cst
    dot = lambda a, b: jnp.dot(a, b, preferred_element_type=F32)
    nblk = CH // SB
    h0s = lax.broadcasted_iota(jnp.int32, (SB, LANES), 1) < HD

    def stack_heads(x, swap=False):
        zero = jnp.zeros_like(x)
        parts = [jnp.where(h0, x, zero), jnp.where(h0, zero, x)]
        return jnp.concatenate(parts[::-1] if swap else parts, axis=0)

    def dot_split(a, bm):
        return dot(a.astype(BF), bm.astype(BF))

    def expand(mc):
        return jnp.where(bd_sb, jnp.concatenate([mc] * (LANES // SB), axis=0), 0.0)

    b = kk * ag
    kd = k * (1.0 + (ag - 1.0) * ka)
    cs = lw
    for sh in [1 << n for n in range(int(math.log2(CH)))]:
        if d == 0:
            cs = cs + jnp.where(trow >= sh, pltpu.roll(cs, sh, 0), 0.0)
        else:
            cs = cs + jnp.where(trow < CH - sh, pltpu.roll(cs, CH - sh, 0), 0.0)
    qh, ql = _split2(r * kd * rk)
    qs = dot(jnp.concatenate([qh, ql], axis=0), bo2)
    bonus = (qs[:CH] + qs[CH:]) * v
    yield
    tot = cs[CH - 1:CH, :] if d == 0 else cs[0:1, :]
    p_inv = jnp.exp(-cs)
    p_end = jnp.exp(tot - cs)
    at = -kk * jnp.exp(cs - lw)
    rt = r * jnp.exp(cs)
    bt = (b * p_inv).astype(BF)
    kt = (kd * p_inv).astype(BF)
    ar = jnp.concatenate([at, rt], axis=0)
    h02 = jnp.concatenate([h0, h0], axis=0)
    g0 = _bdot_nt(jnp.where(h02, ar, 0.0), jnp.concatenate([bt, kt], axis=0))
    g1 = _bdot_nt(jnp.where(h02, 0.0, ar), jnp.concatenate([kt, bt], axis=0))
    w0y0 = _bdot_nt(ar, S)
    yield
    w0, y0 = w0y0[:CH], w0y0[CH:]
    lab = jnp.where(strict[d], jnp.where(h0, g0[:CH], g1[:CH]), 0.0)
    lak = jnp.where(strict[d], jnp.where(h0, g1[:CH], g0[:CH]), 0.0)
    mrb = jnp.where(incl[d], jnp.where(h0, g0[CH:], g1[CH:]), 0.0)
    mrk = jnp.where(incl[d], jnp.where(h0, g1[CH:], g0[CH:]), 0.0)
    vstk_sw = stack_heads(v, swap=True).astype(BF)
    w = w0 + dot(lak.astype(BF), vstk_sw)
    ldiag = jnp.where(same_blk, lab, 0.0)
    loff = jnp.where(same_blk, 0.0, lab)
    mc = ldiag[0:SB]
    for i in range(1, nblk):
        mc = mc + ldiag[i * SB:(i + 1) * SB]
    xc = eye_c + mc
    mc = dot_split(mc, expand(mc))
    yield
    nsq = int(math.log2(SB)) - 1
    for it in range(nsq):
        last = it + 1 == nsq
        res = dot_split(xc if last else jnp.concatenate([xc, mc], axis=0), expand(mc))
        xc = xc + res[:SB]
        if not last:
            mc = res[SB:]
        yield
    ublk = [None] * nblk
    order = list(range(nblk)) if d == 0 else list(range(nblk - 1, -1, -1))
    for n, i in enumerate(order):
        wi = w[i * SB:(i + 1) * SB]
        if n > 0:
            ucur = jnp.concatenate([jnp.zeros((SB, LANES), F32) if ub is None else ub
                                    for ub in ublk], axis=0)
            wi = wi + dot(loff[i * SB:(i + 1) * SB].astype(BF), stack_heads(ucur).astype(BF))
            yield
        m0 = jnp.where(h0s, wi, 0.0).astype(BF)
        m1 = jnp.where(h0s, 0.0, wi).astype(BF)
        rhs = jnp.concatenate([m0] * nblk + [m1] * nblk, axis=0)
        ublk[i] = dot(jnp.where(lane_blk[i], xc, 0.0).astype(BF), rhs)
        yield
    u = jnp.concatenate(ublk, axis=0)
    y = dot(jnp.concatenate([mrb, mrk], axis=1).astype(BF),
            jnp.concatenate([stack_heads(u).astype(BF), vstk_sw], axis=0))
    uvt = jnp.concatenate([u, v], axis=0).T
    ds = _bdot(uvt, jnp.concatenate([b * p_end, kd * p_end], axis=0))
    yield
    s_new = S * jnp.exp(tot) + jnp.where(bd, ds, 0.0)
    return s_new, y0 + y, bonus


def _run_lockstep(gens):
    results = [None] * len(gens)
    pending = list(range(len(gens)))
    while pending:
        for i in list(pending):
            try:
                next(gens[i])
            except StopIteration as stop:
                results[i] = stop.value
                pending.remove(i)
    return results


def _wkv_consts():
    t2 = lax.broadcasted_iota(jnp.int32, (CH, LANES), 0)
    l2 = lax.broadcasted_iota(jnp.int32, (CH, LANES), 1)
    s2 = l2 & (CH - 1)
    strict = [s2 < t2, s2 > t2]
    incl = [s2 <= t2, s2 >= t2]
    h0 = l2 < HD
    same_blk = (t2 // SB) == (s2 // SB)
    ri = lax.broadcasted_iota(jnp.int32, (LANES, LANES), 0)
    ci = lax.broadcasted_iota(jnp.int32, (LANES, LANES), 1)
    bd = (ri // HD) == (ci // HD)
    bo2 = jnp.where(bd, 1.0, 0.0).astype(BF)
    bd_sb = (ri // SB) == (ci // SB)
    t3 = lax.broadcasted_iota(jnp.int32, (SB, LANES), 0)
    l3 = lax.broadcasted_iota(jnp.int32, (SB, LANES), 1)
    eye_c = jnp.where((l3 & (SB - 1)) == t3, 1.0, 0.0)
    lane_blk = [((l3 & (CH - 1)) // SB) == i for i in range(CH // SB)]
    return t2, strict, incl, h0, bd, bo2, same_blk, bd_sb, eye_c, lane_blk


def _wkv_kernel(T, has_s0, *refs):
    (r_ref, k_ref, v_ref, kk_ref, lw0_ref, lw1_ref, ag0_ref, ag1_ref, ka_ref, rk_ref), rest = refs[:10], refs[10:]
    if has_s0:
        s0_ref, y_ref, bon_ref, s_scr = rest
    else:
        y_ref, bon_ref, st_ref, s_scr = rest
    nch = T // CH
    npair = y_ref.shape[1] // LANES
    cst = _wkv_consts()
    if has_s0:
        s_scr[...] = s0_ref[...]
    else:
        s_scr[...] = jnp.zeros_like(s_scr)
    y_ref[...] = jnp.zeros_like(y_ref)
    bon_ref[...] = jnp.zeros_like(bon_ref)
    lw_refs = (lw0_ref, lw1_ref)
    ag_refs = (ag0_ref, ag1_ref)

    def body(c, carry):
        chains = [(p, d) for p in range(npair) for d in range(2)]
        sl = {}
        for p, d in chains:
            r0 = pl.multiple_of((c if d == 0 else nch - 1 - c) * CH, CH)
            sl[p, d] = (pl.ds(r0, CH), pl.ds(p * LANES, LANES))
        args = {}
        for p, d in chains:
            rs, ls = sl[p, d]
            args[p, d] = (s_scr[d, p], r_ref[rs, ls], k_ref[rs, ls], v_ref[rs, ls], kk_ref[rs, ls],
                          lw_refs[d][rs, ls], ag_refs[d][rs, ls], ka_ref[:, ls], rk_ref[:, ls],
                          y_ref[rs, ls], bon_ref[rs, ls])
        outs = _run_lockstep([_wkv_chunk(d, *args[p, d][:9], cst) for p, d in chains])
        for (p, d), (s_new, y, bon) in zip(chains, outs):
            rs, ls = sl[p, d]
            s_scr[d, p] = s_new
            y_ref[rs, ls] = args[p, d][9] + y
            bon_ref[rs, ls] = args[p, d][10] + bon
        return carry

    lax.fori_loop(0, nch, body, 0)
    if not has_s0:
        for d in range(2):
            for p in range(npair):
                s = s_scr[d, p]
                st_ref[d, 2 * p] = s[:HD, :HD]
                st_ref[d, 2 * p + 1] = s[HD:, HD:]


def _wkv(arrs, ka, rk, T, nb, row_blk0, s0_bd=None, lanes=512):
    ng = D // lanes
    npair = lanes // LANES
    seq = pl.BlockSpec((T, lanes), lambda b, g: (row_blk0 + b, g))
    vec = pl.BlockSpec((1, lanes), lambda b, g: (0, g))
    st = pl.BlockSpec((None, 2, npair, LANES, LANES), lambda b, g: (b, 0, g, 0, 0))
    out_seq = pl.BlockSpec((T, lanes), lambda b, g: (b, g))
    in_specs = [seq] * 8 + [vec, vec]
    ins = list(arrs) + [ka, rk]
    out_specs = [out_seq, out_seq]
    out_shape = [jax.ShapeDtypeStruct((nb * T, D), F32)] * 2
    if s0_bd is not None:
        in_specs.append(st)
        ins.append(s0_bd)
    else:
        out_specs.append(pl.BlockSpec((None, 2, 2 * npair, HD, HD), lambda b, g: (b, 0, g, 0, 0)))
        out_shape.append(jax.ShapeDtypeStruct((nb, 2, NH, HD, HD), F32))
    return pl.pallas_call(
        functools.partial(_wkv_kernel, T, s0_bd is not None),
        grid=(nb, ng),
        in_specs=in_specs,
        out_specs=out_specs,
        out_shape=out_shape,
        scratch_shapes=[pltpu.VMEM((2, npair, LANES, LANES), F32)],
        compiler_params=_cp(("parallel", "parallel")),
        name="wkv_T%d" % T,
    )(*ins)


def _proj_kernel(gn, *refs):
    i = pl.program_id(0)
    is_p = i < NT_P
    if gn:
        (yp_ref, ys_ref, ap_ref, as_ref, bp_ref, bs_ref, g_ref, lng_ref, lnb_ref, bo_ref,
         mod_ref, w_ref, nrm_ref, wr_ref, br_ref, o_ref, xt_o, gate_o) = refs
        y = jnp.where(is_p, yp_ref[...], ys_ref[...])
    else:
        y_ref, ap_ref, as_ref, mod_ref, w_ref, nrm_ref, wr_ref, br_ref, o_ref, xt_o, gate_o = refs
        y = y_ref[...]
    a = jnp.where(is_p, ap_ref[...], as_ref[...])
    if gn:
        bo = bo_ref[...]
        mean = _seg_sum(a, bo) * (1.0 / HD)
        cen = a - mean
        var = _seg_sum(cen * cen, bo) * (1.0 / HD)
        yn = cen * lax.rsqrt(var + GN_EPS) * lng_ref[...] + lnb_ref[...]
        a = (yn + jnp.where(is_p, bp_ref[...], bs_ref[...])) * g_ref[...]
    gt = _mod_vec(mod_ref, _tile_mod_row(i), 2)
    y_new = y + gt * _bdot(a, w_ref[...])
    o_ref[...] = y_new
    row = _tile_mod_row(i)
    xt = _modulate(y_new, nrm_ref[...], _mod_vec(mod_ref, row, 3), _mod_vec(mod_ref, row, 4))
    xt_o[...] = xt.astype(BF)
    gate_o[...] = _route(xt, wr_ref[...], br_ref[...])


def _proj_residual(y, a_p, a_s, mod, w, router, gn_args=None, name="proj"):
    tile = pl.BlockSpec((TM, D), lambda i: (i, 0))
    tile_p = pl.BlockSpec((TM, D), lambda i: (jnp.minimum(i, NT_P - 1), 0))
    tile_s = pl.BlockSpec((TM, D), lambda i: (jnp.maximum(i - NT_P, 0), 0))
    full = lambda a: pl.BlockSpec(a.shape, lambda i: (0,) * a.ndim)
    if gn_args is not None:
        ins = [y[0], y[1], a_p, a_s]
        in_specs = [tile_p, tile_s, tile_p, tile_s]
    else:
        ins = [y, a_p, a_s]
        in_specs = [tile, tile_p, tile_s]
    if gn_args is not None:
        b_p, b_s, g, lng, lnb, bo = gn_args
        ins += [b_p, b_s, g, lng, lnb, bo]
        in_specs += [tile_p, tile_s, tile, full(lng), full(lnb), full(bo)]
    ins += [mod, w] + list(router)
    in_specs += [full(mod), full(w)] + [full(a) for a in router]
    return pl.pallas_call(
        functools.partial(_proj_kernel, gn_args is not None),
        grid=(NT,),
        in_specs=in_specs,
        out_specs=[tile, tile, pl.BlockSpec((TM, LANES), lambda i: (i, 0))],
        out_shape=[jax.ShapeDtypeStruct((NTOK, D), F32), jax.ShapeDtypeStruct((NTOK, D), BF),
                   jax.ShapeDtypeStruct((NTOK, LANES), F32)],
        compiler_params=_cp(("parallel",)),
        name=name,
    )(*ins)


def _route(xt, wr, br):
    x1, x2 = _split2(xt)
    w1, w2 = _split2(wr)
    dot = lambda a, b: jnp.dot(a, b, preferred_element_type=F32)
    logits = (dot(x2, w1) + dot(x1, w2)) + dot(x1, w1)
    logits = logits + br
    lane = lax.broadcasted_iota(jnp.int32, logits.shape, 1).astype(F32)
    isg = jnp.logical_and(lane >= NE, lane < NE + NE // EPG)
    mg = jnp.max(jnp.where(isg, logits, NEG), axis=-1, keepdims=True)
    eg = jnp.where(isg, jnp.exp(jnp.minimum(logits - mg, 0.0)), 0.0)
    p_sel = 1.0 / jnp.sum(eg, axis=-1, keepdims=True)
    gidx = jnp.min(jnp.where(jnp.logical_and(isg, logits == mg), lane, 1e3), axis=-1, keepdims=True) - NE
    lo_l = gidx * EPG
    ing = jnp.logical_and(lane >= lo_l, lane < lo_l + EPG)
    me = jnp.max(jnp.where(ing, logits, NEG), axis=-1, keepdims=True)
    ee = jnp.where(ing, jnp.exp(jnp.minimum(logits - me, 0.0)), 0.0)
    se = jnp.sum(ee, axis=-1, keepdims=True)
    pe = ee / se
    i1 = jnp.min(jnp.where(jnp.logical_and(ing, logits == me), lane, 1e3), axis=-1, keepdims=True)
    v1 = 1.0 / se
    rest = jnp.logical_and(ing, lane != i1)
    v2 = jnp.max(jnp.where(rest, pe, -1.0), axis=-1, keepdims=True)
    i2 = jnp.min(jnp.where(jnp.logical_and(rest, pe == v2), lane, 1e3), axis=-1, keepdims=True)
    den = v1 + v2
    return (jnp.where(lane == i1, p_sel * v1 / den, 0.0)
            + jnp.where(lane == i2, p_sel * v2 / den, 0.0)
            + jnp.where(lane == float(GRP_LANE), gidx, 0.0))


def _experts_kernel(final, blk_ref, xt_ref, gate_ref, w1_ref, w3_ref, w2_ref, y_ref, mod_ref, fin_ref, *rest):
    i = pl.program_id(0)
    g = pl.program_id(1)
    ngrp = NE // EPS
    if final:
        op_ref, os_ref, xs_scr, gs_scr, outs_scr, pos_scr = rest
    else:
        o_ref, xs_scr, gs_scr, outs_scr, pos_scr = rest
    ntile_p = NPR // TMX
    dot = lambda a, b: jnp.dot(a, b, preferred_element_type=F32)

    @pl.when(g == 0)
    def _():
        gate = gate_ref[...]
        lane = lax.broadcasted_iota(jnp.int32, gate.shape, 1)
        grp = jnp.sum(jnp.where(lane == GRP_LANE, gate, 0.0), axis=-1, keepdims=True)
        onehot = jnp.where(lane.astype(F32) == grp, 1.0, 0.0)
        tr = lax.broadcasted_iota(jnp.int32, (TMX, TMX), 0)
        br = lax.broadcasted_iota(jnp.int32, (LANES, LANES), 0)
        bc = lax.broadcasted_iota(jnp.int32, (LANES, LANES), 1)
        ltri = jnp.where(bc < br, 1.0, 0.0).astype(BF)
        carry = jnp.zeros((1, LANES), F32)
        ranks = []
        for blk in range(TMX // LANES):
            oh = onehot[blk * LANES:(blk + 1) * LANES]
            ranks.append(dot(ltri, oh.astype(BF)) + carry)
            carry = carry + jnp.sum(oh, axis=0, keepdims=True)
        rank = jnp.concatenate(ranks, axis=0)
        cnt = jnp.broadcast_to(carry, (8, LANES))
        lane8 = lax.broadcasted_iota(jnp.int32, (8, LANES), 1)
        off = jnp.zeros((8, LANES), F32)
        for k in range(1, ngrp):
            off = off + jnp.where(lane8 >= k, pltpu.roll(cnt, k, 1), 0.0)
        pos = jnp.sum(onehot * (off[0:1] + rank), axis=-1, keepdims=True)
        posb = jnp.broadcast_to(pos, (TMX, LANES))
        pos_scr[...] = posb
        pos_row = posb.T[0:1, :]
        perm = jnp.where(tr.astype(F32) == pos_row, 1.0, 0.0).astype(BF)
        xs_scr[...] = dot(perm, xt_ref[...]).astype(BF)
        g1, g2 = _split2(gate)
        gsel = dot(perm, jnp.concatenate([g1, g2], axis=1))
        gs_scr[...] = gsel[:, :LANES] + gsel[:, LANES:]
        outs_scr[...] = jnp.zeros_like(outs_scr)

    blo = blk_ref[2 * (i * ngrp + g)]
    bhi = blk_ref[2 * (i * ngrp + g) + 1]
    for b in range(TMX // SBR):
        @pl.when(jnp.logical_and(b >= blo, b < bhi))
        def _(b=b):
            rows = pl.ds(b * SBR, SBR)
            xb = xs_scr[rows, :]
            gsb = gs_scr[rows, :]
            lane_c = lax.broadcasted_iota(jnp.int32, gsb.shape, 1)
            hids = []
            for j in range(EPS):
                h1 = dot(xb, w1_ref[j])
                h3 = dot(xb, w3_ref[j])
                ge = jnp.sum(jnp.where(lane_c == g * EPS + j, gsb, 0.0), axis=-1, keepdims=True)
                hids.append(((h1 * jax.nn.sigmoid(h1)) * h3 * ge).astype(BF))
            outs_scr[rows, :] += dot(jnp.concatenate(hids, axis=1), w2_ref[...].reshape(EPS * DE, D))

    def result():
        row = jnp.where(i < ntile_p, 0, i - ntile_p + 1)
        tcol = lax.broadcasted_iota(jnp.int32, (TMX, TMX), 1).astype(F32)
        back = jnp.where(tcol == pos_scr[:, 0:1], 1.0, 0.0).astype(BF)
        out = y_ref[...] + _mod_vec(mod_ref, row, 5) * dot(back, outs_scr[...].astype(BF))
        if final:
            ms = jnp.mean(out * out, axis=-1, keepdims=True)
            out = out * lax.rsqrt(ms + NORM_EPS) * fin_ref[...]
        return out

    if final:
        @pl.when(jnp.logical_and(g == NE // EPS - 1, i < ntile_p))
        def _():
            op_ref[...] = result()

        @pl.when(jnp.logical_and(g == NE // EPS - 1, i >= ntile_p))
        def _():
            os_ref[...] = result()
    else:
        @pl.when(g == NE // EPS - 1)
        def _():
            o_ref[...] = result()


def _moe_experts(nch, xt, gate, w1, w3, w2, layer, y, mod, fin, final):
    tile = pl.BlockSpec((TMX, D), lambda i, e, n: (i, 0))
    full = lambda a: pl.BlockSpec(a.shape, lambda i, e, n: (0,) * a.ndim)
    ntile_p = NPR // TMX
    if final:
        out_specs = [pl.BlockSpec((TMX, D), lambda i, e, n: (jnp.minimum(i, ntile_p - 1), 0)),
                     pl.BlockSpec((TMX, D), lambda i, e, n: (jnp.maximum(i - ntile_p, 0), 0))]
        out_shape = [jax.ShapeDtypeStruct((NPR, D), F32), jax.ShapeDtypeStruct((NSR, D), F32)]
    else:
        out_specs = tile
        out_shape = jax.ShapeDtypeStruct((NTOK, D), F32)
    scratch = [pltpu.VMEM((TMX, D), BF), pltpu.VMEM((TMX, LANES), F32), pltpu.VMEM((TMX, D), F32),
               pltpu.VMEM((TMX, LANES), F32)]
    grid_spec = pltpu.PrefetchScalarGridSpec(
        num_scalar_prefetch=1,
        grid=(NTOK // TMX, NE // EPS),
        in_specs=[tile, pl.BlockSpec((TMX, LANES), lambda i, e, n: (i, 0)),
                  pl.BlockSpec((None, EPS, D, DE), lambda i, e, n: (layer, e, 0, 0)),
                  pl.BlockSpec((None, EPS, D, DE), lambda i, e, n: (layer, e, 0, 0)),
                  pl.BlockSpec((None, EPS, DE, D), lambda i, e, n: (layer, e, 0, 0)),
                  tile, full(mod), full(fin)],
        out_specs=out_specs,
        scratch_shapes=scratch)
    return pl.pallas_call(
        functools.partial(_experts_kernel, final),
        grid_spec=grid_spec,
        out_shape=out_shape,
        compiler_params=_cp(("arbitrary" if final else "parallel", "arbitrary")),
        name="moe_experts",
    )(nch, xt, gate, w1, w3, w2, y, mod, fin)


def _qkv_kernel(y_ref, mod_ref, nrm_ref, w_ref, q_o, kp_o, vp_o, ks_o, vs_o, kc_o, vc_o):
    i = pl.program_id(0)
    row = _tile_mod_row(_qkv_tile(i))
    h = _modulate(y_ref[...], nrm_ref[...], _mod_vec(mod_ref, row, 0), _mod_vec(mod_ref, row, 1))
    hb = h.astype(BF)
    kx = jnp.dot(hb, w_ref[:, D:2 * D], preferred_element_type=F32)
    vx = jnp.dot(hb, w_ref[:, 2 * D:], preferred_element_type=F32)
    q_o[...] = jnp.dot(hb, w_ref[:, :D], preferred_element_type=F32)

    @pl.when(i < NT - NT_P)
    def _():
        ks_o[...] = kx
        vs_o[...] = vx

    @pl.when(i >= NT - NT_P)
    def _():
        kp_o[...] = kx
        vp_o[...] = vx
        for hh in range(NH):
            kc_o[pl.ds(hh, TM, stride=NH), :] = kx[:, hh * HD:(hh + 1) * HD]
            vc_o[pl.ds(hh, TM, stride=NH), :] = vx[:, hh * HD:(hh + 1) * HD]


def _qkv_tile(i):
    return jnp.where(i < NT - NT_P, NT_P + i, i - (NT - NT_P))


def _na_qkv(y, mod, nrm, w):
    nt_s = NT - NT_P
    tile = pl.BlockSpec((TM, D), lambda i: (_qkv_tile(i), 0))
    tile_p = pl.BlockSpec((TM, D), lambda i: (jnp.maximum(i - nt_s, 0), 0))
    tile_s = pl.BlockSpec((TM, D), lambda i: (jnp.minimum(i, nt_s - 1), 0))
    full = lambda a: pl.BlockSpec(a.shape, lambda i: (0,) * a.ndim)
    out = lambda n: jax.ShapeDtypeStruct((n, D), F32)
    cache = jax.ShapeDtypeStruct((NPR * NH, HD), F32)
    cache_tile = pl.BlockSpec((TM * NH, HD), lambda i: (jnp.maximum(i - nt_s, 0), 0))
    return pl.pallas_call(
        _qkv_kernel,
        grid=(NT,),
        in_specs=[tile, full(mod), full(nrm), full(w)],
        out_specs=[tile, tile_p, tile_p, tile_s, tile_s, cache_tile, cache_tile],
        out_shape=[out(NTOK), out(NPR), out(NPR), out(NSR), out(NSR), cache, cache],
        compiler_params=_cp(("arbitrary",)),
        name="na_qkv",
    )(y, mod, nrm, w)


def _softmax_rows(s):
    m = jnp.max(s, axis=-1, keepdims=True)
    e = jnp.exp(s - m)
    return e / jnp.sum(e, axis=-1, keepdims=True)


def _ctx_attn_kernel(q_ref, k_ref, v_ref, o_ref):
    scale = HD ** -0.5
    h0 = lax.broadcasted_iota(jnp.int32, (T_P, LANES), 1) < HD
    def head_chain(h, q, kb, vb):
        qm = jnp.where(h0 if h == 0 else jnp.logical_not(h0), q, 0.0)
        s = _bdot_nt(qm, kb)
        yield
        o = jnp.dot(_softmax_rows(s * scale).astype(BF), vb, preferred_element_type=F32)
        yield
        return o

    npair = D // LANES
    for p0 in range(0, npair, CTX_PAIRS):
        gens = []
        for p in range(p0, p0 + CTX_PAIRS):
            ls = pl.ds(p * LANES, LANES)
            q = q_ref[:, ls]
            kb = k_ref[:, ls].astype(BF)
            vb = v_ref[:, ls].astype(BF)
            gens += [head_chain(h, q, kb, vb) for h in range(2)]
        outs = _run_lockstep(gens)
        for j, p in enumerate(range(p0, p0 + CTX_PAIRS)):
            o_ref[:, pl.ds(p * LANES, LANES)] = jnp.where(h0, outs[2 * j], outs[2 * j + 1])


def _ctx_attn(q, k, v):
    blk = pl.BlockSpec((T_P, D), lambda b: (b, 0))
    return pl.pallas_call(
        _ctx_attn_kernel,
        grid=(NB_P,),
        in_specs=[blk] * 3,
        out_specs=blk,
        out_shape=jax.ShapeDtypeStruct((NPR, D), F32),
        compiler_params=_cp(("parallel",)),
        name="ctx_attn",
    )(q, k, v)


def _nbr_attn_kernel(q_ref, k_ref, v_ref, kc_ref, vc_ref, p_ref, o_ref, tz_ref):
    scale = HD ** -0.5
    rows = T_S // GRID_W
    nloc = WIN_R * GRID_W
    qc = lax.broadcasted_iota(jnp.int32, (GRID_W, LANES), 0)
    ln = lax.broadcasted_iota(jnp.int32, (GRID_W, LANES), 1)
    h0 = ln < HD
    kcb = kc_ref[...].astype(BF)
    vcb = vc_ref[...].astype(BF)
    kcol = ln & (GRID_W - 1)
    cstart = jnp.clip(qc - WIN_C // 2, 0, GRID_W - WIN_C)
    valid = jnp.logical_and(kcol >= cstart, kcol < cstart + WIN_C)
    for h in range(2):
        rolled = []
        for j in range(2 * WIN_R - 1):
            prow = jnp.broadcast_to(p_ref[h, j:j + 1, :], (GRID_W, LANES))
            rolled.append((pltpu.roll(prow, 0, 1, stride=1, stride_axis=0),
                           pltpu.roll(prow, GRID_W, 1, stride=1, stride_axis=0)))
        for j in range(2 * WIN_R - 2):
            tz_ref[h, j] = jnp.where(valid, jnp.where(h0, rolled[j][0], rolled[j + 1][1]), NEG)

    def head_chain(h, q, klb, vlb, j0):
        qm = jnp.where(h0 if h == 0 else jnp.logical_not(h0), q, 0.0).astype(BF)
        sl = _bdot_nt(qm, klb)
        sc = _bdot_nt(qm, kcb)
        yield
        bias = jnp.concatenate([tz_ref[h, j0 + 2 * m] for m in range(WIN_R // 2)], axis=1)
        sl = sl * scale + bias
        sc = sc * scale
        m = jnp.maximum(jnp.max(sl, axis=-1, keepdims=True), jnp.max(sc, axis=-1, keepdims=True))
        el = jnp.exp(sl - m)
        ec = jnp.exp(sc - m)
        inv = 1.0 / (jnp.sum(el, axis=-1, keepdims=True) + jnp.sum(ec, axis=-1, keepdims=True))
        o = (jnp.dot((el * inv).astype(BF), vlb, preferred_element_type=F32)
             + jnp.dot((ec * inv).astype(BF), vcb, preferred_element_type=F32))
        yield
        return o

    def body(it, carry):
        gens, slices = [], []
        for j in range(NBR_UNROLL):
            r = it * NBR_UNROLL + j
            start = jnp.clip(r - WIN_R // 2, 0, rows - WIN_R)
            qs = pl.ds(pl.multiple_of(r * GRID_W, GRID_W), GRID_W)
            ks = pl.ds(pl.multiple_of(start * GRID_W, GRID_W), nloc)
            q = q_ref[qs, :]
            klb = k_ref[ks, :].astype(BF)
            vlb = v_ref[ks, :].astype(BF)
            j0 = start - r + WIN_R - 1
            slices.append(qs)
            gens += [head_chain(h, q, klb, vlb, j0) for h in range(2)]
        outs = _run_lockstep(gens)
        for j, qs in enumerate(slices):
            o_ref[qs, :] = jnp.where(h0, outs[2 * j], outs[2 * j + 1])
        return carry

    lax.fori_loop(0, rows // NBR_UNROLL, body, 0)


def _nbr_attn(q, k, v, kc, vc, ptab):
    npair = D // LANES
    nrow = ptab.shape[1]
    qseq = pl.BlockSpec((T_S, LANES), lambda b, p: (NPR // T_S + b, p))
    seq = pl.BlockSpec((T_S, LANES), lambda b, p: (b, p))
    ctx = pl.BlockSpec((kc.shape[0] // NB_S, LANES), lambda b, p: (b, p))
    return pl.pallas_call(
        _nbr_attn_kernel,
        grid=(NB_S, npair),
        in_specs=[qseq, seq, seq, ctx, ctx,
                  pl.BlockSpec((2, nrow, LANES), lambda b, p: (p, 0, 0))],
        out_specs=pl.BlockSpec((T_S, LANES), lambda b, p: (b, p)),
        out_shape=jax.ShapeDtypeStruct((NSR, D), F32),
        scratch_shapes=[pltpu.VMEM((2, nrow - 1, GRID_W, LANES), F32)],
        compiler_params=_cp(("parallel", "parallel")),
        name="nbr_attn",
    )(q, k, v, kc, vc, ptab)


def _block_diag2(a, b):
    z = jnp.zeros_like(a)
    return jnp.concatenate([jnp.concatenate([a, z], axis=1), jnp.concatenate([z, b], axis=1)], axis=0)


def _nbr_bias_table(rpb):
    pad = jnp.zeros(rpb.shape[:2] + (LANES - 2 * WIN_C + 1,), F32)
    return jnp.concatenate([rpb[:, :, WIN_C - 1:], pad, rpb[:, :, :WIN_C - 1]], axis=-1)


def kernel(x_prompt, x_sample, c, state_rwkv, cache_na_k, cache_na_v, c_ctx, norm_mix, norm_ffn, ada_w, ada_b, rwkv_mu, rwkv_w_r, rwkv_w_k, rwkv_w_v, rwkv_w_o, rwkv_w0, rwkv_w1, rwkv_w2, rwkv_a0, rwkv_a1, rwkv_a2, rwkv_g1, rwkv_g2, rwkv_k_k, rwkv_k_a, rwkv_r_k, rwkv_lnx_g, rwkv_lnx_b, na_w_qkv, na_w_o, na_rpb, moe_w_grp, moe_b_grp, moe_w_exp, moe_b_exp, moe_w1, moe_w3, moe_w2, final_norm):
    x_p = x_prompt.reshape(NPR, D)
    x_s = x_sample.reshape(NSR, D)
    c8 = jnp.concatenate([c_ctx[None, :], c, jnp.zeros((8 - 1 - NB_S, D), F32)], axis=0)
    mod = _adaln(c8, ada_w, ada_b)
    ri = jnp.arange(2 * LANES)[:, None] // HD
    bo = (ri == ri.T).astype(BF)
    row = lambda a: a.reshape(1, D)

    w1c = jnp.concatenate([rwkv_w1[0, 0], rwkv_w1[0, 1]], axis=1).astype(BF)
    a1c = jnp.concatenate([rwkv_a1[0, 0], rwkv_a1[0, 1]], axis=1).astype(BF)
    w2bd = _block_diag2(rwkv_w2[0, 0], rwkv_w2[0, 1]).astype(BF)
    a2bd = _block_diag2(rwkv_a2[0, 0], rwkv_a2[0, 1]).astype(BF)
    r, k, v, g, kk, lw0, lw1, ag0, ag1 = _rwkv_front(
        x_p, x_s, mod[0], row(norm_mix[0]), rwkv_mu[0], rwkv_w_r[0].astype(BF), rwkv_w_k[0].astype(BF),
        rwkv_w_v[0].astype(BF), w1c, a1c, rwkv_g1[0].astype(BF), w2bd, a2bd, rwkv_g2[0].astype(BF),
        rwkv_w0[0], rwkv_a0[0], row(rwkv_k_k[0]), bo)
    arrs = (r, k, v, kk, lw0, lw1, ag0, ag1)
    ka, rk = row(rwkv_k_a[0]), row(rwkv_r_k[0])
    ys_p, bon_p, new_state = _wkv(arrs, ka, rk, T_P, NB_P, 0, lanes=D)
    s0 = state_rwkv[:, 0].reshape(NB_S, 2, NH // 2, 2, HD, HD)
    z = jnp.zeros_like(s0[:, :, :, 0])
    s0_bd = jnp.concatenate([jnp.concatenate([s0[:, :, :, 0], z], axis=-1),
                             jnp.concatenate([z, s0[:, :, :, 1]], axis=-1)], axis=-2)
    ys_s, bon_s = _wkv(arrs, ka, rk, T_S, NB_S, NPR // T_S, s0_bd=s0_bd)
    new_state_rwkv = new_state.reshape(NB_P, 1, 2, NH, HD, HD)
    def router(i):
        wr = jnp.concatenate([moe_w_exp[i], moe_w_grp[i],
                              jnp.zeros((D, LANES - NE - NE // EPG), F32)], axis=1)
        br = jnp.concatenate([moe_b_exp[i], moe_b_grp[i],
                              jnp.zeros((LANES - NE - NE // EPG,), F32)]).reshape(1, LANES)
        return (row(norm_ffn[i]), wr, br)

    def experts(y, xt, gate, i, final):
        grp = gate[:, GRP_LANE].astype(jnp.int32).reshape(NTOK // TMX, TMX)
        cnt = jnp.sum(grp[:, :, None] == jnp.arange(NE // EPG)[None, None, :], axis=1).astype(jnp.int32)
        off = jnp.cumsum(cnt, axis=1) - cnt
        blo = off // SBR
        bhi = jnp.where(cnt > 0, (off + cnt - 1) // SBR + 1, blo)
        blk = jnp.stack([blo, bhi], axis=-1).astype(jnp.int32).reshape(-1)
        return _moe_experts(blk, xt, gate, w1b, w3b, w2b, i, y, mod[i], row(final_norm), final)

    w1b, w3b, w2b = moe_w1.astype(BF), moe_w3.astype(BF), moe_w2.astype(BF)

    y, xt, gate = _proj_residual((x_p, x_s), ys_p, ys_s, mod[0], rwkv_w_o[0].astype(BF), router(0),
                                 gn_args=(bon_p, bon_s, g, row(rwkv_lnx_g[0]), row(rwkv_lnx_b[0]), bo),
                                 name="rwkv_out")
    y = experts(y, xt, gate, 0, False)

    q, k_p, v_p, k_s, v_s, kcache, vcache = _na_qkv(y, mod[1], row(norm_mix[1]), na_w_qkv[0].astype(BF))
    o_p = _ctx_attn(q, k_p, v_p)
    kc = cache_na_k[:, 0].reshape(NB_S * cache_na_k.shape[2], D)
    vc = cache_na_v[:, 0].reshape(NB_S * cache_na_v.shape[2], D)
    o_s = _nbr_attn(q, k_s, v_s, kc, vc, _nbr_bias_table(na_rpb[0]))
    y, xt, gate = _proj_residual(y, o_p, o_s, mod[1], na_w_o[0].astype(BF), router(1), name="attn_out")
    y_p, y_s = experts(y, xt, gate, 1, True)

    y_prompt = y_p.reshape(NB_P, T_P, D)
    y_sample = y_s.reshape(NB_S, T_S, D)
    new_k = kcache.reshape(NB_P, 1, T_P, NH, HD)
    new_v = vcache.reshape(NB_P, 1, T_P, NH, HD)
    return (y_prompt, y_sample, new_state_rwkv, new_k, new_v)
```

```python
import functools
import math

import jax
import jax.numpy as jnp
from jax import lax
from jax.experimental import pallas as pl
from jax.experimental.pallas import tpu as pltpu

F32 = jnp.float32
BF = jnp.bfloat16

D = 1024
NH = 16
HD = 64
NB_P, T_P = 32, 256
NB_S, T_S = 2, 1024
NPR = NB_P * T_P
NSR = NB_S * T_S
NTOK = NPR + NSR
TM = 256
NT = NTOK // TM
NT_P = NPR // TM
TPS = T_S // TM
TMX = 1024
NMOD = 6
NE = 16
EPG = 4
DE = 256
EPS = EPG
SBR = 128
GRP_LANE = 127
CH = 64
SB = 32
LANES = 128
GRID_W = 64
WIN_R, WIN_C = 8, 16
NBR_UNROLL = 8
CTX_PAIRS = 4
NORM_EPS = 1e-6
GN_EPS = 64e-5
NEG = -1e30
VMEM_LIMIT = 56 * 1024 * 1024


def _cp(sem):
    return pltpu.CompilerParams(dimension_semantics=sem, vmem_limit_bytes=VMEM_LIMIT)


def _bdot(a, b):
    return jnp.dot(a.astype(BF), b.astype(BF), preferred_element_type=F32)


def _bdot_nt(a, b):
    return lax.dot_general(a.astype(BF), b.astype(BF), (((1,), (1,)), ((), ())),
                           preferred_element_type=F32)


def _split2(x):
    hi = x.astype(BF)
    lo = (x - hi.astype(F32)).astype(BF)
    return hi, lo


def _split3(x):
    hi = x.astype(BF)
    r1 = x - hi.astype(F32)
    mid = r1.astype(BF)
    lo = (r1 - mid.astype(F32)).astype(BF)
    return hi, mid, lo


def _seg_sum(x, bo):
    outs = []
    for c in range(x.shape[1] // 256):
        hi, lo = _split2(x[:, c * 256:(c + 1) * 256])
        outs.append(jnp.dot(hi, bo, preferred_element_type=F32)
                    + jnp.dot(lo, bo, preferred_element_type=F32))
    return outs[0] if len(outs) == 1 else jnp.concatenate(outs, axis=1)


def _tile_mod_row(i):
    return jnp.where(i < NT_P, 0, 1 + (i - NT_P) // TPS)


def _modulate(x, g, sh, sc):
    ms = jnp.mean(x * x, axis=-1, keepdims=True)
    return x * lax.rsqrt(ms + NORM_EPS) * g * (1.0 + sc) + sh


def _mod_vec(mod_ref, row, k):
    return mod_ref[pl.ds(row, 1), pl.ds(k * D, D)]


def _adaln_kernel(c_ref, w_ref, b_ref, o_ref):
    c = c_ref[...]
    s = c * jax.nn.sigmoid(c)
    o_ref[...] = _bdot(s, w_ref[...]) + b_ref[...]


def _adaln(c8, ada_w, ada_b):
    nl = ada_w.shape[0]
    tn = 1536
    return pl.pallas_call(
        _adaln_kernel,
        grid=(nl, NMOD * D // tn),
        in_specs=[pl.BlockSpec((8, D), lambda l, j: (0, 0)),
                  pl.BlockSpec((None, D, tn), lambda l, j: (l, 0, j)),
                  pl.BlockSpec((None, 1, tn), lambda l, j: (l, 0, j))],
        out_specs=pl.BlockSpec((None, 8, tn), lambda l, j: (l, 0, j)),
        out_shape=jax.ShapeDtypeStruct((nl, 8, NMOD * D), F32),
        compiler_params=_cp(("parallel", "parallel")),
        name="adaln",
    )(c8, ada_w, ada_b.reshape(nl, 1, NMOD * D))


def _front_kernel(xp_ref, xs_ref, yp_ref, yn_ref, mod_ref, nrm_ref, mu_ref, wr_ref, wk_ref, wv_ref,
                  w1_ref, a1_ref, g1_ref, w2_ref, a2_ref, g2_ref, w0_ref, a0_ref, kkw_ref, bo_ref,
                  r_o, k_o, v_o, g_o, kk_o, lw0_o, lw1_o, ag0_o, ag1_o):
    i = pl.program_id(0)
    row = _tile_mod_row(i)
    q = (i - NT_P) % TPS
    first = jnp.logical_or(i < NT_P, q == 0)
    last = jnp.logical_or(i < NT_P, q == TPS - 1)
    sh = _mod_vec(mod_ref, row, 0)
    sc = _mod_vec(mod_ref, row, 1)
    g = nrm_ref[...]
    h = _modulate(jnp.where(i < NT_P, xp_ref[...], xs_ref[...]), g, sh, sc)
    hp = _modulate(yp_ref[...], g, sh, sc)[7:8]
    hn = _modulate(yn_ref[...], g, sh, sc)[0:1]
    hp = jnp.where(first, 0.0, hp)
    hn = jnp.where(last, 0.0, hn)
    rid = lax.broadcasted_iota(jnp.int32, (TM, D), 0)
    prev = jnp.where(rid == 0, hp, pltpu.roll(h, 1, 0))
    nxt = jnp.where(rid == TM - 1, hn, pltpu.roll(h, TM - 1, 0))
    dx = 0.5 * (prev + nxt) - h

    def mix(n):
        return (h + dx * mu_ref[n:n + 1, :]).astype(BF)

    r_o[...] = jnp.dot(mix(0), wr_ref[...], preferred_element_type=F32)
    xw = mix(1)
    lora_w = _bdot(jnp.tanh(jnp.dot(xw, w1_ref[...], preferred_element_type=F32)), w2_ref[...])
    hc = 0.5 * math.exp(-0.5)
    lw0_o[...] = -hc * jnp.tanh(0.5 * (w0_ref[0:1, :] + lora_w[:, :D])) - hc
    lw1_o[...] = -hc * jnp.tanh(0.5 * (w0_ref[1:2, :] + lora_w[:, D:])) - hc
    k_raw = jnp.dot(mix(2), wk_ref[...], preferred_element_type=F32)
    k_o[...] = k_raw
    kq = k_raw * kkw_ref[...]
    ss = _seg_sum(kq * kq, bo_ref[...])
    kk_o[...] = kq / jnp.maximum(jnp.sqrt(ss), 1e-12)
    v_o[...] = jnp.dot(mix(3), wv_ref[...], preferred_element_type=F32)
    xa = mix(4)
    lora_a = _bdot(jnp.dot(xa, a1_ref[...], preferred_element_type=F32), a2_ref[...])
    ag0_o[...] = 0.5 * jnp.tanh(0.5 * (a0_ref[0:1, :] + lora_a[:, :D])) + 0.5
    ag1_o[...] = 0.5 * jnp.tanh(0.5 * (a0_ref[1:2, :] + lora_a[:, D:])) + 0.5
    xg = mix(5)
    g_o[...] = _bdot(jax.nn.sigmoid(jnp.dot(xg, g1_ref[...], preferred_element_type=F32)), g2_ref[...])


def _rwkv_front(x_p, x_s, mod, nrm, mu, wr, wk, wv, w1c, a1c, g1, w2bd, a2bd, g2, w0, a0, kkw, bo):
    tile = pl.BlockSpec((TM, D), lambda i: (i, 0))
    nblk8 = NSR // 8
    bpt = TM // 8
    full = lambda a: pl.BlockSpec(a.shape, lambda i: (0,) * a.ndim)
    ins = [x_p, x_s, x_s, x_s, mod, nrm, mu, wr, wk, wv, w1c, a1c, g1, w2bd, a2bd, g2, w0, a0, kkw, bo]
    in_specs = [pl.BlockSpec((TM, D), lambda i: (jnp.minimum(i, NT_P - 1), 0)),
                pl.BlockSpec((TM, D), lambda i: (jnp.maximum(i - NT_P, 0), 0)),
                pl.BlockSpec((8, D), lambda i: (jnp.clip((i - NT_P) * bpt - 1, 0, nblk8 - 1), 0)),
                pl.BlockSpec((8, D), lambda i: (jnp.clip((i - NT_P + 1) * bpt, 0, nblk8 - 1), 0))]
    in_specs += [full(a) for a in ins[4:]]
    out = jax.ShapeDtypeStruct((NTOK, D), F32)
    return pl.pallas_call(
        _front_kernel,
        grid=(NT,),
        in_specs=in_specs,
        out_specs=[tile] * 9,
        out_shape=[out] * 9,
        compiler_params=_cp(("parallel",)),
        name="rwkv_front",
    )(*ins)


def _wkv_chunk(d, S, r, k, v, kk, lw, ag, ka, rk, cst):
    trow, strict, incl, h0, bd, bo2, same_blk, bd_sb, eye_c, lane_blk = cst
    dot = lambda a, b: jnp.dot(a, b, preferred_element_type=F32)
    nblk = CH // SB
    h0s = lax.broadcasted_iota(jnp.int32, (SB, LANES), 1) < HD

    def stack_heads(x, swap=False):
        zero = jnp.zeros_like(x)
        parts = [jnp.where(h0, x, zero), jnp.where(h0, zero, x)]
        return jnp.concatenate(parts[::-1] if swap else parts, axis=0)

    def dot_split(a, bm):
        return dot(a.astype(BF), bm.astype(BF))

    def expand(mc):
        return jnp.where(bd_sb, jnp.concatenate([mc] * (LANES // SB), axis=0), 0.0)

    b = kk * ag
    kd = k * (1.0 + (ag - 1.0) * ka)
    cs = lw
    for sh in [1 << n for n in range(int(math.log2(CH)))]:
        if d == 0:
            cs = cs + jnp.where(trow >= sh, pltpu.roll(cs, sh, 0), 0.0)
        else:
            cs = cs + jnp.where(trow < CH - sh, pltpu.roll(cs, CH - sh, 0), 0.0)
    qh, ql = _split2(r * kd * rk)
    qs = dot(jnp.concatenate([qh, ql], axis=0), bo2)
    bonus = (qs[:CH] + qs[CH:]) * v
    yield
    tot = cs[CH - 1:CH, :] if d == 0 else cs[0:1, :]
    p_inv = jnp.exp(-cs)
    p_end = jnp.exp(tot - cs)
    at = -kk * jnp.exp(cs - lw)
    rt = r * jnp.exp(cs)
    bt = (b * p_inv).astype(BF)
    kt = (kd * p_inv).astype(BF)
    ar = jnp.concatenate([at, rt], axis=0)
    h02 = jnp.concatenate([h0, h0], axis=0)
    g0 = _bdot_nt(jnp.where(h02, ar, 0.0), jnp.concatenate([bt, kt], axis=0))
    g1 = _bdot_nt(jnp.where(h02, 0.0, ar), jnp.concatenate([kt, bt], axis=0))
    w0y0 = _bdot_nt(ar, S)
    yield
    w0, y0 = w0y0[:CH], w0y0[CH:]
    lab = jnp.where(strict[d], jnp.where(h0, g0[:CH], g1[:CH]), 0.0)
    lak = jnp.where(strict[d], jnp.where(h0, g1[:CH], g0[:CH]), 0.0)
    mrb = jnp.where(incl[d], jnp.where(h0, g0[CH:], g1[CH:]), 0.0)
    mrk = jnp.where(incl[d], jnp.where(h0, g1[CH:], g0[CH:]), 0.0)
    vstk_sw = stack_heads(v, swap=True).astype(BF)
    w = w0 + dot(lak.astype(BF), vstk_sw)
    ldiag = jnp.where(same_blk, lab, 0.0)
    loff = jnp.where(same_blk, 0.0, lab)
    mc = ldiag[0:SB]
    for i in range(1, nblk):
        mc = mc + ldiag[i * SB:(i + 1) * SB]
    xc = eye_c + mc
    mc = dot_split(mc, expand(mc))
    yield
    nsq = int(math.log2(SB)) - 1
    for it in range(nsq):
        last = it + 1 == nsq
        res = dot_split(xc if last else jnp.concatenate([xc, mc], axis=0), expand(mc))
        xc = xc + res[:SB]
        if not last:
            mc = res[SB:]
        yield
    ublk = [None] * nblk
    order = list(range(nblk)) if d == 0 else list(range(nblk - 1, -1, -1))
    for n, i in enumerate(order):
        wi = w[i * SB:(i + 1) * SB]
        if n > 0:
            ucur = jnp.concatenate([jnp.zeros((SB, LANES), F32) if ub is None else ub
                                    for ub in ublk], axis=0)
            wi = wi + dot(loff[i * SB:(i + 1) * SB].astype(BF), stack_heads(ucur).astype(BF))
            yield
        m0 = jnp.where(h0s, wi, 0.0).astype(BF)
        m1 = jnp.where(h0s, 0.0, wi).astype(BF)
        rhs = jnp.concatenate([m0] * nblk + [m1] * nblk, axis=0)
        ublk[i] = dot(jnp.where(lane_blk[i], xc, 0.0).astype(BF), rhs)
        yield
    u = jnp.concatenate(ublk, axis=0)
    y = dot(jnp.concatenate([mrb, mrk], axis=1).astype(BF),
            jnp.concatenate([stack_heads(u).astype(BF), vstk_sw], axis=0))
    uvt = jnp.concatenate([u, v], axis=0).T
    ds = _bdot(uvt, jnp.concatenate([b * p_end, kd * p_end], axis=0))
    yield
    s_new = S * jnp.exp(tot) + jnp.where(bd, ds, 0.0)
    return s_new, y0 + y, bonus


def _run_lockstep(gens):
    results = [None] * len(gens)
    pending = list(range(len(gens)))
    while pending:
        for i in list(pending):
            try:
                next(gens[i])
            except StopIteration as stop:
                results[i] = stop.value
                pending.remove(i)
    return results


def _wkv_consts():
    t2 = lax.broadcasted_iota(jnp.int32, (CH, LANES), 0)
    l2 = lax.broadcasted_iota(jnp.int32, (CH, LANES), 1)
    s2 = l2 & (CH - 1)
    strict = [s2 < t2, s2 > t2]
    incl = [s2 <= t2, s2 >= t2]
    h0 = l2 < HD
    same_blk = (t2 // SB) == (s2 // SB)
    ri = lax.broadcasted_iota(jnp.int32, (LANES, LANES), 0)
    ci = lax.broadcasted_iota(jnp.int32, (LANES, LANES), 1)
    bd = (ri // HD) == (ci // HD)
    bo2 = jnp.where(bd, 1.0, 0.0).astype(BF)
    bd_sb = (ri // SB) == (ci // SB)
    t3 = lax.broadcasted_iota(jnp.int32, (SB, LANES), 0)
    l3 = lax.broadcasted_iota(jnp.int32, (SB, LANES), 1)
    eye_c = jnp.where((l3 & (SB - 1)) == t3, 1.0, 0.0)
    lane_blk = [((l3 & (CH - 1)) // SB) == i for i in range(CH // SB)]
    return t2, strict, incl, h0, bd, bo2, same_blk, bd_sb, eye_c, lane_blk


def _wkv_kernel(T, has_s0, *refs):
    (r_ref, k_ref, v_ref, kk_ref, lw0_ref, lw1_ref, ag0_ref, ag1_ref, ka_ref, rk_ref), rest = refs[:10], refs[10:]
    if has_s0:
        s0_ref, y_ref, bon_ref, s_scr = rest
    else:
        y_ref, bon_ref, st_ref, s_scr = rest
    nch = T // CH
    npair = y_ref.shape[1] // LANES
    cst = _wkv_consts()
    if has_s0:
        s_scr[...] = s0_ref[...]
    else:
        s_scr[...] = jnp.zeros_like(s_scr)
    y_ref[...] = jnp.zeros_like(y_ref)
    bon_ref[...] = jnp.zeros_like(bon_ref)
    lw_refs = (lw0_ref, lw1_ref)
    ag_refs = (ag0_ref, ag1_ref)

    def body(c, carry):
        chains = [(p, d) for p in range(npair) for d in range(2)]
        sl = {}
        for p, d in chains:
            r0 = pl.multiple_of((c if d == 0 else nch - 1 - c) * CH, CH)
            sl[p, d] = (pl.ds(r0, CH), pl.ds(p * LANES, LANES))
        args = {}
        for p, d in chains:
            rs, ls = sl[p, d]
            args[p, d] = (s_scr[d, p], r_ref[rs, ls], k_ref[rs, ls], v_ref[rs, ls], kk_ref[rs, ls],
                          lw_refs[d][rs, ls], ag_refs[d][rs, ls], ka_ref[:, ls], rk_ref[:, ls],
                          y_ref[rs, ls], bon_ref[rs, ls])
        outs = _run_lockstep([_wkv_chunk(d, *args[p, d][:9], cst) for p, d in chains])
        for (p, d), (s_new, y, bon) in zip(chains, outs):
            rs, ls = sl[p, d]
            s_scr[d, p] = s_new
            y_ref[rs, ls] = args[p, d][9] + y
            bon_ref[rs, ls] = args[p, d][10] + bon
        return carry

    lax.fori_loop(0, nch, body, 0)
    if not has_s0:
        for d in range(2):
            for p in range(npair):
                s = s_scr[d, p]
                st_ref[d, 2 * p] = s[:HD, :HD]
                st_ref[d, 2 * p + 1] = s[HD:, HD:]


def _wkv(arrs, ka, rk, T, nb, row_blk0, s0_bd=None, lanes=512):
    ng = D // lanes
    npair = lanes // LANES
    seq = pl.BlockSpec((T, lanes), lambda b, g: (row_blk0 + b, g))
    vec = pl.BlockSpec((1, lanes), lambda b, g: (0, g))
    st = pl.BlockSpec((None, 2, npair, LANES, LANES), lambda b, g: (b, 0, g, 0, 0))
    out_seq = pl.BlockSpec((T, lanes), lambda b, g: (b, g))
    in_specs = [seq] * 8 + [vec, vec]
    ins = list(arrs) + [ka, rk]
    out_specs = [out_seq, out_seq]
    out_shape = [jax.ShapeDtypeStruct((nb * T, D), F32)] * 2
    if s0_bd is not None:
        in_specs.append(st)
        ins.append(s0_bd)
    else:
        out_specs.append(pl.BlockSpec((None, 2, 2 * npair, HD, HD), lambda b, g: (b, 0, g, 0, 0)))
        out_shape.append(jax.ShapeDtypeStruct((nb, 2, NH, HD, HD), F32))
    return pl.pallas_call(
        functools.partial(_wkv_kernel, T, s0_bd is not None),
        grid=(nb, ng),
        in_specs=in_specs,
        out_specs=out_specs,
        out_shape=out_shape,
        scratch_shapes=[pltpu.VMEM((2, npair, LANES, LANES), F32)],
        compiler_params=_cp(("parallel", "parallel")),
        name="wkv_T%d" % T,
    )(*ins)


def _proj_kernel(gn, *refs):
    i = pl.program_id(0)
    is_p = i < NT_P
    if gn:
        (yp_ref, ys_ref, ap_ref, as_ref, bp_ref, bs_ref, g_ref, lng_ref, lnb_ref, bo_ref,
         mod_ref, w_ref, nrm_ref, wr_ref, br_ref, o_ref, xt_o, gate_o) = refs
        y = jnp.where(is_p, yp_ref[...], ys_ref[...])
    else:
        y_ref, ap_ref, as_ref, mod_ref, w_ref, nrm_ref, wr_ref, br_ref, o_ref, xt_o, gate_o = refs
        y = y_ref[...]
    a = jnp.where(is_p, ap_ref[...], as_ref[...])
    if gn:
        bo = bo_ref[...]
        mean = _seg_sum(a, bo) * (1.0 / HD)
        cen = a - mean
        var = _seg_sum(cen * cen, bo) * (1.0 / HD)
        yn = cen * lax.rsqrt(var + GN_EPS) * lng_ref[...] + lnb_ref[...]
        a = (yn + jnp.where(is_p, bp_ref[...], bs_ref[...])) * g_ref[...]
    gt = _mod_vec(mod_ref, _tile_mod_row(i), 2)
    y_new = y + gt * _bdot(a, w_ref[...])
    o_ref[...] = y_new
    row = _tile_mod_row(i)
    xt = _modulate(y_new, nrm_ref[...], _mod_vec(mod_ref, row, 3), _mod_vec(mod_ref, row, 4))
    xt_o[...] = xt.astype(BF)
    gate_o[...] = _route(xt, wr_ref[...], br_ref[...])


def _proj_residual(y, a_p, a_s, mod, w, router, gn_args=None, name="proj"):
    tile = pl.BlockSpec((TM, D), lambda i: (i, 0))
    tile_p = pl.BlockSpec((TM, D), lambda i: (jnp.minimum(i, NT_P - 1), 0))
    tile_s = pl.BlockSpec((TM, D), lambda i: (jnp.maximum(i - NT_P, 0), 0))
    full = lambda a: pl.BlockSpec(a.shape, lambda i: (0,) * a.ndim)
    if gn_args is not None:
        ins = [y[0], y[1], a_p, a_s]
        in_specs = [tile_p, tile_s, tile_p, tile_s]
    else:
        ins = [y, a_p, a_s]
        in_specs = [tile, tile_p, tile_s]
    if gn_args is not None:
        b_p, b_s, g, lng, lnb, bo = gn_args
        ins += [b_p, b_s, g, lng, lnb, bo]
        in_specs += [tile_p, tile_s, tile, full(lng), full(lnb), full(bo)]
    ins += [mod, w] + list(router)
    in_specs += [full(mod), full(w)] + [full(a) for a in router]
    return pl.pallas_call(
        functools.partial(_proj_kernel, gn_args is not None),
        grid=(NT,),
        in_specs=in_specs,
        out_specs=[tile, tile, pl.BlockSpec((TM, LANES), lambda i: (i, 0))],
        out_shape=[jax.ShapeDtypeStruct((NTOK, D), F32), jax.ShapeDtypeStruct((NTOK, D), BF),
                   jax.ShapeDtypeStruct((NTOK, LANES), F32)],
        compiler_params=_cp(("parallel",)),
        name=name,
    )(*ins)


def _route(xt, wr, br):
    x1, x2 = _split2(xt)
    w1, w2 = _split2(wr)
    dot = lambda a, b: jnp.dot(a, b, preferred_element_type=F32)
    logits = (dot(x2, w1) + dot(x1, w2)) + dot(x1, w1)
    logits = logits + br
    lane = lax.broadcasted_iota(jnp.int32, logits.shape, 1).astype(F32)
    isg = jnp.logical_and(lane >= NE, lane < NE + NE // EPG)
    mg = jnp.max(jnp.where(isg, logits, NEG), axis=-1, keepdims=True)
    eg = jnp.where(isg, jnp.exp(jnp.minimum(logits - mg, 0.0)), 0.0)
    p_sel = 1.0 / jnp.sum(eg, axis=-1, keepdims=True)
    gidx = jnp.min(jnp.where(jnp.logical_and(isg, logits == mg), lane, 1e3), axis=-1, keepdims=True) - NE
    lo_l = gidx * EPG
    ing = jnp.logical_and(lane >= lo_l, lane < lo_l + EPG)
    me = jnp.max(jnp.where(ing, logits, NEG), axis=-1, keepdims=True)
    ee = jnp.where(ing, jnp.exp(jnp.minimum(logits - me, 0.0)), 0.0)
    se = jnp.sum(ee, axis=-1, keepdims=True)
    pe = ee / se
    i1 = jnp.min(jnp.where(jnp.logical_and(ing, logits == me), lane, 1e3), axis=-1, keepdims=True)
    v1 = 1.0 / se
    rest = jnp.logical_and(ing, lane != i1)
    v2 = jnp.max(jnp.where(rest, pe, -1.0), axis=-1, keepdims=True)
    i2 = jnp.min(jnp.where(jnp.logical_and(rest, pe == v2), lane, 1e3), axis=-1, keepdims=True)
    den = v1 + v2
    return (jnp.where(lane == i1, p_sel * v1 / den, 0.0)
            + jnp.where(lane == i2, p_sel * v2 / den, 0.0)
            + jnp.where(lane == float(GRP_LANE), gidx, 0.0))


def _experts_kernel(final, blk_ref, xt_ref, gate_ref, w1_ref, w3_ref, w2_ref, y_ref, mod_ref, fin_ref, *rest):
    i = pl.program_id(0)
    g = pl.program_id(1)
    ngrp = NE // EPS
    if final:
        op_ref, os_ref, xs_scr, gs_scr, outs_scr, pos_scr = rest
    else:
        o_ref, xs_scr, gs_scr, outs_scr, pos_scr = rest
    ntile_p = NPR // TMX
    dot = lambda a, b: jnp.dot(a, b, preferred_element_type=F32)

    @pl.when(g == 0)
    def _():
        gate = gate_ref[...]
        lane = lax.broadcasted_iota(jnp.int32, gate.shape, 1)
        grp = jnp.sum(jnp.where(lane == GRP_LANE, gate, 0.0), axis=-1, keepdims=True)
        onehot = jnp.where(lane.astype(F32) == grp, 1.0, 0.0)
        tr = lax.broadcasted_iota(jnp.int32, (TMX, TMX), 0)
        br = lax.broadcasted_iota(jnp.int32, (LANES, LANES), 0)
        bc = lax.broadcasted_iota(jnp.int32, (LANES, LANES), 1)
        ltri = jnp.where(bc < br, 1.0, 0.0).astype(BF)
        carry = jnp.zeros((1, LANES), F32)
        ranks = []
        for blk in range(TMX // LANES):
            oh = onehot[blk * LANES:(blk + 1) * LANES]
            ranks.append(dot(ltri, oh.astype(BF)) + carry)
            carry = carry + jnp.sum(oh, axis=0, keepdims=True)
        rank = jnp.concatenate(ranks, axis=0)
        cnt = jnp.broadcast_to(carry, (8, LANES))
        lane8 = lax.broadcasted_iota(jnp.int32, (8, LANES), 1)
        off = jnp.zeros((8, LANES), F32)
        for k in range(1, ngrp):
            off = off + jnp.where(lane8 >= k, pltpu.roll(cnt, k, 1), 0.0)
        pos = jnp.sum(onehot * (off[0:1] + rank), axis=-1, keepdims=True)
        posb = jnp.broadcast_to(pos, (TMX, LANES))
        pos_scr[...] = posb
        pos_row = posb.T[0:1, :]
        perm = jnp.where(tr.astype(F32) == pos_row, 1.0, 0.0).astype(BF)
        xs_scr[...] = dot(perm, xt_ref[...]).astype(BF)
        g1, g2 = _split2(gate)
        gsel = dot(perm, jnp.concatenate([g1, g2], axis=1))
        gs_scr[...] = gsel[:, :LANES] + gsel[:, LANES:]
        outs_scr[...] = jnp.zeros_like(outs_scr)

    blo = blk_ref[2 * (i * ngrp + g)]
    bhi = blk_ref[2 * (i * ngrp + g) + 1]
    for b in range(TMX // SBR):
        @pl.when(jnp.logical_and(b >= blo, b < bhi))
        def _(b=b):
            rows = pl.ds(b * SBR, SBR)
            xb = xs_scr[rows, :]
            gsb = gs_scr[rows, :]
            lane_c = lax.broadcasted_iota(jnp.int32, gsb.shape, 1)
            hids = []
            for j in range(EPS):
                h1 = dot(xb, w1_ref[j])
                h3 = dot(xb, w3_ref[j])
                ge = jnp.sum(jnp.where(lane_c == g * EPS + j, gsb, 0.0), axis=-1, keepdims=True)
                hids.append(((h1 * jax.nn.sigmoid(h1)) * h3 * ge).astype(BF))
            outs_scr[rows, :] += dot(jnp.concatenate(hids, axis=1), w2_ref[...].reshape(EPS * DE, D))

    def result():
        row = jnp.where(i < ntile_p, 0, i - ntile_p + 1)
        tcol = lax.broadcasted_iota(jnp.int32, (TMX, TMX), 1).astype(F32)
        back = jnp.where(tcol == pos_scr[:, 0:1], 1.0, 0.0).astype(BF)
        out = y_ref[...] + _mod_vec(mod_ref, row, 5) * dot(back, outs_scr[...].astype(BF))
        if final:
            ms = jnp.mean(out * out, axis=-1, keepdims=True)
            out = out * lax.rsqrt(ms + NORM_EPS) * fin_ref[...]
        return out

    if final:
        @pl.when(jnp.logical_and(g == NE // EPS - 1, i < ntile_p))
        def _():
            op_ref[...] = result()

        @pl.when(jnp.logical_and(g == NE // EPS - 1, i >= ntile_p))
        def _():
            os_ref[...] = result()
    else:
        @pl.when(g == NE // EPS - 1)
        def _():
            o_ref[...] = result()


def _moe_experts(nch, xt, gate, w1, w3, w2, layer, y, mod, fin, final):
    tile = pl.BlockSpec((TMX, D), lambda i, e, n: (i, 0))
    full = lambda a: pl.BlockSpec(a.shape, lambda i, e, n: (0,) * a.ndim)
    ntile_p = NPR // TMX
    if final:
        out_specs = [pl.BlockSpec((TMX, D), lambda i, e, n: (jnp.minimum(i, ntile_p - 1), 0)),
                     pl.BlockSpec((TMX, D), lambda i, e, n: (jnp.maximum(i - ntile_p, 0), 0))]
        out_shape = [jax.ShapeDtypeStruct((NPR, D), F32), jax.ShapeDtypeStruct((NSR, D), F32)]
    else:
        out_specs = tile
        out_shape = jax.ShapeDtypeStruct((NTOK, D), F32)
    scratch = [pltpu.VMEM((TMX, D), BF), pltpu.VMEM((TMX, LANES), F32), pltpu.VMEM((TMX, D), F32),
               pltpu.VMEM((TMX, LANES), F32)]
    grid_spec = pltpu.PrefetchScalarGridSpec(
        num_scalar_prefetch=1,
        grid=(NTOK // TMX, NE // EPS),
        in_specs=[tile, pl.BlockSpec((TMX, LANES), lambda i, e, n: (i, 0)),
                  pl.BlockSpec((None, EPS, D, DE), lambda i, e, n: (layer, e, 0, 0)),
                  pl.BlockSpec((None, EPS, D, DE), lambda i, e, n: (layer, e, 0, 0)),
                  pl.BlockSpec((None, EPS, DE, D), lambda i, e, n: (layer, e, 0, 0)),
                  tile, full(mod), full(fin)],
        out_specs=out_specs,
        scratch_shapes=scratch)
    return pl.pallas_call(
        functools.partial(_experts_kernel, final),
        grid_spec=grid_spec,
        out_shape=out_shape,
        compiler_params=_cp(("arbitrary" if final else "parallel", "arbitrary")),
        name="moe_experts",
    )(nch, xt, gate, w1, w3, w2, y, mod, fin)


def _qkv_kernel(y_ref, mod_ref, nrm_ref, w_ref, q_o, kp_o, vp_o, ks_o, vs_o, kc_o, vc_o):
    i = pl.program_id(0)
    row = _tile_mod_row(_qkv_tile(i))
    h = _modulate(y_ref[...], nrm_ref[...], _mod_vec(mod_ref, row, 0), _mod_vec(mod_ref, row, 1))
    hb = h.astype(BF)
    kx = jnp.dot(hb, w_ref[:, D:2 * D], preferred_element_type=F32)
    vx = jnp.dot(hb, w_ref[:, 2 * D:], preferred_element_type=F32)
    q_o[...] = jnp.dot(hb, w_ref[:, :D], preferred_element_type=F32)

    @pl.when(i < NT - NT_P)
    def _():
        ks_o[...] = kx
        vs_o[...] = vx

    @pl.when(i >= NT - NT_P)
    def _():
        kp_o[...] = kx
        vp_o[...] = vx
        for hh in range(NH):
            kc_o[pl.ds(hh, TM, stride=NH), :] = kx[:, hh * HD:(hh + 1) * HD]
            vc_o[pl.ds(hh, TM, stride=NH), :] = vx[:, hh * HD:(hh + 1) * HD]


def _qkv_tile(i):
    return jnp.where(i < NT - NT_P, NT_P + i, i - (NT - NT_P))


def _na_qkv(y, mod, nrm, w):
    nt_s = NT - NT_P
    tile = pl.BlockSpec((TM, D), lambda i: (_qkv_tile(i), 0))
    tile_p = pl.BlockSpec((TM, D), lambda i: (jnp.maximum(i - nt_s, 0), 0))
    tile_s = pl.BlockSpec((TM, D), lambda i: (jnp.minimum(i, nt_s - 1), 0))
    full = lambda a: pl.BlockSpec(a.shape, lambda i: (0,) * a.ndim)
    out = lambda n: jax.ShapeDtypeStruct((n, D), F32)
    cache = jax.ShapeDtypeStruct((NPR * NH, HD), F32)
    cache_tile = pl.BlockSpec((TM * NH, HD), lambda i: (jnp.maximum(i - nt_s, 0), 0))
    return pl.pallas_call(
        _qkv_kernel,
        grid=(NT,),
        in_specs=[tile, full(mod), full(nrm), full(w)],
        out_specs=[tile, tile_p, tile_p, tile_s, tile_s, cache_tile, cache_tile],
        out_shape=[out(NTOK), out(NPR), out(NPR), out(NSR), out(NSR), cache, cache],
        compiler_params=_cp(("arbitrary",)),
        name="na_qkv",
    )(y, mod, nrm, w)


def _softmax_rows(s):
    m = jnp.max(s, axis=-1, keepdims=True)
    e = jnp.exp(s - m)
    return e / jnp.sum(e, axis=-1, keepdims=True)


def _ctx_attn_kernel(q_ref, k_ref, v_ref, o_ref):
    scale = HD ** -0.5
    h0 = lax.broadcasted_iota(jnp.int32, (T_P, LANES), 1) < HD
    def head_chain(h, q, kb, vb):
        qm = jnp.where(h0 if h == 0 else jnp.logical_not(h0), q, 0.0)
        s = _bdot_nt(qm, kb)
        yield
        o = jnp.dot(_softmax_rows(s * scale).astype(BF), vb, preferred_element_type=F32)
        yield
        return o

    npair = D // LANES
    for p0 in range(0, npair, CTX_PAIRS):
        gens = []
        for p in range(p0, p0 + CTX_PAIRS):
            ls = pl.ds(p * LANES, LANES)
            q = q_ref[:, ls]
            kb = k_ref[:, ls].astype(BF)
            vb = v_ref[:, ls].astype(BF)
            gens += [head_chain(h, q, kb, vb) for h in range(2)]
        outs = _run_lockstep(gens)
        for j, p in enumerate(range(p0, p0 + CTX_PAIRS)):
            o_ref[:, pl.ds(p * LANES, LANES)] = jnp.where(h0, outs[2 * j], outs[2 * j + 1])


def _ctx_attn(q, k, v):
    blk = pl.BlockSpec((T_P, D), lambda b: (b, 0))
    return pl.pallas_call(
        _ctx_attn_kernel,
        grid=(NB_P,),
        in_specs=[blk] * 3,
        out_specs=blk,
        out_shape=jax.ShapeDtypeStruct((NPR, D), F32),
        compiler_params=_cp(("parallel",)),
        name="ctx_attn",
    )(q, k, v)


def _nbr_attn_kernel(q_ref, k_ref, v_ref, kc_ref, vc_ref, p_ref, o_ref, tz_ref):
    scale = HD ** -0.5
    rows = T_S // GRID_W
    nloc = WIN_R * GRID_W
    qc = lax.broadcasted_iota(jnp.int32, (GRID_W, LANES), 0)
    ln = lax.broadcasted_iota(jnp.int32, (GRID_W, LANES), 1)
    h0 = ln < HD
    kcb = kc_ref[...].astype(BF)
    vcb = vc_ref[...].astype(BF)
    kcol = ln & (GRID_W - 1)
    cstart = jnp.clip(qc - WIN_C // 2, 0, GRID_W - WIN_C)
    valid = jnp.logical_and(kcol >= cstart, kcol < cstart + WIN_C)
    for h in range(2):
        rolled = []
        for j in range(2 * WIN_R - 1):
            prow = jnp.broadcast_to(p_ref[h, j:j + 1, :], (GRID_W, LANES))
            rolled.append((pltpu.roll(prow, 0, 1, stride=1, stride_axis=0),
                           pltpu.roll(prow, GRID_W, 1, stride=1, stride_axis=0)))
        for j in range(2 * WIN_R - 2):
            tz_ref[h, j] = jnp.where(valid, jnp.where(h0, rolled[j][0], rolled[j + 1][1]), NEG)

    def head_chain(h, q, klb, vlb, j0):
        qm = jnp.where(h0 if h == 0 else jnp.logical_not(h0), q, 0.0).astype(BF)
        sl = _bdot_nt(qm, klb)
        sc = _bdot_nt(qm, kcb)
        yield
        bias = jnp.concatenate([tz_ref[h, j0 + 2 * m] for m in range(WIN_R // 2)], axis=1)
        sl = sl * scale + bias
        sc = sc * scale
        m = jnp.maximum(jnp.max(sl, axis=-1, keepdims=True), jnp.max(sc, axis=-1, keepdims=True))
        el = jnp.exp(sl - m)
        ec = jnp.exp(sc - m)
        inv = 1.0 / (jnp.sum(el, axis=-1, keepdims=True) + jnp.sum(ec, axis=-1, keepdims=True))
        o = (jnp.dot((el * inv).astype(BF), vlb, preferred_element_type=F32)
             + jnp.dot((ec * inv).astype(BF), vcb, preferred_element_type=F32))
        yield
        return o

    def body(it, carry):
        gens, slices = [], []
        for j in range(NBR_UNROLL):
            r = it * NBR_UNROLL + j
            start = jnp.clip(r - WIN_R // 2, 0, rows - WIN_R)
            qs = pl.ds(pl.multiple_of(r * GRID_W, GRID_W), GRID_W)
            ks = pl.ds(pl.multiple_of(start * GRID_W, GRID_W), nloc)
            q = q_ref[qs, :]
            klb = k_ref[ks, :].astype(BF)
            vlb = v_ref[ks, :].astype(BF)
            j0 = start - r + WIN_R - 1
            slices.append(qs)
            gens += [head_chain(h, q, klb, vlb, j0) for h in range(2)]
        outs = _run_lockstep(gens)
        for j, qs in enumerate(slices):
            o_ref[qs, :] = jnp.where(h0, outs[2 * j], outs[2 * j + 1])
        return carry

    lax.fori_loop(0, rows // NBR_UNROLL, body, 0)


def _nbr_attn(q, k, v, kc, vc, ptab):
    npair = D // LANES
    nrow = ptab.shape[1]
    qseq = pl.BlockSpec((T_S, LANES), lambda b, p: (NPR // T_S + b, p))
    seq = pl.BlockSpec((T_S, LANES), lambda b, p: (b, p))
    ctx = pl.BlockSpec((kc.shape[0] // NB_S, LANES), lambda b, p: (b, p))
    return pl.pallas_call(
        _nbr_attn_kernel,
        grid=(NB_S, npair),
        in_specs=[qseq, seq, seq, ctx, ctx,
                  pl.BlockSpec((2, nrow, LANES), lambda b, p: (p, 0, 0))],
        out_specs=pl.BlockSpec((T_S, LANES), lambda b, p: (b, p)),
        out_shape=jax.ShapeDtypeStruct((NSR, D), F32),
        scratch_shapes=[pltpu.VMEM((2, nrow - 1, GRID_W, LANES), F32)],
        compiler_params=_cp(("parallel", "parallel")),
        name="nbr_attn",
    )(q, k, v, kc, vc, ptab)


def _block_diag2(a, b):
    z = jnp.zeros_like(a)
    return jnp.concatenate([jnp.concatenate([a, z], axis=1), jnp.concatenate([z, b], axis=1)], axis=0)


def _nbr_bias_table(rpb):
    pad = jnp.zeros(rpb.shape[:2] + (LANES - 2 * WIN_C + 1,), F32)
    return jnp.concatenate([rpb[:, :, WIN_C - 1:], pad, rpb[:, :, :WIN_C - 1]], axis=-1)


def kernel(x_prompt, x_sample, c, state_rwkv, cache_na_k, cache_na_v, c_ctx, norm_mix, norm_ffn, ada_w, ada_b, rwkv_mu, rwkv_w_r, rwkv_w_k, rwkv_w_v, rwkv_w_o, rwkv_w0, rwkv_w1, rwkv_w2, rwkv_a0, rwkv_a1, rwkv_a2, rwkv_g1, rwkv_g2, rwkv_k_k, rwkv_k_a, rwkv_r_k, rwkv_lnx_g, rwkv_lnx_b, na_w_qkv, na_w_o, na_rpb, moe_w_grp, moe_b_grp, moe_w_exp, moe_b_exp, moe_w1, moe_w3, moe_w2, final_norm):
    x_p = x_prompt.reshape(NPR, D)
    x_s = x_sample.reshape(NSR, D)
    c8 = jnp.concatenate([c_ctx[None, :], c, jnp.zeros((8 - 1 - NB_S, D), F32)], axis=0)
    mod = _adaln(c8, ada_w, ada_b)
    ri = jnp.arange(2 * LANES)[:, None] // HD
    bo = (ri == ri.T).astype(BF)
    row = lambda a: a.reshape(1, D)

    w1c = jnp.concatenate([rwkv_w1[0, 0], rwkv_w1[0, 1]], axis=1).astype(BF)
    a1c = jnp.concatenate([rwkv_a1[0, 0], rwkv_a1[0, 1]], axis=1).astype(BF)
    w2bd = _block_diag2(rwkv_w2[0, 0], rwkv_w2[0, 1]).astype(BF)
    a2bd = _block_diag2(rwkv_a2[0, 0], rwkv_a2[0, 1]).astype(BF)
    r, k, v, g, kk, lw0, lw1, ag0, ag1 = _rwkv_front(
        x_p, x_s, mod[0], row(norm_mix[0]), rwkv_mu[0], rwkv_w_r[0].astype(BF), rwkv_w_k[0].astype(BF),
        rwkv_w_v[0].astype(BF), w1c, a1c, rwkv_g1[0].astype(BF), w2bd, a2bd, rwkv_g2[0].astype(BF),
        rwkv_w0[0], rwkv_a0[0], row(rwkv_k_k[0]), bo)
    arrs = (r, k, v, kk, lw0, lw1, ag0, ag1)
    ka, rk = row(rwkv_k_a[0]), row(rwkv_r_k[0])
    ys_p, bon_p, new_state = _wkv(arrs, ka, rk, T_P, NB_P, 0, lanes=D)
    s0 = state_rwkv[:, 0].reshape(NB_S, 2, NH // 2, 2, HD, HD)
    z = jnp.zeros_like(s0[:, :, :, 0])
    s0_bd = jnp.concatenate([jnp.concatenate([s0[:, :, :, 0], z], axis=-1),
                             jnp.concatenate([z, s0[:, :, :, 1]], axis=-1)], axis=-2)
    ys_s, bon_s = _wkv(arrs, ka, rk, T_S, NB_S, NPR // T_S, s0_bd=s0_bd)
    new_state_rwkv = new_state.reshape(NB_P, 1, 2, NH, HD, HD)
    def router(i):
        wr = jnp.concatenate([moe_w_exp[i], moe_w_grp[i],
                              jnp.zeros((D, LANES - NE - NE // EPG), F32)], axis=1)
        br = jnp.concatenate([moe_b_exp[i], moe_b_grp[i],
                              jnp.zeros((LANES - NE - NE // EPG,), F32)]).reshape(1, LANES)
        return (row(norm_ffn[i]), wr, br)

    def experts(y, xt, gate, i, final):
        grp = gate[:, GRP_LANE].astype(jnp.int32).reshape(NTOK // TMX, TMX)
        cnt = jnp.sum(grp[:, :, None] == jnp.arange(NE // EPG)[None, None, :], axis=1).astype(jnp.int32)
        off = jnp.cumsum(cnt, axis=1) - cnt
        blo = off // SBR
        bhi = jnp.where(cnt > 0, (off + cnt - 1) // SBR + 1, blo)
        blk = jnp.stack([blo, bhi], axis=-1).astype(jnp.int32).reshape(-1)
        return _moe_experts(blk, xt, gate, w1b, w3b, w2b, i, y, mod[i], row(final_norm), final)

    w1b, w3b, w2b = moe_w1.astype(BF), moe_w3.astype(BF), moe_w2.astype(BF)

    y, xt, gate = _proj_residual((x_p, x_s), ys_p, ys_s, mod[0], rwkv_w_o[0].astype(BF), router(0),
                                 gn_args=(bon_p, bon_s, g, row(rwkv_lnx_g[0]), row(rwkv_lnx_b[0]), bo),
                                 name="rwkv_out")
    y = experts(y, xt, gate, 0, False)

    q, k_p, v_p, k_s, v_s, kcache, vcache = _na_qkv(y, mod[1], row(norm_mix[1]), na_w_qkv[0].astype(BF))
    o_p = _ctx_attn(q, k_p, v_p)
    kc = cache_na_k[:, 0].reshape(NB_S * cache_na_k.shape[2], D)
    vc = cache_na_v[:, 0].reshape(NB_S * cache_na_v.shape[2], D)
    o_s = _nbr_attn(q, k_s, v_s, kc, vc, _nbr_bias_table(na_rpb[0]))
    y, xt, gate = _proj_residual(y, o_p, o_s, mod[1], na_w_o[0].astype(BF), router(1), name="attn_out")
    y_p, y_s = experts(y, xt, gate, 1, True)

    y_prompt = y_p.reshape(NB_P, T_P, D)
    y_sample = y_s.reshape(NB_S, T_S, D)
    new_k = kcache.reshape(NB_P, 1, T_P, NH, HD)
    new_v = vcache.reshape(NB_P, 1, T_P, NH, HD)
    return (y_prompt, y_sample, new_state_rwkv, new_k, new_v)
```
